```python
import jax, jax.numpy as jnp
from jax import lax
import numpy as np

D_MODEL = 1024
BATCH = 8
SEQ = 4096
DEPTH = 1

CONV_CH = 512
CONV_WIDTH = 31
N_HEADS = 8
N_KV_HEADS = 2
HEAD_DIM = 64
WINDOW = 128
ATTN_BLOCK = 128
N_GROUPS = 4
EXPERTS_PER_GROUP = 8
N_EXPERTS = N_GROUPS * EXPERTS_PER_GROUP
EXPERT_TOP_K = 2
D_FF_EXPERT = 512
MOE_BLOCK = 128
NORM_EPS = 1e-6

Q_DIM = N_HEADS * HEAD_DIM
KV_DIM = N_KV_HEADS * HEAD_DIM
IN_COLS = 2 * CONV_CH + Q_DIM + 2 * KV_DIM + 2 * D_MODEL

kernel_name = "hybrid_conv_swa_sink_alibi_hiermoe"


def rms_norm(x, g):
    xf = x.astype(jnp.float32)
    y = xf * lax.rsqrt(jnp.mean(xf * xf, axis=-1, keepdims=True) + NORM_EPS)
    return (y * g.astype(jnp.float32)).astype(x.dtype)


def layer_norm(x, g, b):
    xf = x.astype(jnp.float32)
    mu = jnp.mean(xf, axis=-1, keepdims=True)
    var = jnp.mean(jnp.square(xf - mu), axis=-1, keepdims=True)
    y = (xf - mu) * lax.rsqrt(var + NORM_EPS)
    return (y * g.astype(jnp.float32) + b.astype(jnp.float32)).astype(x.dtype)


def alibi_slopes():
    return jnp.asarray(np.array([2.0 ** (-8.0 * (h + 1) / N_HEADS) for h in range(N_HEADS)], np.float32))


def conformer_conv(u, w_dw, b_dw, ln_g, ln_b):
    a, gate = jnp.split(u, 2, axis=-1)
    v = a * jax.nn.sigmoid(gate)
    y = lax.conv_general_dilated(
        v, w_dw, window_strides=(1,), padding=[(CONV_WIDTH - 1, 0)],
        dimension_numbers=("NWC", "WIO", "NWC"), feature_group_count=CONV_CH)
    y = y + b_dw
    return jax.nn.silu(layer_norm(y, ln_g, ln_b))


def sliding_window_attention(q, k, v, sinks):
    b, s = q.shape[0], q.shape[1]
    nb = s // ATTN_BLOCK
    grp = N_HEADS // N_KV_HEADS
    qb = q.reshape(b, nb, ATTN_BLOCK, N_KV_HEADS, grp, HEAD_DIM)
    kb = k.reshape(b, nb, ATTN_BLOCK, N_KV_HEADS, HEAD_DIM)
    vb = v.reshape(b, nb, ATTN_BLOCK, N_KV_HEADS, HEAD_DIM)

    def with_prev(t):
        prev = jnp.pad(t[:, :-1], ((0, 0), (1, 0), (0, 0), (0, 0), (0, 0)))
        return jnp.concatenate([prev, t], axis=2)

    kk, vv = with_prev(kb), with_prev(vb)
    scores = jnp.einsum("bnqhgd,bnkhd->bnhgqk", qb, kk,
                        preferred_element_type=jnp.float32) * (HEAD_DIM ** -0.5)
    qi = jnp.arange(ATTN_BLOCK)[:, None]
    kj = jnp.arange(2 * ATTN_BLOCK)[None, :]
    rel = ATTN_BLOCK + qi - kj
    in_window = (rel >= 0) & (rel < WINDOW)
    key_exists = (jnp.arange(nb)[:, None, None] * ATTN_BLOCK + kj[None] - ATTN_BLOCK) >= 0
    mask = in_window[None] & key_exists
    slopes = alibi_slopes().reshape(N_KV_HEADS, grp)
    scores = scores - slopes[:, :, None, None] * rel.astype(jnp.float32)
    scores = jnp.where(mask[None, :, None, None], scores, -jnp.inf)
    sink = sinks.astype(jnp.float32).reshape(N_KV_HEADS, grp)[None, None, :, :, None, None]
    m = jnp.maximum(jnp.max(scores, axis=-1, keepdims=True), sink)
    p = jnp.exp(scores - m)
    p = p / (jnp.sum(p, axis=-1, keepdims=True) + jnp.exp(sink - m))
    out = jnp.einsum("bnhgqk,bnkhd->bnqhgd", p.astype(v.dtype), vv)
    return out.reshape(b, s, Q_DIM)


def hierarchical_moe(xn, w_group, b_group, w_expert, b_expert, w_gate, w_up, w_down):
    b, s, d = xn.shape
    t = b * s
    xt = xn.reshape(t, d)
    g_logits = (xt @ w_group).astype(jnp.float32) + b_group.astype(jnp.float32)
    g_prob = jax.nn.softmax(g_logits, axis=-1)
    g_sel = jnp.argmax(g_logits, axis=-1).astype(jnp.int32)
    p_group = jnp.take_along_axis(g_prob, g_sel[:, None], axis=-1)[:, 0]
    e_logits = ((xt @ w_expert).astype(jnp.float32) + b_expert.astype(jnp.float32)).reshape(t, N_GROUPS, EXPERTS_PER_GROUP)
    e_in = jnp.take_along_axis(e_logits, g_sel[:, None, None], axis=1)[:, 0]
    top_val, top_idx = lax.top_k(e_in, EXPERT_TOP_K)
    weights = p_group[:, None] * jax.nn.softmax(top_val, axis=-1)
    expert_id = (g_sel[:, None] * EXPERTS_PER_GROUP + top_idx.astype(jnp.int32)).reshape(-1)
    token_id = jnp.repeat(jnp.arange(t, dtype=jnp.int32), EXPERT_TOP_K)
    w_flat = weights.reshape(-1)
    n_assign = t * EXPERT_TOP_K
    n_blocks = -(-(n_assign + N_EXPERTS * (MOE_BLOCK - 1)) // MOE_BLOCK)
    order = jnp.argsort(expert_id, stable=True)
    sorted_e = expert_id[order]
    counts = jnp.zeros((N_EXPERTS,), jnp.int32).at[expert_id].add(1)
    padded = ((counts + MOE_BLOCK - 1) // MOE_BLOCK) * MOE_BLOCK
    padded_ends = jnp.cumsum(padded)
    padded_starts = padded_ends - padded
    starts = jnp.cumsum(counts) - counts
    rank = jnp.arange(n_assign, dtype=jnp.int32) - starts[sorted_e]
    dest = padded_starts[sorted_e] + rank
    slot_token = jnp.zeros((n_blocks * MOE_BLOCK,), jnp.int32).at[dest].set(token_id[order])
    slot_w = jnp.zeros((n_blocks * MOE_BLOCK,), jnp.float32).at[dest].set(w_flat[order])
    block_start = jnp.arange(n_blocks, dtype=jnp.int32) * MOE_BLOCK
    block_expert = jnp.clip(jnp.searchsorted(padded_ends, block_start, side="right"), 0, N_EXPERTS - 1).astype(jnp.int32)

    def run_block(args):
        tok, wt, e = args
        xb = xt[tok]
        hmid = jax.nn.silu(xb @ w_gate[e]) * (xb @ w_up[e])
        return (hmid @ w_down[e]) * wt[:, None].astype(xb.dtype)

    ys = lax.map(run_block, (slot_token.reshape(n_blocks, MOE_BLOCK),
                             slot_w.reshape(n_blocks, MOE_BLOCK), block_expert))
    out = jnp.zeros((t, d), xn.dtype).at[slot_token].add(ys.reshape(-1, d))
    return out.reshape(b, s, d)


def setup_inputs(seed: int = 0) -> dict:
    key = jax.random.key(seed)
    ks = jax.random.split(key, 24)
    f32 = jnp.float32

    def nrm(k, shape, scale):
        return jax.random.normal(k, shape, f32) * scale

    L = DEPTH
    return {
        "x": jax.random.normal(ks[0], (BATCH, SEQ, D_MODEL), f32),
        "g_mix": 1.0 + nrm(ks[1], (L, D_MODEL), 0.02),
        "w_in": nrm(ks[2], (L, D_MODEL, IN_COLS), D_MODEL ** -0.5),
        "w_dw": nrm(ks[3], (L, CONV_WIDTH, 1, CONV_CH), CONV_WIDTH ** -0.5),
        "b_dw": nrm(ks[4], (L, CONV_CH), 0.02),
        "ln_conv_g": 1.0 + nrm(ks[5], (L, CONV_CH), 0.02),
        "ln_conv_b": nrm(ks[6], (L, CONV_CH), 0.02),
        "sinks": nrm(ks[7], (L, N_HEADS), 0.5),
        "w_conv_out": nrm(ks[8], (L, CONV_CH, D_MODEL), CONV_CH ** -0.5),
        "w_attn_out": nrm(ks[9], (L, Q_DIM, D_MODEL), Q_DIM ** -0.5),
        "w_out": nrm(ks[10], (L, D_MODEL, D_MODEL), D_MODEL ** -0.5),
        "g_ffn": 1.0 + nrm(ks[11], (L, D_MODEL), 0.02),
        "w_group": nrm(ks[12], (L, D_MODEL, N_GROUPS), D_MODEL ** -0.5),
        "b_group": nrm(ks[13], (L, N_GROUPS), 0.01),
        "w_expert": nrm(ks[14], (L, D_MODEL, N_EXPERTS), D_MODEL ** -0.5),
        "b_expert": nrm(ks[15], (L, N_EXPERTS), 0.01),
        "w_gate": nrm(ks[16], (L, N_EXPERTS, D_MODEL, D_FF_EXPERT), D_MODEL ** -0.5),
        "w_up": nrm(ks[17], (L, N_EXPERTS, D_MODEL, D_FF_EXPERT), D_MODEL ** -0.5),
        "w_down": nrm(ks[18], (L, N_EXPERTS, D_FF_EXPERT, D_MODEL), D_FF_EXPERT ** -0.5),
        "g_final": 1.0 + nrm(ks[19], (D_MODEL,), 0.02),
    }


def reference(x, g_mix, w_in, w_dw, b_dw, ln_conv_g, ln_conv_b, sinks, w_conv_out, w_attn_out,
              w_out, g_ffn, w_group, b_group, w_expert, b_expert, w_gate, w_up, w_down, g_final):
    b, s, _ = x.shape
    cuts = list(np.cumsum([2 * CONV_CH, Q_DIM, KV_DIM, KV_DIM, D_MODEL]))
    h = x
    for l in range(DEPTH):
        xn = rms_norm(h, g_mix[l])
        proj = xn @ w_in[l]
        u_conv, q, k, v, gate_conv, gate_attn = jnp.split(proj, cuts, axis=-1)
        conv_o = conformer_conv(u_conv, w_dw[l], b_dw[l], ln_conv_g[l], ln_conv_b[l]) @ w_conv_out[l]
        attn = sliding_window_attention(q.reshape(b, s, N_HEADS, HEAD_DIM),
                                        k.reshape(b, s, N_KV_HEADS, HEAD_DIM),
                                        v.reshape(b, s, N_KV_HEADS, HEAD_DIM), sinks[l])
        attn_o = attn @ w_attn_out[l]
        merged = jax.nn.sigmoid(gate_conv) * conv_o + jax.nn.sigmoid(gate_attn) * attn_o
        h = h + merged @ w_out[l]
        h = h + hierarchical_moe(rms_norm(h, g_ffn[l]), w_group[l], b_group[l], w_expert[l],
                                 b_expert[l], w_gate[l], w_up[l], w_down[l])
    return rms_norm(h, g_final)
```

```python
import functools

import numpy as np
import jax
import jax.numpy as jnp
from jax import lax
from jax.experimental import pallas as pl
from jax.experimental.pallas import tpu as pltpu

D_MODEL = 1024
CONV_CH = 512
CONV_WIDTH = 31
N_HEADS = 8
N_KV_HEADS = 2
HEAD_DIM = 64
ATTN_BLOCK = 128
N_GROUPS = 4
EXPERTS_PER_GROUP = 8
N_EXPERTS = N_GROUPS * EXPERTS_PER_GROUP
D_FF_EXPERT = 512
NORM_EPS = 1e-6

Q_DIM = N_HEADS * HEAD_DIM
KV_DIM = N_KV_HEADS * HEAD_DIM

LANES = 128
SUBLANES = 8
CONV_HALO = 32
ROUTER_ROWS = SUBLANES + N_EXPERTS
NEG_BIG = -1e30

TOKEN_TILE = 512
ROUTE_TILE = 512
EXPERT_BLOCK = 256
DMA_TILE = 256
VMEM_LIMIT = 56 * 1024 * 1024

f32 = jnp.float32
bf16 = jnp.bfloat16


def _rms(x, g):
    ms = jnp.mean(x * x, axis=-1, keepdims=True)
    return x * lax.rsqrt(ms + NORM_EPS) * g


def _sigmoid(x):
    return 1.0 / (1.0 + jnp.exp(-x))


def _inproj_kernel(x_ref, g_ref, w_ref, u_ref, q_ref, kv_ref, gate_ref):
    xn = _rms(x_ref[...], g_ref[...]).astype(bf16)
    col = 0
    for out_ref in (u_ref, q_ref, kv_ref, gate_ref):
        width = out_ref.shape[1]
        for lo in range(0, width, 1024):
            hi = min(lo + 1024, width)
            out_ref[:, lo:hi] = jnp.dot(
                xn, w_ref[:, col + lo:col + hi], preferred_element_type=f32).astype(out_ref.dtype)
        col += width


def _inproj(x2, g_mix, w_cat):
    t = x2.shape[0]
    tm = TOKEN_TILE
    widths = (2 * CONV_CH, Q_DIM, 4 * KV_DIM, 2 * D_MODEL)
    return pl.pallas_call(
        _inproj_kernel,
        grid=(t // tm,),
        in_specs=[
            pl.BlockSpec((tm, D_MODEL), lambda i: (i, 0)),
            pl.BlockSpec((1, D_MODEL), lambda i: (0, 0)),
            pl.BlockSpec(w_cat.shape, lambda i: (0, 0)),
        ],
        out_specs=[pl.BlockSpec((tm, w), lambda i: (i, 0)) for w in widths],
        out_shape=[jax.ShapeDtypeStruct((t, w), bf16) for w in widths],
        compiler_params=pltpu.CompilerParams(
            dimension_semantics=("arbitrary",), vmem_limit_bytes=VMEM_LIMIT),
        name="inproj",
    )(x2, g_mix, w_cat)


def _mixer_kernel(x_ref, u_ref, up_ref, q_ref, kv_ref, kvp_ref, gate_ref, wdw_ref, bdw_ref,
                  lng_ref, lnb_ref, sink_ref, bias_ref, wco_ref, wao_ref, wo_ref, gffn_ref,
                  wrh_ref, wrl_ref, br_ref,
                  h_ref, xn2_ref, route_ref, wts_ref,
                  vbuf, act, kvall, attn):
    ts = x_ref.shape[0]
    first = pl.program_id(1) == 0

    def glu(ref):
        return ref[:, :CONV_CH].astype(f32) * _sigmoid(ref[:, CONV_CH:].astype(f32))

    vbuf[0:CONV_HALO, :] = jnp.where(first, 0.0, glu(up_ref))
    vbuf[CONV_HALO:, :] = glu(u_ref)

    rows = 32
    tap0 = CONV_HALO - (CONV_WIDTH - 1)

    def conv_chunk(c, carry):
        r0 = pl.multiple_of(c * rows, rows)
        acc = jnp.broadcast_to(bdw_ref[...], (rows, CONV_CH))
        win = vbuf[pl.ds(r0, rows + CONV_HALO), :]
        for r in range(SUBLANES):
            n_a = (CONV_WIDTH - 1 - r) // SUBLANES + 1
            shifted = win[tap0 + r:tap0 + r + SUBLANES * (n_a - 1) + rows]
            for a in range(n_a):
                j = SUBLANES * a + r
                acc = acc + wdw_ref[j:j + 1, :] * shifted[SUBLANES * a:SUBLANES * a + rows]
        mu = jnp.mean(acc, axis=-1, keepdims=True)
        d = acc - mu
        var = jnp.mean(d * d, axis=-1, keepdims=True)
        y = d * lax.rsqrt(var + NORM_EPS) * lng_ref[...] + lnb_ref[...]
        act[pl.ds(r0, rows), :] = (y * _sigmoid(y)).astype(bf16)
        return carry

    lax.fori_loop(0, ts // rows, conv_chunk, 0)

    kvall[0:ATTN_BLOCK, :] = jnp.where(first, jnp.zeros_like(kvp_ref[...]), kvp_ref[...])
    kvall[ATTN_BLOCK:, :] = kv_ref[...]
    nkeys = 2 * ATTN_BLOCK
    left_kv = lax.broadcasted_iota(jnp.int32, (nkeys, LANES), 1) < HEAD_DIM
    left_q = lax.broadcasted_iota(jnp.int32, (ATTN_BLOCK, LANES), 1) < HEAD_DIM
    key_lane = lax.broadcasted_iota(jnp.int32, (ATTN_BLOCK, 2 * nkeys), 1)
    prev_keys = (key_lane % nkeys) < ATTN_BLOCK
    no_prev = jnp.where(prev_keys, jnp.where(first, -jnp.inf, 0.0), 0.0)

    def block_diag(x):
        z = jnp.zeros_like(x)
        return jnp.concatenate([jnp.where(left_kv, x, z), jnp.where(left_kv, z, x)], axis=0)

    def softmax_half(s, sink):
        m = jnp.maximum(jnp.max(s, axis=-1, keepdims=True), sink)
        p = jnp.exp(s - m)
        return p, jnp.sum(p, axis=-1, keepdims=True) + jnp.exp(sink - m)

    for j in range(ts // ATTN_BLOCK):
        r0 = j * ATTN_BLOCK
        for kvh in range(N_KV_HEADS):
            kbd = block_diag(kvall[r0:r0 + nkeys, kvh * LANES:(kvh + 1) * LANES])
            vbd = block_diag(kvall[r0:r0 + nkeys, (N_KV_HEADS + kvh) * LANES:(N_KV_HEADS + kvh + 1) * LANES])
            for ii in range(2):
                i = 2 * kvh + ii
                qb = q_ref[r0:r0 + ATTN_BLOCK, i * LANES:(i + 1) * LANES]
                s = lax.dot_general(qb, kbd, (((1,), (1,)), ((), ())), preferred_element_type=f32)
                s = s + bias_ref[i]
                if j == 0:
                    s = s + no_prev
                pa, la = softmax_half(s[:, :nkeys], sink_ref[2 * i])
                pb, lb = softmax_half(s[:, nkeys:], sink_ref[2 * i + 1])
                p = jnp.concatenate([pa, pb], axis=1).astype(bf16)
                o = jnp.dot(p, vbd, preferred_element_type=f32)
                o = o * jnp.where(left_q, 1.0 / la, 1.0 / lb)
                attn[r0:r0 + ATTN_BLOCK, i * LANES:(i + 1) * LANES] = o.astype(bf16)

    conv_o = jnp.dot(act[...], wco_ref[...], preferred_element_type=f32)
    attn_o = jnp.dot(attn[...], wao_ref[...], preferred_element_type=f32)
    merged = (_sigmoid(gate_ref[:, :D_MODEL].astype(f32)) * conv_o
              + _sigmoid(gate_ref[:, D_MODEL:].astype(f32)) * attn_o).astype(bf16)
    h = x_ref[...] + jnp.dot(merged, wo_ref[...], preferred_element_type=f32)
    h_ref[...] = h

    xn2 = _rms(h, gffn_ref[...])
    xn2_ref[...] = xn2
    xh = xn2.astype(bf16)
    xl = (xn2 - xh.astype(f32)).astype(bf16)
    logits = (jnp.dot(xh, wrh_ref[...], preferred_element_type=f32)
              + jnp.dot(xh, wrl_ref[...], preferred_element_type=f32)
              + jnp.dot(xl, wrh_ref[...], preferred_element_type=f32)) + br_ref[...]
    lt = logits.T
    sub = lax.broadcasted_iota(jnp.int32, (SUBLANES, ts), 0)

    gl = lt[0:SUBLANES]
    gmax = jnp.max(gl, axis=0, keepdims=True)
    gsel = jnp.min(jnp.where(gl == gmax, sub, SUBLANES), axis=0, keepdims=True)
    p_group = 1.0 / jnp.sum(jnp.exp(gl - gmax), axis=0, keepdims=True)

    e_in = lt[SUBLANES:2 * SUBLANES]
    for g in range(1, N_GROUPS):
        e_in = jnp.where(gsel == g, lt[(g + 1) * SUBLANES:(g + 2) * SUBLANES], e_in)
    m1 = jnp.max(e_in, axis=0, keepdims=True)
    i1 = jnp.min(jnp.where(e_in == m1, sub, SUBLANES), axis=0, keepdims=True)
    rest = jnp.where(sub == i1, -jnp.inf, e_in)
    m2 = jnp.max(rest, axis=0, keepdims=True)
    i2 = jnp.min(jnp.where(rest == m2, sub, SUBLANES), axis=0, keepdims=True)
    t2 = jnp.exp(m2 - m1)
    w1 = p_group / (1.0 + t2)
    w2 = p_group * t2 / (1.0 + t2)
    base = gsel * EXPERTS_PER_GROUP
    route_ref[...] = jnp.where(sub == 0, base + i1, jnp.where(sub == 1, base + i2, 0))
    wts_ref[...] = jnp.where(sub == 0, w1, jnp.where(sub == 1, w2, 0.0))


def _attn_bias():
    qi = np.arange(ATTN_BLOCK)[:, None]
    kj = np.arange(2 * ATTN_BLOCK)[None, :]
    rel = (ATTN_BLOCK + qi - kj).astype(np.float32)
    ok = (rel >= 0) & (rel < ATTN_BLOCK)
    slopes = np.array([2.0 ** (-8.0 * (h + 1) / N_HEADS) for h in range(N_HEADS)], np.float32)
    per_head = [np.where(ok, -(slopes[h] * rel), -np.inf).astype(np.float32) for h in range(N_HEADS)]
    return np.stack([np.concatenate([per_head[2 * i], per_head[2 * i + 1]], axis=1)
                     for i in range(N_HEADS // 2)])


def _mixer(x2, u, q, kv, gates, w_dw, b_dw, ln_g, ln_b, sinks, wco, wao, wo, g_ffn, wr_hi, wr_lo, b_r,
           batch, seq):
    t = x2.shape[0]
    ts = TOKEN_TILE
    ns = seq // ts
    bias = jnp.asarray(_attn_bias())

    def row(b, s):
        return b * ns + s

    def full(a):
        return pl.BlockSpec(a.shape, lambda b, s: (0,) * a.ndim)

    in_specs = [
        pl.BlockSpec((ts, D_MODEL), lambda b, s: (row(b, s), 0)),
        pl.BlockSpec((ts, 2 * CONV_CH), lambda b, s: (row(b, s), 0)),
        pl.BlockSpec((CONV_HALO, 2 * CONV_CH),
                     lambda b, s: (jnp.maximum(row(b, s) * (ts // CONV_HALO) - 1, 0), 0)),
        pl.BlockSpec((ts, Q_DIM), lambda b, s: (row(b, s), 0)),
        pl.BlockSpec((ts, 4 * KV_DIM), lambda b, s: (row(b, s), 0)),
        pl.BlockSpec((ATTN_BLOCK, 4 * KV_DIM),
                     lambda b, s: (jnp.maximum(row(b, s) * (ts // ATTN_BLOCK) - 1, 0), 0)),
        pl.BlockSpec((ts, 2 * D_MODEL), lambda b, s: (row(b, s), 0)),
        full(w_dw), full(b_dw), full(ln_g), full(ln_b),
        pl.BlockSpec(memory_space=pltpu.SMEM),
        full(bias), full(wco), full(wao), full(wo), full(g_ffn), full(wr_hi), full(wr_lo), full(b_r),
    ]
    out_specs = [
        pl.BlockSpec((ts, D_MODEL), lambda b, s: (row(b, s), 0)),
        pl.BlockSpec((ts, D_MODEL), lambda b, s: (row(b, s), 0)),
        pl.BlockSpec((SUBLANES, ts), lambda b, s: (0, row(b, s))),
        pl.BlockSpec((SUBLANES, ts), lambda b, s: (0, row(b, s))),
    ]
    out_shape = [
        jax.ShapeDtypeStruct((t, D_MODEL), f32),
        jax.ShapeDtypeStruct((t, D_MODEL), f32),
        jax.ShapeDtypeStruct((SUBLANES, t), jnp.int32),
        jax.ShapeDtypeStruct((SUBLANES, t), f32),
    ]
    return pl.pallas_call(
        _mixer_kernel,
        grid=(batch, ns),
        in_specs=in_specs,
        out_specs=out_specs,
        out_shape=out_shape,
        scratch_shapes=[
            pltpu.VMEM((ts + CONV_HALO, CONV_CH), f32),
            pltpu.VMEM((ts, CONV_CH), bf16),
            pltpu.VMEM((ts + ATTN_BLOCK, 4 * KV_DIM), bf16),
            pltpu.VMEM((ts, Q_DIM), bf16),
        ],
        compiler_params=pltpu.CompilerParams(
            dimension_semantics=("arbitrary", "arbitrary"), vmem_limit_bytes=VMEM_LIMIT),
        name="mixer",
    )(x2, u, u, q, kv, kv, gates, w_dw, b_dw, ln_g, ln_b, sinks, bias, wco, wao, wo, g_ffn,
      wr_hi, wr_lo, b_r)


def _num_blocks(t):
    return -(-(2 * t + N_EXPERTS * (EXPERT_BLOCK - 1)) // EXPERT_BLOCK)


def _route_kernel(route_ref, slot_ref, meta_ref, rank_scr):
    t = route_ref.shape[1]
    tr = ROUTE_TILE
    nbp = meta_ref.shape[1]
    eiota = lax.broadcasted_iota(jnp.int32, (N_EXPERTS, tr), 0)
    before = (lax.broadcasted_iota(jnp.int32, (tr, tr), 0)
              < lax.broadcasted_iota(jnp.int32, (tr, tr), 1)).astype(bf16)

    def masks(i):
        off = pl.multiple_of(i * tr, tr)
        m1 = eiota == route_ref[0:1, pl.ds(off, tr)]
        m2 = eiota == route_ref[1:2, pl.ds(off, tr)]
        return off, m1, m2

    def count(i, carry):
        off, m1, m2 = masks(i)
        onehot = jnp.where(m1 | m2, 1.0, 0.0)
        seen = jnp.dot(onehot.astype(bf16), before, preferred_element_type=f32) + carry
        rank_scr[0:1, pl.ds(off, tr)] = jnp.sum(jnp.where(m1, seen, 0.0), axis=0, keepdims=True)
        rank_scr[1:2, pl.ds(off, tr)] = jnp.sum(jnp.where(m2, seen, 0.0), axis=0, keepdims=True)
        return carry + jnp.sum(onehot, axis=1, keepdims=True)

    counts = lax.fori_loop(0, t // tr, count, jnp.zeros((N_EXPERTS, 1), f32))

    nblk = jnp.floor((counts + (EXPERT_BLOCK - 1)) * (1.0 / EXPERT_BLOCK))
    nblk_b = jnp.broadcast_to(nblk, (N_EXPERTS, LANES))
    hi = jnp.floor(nblk_b * (1.0 / 16.0))
    lo = nblk_b - 16.0 * hi
    lower = (lax.broadcasted_iota(jnp.int32, (N_EXPERTS, N_EXPERTS), 1)
             < lax.broadcasted_iota(jnp.int32, (N_EXPERTS, N_EXPERTS), 0)).astype(bf16)
    prefix = (16.0 * jnp.dot(lower, hi.astype(bf16), preferred_element_type=f32)
              + jnp.dot(lower, lo.astype(bf16), preferred_element_type=f32))
    pstart = (prefix * EXPERT_BLOCK)[:, 0:1]
    pend = pstart + nblk * EXPERT_BLOCK

    slot_ref[...] = jnp.zeros(slot_ref.shape, jnp.int32)

    def place(i, carry):
        off, m1, m2 = masks(i)
        s1 = jnp.sum(jnp.where(m1, pstart, 0.0), axis=0, keepdims=True) + rank_scr[0:1, pl.ds(off, tr)]
        s2 = jnp.sum(jnp.where(m2, pstart, 0.0), axis=0, keepdims=True) + rank_scr[1:2, pl.ds(off, tr)]
        slot_ref[0:1, pl.ds(off, tr)] = s1.astype(jnp.int32)
        slot_ref[1:2, pl.ds(off, tr)] = s2.astype(jnp.int32)
        return carry

    lax.fori_loop(0, t // tr, place, 0)

    bstart = (lax.broadcasted_iota(jnp.int32, (N_EXPERTS, nbp), 1) * EXPERT_BLOCK).astype(f32)
    bexp = jnp.sum(jnp.where(pend <= bstart, 1.0, 0.0), axis=0, keepdims=True)
    bexp = jnp.minimum(bexp, N_EXPERTS - 1.0)
    nact = jnp.sum(nblk, axis=0, keepdims=True)
    row = lax.broadcasted_iota(jnp.int32, (SUBLANES, nbp), 0)
    meta = jnp.where(row == 0, bexp, jnp.where(row == 1, nact, 0.0))
    meta_ref[...] = meta.astype(jnp.int32)


def _route(route):
    t = route.shape[1]
    nbp = -(-_num_blocks(t) // LANES) * LANES
    return pl.pallas_call(
        _route_kernel,
        grid=(1,),
        in_specs=[pl.BlockSpec(route.shape, lambda i: (0, 0))],
        out_specs=[pl.BlockSpec(route.shape, lambda i: (0, 0)),
                   pl.BlockSpec((SUBLANES, nbp), lambda i: (0, 0))],
        out_shape=[jax.ShapeDtypeStruct(route.shape, jnp.int32),
                   jax.ShapeDtypeStruct((SUBLANES, nbp), jnp.int32)],
        scratch_shapes=[pltpu.VMEM(route.shape, f32)],
        compiler_params=pltpu.CompilerParams(
            dimension_semantics=("arbitrary",), vmem_limit_bytes=VMEM_LIMIT),
        name="route",
    )(route)


def _scatter_kernel(slot_ref, x_ref, xs_in_ref, xs_ref, sem):
    del xs_in_ref
    tm = x_ref.shape[0]

    def copy(tok, slot):
        return pltpu.make_async_copy(x_ref.at[pl.ds(tok, 1)], xs_ref.at[pl.ds(slot, 1)], sem)

    def issue(tok, carry):
        copy(tok, slot_ref[0, tok]).start()
        copy(tok, slot_ref[1, tok]).start()
        return carry

    def drain(tok, carry):
        copy(0, 0).wait()
        copy(0, 0).wait()
        return carry

    lax.fori_loop(0, tm, issue, 0)
    lax.fori_loop(0, tm, drain, 0)


def _scatter(slots, xn2, n_rows):
    t = xn2.shape[0]
    tm = DMA_TILE
    xs0 = jnp.zeros((n_rows, D_MODEL), xn2.dtype)
    return pl.pallas_call(
        _scatter_kernel,
        grid=(t // tm,),
        in_specs=[
            pl.BlockSpec((SUBLANES, tm), lambda i: (0, i), memory_space=pltpu.SMEM),
            pl.BlockSpec((tm, D_MODEL), lambda i: (i, 0)),
            pl.BlockSpec(memory_space=pl.ANY),
        ],
        out_specs=pl.BlockSpec(memory_space=pl.ANY),
        out_shape=jax.ShapeDtypeStruct((n_rows, D_MODEL), xn2.dtype),
        scratch_shapes=[pltpu.SemaphoreType.DMA],
        input_output_aliases={2: 0},
        compiler_params=pltpu.CompilerParams(
            dimension_semantics=("arbitrary",), vmem_limit_bytes=VMEM_LIMIT, has_side_effects=True),
        name="scatter",
    )(slots, xn2, xs0)


def _expert_kernel(bexp_ref, nact_ref, xs_ref, wg_ref, wu_ref, wd_ref, ys_ref):
    del bexp_ref
    active = pl.program_id(0) < nact_ref[0]

    @pl.when(jnp.logical_not(active))
    def _():
        ys_ref[...] = jnp.zeros(ys_ref.shape, ys_ref.dtype)

    @pl.when(active)
    def _():
        x = xs_ref[...].astype(bf16)
        g = jnp.dot(x, wg_ref[0], preferred_element_type=f32)
        u = jnp.dot(x, wu_ref[0], preferred_element_type=f32)
        hmid = (g * _sigmoid(g) * u).astype(bf16)
        ys_ref[...] = jnp.dot(hmid, wd_ref[0], preferred_element_type=f32)


def _experts(bexp, nact, xs, wg, wu, wd):
    n_rows = xs.shape[0]
    nb = n_rows // EXPERT_BLOCK

    def blk(b, bexp_ref, nact_ref):
        return jnp.minimum(b, nact_ref[0] - 1)

    grid_spec = pltpu.PrefetchScalarGridSpec(
        num_scalar_prefetch=2,
        grid=(nb,),
        in_specs=[
            pl.BlockSpec((EXPERT_BLOCK, D_MODEL), lambda b, e, n: (blk(b, e, n), 0)),
            pl.BlockSpec((1, D_MODEL, D_FF_EXPERT), lambda b, e, n: (e[blk(b, e, n)], 0, 0)),
            pl.BlockSpec((1, D_MODEL, D_FF_EXPERT), lambda b, e, n: (e[blk(b, e, n)], 0, 0)),
            pl.BlockSpec((1, D_FF_EXPERT, D_MODEL), lambda b, e, n: (e[blk(b, e, n)], 0, 0)),
        ],
        out_specs=pl.BlockSpec((EXPERT_BLOCK, D_MODEL), lambda b, e, n: (b, 0)),
    )
    return pl.pallas_call(
        _expert_kernel,
        grid_spec=grid_spec,
        out_shape=jax.ShapeDtypeStruct((n_rows, D_MODEL), f32),
        compiler_params=pltpu.CompilerParams(
            dimension_semantics=("arbitrary",), vmem_limit_bytes=VMEM_LIMIT),
        name="experts",
    )(bexp, nact, xs, wg, wu, wd)


def _combine_kernel(slot_ref, h_ref, wts_ref, gf_ref, ys_ref, out_ref, ybuf, sem):
    tm = h_ref.shape[0]

    def copy(k, tok, slot):
        return pltpu.make_async_copy(ys_ref.at[pl.ds(slot, 1)], ybuf.at[k, pl.ds(tok, 1)], sem)

    def issue(tok, carry):
        copy(0, tok, slot_ref[0, tok]).start()
        copy(1, tok, slot_ref[1, tok]).start()
        return carry

    def drain(tok, carry):
        copy(0, 0, 0).wait()
        copy(1, 0, 0).wait()
        return carry

    lax.fori_loop(0, tm, issue, 0)
    lax.fori_loop(0, tm, drain, 0)

    w = jnp.concatenate([wts_ref[...], jnp.zeros((LANES - SUBLANES, tm), f32)], axis=0).T
    moe = ybuf[0] * w[:, 0:1] + ybuf[1] * w[:, 1:2]
    out_ref[...] = _rms(h_ref[...] + moe, gf_ref[...])


def _combine(slots, h, wts, g_final, ys):
    t = h.shape[0]
    tm = DMA_TILE
    return pl.pallas_call(
        _combine_kernel,
        grid=(t // tm,),
        in_specs=[
            pl.BlockSpec((SUBLANES, tm), lambda i: (0, i), memory_space=pltpu.SMEM),
            pl.BlockSpec((tm, D_MODEL), lambda i: (i, 0)),
            pl.BlockSpec((SUBLANES, tm), lambda i: (0, i)),
            pl.BlockSpec((1, D_MODEL), lambda i: (0, 0)),
            pl.BlockSpec(memory_space=pl.ANY),
        ],
        out_specs=pl.BlockSpec((tm, D_MODEL), lambda i: (i, 0)),
        out_shape=jax.ShapeDtypeStruct((t, D_MODEL), f32),
        scratch_shapes=[pltpu.VMEM((2, tm, D_MODEL), f32), pltpu.SemaphoreType.DMA],
        compiler_params=pltpu.CompilerParams(
            dimension_semantics=("arbitrary",), vmem_limit_bytes=VMEM_LIMIT),
        name="combine",
    )(slots, h, wts, g_final, ys)


def _prep_w_in(w_in):
    c0 = 2 * CONV_CH
    c1 = c0 + Q_DIM
    c2 = c1 + KV_DIM
    c3 = c2 + KV_DIM
    wk = w_in[:, c1:c2]
    wv = w_in[:, c2:c3]

    def dup(w):
        return jnp.concatenate([w[:, h * HEAD_DIM:(h + 1) * HEAD_DIM] for h in range(N_KV_HEADS)
                                for _ in range(2)], axis=1)

    return jnp.concatenate(
        [w_in[:, :c0], w_in[:, c0:c1] * (HEAD_DIM ** -0.5), dup(wk), dup(wv), w_in[:, c3:]],
        axis=1).astype(bf16)


def _prep_router(w_group, b_group, w_expert, b_expert):
    d = w_group.shape[0]
    w = jnp.zeros((d, LANES), f32)
    w = w.at[:, :N_GROUPS].set(w_group).at[:, SUBLANES:ROUTER_ROWS].set(w_expert)
    b = jnp.full((1, LANES), NEG_BIG, f32)
    b = b.at[0, :N_GROUPS].set(b_group).at[0, SUBLANES:ROUTER_ROWS].set(b_expert)
    w_hi = w.astype(bf16)
    w_lo = (w - w_hi.astype(f32)).astype(bf16)
    return w_hi, w_lo, b


def kernel(x, g_mix, w_in, w_dw, b_dw, ln_conv_g, ln_conv_b, sinks, w_conv_out, w_attn_out, w_out, g_ffn,
           w_group, b_group, w_expert, b_expert, w_gate, w_up, w_down, g_final):
    batch, seq, d = x.shape
    assert d == D_MODEL and seq % TOKEN_TILE == 0 and g_mix.shape[0] == 1
    t = batch * seq
    x2 = x.reshape(t, d)

    u, q, kv, gates = _inproj(x2, g_mix[0][None, :], _prep_w_in(w_in[0]))

    wr_hi, wr_lo, b_r = _prep_router(w_group[0], b_group[0], w_expert[0], b_expert[0])
    h, xn2, route, wts = _mixer(
        x2, u, q, kv, gates, w_dw[0, :, 0, :], b_dw[0][None, :], ln_conv_g[0][None, :],
        ln_conv_b[0][None, :], sinks[0], w_conv_out[0].astype(bf16), w_attn_out[0].astype(bf16),
        w_out[0].astype(bf16), g_ffn[0][None, :], wr_hi, wr_lo, b_r, batch, seq)

    slots, meta = _route(route)
    nb = _num_blocks(t)
    xs = _scatter(slots, xn2, nb * EXPERT_BLOCK)
    ys = _experts(meta[0, :nb], meta[1, :1], xs, w_gate[0].astype(bf16), w_up[0].astype(bf16),
                  w_down[0].astype(bf16))
    out = _combine(slots, h, wts, g_final[None, :], ys)
    return out.reshape(batch, seq, d)
```

```python
import numpy as np
import jax
import jax.numpy as jnp
from jax import lax
from jax.experimental import pallas as pl
from jax.experimental.pallas import tpu as pltpu

D_MODEL = 1024
CONV_CH = 512
CONV_WIDTH = 31
N_HEADS = 8
N_KV_HEADS = 2
HEAD_DIM = 64
ATTN_BLOCK = 128
N_GROUPS = 4
EXPERTS_PER_GROUP = 8
N_EXPERTS = N_GROUPS * EXPERTS_PER_GROUP
D_FF_EXPERT = 512
NORM_EPS = 1e-6

Q_DIM = N_HEADS * HEAD_DIM
KV_DIM = N_KV_HEADS * HEAD_DIM

LANES = 128
SUBLANES = 8
CONV_HALO = 32
ROUTER_ROWS = SUBLANES + N_EXPERTS
NEG_BIG = -1e30

TOKEN_TILE = 512
ROUTE_TILE = 512
RUN_ALIGN = SUBLANES
EXPERT_BLOCK = 256
VMEM_LIMIT = 56 * 1024 * 1024

f32 = jnp.float32
bf16 = jnp.bfloat16


def _rms(x, g):
    ms = jnp.mean(x * x, axis=-1, keepdims=True)
    return x * lax.rsqrt(ms + NORM_EPS) * g


def _sigmoid(x):
    return 1.0 / (1.0 + jnp.exp(-x))


def _inproj_kernel(x_ref, g_ref, w_ref, u_ref, q_ref, kv_ref, gate_ref):
    xn = _rms(x_ref[...], g_ref[...]).astype(bf16)
    col = 0
    for out_ref in (u_ref, q_ref, kv_ref, gate_ref):
        width = out_ref.shape[1]
        for lo in range(0, width, 1024):
            hi = min(lo + 1024, width)
            out_ref[:, lo:hi] = jnp.dot(
                xn, w_ref[:, col + lo:col + hi], preferred_element_type=f32).astype(out_ref.dtype)
        col += width


def _inproj(x2, g_mix, w_cat):
    t = x2.shape[0]
    tm = TOKEN_TILE
    widths = (2 * CONV_CH, Q_DIM, 4 * KV_DIM, 2 * D_MODEL)
    return pl.pallas_call(
        _inproj_kernel,
        grid=(t // tm,),
        in_specs=[
            pl.BlockSpec((tm, D_MODEL), lambda i: (i, 0)),
            pl.BlockSpec((1, D_MODEL), lambda i: (0, 0)),
            pl.BlockSpec(w_cat.shape, lambda i: (0, 0)),
        ],
        out_specs=[pl.BlockSpec((tm, w), lambda i: (i, 0)) for w in widths],
        out_shape=[jax.ShapeDtypeStruct((t, w), bf16) for w in widths],
        compiler_params=pltpu.CompilerParams(
            dimension_semantics=("arbitrary",), vmem_limit_bytes=VMEM_LIMIT),
        name="inproj",
    )(x2, g_mix, w_cat)


def _mixer_kernel(x_ref, u_ref, up_ref, q_ref, kv_ref, kvp_ref, gate_ref, wdw_ref, bdw_ref,
                  lng_ref, lnb_ref, sink_ref, bias_ref, wco_ref, wao_ref, wo_ref, gffn_ref,
                  wrh_ref, wrl_ref, br_ref,
                  h_ref, xn2_ref, route_ref, wts_ref,
                  vbuf, act, kvall, attn):
    ts = x_ref.shape[0]
    first = pl.program_id(1) == 0

    def glu(ref):
        return ref[:, :CONV_CH].astype(f32) * _sigmoid(ref[:, CONV_CH:].astype(f32))

    vbuf[0:CONV_HALO, :] = jnp.where(first, 0.0, glu(up_ref))
    vbuf[CONV_HALO:, :] = glu(u_ref)

    rows = 32
    tap0 = CONV_HALO - (CONV_WIDTH - 1)

    def conv_chunk(c, carry):
        r0 = pl.multiple_of(c * rows, rows)
        acc = jnp.broadcast_to(bdw_ref[...], (rows, CONV_CH))
        win = vbuf[pl.ds(r0, rows + CONV_HALO), :]
        for r in range(SUBLANES):
            n_a = (CONV_WIDTH - 1 - r) // SUBLANES + 1
            shifted = win[tap0 + r:tap0 + r + SUBLANES * (n_a - 1) + rows]
            for a in range(n_a):
                j = SUBLANES * a + r
                acc = acc + wdw_ref[j:j + 1, :] * shifted[SUBLANES * a:SUBLANES * a + rows]
        mu = jnp.mean(acc, axis=-1, keepdims=True)
        d = acc - mu
        var = jnp.mean(d * d, axis=-1, keepdims=True)
        y = d * lax.rsqrt(var + NORM_EPS) * lng_ref[...] + lnb_ref[...]
        act[pl.ds(r0, rows), :] = (y * _sigmoid(y)).astype(bf16)
        return carry

    lax.fori_loop(0, ts // rows, conv_chunk, 0)

    kvall[0:ATTN_BLOCK, :] = jnp.where(first, jnp.zeros_like(kvp_ref[...]), kvp_ref[...])
    kvall[ATTN_BLOCK:, :] = kv_ref[...]
    nkeys = 2 * ATTN_BLOCK
    left_kv = lax.broadcasted_iota(jnp.int32, (nkeys, LANES), 1) < HEAD_DIM
    left_q = lax.broadcasted_iota(jnp.int32, (ATTN_BLOCK, LANES), 1) < HEAD_DIM
    key_lane = lax.broadcasted_iota(jnp.int32, (ATTN_BLOCK, 2 * nkeys), 1)
    prev_keys = (key_lane % nkeys) < ATTN_BLOCK
    no_prev = jnp.where(prev_keys, jnp.where(first, -jnp.inf, 0.0), 0.0)

    def block_diag(x):
        z = jnp.zeros_like(x)
        return jnp.concatenate([jnp.where(left_kv, x, z), jnp.where(left_kv, z, x)], axis=0)

    def softmax_half(s, sink):
        m = jnp.maximum(jnp.max(s, axis=-1, keepdims=True), sink)
        p = jnp.exp(s - m)
        return p, jnp.sum(p, axis=-1, keepdims=True) + jnp.exp(sink - m)

    for j in range(ts // ATTN_BLOCK):
        r0 = j * ATTN_BLOCK
        for kvh in range(N_KV_HEADS):
            kbd = block_diag(kvall[r0:r0 + nkeys, kvh * LANES:(kvh + 1) * LANES])
            vbd = block_diag(kvall[r0:r0 + nkeys, (N_KV_HEADS + kvh) * LANES:(N_KV_HEADS + kvh + 1) * LANES])
            for ii in range(2):
                i = 2 * kvh + ii
                qb = q_ref[r0:r0 + ATTN_BLOCK, i * LANES:(i + 1) * LANES]
                s = lax.dot_general(qb, kbd, (((1,), (1,)), ((), ())), preferred_element_type=f32)
                s = s + bias_ref[i]
                if j == 0:
                    s = s + no_prev
                pa, la = softmax_half(s[:, :nkeys], sink_ref[2 * i])
                pb, lb = softmax_half(s[:, nkeys:], sink_ref[2 * i + 1])
                p = jnp.concatenate([pa, pb], axis=1).astype(bf16)
                o = jnp.dot(p, vbd, preferred_element_type=f32)
                o = o * jnp.where(left_q, 1.0 / la, 1.0 / lb)
                attn[r0:r0 + ATTN_BLOCK, i * LANES:(i + 1) * LANES] = o.astype(bf16)

    conv_o = jnp.dot(act[...], wco_ref[...], preferred_element_type=f32)
    attn_o = jnp.dot(attn[...], wao_ref[...], preferred_element_type=f32)
    merged = (_sigmoid(gate_ref[:, :D_MODEL].astype(f32)) * conv_o
              + _sigmoid(gate_ref[:, D_MODEL:].astype(f32)) * attn_o).astype(bf16)
    h = x_ref[...] + jnp.dot(merged, wo_ref[...], preferred_element_type=f32)
    h_ref[...] = h

    xn2 = _rms(h, gffn_ref[...])
    xn2_ref[...] = xn2.astype(bf16)
    xh = xn2.astype(bf16)
    xl = (xn2 - xh.astype(f32)).astype(bf16)
    logits = (jnp.dot(xh, wrh_ref[...], preferred_element_type=f32)
              + jnp.dot(xh, wrl_ref[...], preferred_element_type=f32)
              + jnp.dot(xl, wrh_ref[...], preferred_element_type=f32)) + br_ref[...]
    lt = logits.T
    sub = lax.broadcasted_iota(jnp.int32, (SUBLANES, ts), 0)

    gl = lt[0:SUBLANES]
    gmax = jnp.max(gl, axis=0, keepdims=True)
    gsel = jnp.min(jnp.where(gl == gmax, sub, SUBLANES), axis=0, keepdims=True)
    p_group = 1.0 / jnp.sum(jnp.exp(gl - gmax), axis=0, keepdims=True)

    e_in = lt[SUBLANES:2 * SUBLANES]
    for g in range(1, N_GROUPS):
        e_in = jnp.where(gsel == g, lt[(g + 1) * SUBLANES:(g + 2) * SUBLANES], e_in)
    m1 = jnp.max(e_in, axis=0, keepdims=True)
    i1 = jnp.min(jnp.where(e_in == m1, sub, SUBLANES), axis=0, keepdims=True)
    rest = jnp.where(sub == i1, -jnp.inf, e_in)
    m2 = jnp.max(rest, axis=0, keepdims=True)
    i2 = jnp.min(jnp.where(rest == m2, sub, SUBLANES), axis=0, keepdims=True)
    t2 = jnp.exp(m2 - m1)
    w1 = p_group / (1.0 + t2)
    w2 = p_group * t2 / (1.0 + t2)
    base = gsel * EXPERTS_PER_GROUP
    route_ref[...] = jnp.where(sub == 0, base + i1, jnp.where(sub == 1, base + i2, 0))
    wts_ref[...] = jnp.where(sub == 0, w1, jnp.where(sub == 1, w2, 0.0))


def _attn_bias():
    qi = np.arange(ATTN_BLOCK)[:, None]
    kj = np.arange(2 * ATTN_BLOCK)[None, :]
    rel = (ATTN_BLOCK + qi - kj).astype(np.float32)
    ok = (rel >= 0) & (rel < ATTN_BLOCK)
    slopes = np.array([2.0 ** (-8.0 * (h + 1) / N_HEADS) for h in range(N_HEADS)], np.float32)
    per_head = [np.where(ok, -(slopes[h] * rel), -np.inf).astype(np.float32) for h in range(N_HEADS)]
    return np.stack([np.concatenate([per_head[2 * i], per_head[2 * i + 1]], axis=1)
                     for i in range(N_HEADS // 2)])


def _mixer(x2, u, q, kv, gates, w_dw, b_dw, ln_g, ln_b, sinks, wco, wao, wo, g_ffn, wr_hi, wr_lo, b_r,
           batch, seq):
    t = x2.shape[0]
    ts = TOKEN_TILE
    ns = seq // ts
    bias = jnp.asarray(_attn_bias())

    def row(b, s):
        return b * ns + s

    def full(a):
        return pl.BlockSpec(a.shape, lambda b, s: (0,) * a.ndim)

    in_specs = [
        pl.BlockSpec((ts, D_MODEL), lambda b, s: (row(b, s), 0)),
        pl.BlockSpec((ts, 2 * CONV_CH), lambda b, s: (row(b, s), 0)),
        pl.BlockSpec((CONV_HALO, 2 * CONV_CH),
                     lambda b, s: (jnp.maximum(row(b, s) * (ts // CONV_HALO) - 1, 0), 0)),
        pl.BlockSpec((ts, Q_DIM), lambda b, s: (row(b, s), 0)),
        pl.BlockSpec((ts, 4 * KV_DIM), lambda b, s: (row(b, s), 0)),
        pl.BlockSpec((ATTN_BLOCK, 4 * KV_DIM),
                     lambda b, s: (jnp.maximum(row(b, s) * (ts // ATTN_BLOCK) - 1, 0), 0)),
        pl.BlockSpec((ts, 2 * D_MODEL), lambda b, s: (row(b, s), 0)),
        full(w_dw), full(b_dw), full(ln_g), full(ln_b),
        pl.BlockSpec(memory_space=pltpu.SMEM),
        full(bias), full(wco), full(wao), full(wo), full(g_ffn), full(wr_hi), full(wr_lo), full(b_r),
    ]
    out_specs = [
        pl.BlockSpec((ts, D_MODEL), lambda b, s: (row(b, s), 0)),
        pl.BlockSpec((ts, D_MODEL), lambda b, s: (row(b, s), 0)),
        pl.BlockSpec((SUBLANES, ts), lambda b, s: (0, row(b, s))),
        pl.BlockSpec((SUBLANES, ts), lambda b, s: (0, row(b, s))),
    ]
    out_shape = [
        jax.ShapeDtypeStruct((t, D_MODEL), f32),
        jax.ShapeDtypeStruct((t, D_MODEL), bf16),
        jax.ShapeDtypeStruct((SUBLANES, t), jnp.int32),
        jax.ShapeDtypeStruct((SUBLANES, t), f32),
    ]
    return pl.pallas_call(
        _mixer_kernel,
        grid=(batch, ns),
        in_specs=in_specs,
        out_specs=out_specs,
        out_shape=out_shape,
        scratch_shapes=[
            pltpu.VMEM((ts + CONV_HALO, CONV_CH), f32),
            pltpu.VMEM((ts, CONV_CH), bf16),
            pltpu.VMEM((ts + ATTN_BLOCK, 4 * KV_DIM), bf16),
            pltpu.VMEM((ts, Q_DIM), bf16),
        ],
        compiler_params=pltpu.CompilerParams(
            dimension_semantics=("arbitrary", "arbitrary"), vmem_limit_bytes=VMEM_LIMIT),
        name="mixer",
    )(x2, u, u, q, kv, kv, gates, w_dw, b_dw, ln_g, ln_b, sinks, bias, wco, wao, wo, g_ffn,
      wr_hi, wr_lo, b_r)


def _local_rows(ts):
    return -(-(2 * ts + (RUN_ALIGN - 1) * N_EXPERTS) // LANES) * LANES


def _num_blocks(t):
    run_rows = 2 * t + (RUN_ALIGN - 1) * N_EXPERTS * (t // ROUTE_TILE)
    return -(-(run_rows + N_EXPERTS * (EXPERT_BLOCK - RUN_ALIGN)) // EXPERT_BLOCK)


def _route_kernel(route_ref, lpos_ref, tab_ref, seg_ref, meta_ref):
    t = route_ref.shape[1]
    tr = ROUTE_TILE
    nbp = meta_ref.shape[1]
    chunks_per_block = EXPERT_BLOCK // RUN_ALIGN
    eiota = lax.broadcasted_iota(jnp.int32, (N_EXPERTS, tr), 0)
    before = (lax.broadcasted_iota(jnp.int32, (tr, tr), 0)
              < lax.broadcasted_iota(jnp.int32, (tr, tr), 1)).astype(bf16)
    lower = (lax.broadcasted_iota(jnp.int32, (N_EXPERTS, N_EXPERTS), 1)
             < lax.broadcasted_iota(jnp.int32, (N_EXPERTS, N_EXPERTS), 0)).astype(bf16)
    sub = lax.broadcasted_iota(jnp.int32, (N_EXPERTS, LANES), 0)
    lane = lax.broadcasted_iota(jnp.int32, (N_EXPERTS, LANES), 1)

    def to_lanes(col, offset):
        return jnp.sum(jnp.where(sub + offset == lane, col, 0.0), axis=0, keepdims=True)

    def expert_prefix(col):
        b = jnp.broadcast_to(col, (N_EXPERTS, LANES))
        hi = jnp.floor(b * (1.0 / 16.0))
        lo = b - 16.0 * hi
        return (16.0 * jnp.dot(lower, hi.astype(bf16), preferred_element_type=f32)
                + jnp.dot(lower, lo.astype(bf16), preferred_element_type=f32))[:, 0:1]

    lpos_ref[...] = jnp.zeros(lpos_ref.shape, jnp.int32)

    def step(i, seen_chunks):
        off = pl.multiple_of(i * tr, tr)
        m1 = eiota == route_ref[0:1, pl.ds(off, tr)]
        m2 = eiota == route_ref[1:2, pl.ds(off, tr)]
        onehot = jnp.where(m1 | m2, 1.0, 0.0)
        within = jnp.dot(onehot.astype(bf16), before, preferred_element_type=f32)
        run_chunks = jnp.floor((jnp.sum(onehot, axis=1, keepdims=True) + (RUN_ALIGN - 1)) * (1.0 / RUN_ALIGN))
        run_start = expert_prefix(run_chunks)
        pos = within + RUN_ALIGN * run_start
        lpos_ref[0:1, pl.ds(off, tr)] = jnp.sum(jnp.where(m1, pos, 0.0), axis=0, keepdims=True).astype(jnp.int32)
        lpos_ref[1:2, pl.ds(off, tr)] = jnp.sum(jnp.where(m2, pos, 0.0), axis=0, keepdims=True).astype(jnp.int32)
        row = to_lanes(seen_chunks, 0) + to_lanes(run_chunks, N_EXPERTS) + to_lanes(run_start, 2 * N_EXPERTS)
        tab_ref[i] = jnp.broadcast_to(row, (SUBLANES, LANES)).astype(jnp.int32)
        return seen_chunks + run_chunks

    used_chunks = lax.fori_loop(0, t // tr, step, jnp.zeros((N_EXPERTS, 1), f32))

    nblk = jnp.floor((used_chunks + (chunks_per_block - 1)) * (1.0 / chunks_per_block))
    first_blk = expert_prefix(nblk)
    nact = jnp.sum(nblk, axis=0, keepdims=True)
    seg_row = (to_lanes(first_blk * chunks_per_block, 0) + to_lanes(used_chunks, N_EXPERTS)
               + to_lanes(nblk * chunks_per_block, 2 * N_EXPERTS)
               + jnp.where(lane[0:1] == 3 * N_EXPERTS, nact, 0.0))
    seg_ref[...] = jnp.broadcast_to(seg_row, (SUBLANES, LANES)).astype(jnp.int32)

    blk = lax.broadcasted_iota(jnp.int32, (N_EXPERTS, nbp), 1).astype(f32)
    owner = (first_blk <= blk) & (blk < first_blk + nblk)
    expert_id = lax.broadcasted_iota(jnp.int32, (N_EXPERTS, nbp), 0).astype(f32)
    bexp = jnp.sum(jnp.where(owner, expert_id, 0.0), axis=0, keepdims=True)
    row8 = lax.broadcasted_iota(jnp.int32, (SUBLANES, nbp), 0)
    meta_ref[...] = jnp.where(row8 == 0, bexp, jnp.where(row8 == 1, nact, 0.0)).astype(jnp.int32)


def _route(route):
    t = route.shape[1]
    nt = t // ROUTE_TILE
    nbp = -(-_num_blocks(t) // LANES) * LANES
    return pl.pallas_call(
        _route_kernel,
        grid=(1,),
        in_specs=[pl.BlockSpec(route.shape, lambda i: (0, 0))],
        out_specs=[pl.BlockSpec(route.shape, lambda i: (0, 0)),
                   pl.BlockSpec((nt, SUBLANES, LANES), lambda i: (0, 0, 0)),
                   pl.BlockSpec((SUBLANES, LANES), lambda i: (0, 0)),
                   pl.BlockSpec((SUBLANES, nbp), lambda i: (0, 0))],
        out_shape=[jax.ShapeDtypeStruct(route.shape, jnp.int32),
                   jax.ShapeDtypeStruct((nt, SUBLANES, LANES), jnp.int32),
                   jax.ShapeDtypeStruct((SUBLANES, LANES), jnp.int32),
                   jax.ShapeDtypeStruct((SUBLANES, nbp), jnp.int32)],
        compiler_params=pltpu.CompilerParams(
            dimension_semantics=("arbitrary",), vmem_limit_bytes=VMEM_LIMIT),
        name="route",
    )(route)


def _for_each_run_chunk(tab_ref, seg_ref, fn):
    def run(e, carry):
        g0 = seg_ref[0, e] + tab_ref[0, 0, e]
        n = tab_ref[0, 0, N_EXPERTS + e]
        l0 = tab_ref[0, 0, 2 * N_EXPERTS + e]

        def chunk(c, carry2):
            fn(pl.multiple_of((l0 + c) * RUN_ALIGN, RUN_ALIGN), pl.multiple_of((g0 + c) * RUN_ALIGN, RUN_ALIGN))
            return carry2

        lax.fori_loop(0, n, chunk, 0)
        return carry

    lax.fori_loop(0, N_EXPERTS, run, 0)


def _scatter_kernel(tab_ref, seg_ref, lpos_ref, x_ref, xs_ref, xsl, zrows, sem, zsem):
    n_blocks = xs_ref.shape[0] // EXPERT_BLOCK
    lrows, ts = xsl.shape[0], x_ref.shape[0]
    j = lax.broadcasted_iota(jnp.int32, (lrows, ts), 0)
    perm = jnp.where((j == lpos_ref[0:1, :]) | (j == lpos_ref[1:2, :]), 1.0, 0.0).astype(bf16)
    xsl[...] = jnp.dot(perm, x_ref[...], preferred_element_type=f32)

    def copy(l, g):
        return pltpu.make_async_copy(xsl.at[pl.ds(l, RUN_ALIGN)], xs_ref.at[pl.ds(g, RUN_ALIGN)], sem)

    _for_each_run_chunk(tab_ref, seg_ref, lambda l, g: copy(l, g).start())
    _for_each_run_chunk(tab_ref, seg_ref, lambda l, g: copy(l, g).wait())

    @pl.when(pl.program_id(0) == pl.num_programs(0) - 1)
    def _():
        zrows[...] = jnp.zeros(zrows.shape, zrows.dtype)

        def zcopy(g):
            return pltpu.make_async_copy(zrows.at[pl.ds(0, RUN_ALIGN)], xs_ref.at[pl.ds(g, RUN_ALIGN)], zsem)

        def zblock(b):
            return pltpu.make_async_copy(
                zrows, xs_ref.at[pl.ds(pl.multiple_of(b * EXPERT_BLOCK, EXPERT_BLOCK), EXPERT_BLOCK)], zsem)

        def for_each_unused_block(fn):
            def body(b, carry):
                fn(b)
                return carry

            lax.fori_loop(seg_ref[0, 3 * N_EXPERTS], n_blocks, body, 0)

        for_each_unused_block(lambda b: zblock(b).start())
        for_each_unused_block(lambda b: zblock(b).wait())

        def for_each_pad_chunk(fn):
            def seg(e, carry):
                g0 = seg_ref[0, e]
                used = seg_ref[0, N_EXPERTS + e]
                total = seg_ref[0, 2 * N_EXPERTS + e]

                def chunk(c, carry2):
                    fn(pl.multiple_of((g0 + c) * RUN_ALIGN, RUN_ALIGN))
                    return carry2

                lax.fori_loop(used, total, chunk, 0)
                return carry

            lax.fori_loop(0, N_EXPERTS, seg, 0)

        for_each_pad_chunk(lambda g: zcopy(g).start())
        for_each_pad_chunk(lambda g: zcopy(g).wait())


def _scatter(tab, seg, lpos, xn2, n_rows):
    t = xn2.shape[0]
    ts = ROUTE_TILE
    return pl.pallas_call(
        _scatter_kernel,
        grid=(t // ts,),
        in_specs=[
            pl.BlockSpec((1, SUBLANES, LANES), lambda i: (i, 0, 0), memory_space=pltpu.SMEM),
            pl.BlockSpec((SUBLANES, LANES), lambda i: (0, 0), memory_space=pltpu.SMEM),
            pl.BlockSpec((SUBLANES, ts), lambda i: (0, i)),
            pl.BlockSpec((ts, D_MODEL), lambda i: (i, 0)),
        ],
        out_specs=pl.BlockSpec(memory_space=pl.ANY),
        out_shape=jax.ShapeDtypeStruct((n_rows, D_MODEL), f32),
        scratch_shapes=[pltpu.VMEM((_local_rows(ts), D_MODEL), f32), pltpu.VMEM((EXPERT_BLOCK, D_MODEL), f32),
                        pltpu.SemaphoreType.DMA, pltpu.SemaphoreType.DMA],
        compiler_params=pltpu.CompilerParams(
            dimension_semantics=("arbitrary",), vmem_limit_bytes=VMEM_LIMIT, has_side_effects=True),
        name="scatter",
    )(tab, seg, lpos, xn2)


def _expert_kernel(bexp_ref, nact_ref, xs_ref, wg_ref, wu_ref, wd_ref, ys_ref):
    del bexp_ref
    active = pl.program_id(0) < nact_ref[0]

    @pl.when(jnp.logical_not(active))
    def _():
        ys_ref[...] = jnp.zeros(ys_ref.shape, ys_ref.dtype)

    @pl.when(active)
    def _():
        x = xs_ref[...].astype(bf16)
        g = jnp.dot(x, wg_ref[0], preferred_element_type=f32)
        u = jnp.dot(x, wu_ref[0], preferred_element_type=f32)
        hmid = (g * _sigmoid(g) * u).astype(bf16)
        ys_ref[...] = jnp.dot(hmid, wd_ref[0], preferred_element_type=f32)


def _experts(bexp, nact, xs, wg, wu, wd):
    n_rows = xs.shape[0]
    nb = n_rows // EXPERT_BLOCK

    def blk(b, bexp_ref, nact_ref):
        return jnp.minimum(b, nact_ref[0] - 1)

    grid_spec = pltpu.PrefetchScalarGridSpec(
        num_scalar_prefetch=2,
        grid=(nb,),
        in_specs=[
            pl.BlockSpec((EXPERT_BLOCK, D_MODEL), lambda b, e, n: (blk(b, e, n), 0)),
            pl.BlockSpec((1, D_MODEL, D_FF_EXPERT), lambda b, e, n: (e[blk(b, e, n)], 0, 0)),
            pl.BlockSpec((1, D_MODEL, D_FF_EXPERT), lambda b, e, n: (e[blk(b, e, n)], 0, 0)),
            pl.BlockSpec((1, D_FF_EXPERT, D_MODEL), lambda b, e, n: (e[blk(b, e, n)], 0, 0)),
        ],
        out_specs=pl.BlockSpec((EXPERT_BLOCK, D_MODEL), lambda b, e, n: (b, 0)),
    )
    return pl.pallas_call(
        _expert_kernel,
        grid_spec=grid_spec,
        out_shape=jax.ShapeDtypeStruct((n_rows, D_MODEL), f32),
        compiler_params=pltpu.CompilerParams(
            dimension_semantics=("arbitrary",), vmem_limit_bytes=VMEM_LIMIT),
        name="experts",
    )(bexp, nact, xs, wg, wu, wd)


def _combine_kernel(tab_ref, seg_ref, lpos_ref, wts_ref, h_ref, gf_ref, ys_ref, out_ref, ybuf, sem):
    ts, lrows = h_ref.shape[0], ybuf.shape[0]
    ybuf[2 * ts:, :] = jnp.zeros((lrows - 2 * ts, D_MODEL), f32)

    def copy(l, g):
        return pltpu.make_async_copy(ys_ref.at[pl.ds(g, RUN_ALIGN)], ybuf.at[pl.ds(l, RUN_ALIGN)], sem)

    _for_each_run_chunk(tab_ref, seg_ref, lambda l, g: copy(l, g).start())
    _for_each_run_chunk(tab_ref, seg_ref, lambda l, g: copy(l, g).wait())

    info = jnp.concatenate([lpos_ref[...].astype(f32), wts_ref[...],
                            jnp.zeros((LANES - 2 * SUBLANES, ts), f32)], axis=0).T
    jl = lax.broadcasted_iota(jnp.int32, (ts, lrows), 1).astype(f32)
    mix = (jnp.where(jl == info[:, 0:1], info[:, SUBLANES:SUBLANES + 1], 0.0)
           + jnp.where(jl == info[:, 1:2], info[:, SUBLANES + 1:SUBLANES + 2], 0.0)).astype(bf16)
    moe = jnp.dot(mix, ybuf[...].astype(bf16), preferred_element_type=f32)
    out_ref[...] = _rms(h_ref[...] + moe, gf_ref[...])


def _combine(tab, seg, lpos, wts, h, g_final, ys):
    t = h.shape[0]
    ts = ROUTE_TILE
    return pl.pallas_call(
        _combine_kernel,
        grid=(t // ts,),
        in_specs=[
            pl.BlockSpec((1, SUBLANES, LANES), lambda i: (i, 0, 0), memory_space=pltpu.SMEM),
            pl.BlockSpec((SUBLANES, LANES), lambda i: (0, 0), memory_space=pltpu.SMEM),
            pl.BlockSpec((SUBLANES, ts), lambda i: (0, i)),
            pl.BlockSpec((SUBLANES, ts), lambda i: (0, i)),
            pl.BlockSpec((ts, D_MODEL), lambda i: (i, 0)),
            pl.BlockSpec((1, D_MODEL), lambda i: (0, 0)),
            pl.BlockSpec(memory_space=pl.ANY),
        ],
        out_specs=pl.BlockSpec((ts, D_MODEL), lambda i: (i, 0)),
        out_shape=jax.ShapeDtypeStruct((t, D_MODEL), f32),
        scratch_shapes=[pltpu.VMEM((_local_rows(ts), D_MODEL), f32), pltpu.SemaphoreType.DMA],
        compiler_params=pltpu.CompilerParams(
            dimension_semantics=("arbitrary",), vmem_limit_bytes=VMEM_LIMIT),
        name="combine",
    )(tab, seg, lpos, wts, h, g_final, ys)


def _prep_w_in(w_in):
    c0 = 2 * CONV_CH
    c1 = c0 + Q_DIM
    c2 = c1 + KV_DIM
    c3 = c2 + KV_DIM
    wk = w_in[:, c1:c2]
    wv = w_in[:, c2:c3]

    def dup(w):
        return jnp.concatenate([w[:, h * HEAD_DIM:(h + 1) * HEAD_DIM] for h in range(N_KV_HEADS)
                                for _ in range(2)], axis=1)

    return jnp.concatenate(
        [w_in[:, :c0], w_in[:, c0:c1] * (HEAD_DIM ** -0.5), dup(wk), dup(wv), w_in[:, c3:]],
        axis=1).astype(bf16)


def _prep_router(w_group, b_group, w_expert, b_expert):
    d = w_group.shape[0]
    w = jnp.zeros((d, LANES), f32)
    w = w.at[:, :N_GROUPS].set(w_group).at[:, SUBLANES:ROUTER_ROWS].set(w_expert)
    b = jnp.full((1, LANES), NEG_BIG, f32)
    b = b.at[0, :N_GROUPS].set(b_group).at[0, SUBLANES:ROUTER_ROWS].set(b_expert)
    w_hi = w.astype(bf16)
    w_lo = (w - w_hi.astype(f32)).astype(bf16)
    return w_hi, w_lo, b


def kernel(x, g_mix, w_in, w_dw, b_dw, ln_conv_g, ln_conv_b, sinks, w_conv_out, w_attn_out, w_out, g_ffn,
           w_group, b_group, w_expert, b_expert, w_gate, w_up, w_down, g_final):
    batch, seq, d = x.shape
    assert d == D_MODEL and seq % TOKEN_TILE == 0 and g_mix.shape[0] == 1
    t = batch * seq
    x2 = x.reshape(t, d)

    u, q, kv, gates = _inproj(x2, g_mix[0][None, :], _prep_w_in(w_in[0]))

    wr_hi, wr_lo, b_r = _prep_router(w_group[0], b_group[0], w_expert[0], b_expert[0])
    h, xn2, route, wts = _mixer(
        x2, u, q, kv, gates, w_dw[0, :, 0, :], b_dw[0][None, :], ln_conv_g[0][None, :],
        ln_conv_b[0][None, :], sinks[0], w_conv_out[0].astype(bf16), w_attn_out[0].astype(bf16),
        w_out[0].astype(bf16), g_ffn[0][None, :], wr_hi, wr_lo, b_r, batch, seq)

    lpos, tab, seg, meta = _route(route)
    nb = _num_blocks(t)
    xs = _scatter(tab, seg, lpos, xn2, nb * EXPERT_BLOCK)
    ys = _experts(meta[0, :nb], meta[1, :1], xs, w_gate[0].astype(bf16), w_up[0].astype(bf16),
                  w_down[0].astype(bf16))
    out = _combine(tab, seg, lpos, wts, h, g_final[None, :], ys)
    return out.reshape(batch, seq, d)
```

```python
import numpy as np
import jax
import jax.numpy as jnp
from jax import lax
from jax.experimental import pallas as pl
from jax.experimental.pallas import tpu as pltpu

D_MODEL = 1024
CONV_CH = 512
CONV_WIDTH = 31
N_HEADS = 8
N_KV_HEADS = 2
HEAD_DIM = 64
ATTN_BLOCK = 128
N_GROUPS = 4
EXPERTS_PER_GROUP = 8
N_EXPERTS = N_GROUPS * EXPERTS_PER_GROUP
D_FF_EXPERT = 512
NORM_EPS = 1e-6

Q_DIM = N_HEADS * HEAD_DIM
KV_DIM = N_KV_HEADS * HEAD_DIM

LANES = 128
SUBLANES = 8
CONV_HALO = 32
CONV_ROWS = 32
ROUTER_ROWS = SUBLANES + N_EXPERTS
NEG_BIG = -1e30

TOKEN_TILE = 512
ROUTE_TILE = 512
RUN_ALIGN = SUBLANES
EXPERT_BLOCK = 512
VMEM_LIMIT = 56 * 1024 * 1024

f32 = jnp.float32
bf16 = jnp.bfloat16


def _rms(x, g):
    ms = jnp.mean(x * x, axis=-1, keepdims=True)
    return x * lax.rsqrt(ms + NORM_EPS) * g


def _sigmoid(x):
    return 1.0 / (1.0 + jnp.exp(-x))


def _inproj_kernel(x_ref, g_ref, w_ref, u_ref, q_ref, kv_ref, gate_ref):
    xn = _rms(x_ref[...], g_ref[...]).astype(bf16)
    col = 0
    for out_ref in (u_ref, q_ref, kv_ref, gate_ref):
        width = out_ref.shape[1]
        for lo in range(0, width, 1024):
            hi = min(lo + 1024, width)
            out_ref[:, lo:hi] = jnp.dot(
                xn, w_ref[:, col + lo:col + hi], preferred_element_type=f32).astype(out_ref.dtype)
        col += width


def _inproj(x2, g_mix, w_cat):
    t = x2.shape[0]
    tm = TOKEN_TILE
    widths = (2 * CONV_CH, Q_DIM, 4 * KV_DIM, 2 * D_MODEL)
    return pl.pallas_call(
        _inproj_kernel,
        grid=(t // tm,),
        in_specs=[
            pl.BlockSpec((tm, D_MODEL), lambda i: (i, 0)),
            pl.BlockSpec((1, D_MODEL), lambda i: (0, 0)),
            pl.BlockSpec(w_cat.shape, lambda i: (0, 0)),
        ],
        out_specs=[pl.BlockSpec((tm, w), lambda i: (i, 0)) for w in widths],
        out_shape=[jax.ShapeDtypeStruct((t, w), bf16) for w in widths],
        compiler_params=pltpu.CompilerParams(
            dimension_semantics=("arbitrary",), vmem_limit_bytes=VMEM_LIMIT),
        name="inproj",
    )(x2, g_mix, w_cat)


def _mixer_kernel(x_ref, u_ref, up_ref, q_ref, kv_ref, kvp_ref, gate_ref, shift_ref, wdw_ref, bdw_ref,
                  lng_ref, lnb_ref, sink_ref, bias_ref, wco_ref, wao_ref, wo_ref, gffn_ref,
                  wrh_ref, wrl_ref, br_ref,
                  h_ref, xn2_ref, route_ref, wts_ref,
                  vbuf, act, kvall, attn):
    ts = x_ref.shape[0]
    first = pl.program_id(1) == 0

    def glu(ref):
        return ref[:, :CONV_CH].astype(f32) * _sigmoid(ref[:, CONV_CH:].astype(f32))

    vbuf[0:CONV_HALO, :] = jnp.where(first, 0.0, glu(up_ref)).astype(bf16)
    vbuf[CONV_HALO:, :] = glu(u_ref).astype(bf16)

    rows = CONV_ROWS
    win_rows = rows + CONV_HALO

    def conv_chunk(c, carry):
        r0 = pl.multiple_of(c * rows, rows)
        acc = jnp.broadcast_to(bdw_ref[...], (rows, CONV_CH))
        win = vbuf[pl.ds(r0, win_rows), :]
        shifted = jnp.dot(shift_ref[...], win, preferred_element_type=f32)
        for r in range(SUBLANES):
            for a in range((CONV_WIDTH - 1 - r) // SUBLANES + 1):
                j = SUBLANES * a + r
                lo = r * win_rows + SUBLANES * a
                acc = acc + wdw_ref[j:j + 1, :] * shifted[lo:lo + rows]
        mu = jnp.mean(acc, axis=-1, keepdims=True)
        d = acc - mu
        var = jnp.mean(d * d, axis=-1, keepdims=True)
        y = d * lax.rsqrt(var + NORM_EPS) * lng_ref[...] + lnb_ref[...]
        act[pl.ds(r0, rows), :] = (y * _sigmoid(y)).astype(bf16)
        return carry

    lax.fori_loop(0, ts // rows, conv_chunk, 0, unroll=8)

    kvall[0:ATTN_BLOCK, :] = jnp.where(first, jnp.zeros_like(kvp_ref[...]), kvp_ref[...])
    kvall[ATTN_BLOCK:, :] = kv_ref[...]
    nkeys = 2 * ATTN_BLOCK
    left_kv = lax.broadcasted_iota(jnp.int32, (nkeys, LANES), 1) < HEAD_DIM
    left_q = lax.broadcasted_iota(jnp.int32, (ATTN_BLOCK, LANES), 1) < HEAD_DIM
    key_lane = lax.broadcasted_iota(jnp.int32, (ATTN_BLOCK, 2 * nkeys), 1)
    prev_keys = (key_lane % nkeys) < ATTN_BLOCK
    no_prev = jnp.where(prev_keys, jnp.where(first, -jnp.inf, 0.0), 0.0)

    def block_diag(x):
        z = jnp.zeros_like(x)
        return jnp.concatenate([jnp.where(left_kv, x, z), jnp.where(left_kv, z, x)], axis=0)

    def softmax_half(s, sink):
        m = jnp.maximum(jnp.max(s, axis=-1, keepdims=True), sink)
        p = jnp.exp(s - m)
        return p, jnp.sum(p, axis=-1, keepdims=True) + jnp.exp(sink - m)

    for j in range(ts // ATTN_BLOCK):
        r0 = j * ATTN_BLOCK
        for kvh in range(N_KV_HEADS):
            kbd = block_diag(kvall[r0:r0 + nkeys, kvh * LANES:(kvh + 1) * LANES])
            vbd = block_diag(kvall[r0:r0 + nkeys, (N_KV_HEADS + kvh) * LANES:(N_KV_HEADS + kvh + 1) * LANES])
            for ii in range(2):
                i = 2 * kvh + ii
                qb = q_ref[r0:r0 + ATTN_BLOCK, i * LANES:(i + 1) * LANES]
                s = lax.dot_general(qb, kbd, (((1,), (1,)), ((), ())), preferred_element_type=f32)
                s = s + bias_ref[i]
                if j == 0:
                    s = s + no_prev
                pa, la = softmax_half(s[:, :nkeys], sink_ref[2 * i])
                pb, lb = softmax_half(s[:, nkeys:], sink_ref[2 * i + 1])
                p = jnp.concatenate([pa, pb], axis=1).astype(bf16)
                o = jnp.dot(p, vbd, preferred_element_type=f32)
                o = o * jnp.where(left_q, 1.0 / la, 1.0 / lb)
                attn[r0:r0 + ATTN_BLOCK, i * LANES:(i + 1) * LANES] = o.astype(bf16)

    conv_o = jnp.dot(act[...], wco_ref[...], preferred_element_type=f32)
    attn_o = jnp.dot(attn[...], wao_ref[...], preferred_element_type=f32)
    merged = (_sigmoid(gate_ref[:, :D_MODEL].astype(f32)) * conv_o
              + _sigmoid(gate_ref[:, D_MODEL:].astype(f32)) * attn_o).astype(bf16)
    h = x_ref[...] + jnp.dot(merged, wo_ref[...], preferred_element_type=f32)
    h_ref[...] = h

    xn2 = _rms(h, gffn_ref[...])
    xn2_ref[...] = xn2.astype(bf16)
    xh = xn2.astype(bf16)
    xl = (xn2 - xh.astype(f32)).astype(bf16)
    logits = (jnp.dot(xh, wrh_ref[...], preferred_element_type=f32)
              + jnp.dot(xh, wrl_ref[...], preferred_element_type=f32)
              + jnp.dot(xl, wrh_ref[...], preferred_element_type=f32)) + br_ref[...]
    lt = logits.T
    sub = lax.broadcasted_iota(jnp.int32, (SUBLANES, ts), 0)

    gl = lt[0:SUBLANES]
    gmax = jnp.max(gl, axis=0, keepdims=True)
    gsel = jnp.min(jnp.where(gl == gmax, sub, SUBLANES), axis=0, keepdims=True)
    p_group = 1.0 / jnp.sum(jnp.exp(gl - gmax), axis=0, keepdims=True)

    e_in = lt[SUBLANES:2 * SUBLANES]
    for g in range(1, N_GROUPS):
        e_in = jnp.where(gsel == g, lt[(g + 1) * SUBLANES:(g + 2) * SUBLANES], e_in)
    m1 = jnp.max(e_in, axis=0, keepdims=True)
    i1 = jnp.min(jnp.where(e_in == m1, sub, SUBLANES), axis=0, keepdims=True)
    rest = jnp.where(sub == i1, -jnp.inf, e_in)
    m2 = jnp.max(rest, axis=0, keepdims=True)
    i2 = jnp.min(jnp.where(rest == m2, sub, SUBLANES), axis=0, keepdims=True)
    t2 = jnp.exp(m2 - m1)
    w1 = p_group / (1.0 + t2)
    w2 = p_group * t2 / (1.0 + t2)
    base = gsel * EXPERTS_PER_GROUP
    route_ref[...] = jnp.where(sub == 0, base + i1, jnp.where(sub == 1, base + i2, 0))
    wts_ref[...] = jnp.where(sub == 0, w1, jnp.where(sub == 1, w2, 0.0))


def _conv_shift():
    w = CONV_ROWS + CONV_HALO
    tap0 = CONV_HALO - (CONV_WIDTH - 1)
    r, i, s = np.meshgrid(np.arange(SUBLANES), np.arange(w), np.arange(w), indexing="ij")
    return (s == tap0 + r + i).astype(np.float32).reshape(SUBLANES * w, w)


def _attn_bias():
    qi = np.arange(ATTN_BLOCK)[:, None]
    kj = np.arange(2 * ATTN_BLOCK)[None, :]
    rel = (ATTN_BLOCK + qi - kj).astype(np.float32)
    ok = (rel >= 0) & (rel < ATTN_BLOCK)
    slopes = np.array([2.0 ** (-8.0 * (h + 1) / N_HEADS) for h in range(N_HEADS)], np.float32)
    per_head = [np.where(ok, -(slopes[h] * rel), -np.inf).astype(np.float32) for h in range(N_HEADS)]
    return np.stack([np.concatenate([per_head[2 * i], per_head[2 * i + 1]], axis=1)
                     for i in range(N_HEADS // 2)])


def _mixer(x2, u, q, kv, gates, w_dw, b_dw, ln_g, ln_b, sinks, wco, wao, wo, g_ffn, wr_hi, wr_lo, b_r,
           batch, seq):
    t = x2.shape[0]
    ts = TOKEN_TILE
    ns = seq // ts
    bias = jnp.asarray(_attn_bias())
    shift = jnp.asarray(_conv_shift(), dtype=bf16)

    def row(b, s):
        return b * ns + s

    def full(a):
        return pl.BlockSpec(a.shape, lambda b, s: (0,) * a.ndim)

    in_specs = [
        pl.BlockSpec((ts, D_MODEL), lambda b, s: (row(b, s), 0)),
        pl.BlockSpec((ts, 2 * CONV_CH), lambda b, s: (row(b, s), 0)),
        pl.BlockSpec((CONV_HALO, 2 * CONV_CH),
                     lambda b, s: (jnp.maximum(row(b, s) * (ts // CONV_HALO) - 1, 0), 0)),
        pl.BlockSpec((ts, Q_DIM), lambda b, s: (row(b, s), 0)),
        pl.BlockSpec((ts, 4 * KV_DIM), lambda b, s: (row(b, s), 0)),
        pl.BlockSpec((ATTN_BLOCK, 4 * KV_DIM),
                     lambda b, s: (jnp.maximum(row(b, s) * (ts // ATTN_BLOCK) - 1, 0), 0)),
        pl.BlockSpec((ts, 2 * D_MODEL), lambda b, s: (row(b, s), 0)),
        full(shift), full(w_dw), full(b_dw), full(ln_g), full(ln_b),
        pl.BlockSpec(memory_space=pltpu.SMEM),
        full(bias), full(wco), full(wao), full(wo), full(g_ffn), full(wr_hi), full(wr_lo), full(b_r),
    ]
    out_specs = [
        pl.BlockSpec((ts, D_MODEL), lambda b, s: (row(b, s), 0)),
        pl.BlockSpec((ts, D_MODEL), lambda b, s: (row(b, s), 0)),
        pl.BlockSpec((SUBLANES, ts), lambda b, s: (0, row(b, s))),
        pl.BlockSpec((SUBLANES, ts), lambda b, s: (0, row(b, s))),
    ]
    out_shape = [
        jax.ShapeDtypeStruct((t, D_MODEL), f32),
        jax.ShapeDtypeStruct((t, D_MODEL), bf16),
        jax.ShapeDtypeStruct((SUBLANES, t), jnp.int32),
        jax.ShapeDtypeStruct((SUBLANES, t), f32),
    ]
    return pl.pallas_call(
        _mixer_kernel,
        grid=(batch, ns),
        in_specs=in_specs,
        out_specs=out_specs,
        out_shape=out_shape,
        scratch_shapes=[
            pltpu.VMEM((ts + CONV_HALO, CONV_CH), bf16),
            pltpu.VMEM((ts, CONV_CH), bf16),
            pltpu.VMEM((ts + ATTN_BLOCK, 4 * KV_DIM), bf16),
            pltpu.VMEM((ts, Q_DIM), bf16),
        ],
        compiler_params=pltpu.CompilerParams(
            dimension_semantics=("arbitrary", "arbitrary"), vmem_limit_bytes=VMEM_LIMIT),
        name="mixer",
    )(x2, u, u, q, kv, kv, gates, shift, w_dw, b_dw, ln_g, ln_b, sinks, bias, wco, wao, wo, g_ffn,
      wr_hi, wr_lo, b_r)


def _local_rows(ts):
    return -(-(2 * ts + (RUN_ALIGN - 1) * N_EXPERTS) // LANES) * LANES


def _num_blocks(t):
    run_rows = 2 * t + (RUN_ALIGN - 1) * N_EXPERTS * (t // ROUTE_TILE)
    return -(-(run_rows + N_EXPERTS * (EXPERT_BLOCK - RUN_ALIGN)) // EXPERT_BLOCK)


def _route_kernel(route_ref, lpos_ref, tab_ref, seg_ref, meta_ref):
    t = route_ref.shape[1]
    tr = ROUTE_TILE
    nbp = meta_ref.shape[1]
    chunks_per_block = EXPERT_BLOCK // RUN_ALIGN
    eiota = lax.broadcasted_iota(jnp.int32, (N_EXPERTS, tr), 0)
    before = (lax.broadcasted_iota(jnp.int32, (tr, tr), 0)
              < lax.broadcasted_iota(jnp.int32, (tr, tr), 1)).astype(bf16)
    lower = (lax.broadcasted_iota(jnp.int32, (N_EXPERTS, N_EXPERTS), 1)
             < lax.broadcasted_iota(jnp.int32, (N_EXPERTS, N_EXPERTS), 0)).astype(bf16)
    sub = lax.broadcasted_iota(jnp.int32, (N_EXPERTS, LANES), 0)
    lane = lax.broadcasted_iota(jnp.int32, (N_EXPERTS, LANES), 1)

    def to_lanes(col, offset):
        return jnp.sum(jnp.where(sub + offset == lane, col, 0.0), axis=0, keepdims=True)

    def expert_prefix(col):
        b = jnp.broadcast_to(col, (N_EXPERTS, LANES))
        hi = jnp.floor(b * (1.0 / 16.0))
        lo = b - 16.0 * hi
        return (16.0 * jnp.dot(lower, hi.astype(bf16), preferred_element_type=f32)
                + jnp.dot(lower, lo.astype(bf16), preferred_element_type=f32))[:, 0:1]

    lpos_ref[...] = jnp.zeros(lpos_ref.shape, jnp.int32)

    def step(i, seen_chunks):
        off = pl.multiple_of(i * tr, tr)
        m1 = eiota == route_ref[0:1, pl.ds(off, tr)]
        m2 = eiota == route_ref[1:2, pl.ds(off, tr)]
        onehot = jnp.where(m1 | m2, 1.0, 0.0)
        within = jnp.dot(onehot.astype(bf16), before, preferred_element_type=f32)
        run_chunks = jnp.floor((jnp.sum(onehot, axis=1, keepdims=True) + (RUN_ALIGN - 1)) * (1.0 / RUN_ALIGN))
        run_start = expert_prefix(run_chunks)
        pos = within + RUN_ALIGN * run_start
        lpos_ref[0:1, pl.ds(off, tr)] = jnp.sum(jnp.where(m1, pos, 0.0), axis=0, keepdims=True).astype(jnp.int32)
        lpos_ref[1:2, pl.ds(off, tr)] = jnp.sum(jnp.where(m2, pos, 0.0), axis=0, keepdims=True).astype(jnp.int32)
        row = to_lanes(seen_chunks, 0) + to_lanes(run_chunks, N_EXPERTS) + to_lanes(run_start, 2 * N_EXPERTS)
        tab_ref[i] = jnp.broadcast_to(row, (SUBLANES, LANES)).astype(jnp.int32)
        return seen_chunks + run_chunks

    used_chunks = lax.fori_loop(0, t // tr, step, jnp.zeros((N_EXPERTS, 1), f32))

    nblk = jnp.floor((used_chunks + (chunks_per_block - 1)) * (1.0 / chunks_per_block))
    first_blk = expert_prefix(nblk)
    nact = jnp.sum(nblk, axis=0, keepdims=True)
    seg_row = (to_lanes(first_blk * chunks_per_block, 0) + to_lanes(used_chunks, N_EXPERTS)
               + to_lanes(nblk * chunks_per_block, 2 * N_EXPERTS)
               + jnp.where(lane[0:1] == 3 * N_EXPERTS, nact, 0.0))
    seg_ref[...] = jnp.broadcast_to(seg_row, (SUBLANES, LANES)).astype(jnp.int32)

    blk = lax.broadcasted_iota(jnp.int32, (N_EXPERTS, nbp), 1).astype(f32)
    owner = (first_blk <= blk) & (blk < first_blk + nblk)
    expert_id = lax.broadcasted_iota(jnp.int32, (N_EXPERTS, nbp), 0).astype(f32)
    bexp = jnp.sum(jnp.where(owner, expert_id, 0.0), axis=0, keepdims=True)
    row8 = lax.broadcasted_iota(jnp.int32, (SUBLANES, nbp), 0)
    meta_ref[...] = jnp.where(row8 == 0, bexp, jnp.where(row8 == 1, nact, 0.0)).astype(jnp.int32)


def _route(route):
    t = route.shape[1]
    nt = t // ROUTE_TILE
    nbp = -(-_num_blocks(t) // LANES) * LANES
    return pl.pallas_call(
        _route_kernel,
        grid=(1,),
        in_specs=[pl.BlockSpec(route.shape, lambda i: (0, 0))],
        out_specs=[pl.BlockSpec(route.shape, lambda i: (0, 0)),
                   pl.BlockSpec((nt, SUBLANES, LANES), lambda i: (0, 0, 0)),
                   pl.BlockSpec((SUBLANES, LANES), lambda i: (0, 0)),
                   pl.BlockSpec((SUBLANES, nbp), lambda i: (0, 0))],
        out_shape=[jax.ShapeDtypeStruct(route.shape, jnp.int32),
                   jax.ShapeDtypeStruct((nt, SUBLANES, LANES), jnp.int32),
                   jax.ShapeDtypeStruct((SUBLANES, LANES), jnp.int32),
                   jax.ShapeDtypeStruct((SUBLANES, nbp), jnp.int32)],
        compiler_params=pltpu.CompilerParams(
            dimension_semantics=("arbitrary",), vmem_limit_bytes=VMEM_LIMIT),
        name="route",
    )(route)


def _for_each_run_chunk(tab_ref, seg_ref, fn):
    def run(e, carry):
        g0 = seg_ref[0, e] + tab_ref[0, 0, e]
        n = tab_ref[0, 0, N_EXPERTS + e]
        l0 = tab_ref[0, 0, 2 * N_EXPERTS + e]

        def chunk(c, carry2):
            fn(pl.multiple_of((l0 + c) * RUN_ALIGN, RUN_ALIGN), pl.multiple_of((g0 + c) * RUN_ALIGN, RUN_ALIGN))
            return carry2

        lax.fori_loop(0, n, chunk, 0)
        return carry

    lax.fori_loop(0, N_EXPERTS, run, 0)


def _scatter_kernel(tab_ref, seg_ref, lpos_ref, x_ref, xs_ref, xsl, zrows, sem, zsem):
    n_blocks = xs_ref.shape[0] // EXPERT_BLOCK
    lrows, ts = xsl.shape[0], x_ref.shape[0]
    j = lax.broadcasted_iota(jnp.int32, (lrows, ts), 0)
    perm = jnp.where((j == lpos_ref[0:1, :]) | (j == lpos_ref[1:2, :]), 1.0, 0.0).astype(bf16)
    xsl[...] = jnp.dot(perm, x_ref[...], preferred_element_type=f32)

    def copy(l, g):
        return pltpu.make_async_copy(xsl.at[pl.ds(l, RUN_ALIGN)], xs_ref.at[pl.ds(g, RUN_ALIGN)], sem)

    _for_each_run_chunk(tab_ref, seg_ref, lambda l, g: copy(l, g).start())
    _for_each_run_chunk(tab_ref, seg_ref, lambda l, g: copy(l, g).wait())

    @pl.when(pl.program_id(0) == pl.num_programs(0) - 1)
    def _():
        zrows[...] = jnp.zeros(zrows.shape, zrows.dtype)

        def zcopy(g):
            return pltpu.make_async_copy(zrows.at[pl.ds(0, RUN_ALIGN)], xs_ref.at[pl.ds(g, RUN_ALIGN)], zsem)

        def zblock(b):
            return pltpu.make_async_copy(
                zrows, xs_ref.at[pl.ds(pl.multiple_of(b * EXPERT_BLOCK, EXPERT_BLOCK), EXPERT_BLOCK)], zsem)

        def for_each_unused_block(fn):
            def body(b, carry):
                fn(b)
                return carry

            lax.fori_loop(seg_ref[0, 3 * N_EXPERTS], n_blocks, body, 0)

        for_each_unused_block(lambda b: zblock(b).start())
        for_each_unused_block(lambda b: zblock(b).wait())

        def for_each_pad_chunk(fn):
            def seg(e, carry):
                g0 = seg_ref[0, e]
                used = seg_ref[0, N_EXPERTS + e]
                total = seg_ref[0, 2 * N_EXPERTS + e]

                def chunk(c, carry2):
                    fn(pl.multiple_of((g0 + c) * RUN_ALIGN, RUN_ALIGN))
                    return carry2

                lax.fori_loop(used, total, chunk, 0)
                return carry

            lax.fori_loop(0, N_EXPERTS, seg, 0)

        for_each_pad_chunk(lambda g: zcopy(g).start())
        for_each_pad_chunk(lambda g: zcopy(g).wait())


def _scatter(tab, seg, lpos, xn2, n_rows):
    t = xn2.shape[0]
    ts = ROUTE_TILE
    return pl.pallas_call(
        _scatter_kernel,
        grid=(t // ts,),
        in_specs=[
            pl.BlockSpec((1, SUBLANES, LANES), lambda i: (i, 0, 0), memory_space=pltpu.SMEM),
            pl.BlockSpec((SUBLANES, LANES), lambda i: (0, 0), memory_space=pltpu.SMEM),
            pl.BlockSpec((SUBLANES, ts), lambda i: (0, i)),
            pl.BlockSpec((ts, D_MODEL), lambda i: (i, 0)),
        ],
        out_specs=pl.BlockSpec(memory_space=pl.ANY),
        out_shape=jax.ShapeDtypeStruct((n_rows, D_MODEL), f32),
        scratch_shapes=[pltpu.VMEM((_local_rows(ts), D_MODEL), f32), pltpu.VMEM((EXPERT_BLOCK, D_MODEL), f32),
                        pltpu.SemaphoreType.DMA, pltpu.SemaphoreType.DMA],
        compiler_params=pltpu.CompilerParams(
            dimension_semantics=("arbitrary",), vmem_limit_bytes=VMEM_LIMIT, has_side_effects=True),
        name="scatter",
    )(tab, seg, lpos, xn2)


def _expert_kernel(bexp_ref, nact_ref, xs_ref, wg_ref, wu_ref, wd_ref, ys_ref, wg_b, wu_b, wd_b):
    b = pl.program_id(0)
    active = b < nact_ref[0]
    new_expert = jnp.logical_or(b == 0, bexp_ref[b] != bexp_ref[jnp.maximum(b - 1, 0)])

    @pl.when(jnp.logical_not(active))
    def _():
        ys_ref[...] = jnp.zeros(ys_ref.shape, ys_ref.dtype)

    @pl.when(jnp.logical_and(active, new_expert))
    def _():
        wg_b[...] = wg_ref[0].astype(bf16)
        wu_b[...] = wu_ref[0].astype(bf16)
        wd_b[...] = wd_ref[0].astype(bf16)

    @pl.when(active)
    def _():
        x = xs_ref[...].astype(bf16)
        g = jnp.dot(x, wg_b[...], preferred_element_type=f32)
        u = jnp.dot(x, wu_b[...], preferred_element_type=f32)
        hmid = (g * _sigmoid(g) * u).astype(bf16)
        ys_ref[...] = jnp.dot(hmid, wd_b[...], preferred_element_type=f32)


def _experts(bexp, nact, xs, wg, wu, wd):
    n_rows = xs.shape[0]
    nb = n_rows // EXPERT_BLOCK

    def blk(b, bexp_ref, nact_ref):
        return jnp.minimum(b, nact_ref[0] - 1)

    grid_spec = pltpu.PrefetchScalarGridSpec(
        num_scalar_prefetch=2,
        grid=(nb,),
        in_specs=[
            pl.BlockSpec((EXPERT_BLOCK, D_MODEL), lambda b, e, n: (blk(b, e, n), 0)),
            pl.BlockSpec((1, D_MODEL, D_FF_EXPERT), lambda b, e, n: (e[blk(b, e, n)], 0, 0)),
            pl.BlockSpec((1, D_MODEL, D_FF_EXPERT), lambda b, e, n: (e[blk(b, e, n)], 0, 0)),
            pl.BlockSpec((1, D_FF_EXPERT, D_MODEL), lambda b, e, n: (e[blk(b, e, n)], 0, 0)),
        ],
        out_specs=pl.BlockSpec((EXPERT_BLOCK, D_MODEL), lambda b, e, n: (b, 0)),
        scratch_shapes=[pltpu.VMEM((D_MODEL, D_FF_EXPERT), bf16), pltpu.VMEM((D_MODEL, D_FF_EXPERT), bf16),
                        pltpu.VMEM((D_FF_EXPERT, D_MODEL), bf16)],
    )
    return pl.pallas_call(
        _expert_kernel,
        grid_spec=grid_spec,
        out_shape=jax.ShapeDtypeStruct((n_rows, D_MODEL), f32),
        compiler_params=pltpu.CompilerParams(
            dimension_semantics=("arbitrary",), vmem_limit_bytes=VMEM_LIMIT),
        name="experts",
    )(bexp, nact, xs, wg, wu, wd)


def _combine_kernel(tab_ref, seg_ref, lpos_ref, wts_ref, h_ref, gf_ref, ys_ref, out_ref, ybuf, sem):
    ts, lrows = h_ref.shape[0], ybuf.shape[0]
    ybuf[2 * ts:, :] = jnp.zeros((lrows - 2 * ts, D_MODEL), f32)

    def copy(l, g):
        return pltpu.make_async_copy(ys_ref.at[pl.ds(g, RUN_ALIGN)], ybuf.at[pl.ds(l, RUN_ALIGN)], sem)

    _for_each_run_chunk(tab_ref, seg_ref, lambda l, g: copy(l, g).start())
    _for_each_run_chunk(tab_ref, seg_ref, lambda l, g: copy(l, g).wait())

    info = jnp.concatenate([lpos_ref[...].astype(f32), wts_ref[...],
                            jnp.zeros((LANES - 2 * SUBLANES, ts), f32)], axis=0).T
    jl = lax.broadcasted_iota(jnp.int32, (ts, lrows), 1).astype(f32)
    mix = (jnp.where(jl == info[:, 0:1], info[:, SUBLANES:SUBLANES + 1], 0.0)
           + jnp.where(jl == info[:, 1:2], info[:, SUBLANES + 1:SUBLANES + 2], 0.0)).astype(bf16)
    moe = jnp.dot(mix, ybuf[...].astype(bf16), preferred_element_type=f32)
    out_ref[...] = _rms(h_ref[...] + moe, gf_ref[...])


def _combine(tab, seg, lpos, wts, h, g_final, ys):
    t = h.shape[0]
    ts = ROUTE_TILE
    return pl.pallas_call(
        _combine_kernel,
        grid=(t // ts,),
        in_specs=[
            pl.BlockSpec((1, SUBLANES, LANES), lambda i: (i, 0, 0), memory_space=pltpu.SMEM),
            pl.BlockSpec((SUBLANES, LANES), lambda i: (0, 0), memory_space=pltpu.SMEM),
            pl.BlockSpec((SUBLANES, ts), lambda i: (0, i)),
            pl.BlockSpec((SUBLANES, ts), lambda i: (0, i)),
            pl.BlockSpec((ts, D_MODEL), lambda i: (i, 0)),
            pl.BlockSpec((1, D_MODEL), lambda i: (0, 0)),
            pl.BlockSpec(memory_space=pl.ANY),
        ],
        out_specs=pl.BlockSpec((ts, D_MODEL), lambda i: (i, 0)),
        out_shape=jax.ShapeDtypeStruct((t, D_MODEL), f32),
        scratch_shapes=[pltpu.VMEM((_local_rows(ts), D_MODEL), f32), pltpu.SemaphoreType.DMA],
        compiler_params=pltpu.CompilerParams(
            dimension_semantics=("arbitrary",), vmem_limit_bytes=VMEM_LIMIT),
        name="combine",
    )(tab, seg, lpos, wts, h, g_final, ys)


def _prep_w_in(w_in):
    c0 = 2 * CONV_CH
    c1 = c0 + Q_DIM
    c2 = c1 + KV_DIM
    c3 = c2 + KV_DIM
    wk = w_in[:, c1:c2]
    wv = w_in[:, c2:c3]

    def dup(w):
        return jnp.concatenate([w[:, h * HEAD_DIM:(h + 1) * HEAD_DIM] for h in range(N_KV_HEADS)
                                for _ in range(2)], axis=1)

    return jnp.concatenate(
        [w_in[:, :c0], w_in[:, c0:c1] * (HEAD_DIM ** -0.5), dup(wk), dup(wv), w_in[:, c3:]],
        axis=1).astype(bf16)


def _prep_router(w_group, b_group, w_expert, b_expert):
    d = w_group.shape[0]
    w = jnp.zeros((d, LANES), f32)
    w = w.at[:, :N_GROUPS].set(w_group).at[:, SUBLANES:ROUTER_ROWS].set(w_expert)
    b = jnp.full((1, LANES), NEG_BIG, f32)
    b = b.at[0, :N_GROUPS].set(b_group).at[0, SUBLANES:ROUTER_ROWS].set(b_expert)
    w_hi = w.astype(bf16)
    w_lo = (w - w_hi.astype(f32)).astype(bf16)
    return w_hi, w_lo, b


def kernel(x, g_mix, w_in, w_dw, b_dw, ln_conv_g, ln_conv_b, sinks, w_conv_out, w_attn_out, w_out, g_ffn,
           w_group, b_group, w_expert, b_expert, w_gate, w_up, w_down, g_final):
    batch, seq, d = x.shape
    assert d == D_MODEL and seq % TOKEN_TILE == 0 and g_mix.shape[0] == 1
    t = batch * seq
    x2 = x.reshape(t, d)

    u, q, kv, gates = _inproj(x2, g_mix[0][None, :], _prep_w_in(w_in[0]))

    wr_hi, wr_lo, b_r = _prep_router(w_group[0], b_group[0], w_expert[0], b_expert[0])
    h, xn2, route, wts = _mixer(
        x2, u, q, kv, gates, w_dw[0, :, 0, :], b_dw[0][None, :], ln_conv_g[0][None, :],
        ln_conv_b[0][None, :], sinks[0], w_conv_out[0].astype(bf16), w_attn_out[0].astype(bf16),
        w_out[0].astype(bf16), g_ffn[0][None, :], wr_hi, wr_lo, b_r, batch, seq)

    lpos, tab, seg, meta = _route(route)
    nb = _num_blocks(t)
    xs = _scatter(tab, seg, lpos, xn2, nb * EXPERT_BLOCK)
    ys = _experts(meta[0, :nb], meta[1, :1], xs, w_gate[0], w_up[0], w_down[0])
    out = _combine(tab, seg, lpos, wts, h, g_final[None, :], ys)
    return out.reshape(batch, seq, d)
```

```python
import numpy as np
import jax
import jax.numpy as jnp
from jax import lax
from jax.experimental import pallas as pl
from jax.experimental.pallas import tpu as pltpu

D_MODEL = 1024
CONV_CH = 512
CONV_WIDTH = 31
N_HEADS = 8
N_KV_HEADS = 2
HEAD_DIM = 64
ATTN_BLOCK = 128
N_GROUPS = 4
EXPERTS_PER_GROUP = 8
N_EXPERTS = N_GROUPS * EXPERTS_PER_GROUP
D_FF_EXPERT = 512
NORM_EPS = 1e-6

Q_DIM = N_HEADS * HEAD_DIM
KV_DIM = N_KV_HEADS * HEAD_DIM

LANES = 128
SUBLANES = 8
CONV_HALO = 32
CONV_ROWS = 32
ROUTER_ROWS = SUBLANES + N_EXPERTS
NEG_BIG = -1e30

TOKEN_TILE = 512
ROUTE_TILE = 512
RUN_ALIGN = SUBLANES
EXPERT_BLOCK = 512
VMEM_LIMIT = 56 * 1024 * 1024

f32 = jnp.float32
bf16 = jnp.bfloat16


def _rms(x, g):
    ms = jnp.mean(x * x, axis=-1, keepdims=True)
    return x * lax.rsqrt(ms + NORM_EPS) * g


def _sigmoid(x):
    return 1.0 / (1.0 + jnp.exp(-x))


def _inproj_kernel(x_ref, g_ref, w_ref, u_ref, q_ref, kv_ref, gate_ref):
    xn = _rms(x_ref[...], g_ref[...]).astype(bf16)
    col = 0
    for out_ref in (u_ref, q_ref, kv_ref, gate_ref):
        width = out_ref.shape[1]
        for lo in range(0, width, 1024):
            hi = min(lo + 1024, width)
            out_ref[:, lo:hi] = jnp.dot(
                xn, w_ref[:, col + lo:col + hi], preferred_element_type=f32).astype(out_ref.dtype)
        col += width


def _inproj(x2, g_mix, w_cat):
    t = x2.shape[0]
    tm = TOKEN_TILE
    widths = (2 * CONV_CH, Q_DIM, 4 * KV_DIM, 2 * D_MODEL)
    return pl.pallas_call(
        _inproj_kernel,
        grid=(t // tm,),
        in_specs=[
            pl.BlockSpec((tm, D_MODEL), lambda i: (i, 0)),
            pl.BlockSpec((1, D_MODEL), lambda i: (0, 0)),
            pl.BlockSpec(w_cat.shape, lambda i: (0, 0)),
        ],
        out_specs=[pl.BlockSpec((tm, w), lambda i: (i, 0)) for w in widths],
        out_shape=[jax.ShapeDtypeStruct((t, w), bf16) for w in widths],
        compiler_params=pltpu.CompilerParams(
            dimension_semantics=("arbitrary",), vmem_limit_bytes=VMEM_LIMIT),
        name="inproj",
    )(x2, g_mix, w_cat)


def _mixer_kernel(x_ref, u_ref, up_ref, q_ref, kv_ref, kvp_ref, gate_ref, shift_ref, wdw_ref, bdw_ref,
                  lng_ref, lnb_ref, sink_ref, bias_ref, wco_ref, wao_ref, wo_ref, gffn_ref,
                  wrh_ref, wrl_ref, br_ref,
                  h_ref, xn2_ref, route_ref, wts_ref,
                  vbuf, act, kvall, attn):
    ts = x_ref.shape[0]
    first = pl.program_id(1) == 0

    def glu(ref):
        return ref[:, :CONV_CH].astype(f32) * _sigmoid(ref[:, CONV_CH:].astype(f32))

    vbuf[0:CONV_HALO, :] = jnp.where(first, 0.0, glu(up_ref)).astype(bf16)
    vbuf[CONV_HALO:, :] = glu(u_ref).astype(bf16)

    rows = CONV_ROWS
    win_rows = rows + CONV_HALO

    def conv_chunk(c, carry):
        r0 = pl.multiple_of(c * rows, rows)
        acc = jnp.broadcast_to(bdw_ref[...], (rows, CONV_CH))
        win = vbuf[pl.ds(r0, win_rows), :]
        shifted = jnp.dot(shift_ref[...], win, preferred_element_type=f32)
        for r in range(SUBLANES):
            for a in range((CONV_WIDTH - 1 - r) // SUBLANES + 1):
                j = SUBLANES * a + r
                lo = r * win_rows + SUBLANES * a
                acc = acc + wdw_ref[j:j + 1, :] * shifted[lo:lo + rows]
        mu = jnp.mean(acc, axis=-1, keepdims=True)
        d = acc - mu
        var = jnp.mean(d * d, axis=-1, keepdims=True)
        y = d * lax.rsqrt(var + NORM_EPS) * lng_ref[...] + lnb_ref[...]
        act[pl.ds(r0, rows), :] = (y * _sigmoid(y)).astype(bf16)
        return carry

    lax.fori_loop(0, ts // rows, conv_chunk, 0, unroll=8)

    kvall[0:ATTN_BLOCK, :] = jnp.where(first, jnp.zeros_like(kvp_ref[...]), kvp_ref[...])
    kvall[ATTN_BLOCK:, :] = kv_ref[...]
    nkeys = 2 * ATTN_BLOCK
    left_kv = lax.broadcasted_iota(jnp.int32, (nkeys, LANES), 1) < HEAD_DIM
    left_q = lax.broadcasted_iota(jnp.int32, (ATTN_BLOCK, LANES), 1) < HEAD_DIM
    key_lane = lax.broadcasted_iota(jnp.int32, (ATTN_BLOCK, 2 * nkeys), 1)
    prev_keys = (key_lane % nkeys) < ATTN_BLOCK
    no_prev = jnp.where(prev_keys, jnp.where(first, -jnp.inf, 0.0), 0.0)

    def block_diag(x):
        z = jnp.zeros_like(x)
        return jnp.concatenate([jnp.where(left_kv, x, z), jnp.where(left_kv, z, x)], axis=0)

    def softmax_half(s, sink):
        m = jnp.maximum(jnp.max(s, axis=-1, keepdims=True), sink)
        p = jnp.exp(s - m)
        return p, jnp.sum(p, axis=-1, keepdims=True) + jnp.exp(sink - m)

    for j in range(ts // ATTN_BLOCK):
        r0 = j * ATTN_BLOCK
        for kvh in range(N_KV_HEADS):
            kbd = block_diag(kvall[r0:r0 + nkeys, kvh * LANES:(kvh + 1) * LANES])
            vbd = block_diag(kvall[r0:r0 + nkeys, (N_KV_HEADS + kvh) * LANES:(N_KV_HEADS + kvh + 1) * LANES])
            for ii in range(2):
                i = 2 * kvh + ii
                qb = q_ref[r0:r0 + ATTN_BLOCK, i * LANES:(i + 1) * LANES]
                s = lax.dot_general(qb, kbd, (((1,), (1,)), ((), ())), preferred_element_type=f32)
                s = s + bias_ref[i]
                if j == 0:
                    s = s + no_prev
                pa, la = softmax_half(s[:, :nkeys], sink_ref[2 * i])
                pb, lb = softmax_half(s[:, nkeys:], sink_ref[2 * i + 1])
                p = jnp.concatenate([pa, pb], axis=1).astype(bf16)
                o = jnp.dot(p, vbd, preferred_element_type=f32)
                o = o * jnp.where(left_q, 1.0 / la, 1.0 / lb)
                attn[r0:r0 + ATTN_BLOCK, i * LANES:(i + 1) * LANES] = o.astype(bf16)

    conv_o = jnp.dot(act[...], wco_ref[...], preferred_element_type=f32)
    attn_o = jnp.dot(attn[...], wao_ref[...], preferred_element_type=f32)
    merged = (_sigmoid(gate_ref[:, :D_MODEL].astype(f32)) * conv_o
              + _sigmoid(gate_ref[:, D_MODEL:].astype(f32)) * attn_o).astype(bf16)
    h = x_ref[...] + jnp.dot(merged, wo_ref[...], preferred_element_type=f32)
    h_ref[...] = h

    xn2 = _rms(h, gffn_ref[...])
    xn2_ref[...] = xn2.astype(bf16)
    xh = xn2.astype(bf16)
    xl = (xn2 - xh.astype(f32)).astype(bf16)
    logits = (jnp.dot(xh, wrh_ref[...], preferred_element_type=f32)
              + jnp.dot(xh, wrl_ref[...], preferred_element_type=f32)
              + jnp.dot(xl, wrh_ref[...], preferred_element_type=f32)) + br_ref[...]
    lt = logits.T
    sub = lax.broadcasted_iota(jnp.int32, (SUBLANES, ts), 0)

    gl = lt[0:SUBLANES]
    gmax = jnp.max(gl, axis=0, keepdims=True)
    gsel = jnp.min(jnp.where(gl == gmax, sub, SUBLANES), axis=0, keepdims=True)
    p_group = 1.0 / jnp.sum(jnp.exp(gl - gmax), axis=0, keepdims=True)

    e_in = lt[SUBLANES:2 * SUBLANES]
    for g in range(1, N_GROUPS):
        e_in = jnp.where(gsel == g, lt[(g + 1) * SUBLANES:(g + 2) * SUBLANES], e_in)
    m1 = jnp.max(e_in, axis=0, keepdims=True)
    i1 = jnp.min(jnp.where(e_in == m1, sub, SUBLANES), axis=0, keepdims=True)
    rest = jnp.where(sub == i1, -jnp.inf, e_in)
    m2 = jnp.max(rest, axis=0, keepdims=True)
    i2 = jnp.min(jnp.where(rest == m2, sub, SUBLANES), axis=0, keepdims=True)
    t2 = jnp.exp(m2 - m1)
    w1 = p_group / (1.0 + t2)
    w2 = p_group * t2 / (1.0 + t2)
    base = gsel * EXPERTS_PER_GROUP
    route_ref[...] = jnp.where(sub == 0, base + i1, jnp.where(sub == 1, base + i2, 0))
    wts_ref[...] = jnp.where(sub == 0, w1, jnp.where(sub == 1, w2, 0.0))


def _conv_shift():
    w = CONV_ROWS + CONV_HALO
    tap0 = CONV_HALO - (CONV_WIDTH - 1)
    r, i, s = np.meshgrid(np.arange(SUBLANES), np.arange(w), np.arange(w), indexing="ij")
    return (s == tap0 + r + i).astype(np.float32).reshape(SUBLANES * w, w)


def _attn_bias():
    qi = np.arange(ATTN_BLOCK)[:, None]
    kj = np.arange(2 * ATTN_BLOCK)[None, :]
    rel = (ATTN_BLOCK + qi - kj).astype(np.float32)
    ok = (rel >= 0) & (rel < ATTN_BLOCK)
    slopes = np.array([2.0 ** (-8.0 * (h + 1) / N_HEADS) for h in range(N_HEADS)], np.float32)
    per_head = [np.where(ok, -(slopes[h] * rel), -np.inf).astype(np.float32) for h in range(N_HEADS)]
    return np.stack([np.concatenate([per_head[2 * i], per_head[2 * i + 1]], axis=1)
                     for i in range(N_HEADS // 2)])


def _mixer(x2, u, q, kv, gates, w_dw, b_dw, ln_g, ln_b, sinks, wco, wao, wo, g_ffn, wr_hi, wr_lo, b_r,
           batch, seq):
    t = x2.shape[0]
    ts = TOKEN_TILE
    ns = seq // ts
    bias = jnp.asarray(_attn_bias())
    shift = jnp.asarray(_conv_shift(), dtype=bf16)

    def row(b, s):
        return b * ns + s

    def full(a):
        return pl.BlockSpec(a.shape, lambda b, s: (0,) * a.ndim)

    in_specs = [
        pl.BlockSpec((ts, D_MODEL), lambda b, s: (row(b, s), 0)),
        pl.BlockSpec((ts, 2 * CONV_CH), lambda b, s: (row(b, s), 0)),
        pl.BlockSpec((CONV_HALO, 2 * CONV_CH),
                     lambda b, s: (jnp.maximum(row(b, s) * (ts // CONV_HALO) - 1, 0), 0)),
        pl.BlockSpec((ts, Q_DIM), lambda b, s: (row(b, s), 0)),
        pl.BlockSpec((ts, 4 * KV_DIM), lambda b, s: (row(b, s), 0)),
        pl.BlockSpec((ATTN_BLOCK, 4 * KV_DIM),
                     lambda b, s: (jnp.maximum(row(b, s) * (ts // ATTN_BLOCK) - 1, 0), 0)),
        pl.BlockSpec((ts, 2 * D_MODEL), lambda b, s: (row(b, s), 0)),
        full(shift), full(w_dw), full(b_dw), full(ln_g), full(ln_b),
        pl.BlockSpec(memory_space=pltpu.SMEM),
        full(bias), full(wco), full(wao), full(wo), full(g_ffn), full(wr_hi), full(wr_lo), full(b_r),
    ]
    out_specs = [
        pl.BlockSpec((ts, D_MODEL), lambda b, s: (row(b, s), 0)),
        pl.BlockSpec((ts, D_MODEL), lambda b, s: (row(b, s), 0)),
        pl.BlockSpec((SUBLANES, ts), lambda b, s: (0, row(b, s))),
        pl.BlockSpec((SUBLANES, ts), lambda b, s: (0, row(b, s))),
    ]
    out_shape = [
        jax.ShapeDtypeStruct((t, D_MODEL), f32),
        jax.ShapeDtypeStruct((t, D_MODEL), bf16),
        jax.ShapeDtypeStruct((SUBLANES, t), jnp.int32),
        jax.ShapeDtypeStruct((SUBLANES, t), f32),
    ]
    return pl.pallas_call(
        _mixer_kernel,
        grid=(batch, ns),
        in_specs=in_specs,
        out_specs=out_specs,
        out_shape=out_shape,
        scratch_shapes=[
            pltpu.VMEM((ts + CONV_HALO, CONV_CH), bf16),
            pltpu.VMEM((ts, CONV_CH), bf16),
            pltpu.VMEM((ts + ATTN_BLOCK, 4 * KV_DIM), bf16),
            pltpu.VMEM((ts, Q_DIM), bf16),
        ],
        compiler_params=pltpu.CompilerParams(
            dimension_semantics=("arbitrary", "arbitrary"), vmem_limit_bytes=VMEM_LIMIT),
        name="mixer",
    )(x2, u, u, q, kv, kv, gates, shift, w_dw, b_dw, ln_g, ln_b, sinks, bias, wco, wao, wo, g_ffn,
      wr_hi, wr_lo, b_r)


def _local_rows(ts):
    return -(-(2 * ts + (RUN_ALIGN - 1) * N_EXPERTS) // LANES) * LANES


TABLE_LANES = 2 * LANES
SEG_NACT_LANE = 3 * N_EXPERTS
SEG_SPARE_LANE = 3 * N_EXPERTS + 1


def _num_blocks(t):
    run_rows = 2 * t + (RUN_ALIGN - 1) * N_EXPERTS * (t // ROUTE_TILE)
    return -(-(run_rows + N_EXPERTS * (EXPERT_BLOCK - RUN_ALIGN)) // EXPERT_BLOCK)


def _route_kernel(route_ref, lpos_ref, tab_ref, seg_ref, meta_ref):
    t = route_ref.shape[1]
    tr = ROUTE_TILE
    nbp = meta_ref.shape[1]
    chunks_per_block = EXPERT_BLOCK // RUN_ALIGN
    eiota = lax.broadcasted_iota(jnp.int32, (N_EXPERTS, tr), 0)
    before = (lax.broadcasted_iota(jnp.int32, (tr, tr), 0)
              < lax.broadcasted_iota(jnp.int32, (tr, tr), 1)).astype(bf16)
    lower = (lax.broadcasted_iota(jnp.int32, (N_EXPERTS, N_EXPERTS), 1)
             < lax.broadcasted_iota(jnp.int32, (N_EXPERTS, N_EXPERTS), 0)).astype(bf16)
    sub = lax.broadcasted_iota(jnp.int32, (N_EXPERTS, LANES), 0)
    lane = lax.broadcasted_iota(jnp.int32, (N_EXPERTS, LANES), 1)

    def to_lanes(col, offset):
        return jnp.sum(jnp.where(sub + offset == lane, col, 0.0), axis=0, keepdims=True)

    def expert_prefix(col):
        b = jnp.broadcast_to(col, (N_EXPERTS, LANES))
        hi = jnp.floor(b * (1.0 / 16.0))
        lo = b - 16.0 * hi
        return (16.0 * jnp.dot(lower, hi.astype(bf16), preferred_element_type=f32)
                + jnp.dot(lower, lo.astype(bf16), preferred_element_type=f32))[:, 0:1]

    lpos_ref[...] = jnp.zeros(lpos_ref.shape, jnp.int32)
    chunk_id = lax.broadcasted_iota(jnp.int32, (N_EXPERTS, TABLE_LANES), 1).astype(f32)
    chunk_expert = lax.broadcasted_iota(jnp.int32, (N_EXPERTS, TABLE_LANES), 0).astype(f32)

    def step(i, seen_chunks):
        off = pl.multiple_of(i * tr, tr)
        m1 = eiota == route_ref[0:1, pl.ds(off, tr)]
        m2 = eiota == route_ref[1:2, pl.ds(off, tr)]
        onehot = jnp.where(m1 | m2, 1.0, 0.0)
        within = jnp.dot(onehot.astype(bf16), before, preferred_element_type=f32)
        run_chunks = jnp.floor((jnp.sum(onehot, axis=1, keepdims=True) + (RUN_ALIGN - 1)) * (1.0 / RUN_ALIGN))
        run_start = expert_prefix(run_chunks)
        pos = within + RUN_ALIGN * run_start
        lpos_ref[0:1, pl.ds(off, tr)] = jnp.sum(jnp.where(m1, pos, 0.0), axis=0, keepdims=True).astype(jnp.int32)
        lpos_ref[1:2, pl.ds(off, tr)] = jnp.sum(jnp.where(m2, pos, 0.0), axis=0, keepdims=True).astype(jnp.int32)
        owner = (run_start <= chunk_id) & (chunk_id < run_start + run_chunks)
        rel = jnp.sum(jnp.where(owner, seen_chunks + chunk_id - run_start, 0.0), axis=0, keepdims=True)
        eid = jnp.sum(jnp.where(owner, chunk_expert, 0.0), axis=0, keepdims=True)
        n_used = jnp.sum(run_chunks, axis=0, keepdims=True)
        unused = chunk_id[0:1] >= n_used
        rel = jnp.where(unused, chunk_id[0:1] - n_used, rel)
        eid = jnp.where(unused, (SEG_SPARE_LANE + lax.rem(i, 2)).astype(f32), eid)
        row = jnp.concatenate([rel, eid], axis=1)
        tab_ref[i] = jnp.broadcast_to(row, (SUBLANES, 2 * TABLE_LANES)).astype(jnp.int32)
        return seen_chunks + run_chunks

    used_chunks = lax.fori_loop(0, t // tr, step, jnp.zeros((N_EXPERTS, 1), f32))

    nblk = jnp.floor((used_chunks + (chunks_per_block - 1)) * (1.0 / chunks_per_block))
    first_blk = expert_prefix(nblk)
    nact = jnp.sum(nblk, axis=0, keepdims=True)
    spare_chunk = _num_blocks(t) * chunks_per_block
    seg_row = (to_lanes(first_blk * chunks_per_block, 0) + to_lanes(used_chunks, N_EXPERTS)
               + to_lanes(nblk * chunks_per_block, 2 * N_EXPERTS)
               + jnp.where(lane[0:1] == SEG_NACT_LANE, nact, 0.0)
               + jnp.where(lane[0:1] == SEG_SPARE_LANE, float(spare_chunk), 0.0)
               + jnp.where(lane[0:1] == SEG_SPARE_LANE + 1, float(spare_chunk + _local_rows(tr) // RUN_ALIGN), 0.0))
    seg_ref[...] = jnp.broadcast_to(seg_row, (SUBLANES, LANES)).astype(jnp.int32)

    blk = lax.broadcasted_iota(jnp.int32, (N_EXPERTS, nbp), 1).astype(f32)
    owner = (first_blk <= blk) & (blk < first_blk + nblk)
    expert_id = lax.broadcasted_iota(jnp.int32, (N_EXPERTS, nbp), 0).astype(f32)
    bexp = jnp.sum(jnp.where(owner, expert_id, 0.0), axis=0, keepdims=True)
    row8 = lax.broadcasted_iota(jnp.int32, (SUBLANES, nbp), 0)
    meta_ref[...] = jnp.where(row8 == 0, bexp, jnp.where(row8 == 1, nact, 0.0)).astype(jnp.int32)


def _route(route):
    t = route.shape[1]
    nt = t // ROUTE_TILE
    nbp = -(-_num_blocks(t) // LANES) * LANES
    return pl.pallas_call(
        _route_kernel,
        grid=(1,),
        in_specs=[pl.BlockSpec(route.shape, lambda i: (0, 0))],
        out_specs=[pl.BlockSpec(route.shape, lambda i: (0, 0)),
                   pl.BlockSpec((nt, SUBLANES, 2 * TABLE_LANES), lambda i: (0, 0, 0)),
                   pl.BlockSpec((SUBLANES, LANES), lambda i: (0, 0)),
                   pl.BlockSpec((SUBLANES, nbp), lambda i: (0, 0))],
        out_shape=[jax.ShapeDtypeStruct(route.shape, jnp.int32),
                   jax.ShapeDtypeStruct((nt, SUBLANES, 2 * TABLE_LANES), jnp.int32),
                   jax.ShapeDtypeStruct((SUBLANES, LANES), jnp.int32),
                   jax.ShapeDtypeStruct((SUBLANES, nbp), jnp.int32)],
        compiler_params=pltpu.CompilerParams(
            dimension_semantics=("arbitrary",), vmem_limit_bytes=VMEM_LIMIT),
        name="route",
    )(route)


def _chunk_row(tab_ref, seg_ref, k):
    chunk = seg_ref[0, tab_ref[0, 0, TABLE_LANES + k]] + tab_ref[0, 0, k]
    return pl.multiple_of(chunk * RUN_ALIGN, RUN_ALIGN)


def _scatter_kernel(tab_ref, seg_ref, lpos_ref, x_ref, xs_ref, xsl, zrows, sem, zsem):
    lrows, ts = xsl.shape[1], x_ref.shape[0]
    n_blocks = (xs_ref.shape[0] - 2 * lrows) // EXPERT_BLOCK
    i = pl.program_id(0)
    last = pl.num_programs(0) - 1
    slot = lax.rem(i, 2)

    def drain(s):
        pltpu.make_async_copy(xsl.at[s], xs_ref.at[pl.ds(0, lrows)], sem.at[s]).wait()

    @pl.when(i >= 2)
    def _():
        drain(slot)

    j = lax.broadcasted_iota(jnp.int32, (lrows, ts), 0)
    perm = jnp.where((j == lpos_ref[0:1, :]) | (j == lpos_ref[1:2, :]), 1.0, 0.0).astype(bf16)
    xsl[slot] = jnp.dot(perm, x_ref[...], preferred_element_type=f32)

    for k in range(lrows // RUN_ALIGN):
        pltpu.make_async_copy(xsl.at[slot, pl.ds(k * RUN_ALIGN, RUN_ALIGN)],
                              xs_ref.at[pl.ds(_chunk_row(tab_ref, seg_ref, k), RUN_ALIGN)], sem.at[slot]).start()

    @pl.when(i == last)
    def _():
        drain(slot)

        @pl.when(i >= 1)
        def _():
            drain(1 - slot)

        zrows[...] = jnp.zeros(zrows.shape, zrows.dtype)

        def zcopy(row, n):
            return pltpu.make_async_copy(zrows.at[pl.ds(0, n)], xs_ref.at[pl.ds(row, n)], zsem)

        def for_each_unused_block(fn):
            def body(b, carry):
                fn(pl.multiple_of(b * EXPERT_BLOCK, EXPERT_BLOCK), EXPERT_BLOCK)
                return carry

            lax.fori_loop(seg_ref[0, SEG_NACT_LANE], n_blocks, body, 0)
            for off in range(0, 2 * lrows, EXPERT_BLOCK):
                fn(n_blocks * EXPERT_BLOCK + off, min(EXPERT_BLOCK, 2 * lrows - off))

        for_each_unused_block(lambda row, n: zcopy(row, n).start())
        for_each_unused_block(lambda row, n: zcopy(row, n).wait())

        def for_each_pad_chunk(fn):
            def seg(e, carry):
                g0 = seg_ref[0, e]
                used = seg_ref[0, N_EXPERTS + e]
                total = seg_ref[0, 2 * N_EXPERTS + e]

                def chunk(c, carry2):
                    fn(pl.multiple_of((g0 + c) * RUN_ALIGN, RUN_ALIGN), RUN_ALIGN)
                    return carry2

                lax.fori_loop(used, total, chunk, 0)
                return carry

            lax.fori_loop(0, N_EXPERTS, seg, 0)

        for_each_pad_chunk(lambda row, n: zcopy(row, n).start())
        for_each_pad_chunk(lambda row, n: zcopy(row, n).wait())


def _scatter(tab, seg, lpos, xn2, n_blocks):
    t = xn2.shape[0]
    ts = ROUTE_TILE
    lrows = _local_rows(ts)
    return pl.pallas_call(
        _scatter_kernel,
        grid=(t // ts,),
        in_specs=[
            pl.BlockSpec((1, SUBLANES, 2 * TABLE_LANES), lambda i: (i, 0, 0), memory_space=pltpu.SMEM),
            pl.BlockSpec((SUBLANES, LANES), lambda i: (0, 0), memory_space=pltpu.SMEM),
            pl.BlockSpec((SUBLANES, ts), lambda i: (0, i)),
            pl.BlockSpec((ts, D_MODEL), lambda i: (i, 0)),
        ],
        out_specs=pl.BlockSpec(memory_space=pl.ANY),
        out_shape=jax.ShapeDtypeStruct((n_blocks * EXPERT_BLOCK + 2 * lrows, D_MODEL), f32),
        scratch_shapes=[pltpu.VMEM((2, lrows, D_MODEL), f32), pltpu.VMEM((EXPERT_BLOCK, D_MODEL), f32),
                        pltpu.SemaphoreType.DMA((2,)), pltpu.SemaphoreType.DMA],
        compiler_params=pltpu.CompilerParams(
            dimension_semantics=("arbitrary",), vmem_limit_bytes=VMEM_LIMIT, has_side_effects=True),
        name="scatter",
    )(tab, seg, lpos, xn2)


def _expert_kernel(bexp_ref, nact_ref, xs_ref, wg_ref, wu_ref, wd_ref, ys_ref, wg_b, wu_b, wd_b):
    b = pl.program_id(0)
    active = b < nact_ref[0]
    new_expert = jnp.logical_or(b == 0, bexp_ref[b] != bexp_ref[jnp.maximum(b - 1, 0)])

    @pl.when(jnp.logical_not(active))
    def _():
        ys_ref[...] = jnp.zeros(ys_ref.shape, ys_ref.dtype)

    @pl.when(jnp.logical_and(active, new_expert))
    def _():
        wg_b[...] = wg_ref[0].astype(bf16)
        wu_b[...] = wu_ref[0].astype(bf16)
        wd_b[...] = wd_ref[0].astype(bf16)

    @pl.when(active)
    def _():
        x = xs_ref[...].astype(bf16)
        g = jnp.dot(x, wg_b[...], preferred_element_type=f32)
        u = jnp.dot(x, wu_b[...], preferred_element_type=f32)
        hmid = (g * _sigmoid(g) * u).astype(bf16)
        ys_ref[...] = jnp.dot(hmid, wd_b[...], preferred_element_type=f32)


def _experts(bexp, nact, xs, wg, wu, wd, nb):
    n_rows = nb * EXPERT_BLOCK

    def blk(b, bexp_ref, nact_ref):
        return jnp.minimum(b, nact_ref[0] - 1)

    grid_spec = pltpu.PrefetchScalarGridSpec(
        num_scalar_prefetch=2,
        grid=(nb,),
        in_specs=[
            pl.BlockSpec((EXPERT_BLOCK, D_MODEL), lambda b, e, n: (blk(b, e, n), 0)),
            pl.BlockSpec((1, D_MODEL, D_FF_EXPERT), lambda b, e, n: (e[blk(b, e, n)], 0, 0)),
            pl.BlockSpec((1, D_MODEL, D_FF_EXPERT), lambda b, e, n: (e[blk(b, e, n)], 0, 0)),
            pl.BlockSpec((1, D_FF_EXPERT, D_MODEL), lambda b, e, n: (e[blk(b, e, n)], 0, 0)),
        ],
        out_specs=pl.BlockSpec((EXPERT_BLOCK, D_MODEL), lambda b, e, n: (b, 0)),
        scratch_shapes=[pltpu.VMEM((D_MODEL, D_FF_EXPERT), bf16), pltpu.VMEM((D_MODEL, D_FF_EXPERT), bf16),
                        pltpu.VMEM((D_FF_EXPERT, D_MODEL), bf16)],
    )
    return pl.pallas_call(
        _expert_kernel,
        grid_spec=grid_spec,
        out_shape=jax.ShapeDtypeStruct((n_rows, D_MODEL), f32),
        compiler_params=pltpu.CompilerParams(
            dimension_semantics=("arbitrary",), vmem_limit_bytes=VMEM_LIMIT),
        name="experts",
    )(bexp, nact, xs, wg, wu, wd)


def _combine_kernel(tab_ref, tab_next_ref, seg_ref, lpos_ref, wts_ref, h_ref, gf_ref, ys_ref, out_ref, ybuf, sem):
    ts, lrows = h_ref.shape[0], ybuf.shape[1]
    i = pl.program_id(0)
    slot = lax.rem(i, 2)

    def fetch(t_ref, s):
        for k in range(lrows // RUN_ALIGN):
            pltpu.make_async_copy(ys_ref.at[pl.ds(_chunk_row(t_ref, seg_ref, k), RUN_ALIGN)],
                                  ybuf.at[s, pl.ds(k * RUN_ALIGN, RUN_ALIGN)], sem.at[s]).start()

    @pl.when(i == 0)
    def _():
        fetch(tab_ref, 0)

    @pl.when(i + 1 < pl.num_programs(0))
    def _():
        fetch(tab_next_ref, 1 - slot)

    pltpu.make_async_copy(ys_ref.at[pl.ds(0, lrows)], ybuf.at[slot], sem.at[slot]).wait()

    info = jnp.concatenate([lpos_ref[...].astype(f32), wts_ref[...],
                            jnp.zeros((LANES - 2 * SUBLANES, ts), f32)], axis=0).T
    jl = lax.broadcasted_iota(jnp.int32, (ts, lrows), 1).astype(f32)
    mix = (jnp.where(jl == info[:, 0:1], info[:, SUBLANES:SUBLANES + 1], 0.0)
           + jnp.where(jl == info[:, 1:2], info[:, SUBLANES + 1:SUBLANES + 2], 0.0)).astype(bf16)
    moe = jnp.dot(mix, ybuf[slot].astype(bf16), preferred_element_type=f32)
    out_ref[...] = _rms(h_ref[...] + moe, gf_ref[...])


def _combine(tab, seg, lpos, wts, h, g_final, ys):
    t = h.shape[0]
    ts = ROUTE_TILE
    nt = t // ts
    lrows = _local_rows(ts)
    assert ys.shape[0] >= lrows
    return pl.pallas_call(
        _combine_kernel,
        grid=(nt,),
        in_specs=[
            pl.BlockSpec((1, SUBLANES, 2 * TABLE_LANES), lambda i: (i, 0, 0), memory_space=pltpu.SMEM),
            pl.BlockSpec((1, SUBLANES, 2 * TABLE_LANES), lambda i: (jnp.minimum(i + 1, nt - 1), 0, 0),
                         memory_space=pltpu.SMEM),
            pl.BlockSpec((SUBLANES, LANES), lambda i: (0, 0), memory_space=pltpu.SMEM),
            pl.BlockSpec((SUBLANES, ts), lambda i: (0, i)),
            pl.BlockSpec((SUBLANES, ts), lambda i: (0, i)),
            pl.BlockSpec((ts, D_MODEL), lambda i: (i, 0)),
            pl.BlockSpec((1, D_MODEL), lambda i: (0, 0)),
            pl.BlockSpec(memory_space=pl.ANY),
        ],
        out_specs=pl.BlockSpec((ts, D_MODEL), lambda i: (i, 0)),
        out_shape=jax.ShapeDtypeStruct((t, D_MODEL), f32),
        scratch_shapes=[pltpu.VMEM((2, lrows, D_MODEL), f32), pltpu.SemaphoreType.DMA((2,))],
        compiler_params=pltpu.CompilerParams(
            dimension_semantics=("arbitrary",), vmem_limit_bytes=VMEM_LIMIT),
        name="combine",
    )(tab, tab, seg, lpos, wts, h, g_final, ys)


def _prep_w_in(w_in):
    c0 = 2 * CONV_CH
    c1 = c0 + Q_DIM
    c2 = c1 + KV_DIM
    c3 = c2 + KV_DIM
    wk = w_in[:, c1:c2]
    wv = w_in[:, c2:c3]

    def dup(w):
        return jnp.concatenate([w[:, h * HEAD_DIM:(h + 1) * HEAD_DIM] for h in range(N_KV_HEADS)
                                for _ in range(2)], axis=1)

    return jnp.concatenate(
        [w_in[:, :c0], w_in[:, c0:c1] * (HEAD_DIM ** -0.5), dup(wk), dup(wv), w_in[:, c3:]],
        axis=1).astype(bf16)


def _prep_router(w_group, b_group, w_expert, b_expert):
    d = w_group.shape[0]
    w = jnp.zeros((d, LANES), f32)
    w = w.at[:, :N_GROUPS].set(w_group).at[:, SUBLANES:ROUTER_ROWS].set(w_expert)
    b = jnp.full((1, LANES), NEG_BIG, f32)
    b = b.at[0, :N_GROUPS].set(b_group).at[0, SUBLANES:ROUTER_ROWS].set(b_expert)
    w_hi = w.astype(bf16)
    w_lo = (w - w_hi.astype(f32)).astype(bf16)
    return w_hi, w_lo, b


def kernel(x, g_mix, w_in, w_dw, b_dw, ln_conv_g, ln_conv_b, sinks, w_conv_out, w_attn_out, w_out, g_ffn,
           w_group, b_group, w_expert, b_expert, w_gate, w_up, w_down, g_final):
    batch, seq, d = x.shape
    assert d == D_MODEL and seq % TOKEN_TILE == 0 and g_mix.shape[0] == 1
    t = batch * seq
    x2 = x.reshape(t, d)

    u, q, kv, gates = _inproj(x2, g_mix[0][None, :], _prep_w_in(w_in[0]))

    wr_hi, wr_lo, b_r = _prep_router(w_group[0], b_group[0], w_expert[0], b_expert[0])
    h, xn2, route, wts = _mixer(
        x2, u, q, kv, gates, w_dw[0, :, 0, :], b_dw[0][None, :], ln_conv_g[0][None, :],
        ln_conv_b[0][None, :], sinks[0], w_conv_out[0].astype(bf16), w_attn_out[0].astype(bf16),
        w_out[0].astype(bf16), g_ffn[0][None, :], wr_hi, wr_lo, b_r, batch, seq)

    lpos, tab, seg, meta = _route(route)
    nb = _num_blocks(t)
    xs = _scatter(tab, seg, lpos, xn2, nb)
    ys = _experts(meta[0, :nb], meta[1, :1], xs, w_gate[0], w_up[0], w_down[0], nb)
    seg_back = seg.at[:, SEG_SPARE_LANE:SEG_SPARE_LANE + 2].set(0)
    out = _combine(tab, seg_back, lpos, wts, h, g_final[None, :], ys)
    return out.reshape(batch, seq, d)
```

```python
import functools

import numpy as np
import jax
import jax.numpy as jnp
from jax import lax
from jax.experimental import pallas as pl
from jax.experimental.pallas import tpu as pltpu

D_MODEL = 1024
CONV_CH = 512
CONV_WIDTH = 31
N_HEADS = 8
N_KV_HEADS = 2
HEAD_DIM = 64
ATTN_BLOCK = 128
N_GROUPS = 4
EXPERTS_PER_GROUP = 8
N_EXPERTS = N_GROUPS * EXPERTS_PER_GROUP
D_FF_EXPERT = 512
NORM_EPS = 1e-6

Q_DIM = N_HEADS * HEAD_DIM
KV_DIM = N_KV_HEADS * HEAD_DIM

LANES = 128
SUBLANES = 8
CONV_HALO = 32
CONV_ROWS = 32
PROJ_PIECE = 256
ROUTER_ROWS = SUBLANES + N_EXPERTS
NEG_BIG = -1e30

TOKEN_TILE = 512
ROUTE_TILE = 512
RUN_ALIGN = SUBLANES
EXPERT_BLOCK = 512
VMEM_LIMIT = 56 * 1024 * 1024

f32 = jnp.float32
bf16 = jnp.bfloat16


def _rms(x, g):
    ms = jnp.mean(x * x, axis=-1, keepdims=True)
    return x * lax.rsqrt(ms + NORM_EPS) * g


def _sigmoid(x):
    return 1.0 / (1.0 + jnp.exp(-x))


def _inproj_kernel(tiles_per_seq, x_ref, g_ref, w_ref, wdw_ref, bdw_ref, lng_ref, lnb_ref,
                   act_ref, q_ref, kv_ref, sgate_ref, vbuf):
    tm = x_ref.shape[0]
    xn = _rms(x_ref[...], g_ref[...]).astype(bf16)

    def proj(lo, hi):
        return jnp.dot(xn, w_ref[:, lo:hi], preferred_element_type=f32)

    n_cb = CONV_CH // LANES
    first = lax.rem(pl.program_id(0), tiles_per_seq) == 0

    @pl.when(first)
    def _():
        vbuf[:, 0:CONV_HALO, :] = jnp.zeros((n_cb, CONV_HALO, LANES), f32)

    @pl.when(jnp.logical_not(first))
    def _():
        vbuf[:, 0:CONV_HALO, :] = vbuf[:, tm:tm + CONV_HALO, :]

    u = proj(0, 2 * CONV_CH)
    v = u[:, :CONV_CH] * _sigmoid(u[:, CONV_CH:])
    for b in range(n_cb):
        vbuf[b, CONV_HALO:, :] = v[:, b * LANES:(b + 1) * LANES]

    rows = CONV_ROWS
    n_steps = tm // rows
    tap0 = CONV_HALO - (CONV_WIDTH - 1)
    for c in range(n_steps):
        accs = []
        for b in range(n_cb):
            acc = jnp.broadcast_to(bdw_ref[:, b * LANES:(b + 1) * LANES], (rows, LANES))
            for j in range(CONV_WIDTH):
                r0 = c * rows + tap0 + j
                acc = acc + wdw_ref[j:j + 1, b * LANES:(b + 1) * LANES] * vbuf[b, r0:r0 + rows, :]
            accs.append(acc)
        mu = sum(jnp.sum(a, axis=-1, keepdims=True) for a in accs) * (1.0 / CONV_CH)
        ds = [a - mu for a in accs]
        var = sum(jnp.sum(d * d, axis=-1, keepdims=True) for d in ds) * (1.0 / CONV_CH)
        inv = lax.rsqrt(var + NORM_EPS)
        for b in range(n_cb):
            y = ds[b] * inv * lng_ref[:, b * LANES:(b + 1) * LANES] + lnb_ref[:, b * LANES:(b + 1) * LANES]
            act_ref[c * rows:(c + 1) * rows, b * LANES:(b + 1) * LANES] = (y * _sigmoid(y)).astype(bf16)

    col = 2 * CONV_CH
    for out_ref, fn in ((q_ref, lambda z: z), (kv_ref, lambda z: z), (sgate_ref, _sigmoid)):
        for lo in range(0, out_ref.shape[1], PROJ_PIECE):
            out_ref[:, lo:lo + PROJ_PIECE] = fn(proj(col + lo, col + lo + PROJ_PIECE)).astype(bf16)
        col += out_ref.shape[1]


def _inproj(x2, g_mix, w_cat, w_dw, b_dw, ln_g, ln_b, seq):
    t = x2.shape[0]
    tm = TOKEN_TILE
    widths = (CONV_CH, Q_DIM, 4 * KV_DIM, 2 * D_MODEL)

    def full(a):
        return pl.BlockSpec(a.shape, lambda i: (0,) * a.ndim)

    return pl.pallas_call(
        functools.partial(_inproj_kernel, seq // tm),
        grid=(t // tm,),
        in_specs=[pl.BlockSpec((tm, D_MODEL), lambda i: (i, 0)),
                  full(g_mix), full(w_cat), full(w_dw), full(b_dw), full(ln_g), full(ln_b)],
        out_specs=[pl.BlockSpec((tm, w), lambda i: (i, 0)) for w in widths],
        out_shape=[jax.ShapeDtypeStruct((t, w), bf16) for w in widths],
        scratch_shapes=[pltpu.VMEM((CONV_CH // LANES, tm + CONV_HALO, LANES), f32)],
        compiler_params=pltpu.CompilerParams(
            dimension_semantics=("arbitrary",), vmem_limit_bytes=VMEM_LIMIT),
        name="inproj",
    )(x2, g_mix, w_cat, w_dw, b_dw, ln_g, ln_b)


def _mixer_kernel(x_ref, act_ref, q_ref, kv_ref, kvp_ref, sgate_ref, sink_ref, bias_ref, wco_ref, wao_ref,
                  wo_ref, gffn_ref, wrh_ref, wrl_ref, br_ref,
                  h_ref, xn2_ref, route_ref, wts_ref,
                  kvall, attn):
    ts = x_ref.shape[0]
    first = pl.program_id(1) == 0

    kvall[0:ATTN_BLOCK, :] = jnp.where(first, jnp.zeros_like(kvp_ref[...]), kvp_ref[...])
    kvall[ATTN_BLOCK:, :] = kv_ref[...]
    nkeys = 2 * ATTN_BLOCK
    left_kv = lax.broadcasted_iota(jnp.int32, (nkeys, LANES), 1) < HEAD_DIM
    left_q = lax.broadcasted_iota(jnp.int32, (ATTN_BLOCK, LANES), 1) < HEAD_DIM
    key_lane = lax.broadcasted_iota(jnp.int32, (ATTN_BLOCK, 2 * nkeys), 1)
    prev_keys = (key_lane % nkeys) < ATTN_BLOCK
    no_prev = jnp.where(prev_keys, jnp.where(first, -jnp.inf, 0.0), 0.0)

    def block_diag(x):
        z = jnp.zeros_like(x)
        return jnp.concatenate([jnp.where(left_kv, x, z), jnp.where(left_kv, z, x)], axis=0)

    def softmax_half(s, sink):
        m = jnp.maximum(jnp.max(s, axis=-1, keepdims=True), sink)
        p = jnp.exp(s - m)
        return p, jnp.sum(p, axis=-1, keepdims=True) + jnp.exp(sink - m)

    for j in range(ts // ATTN_BLOCK):
        r0 = j * ATTN_BLOCK
        for kvh in range(N_KV_HEADS):
            kbd = block_diag(kvall[r0:r0 + nkeys, kvh * LANES:(kvh + 1) * LANES])
            vbd = block_diag(kvall[r0:r0 + nkeys, (N_KV_HEADS + kvh) * LANES:(N_KV_HEADS + kvh + 1) * LANES])
            for ii in range(2):
                i = 2 * kvh + ii
                qb = q_ref[r0:r0 + ATTN_BLOCK, i * LANES:(i + 1) * LANES]
                s = lax.dot_general(qb, kbd, (((1,), (1,)), ((), ())), preferred_element_type=f32)
                s = s + bias_ref[i]
                if j == 0:
                    s = s + no_prev
                pa, la = softmax_half(s[:, :nkeys], sink_ref[2 * i])
                pb, lb = softmax_half(s[:, nkeys:], sink_ref[2 * i + 1])
                p = jnp.concatenate([pa, pb], axis=1).astype(bf16)
                o = jnp.dot(p, vbd, preferred_element_type=f32)
                o = o * jnp.where(left_q, 1.0 / la, 1.0 / lb)
                attn[r0:r0 + ATTN_BLOCK, i * LANES:(i + 1) * LANES] = o.astype(bf16)

    conv_o = jnp.dot(act_ref[...], wco_ref[...], preferred_element_type=f32)
    attn_o = jnp.dot(attn[...], wao_ref[...], preferred_element_type=f32)
    merged = (sgate_ref[:, :D_MODEL].astype(f32) * conv_o
              + sgate_ref[:, D_MODEL:].astype(f32) * attn_o).astype(bf16)
    h = x_ref[...] + jnp.dot(merged, wo_ref[...], preferred_element_type=f32)
    h_ref[...] = h

    xn2 = _rms(h, gffn_ref[...])
    xn2_ref[...] = xn2.astype(bf16)
    xh = xn2.astype(bf16)
    xl = (xn2 - xh.astype(f32)).astype(bf16)
    logits = (jnp.dot(xh, wrh_ref[...], preferred_element_type=f32)
              + jnp.dot(xh, wrl_ref[...], preferred_element_type=f32)
              + jnp.dot(xl, wrh_ref[...], preferred_element_type=f32)) + br_ref[...]
    lt = logits.T
    sub = lax.broadcasted_iota(jnp.int32, (SUBLANES, ts), 0)

    gl = lt[0:SUBLANES]
    gmax = jnp.max(gl, axis=0, keepdims=True)
    gsel = jnp.min(jnp.where(gl == gmax, sub, SUBLANES), axis=0, keepdims=True)
    p_group = 1.0 / jnp.sum(jnp.exp(gl - gmax), axis=0, keepdims=True)

    e_in = lt[SUBLANES:2 * SUBLANES]
    for g in range(1, N_GROUPS):
        e_in = jnp.where(gsel == g, lt[(g + 1) * SUBLANES:(g + 2) * SUBLANES], e_in)
    m1 = jnp.max(e_in, axis=0, keepdims=True)
    i1 = jnp.min(jnp.where(e_in == m1, sub, SUBLANES), axis=0, keepdims=True)
    rest = jnp.where(sub == i1, -jnp.inf, e_in)
    m2 = jnp.max(rest, axis=0, keepdims=True)
    i2 = jnp.min(jnp.where(rest == m2, sub, SUBLANES), axis=0, keepdims=True)
    t2 = jnp.exp(m2 - m1)
    w1 = p_group / (1.0 + t2)
    w2 = p_group * t2 / (1.0 + t2)
    base = gsel * EXPERTS_PER_GROUP
    route_ref[...] = jnp.where(sub == 0, base + i1, jnp.where(sub == 1, base + i2, 0))
    wts_ref[...] = jnp.where(sub == 0, w1, jnp.where(sub == 1, w2, 0.0))


def _attn_bias():
    qi = np.arange(ATTN_BLOCK)[:, None]
    kj = np.arange(2 * ATTN_BLOCK)[None, :]
    rel = (ATTN_BLOCK + qi - kj).astype(np.float32)
    ok = (rel >= 0) & (rel < ATTN_BLOCK)
    slopes = np.array([2.0 ** (-8.0 * (h + 1) / N_HEADS) for h in range(N_HEADS)], np.float32)
    per_head = [np.where(ok, -(slopes[h] * rel), -np.inf).astype(np.float32) for h in range(N_HEADS)]
    return np.stack([np.concatenate([per_head[2 * i], per_head[2 * i + 1]], axis=1)
                     for i in range(N_HEADS // 2)])


def _mixer(x2, act, q, kv, sgates, sinks, wco, wao, wo, g_ffn, wr_hi, wr_lo, b_r, batch, seq):
    t = x2.shape[0]
    ts = TOKEN_TILE
    ns = seq // ts
    bias = jnp.asarray(_attn_bias())

    def row(b, s):
        return b * ns + s

    def full(a):
        return pl.BlockSpec(a.shape, lambda b, s: (0,) * a.ndim)

    in_specs = [
        pl.BlockSpec((ts, D_MODEL), lambda b, s: (row(b, s), 0)),
        pl.BlockSpec((ts, CONV_CH), lambda b, s: (row(b, s), 0)),
        pl.BlockSpec((ts, Q_DIM), lambda b, s: (row(b, s), 0)),
        pl.BlockSpec((ts, 4 * KV_DIM), lambda b, s: (row(b, s), 0)),
        pl.BlockSpec((ATTN_BLOCK, 4 * KV_DIM),
                     lambda b, s: (jnp.maximum(row(b, s) * (ts // ATTN_BLOCK) - 1, 0), 0)),
        pl.BlockSpec((ts, 2 * D_MODEL), lambda b, s: (row(b, s), 0)),
        pl.BlockSpec(memory_space=pltpu.SMEM),
        full(bias), full(wco), full(wao), full(wo), full(g_ffn), full(wr_hi), full(wr_lo), full(b_r),
    ]
    out_specs = [
        pl.BlockSpec((ts, D_MODEL), lambda b, s: (row(b, s), 0)),
        pl.BlockSpec((ts, D_MODEL), lambda b, s: (row(b, s), 0)),
        pl.BlockSpec((SUBLANES, ts), lambda b, s: (0, row(b, s))),
        pl.BlockSpec((SUBLANES, ts), lambda b, s: (0, row(b, s))),
    ]
    out_shape = [
        jax.ShapeDtypeStruct((t, D_MODEL), f32),
        jax.ShapeDtypeStruct((t, D_MODEL), bf16),
        jax.ShapeDtypeStruct((SUBLANES, t), jnp.int32),
        jax.ShapeDtypeStruct((SUBLANES, t), f32),
    ]
    return pl.pallas_call(
        _mixer_kernel,
        grid=(batch, ns),
        in_specs=in_specs,
        out_specs=out_specs,
        out_shape=out_shape,
        scratch_shapes=[
            pltpu.VMEM((ts + ATTN_BLOCK, 4 * KV_DIM), bf16),
            pltpu.VMEM((ts, Q_DIM), bf16),
        ],
        compiler_params=pltpu.CompilerParams(
            dimension_semantics=("arbitrary", "arbitrary"), vmem_limit_bytes=VMEM_LIMIT),
        name="mixer",
    )(x2, act, q, kv, kv, sgates, sinks, bias, wco, wao, wo, g_ffn, wr_hi, wr_lo, b_r)


def _local_rows(ts):
    return -(-(2 * ts + (RUN_ALIGN - 1) * N_EXPERTS) // LANES) * LANES


TABLE_LANES = 2 * LANES
SEG_NACT_LANE = 3 * N_EXPERTS
SEG_SPARE_LANE = 3 * N_EXPERTS + 1


def _num_blocks(t):
    run_rows = 2 * t + (RUN_ALIGN - 1) * N_EXPERTS * (t // ROUTE_TILE)
    return -(-(run_rows + N_EXPERTS * (EXPERT_BLOCK - RUN_ALIGN)) // EXPERT_BLOCK)


def _route_kernel(route_ref, lpos_ref, tab_ref, seg_ref, meta_ref):
    t = route_ref.shape[1]
    tr = ROUTE_TILE
    nbp = meta_ref.shape[1]
    chunks_per_block = EXPERT_BLOCK // RUN_ALIGN
    eiota = lax.broadcasted_iota(jnp.int32, (N_EXPERTS, tr), 0)
    before = (lax.broadcasted_iota(jnp.int32, (tr, tr), 0)
              < lax.broadcasted_iota(jnp.int32, (tr, tr), 1)).astype(bf16)
    lower = (lax.broadcasted_iota(jnp.int32, (N_EXPERTS, N_EXPERTS), 1)
             < lax.broadcasted_iota(jnp.int32, (N_EXPERTS, N_EXPERTS), 0)).astype(bf16)
    sub = lax.broadcasted_iota(jnp.int32, (N_EXPERTS, LANES), 0)
    lane = lax.broadcasted_iota(jnp.int32, (N_EXPERTS, LANES), 1)

    def to_lanes(col, offset):
        return jnp.sum(jnp.where(sub + offset == lane, col, 0.0), axis=0, keepdims=True)

    def expert_prefix(col):
        b = jnp.broadcast_to(col, (N_EXPERTS, LANES))
        hi = jnp.floor(b * (1.0 / 16.0))
        lo = b - 16.0 * hi
        return (16.0 * jnp.dot(lower, hi.astype(bf16), preferred_element_type=f32)
                + jnp.dot(lower, lo.astype(bf16), preferred_element_type=f32))[:, 0:1]

    lpos_ref[...] = jnp.zeros(lpos_ref.shape, jnp.int32)
    chunk_id = lax.broadcasted_iota(jnp.int32, (N_EXPERTS, TABLE_LANES), 1).astype(f32)
    chunk_expert = lax.broadcasted_iota(jnp.int32, (N_EXPERTS, TABLE_LANES), 0).astype(f32)

    def step(i, seen_chunks):
        off = pl.multiple_of(i * tr, tr)
        m1 = eiota == route_ref[0:1, pl.ds(off, tr)]
        m2 = eiota == route_ref[1:2, pl.ds(off, tr)]
        onehot = jnp.where(m1 | m2, 1.0, 0.0)
        within = jnp.dot(onehot.astype(bf16), before, preferred_element_type=f32)
        run_chunks = jnp.floor((jnp.sum(onehot, axis=1, keepdims=True) + (RUN_ALIGN - 1)) * (1.0 / RUN_ALIGN))
        run_start = expert_prefix(run_chunks)
        pos = within + RUN_ALIGN * run_start
        lpos_ref[0:1, pl.ds(off, tr)] = jnp.sum(jnp.where(m1, pos, 0.0), axis=0, keepdims=True).astype(jnp.int32)
        lpos_ref[1:2, pl.ds(off, tr)] = jnp.sum(jnp.where(m2, pos, 0.0), axis=0, keepdims=True).astype(jnp.int32)
        owner = (run_start <= chunk_id) & (chunk_id < run_start + run_chunks)
        rel = jnp.sum(jnp.where(owner, seen_chunks + chunk_id - run_start, 0.0), axis=0, keepdims=True)
        eid = jnp.sum(jnp.where(owner, chunk_expert, 0.0), axis=0, keepdims=True)
        n_used = jnp.sum(run_chunks, axis=0, keepdims=True)
        unused = chunk_id[0:1] >= n_used
        rel = jnp.where(unused, chunk_id[0:1] - n_used, rel)
        eid = jnp.where(unused, (SEG_SPARE_LANE + lax.rem(i, 2)).astype(f32), eid)
        row = jnp.concatenate([rel, eid], axis=1)
        tab_ref[i] = jnp.broadcast_to(row, (SUBLANES, 2 * TABLE_LANES)).astype(jnp.int32)
        return seen_chunks + run_chunks

    used_chunks = lax.fori_loop(0, t // tr, step, jnp.zeros((N_EXPERTS, 1), f32))

    nblk = jnp.floor((used_chunks + (chunks_per_block - 1)) * (1.0 / chunks_per_block))
    first_blk = expert_prefix(nblk)
    nact = jnp.sum(nblk, axis=0, keepdims=True)
    spare_chunk = _num_blocks(t) * chunks_per_block
    seg_row = (to_lanes(first_blk * chunks_per_block, 0) + to_lanes(used_chunks, N_EXPERTS)
               + to_lanes(nblk * chunks_per_block, 2 * N_EXPERTS)
               + jnp.where(lane[0:1] == SEG_NACT_LANE, nact, 0.0)
               + jnp.where(lane[0:1] == SEG_SPARE_LANE, float(spare_chunk), 0.0)
               + jnp.where(lane[0:1] == SEG_SPARE_LANE + 1, float(spare_chunk + _local_rows(tr) // RUN_ALIGN), 0.0))
    seg_ref[...] = jnp.broadcast_to(seg_row, (SUBLANES, LANES)).astype(jnp.int32)

    blk = lax.broadcasted_iota(jnp.int32, (N_EXPERTS, nbp), 1).astype(f32)
    owner = (first_blk <= blk) & (blk < first_blk + nblk)
    expert_id = lax.broadcasted_iota(jnp.int32, (N_EXPERTS, nbp), 0).astype(f32)
    bexp = jnp.sum(jnp.where(owner, expert_id, 0.0), axis=0, keepdims=True)
    row8 = lax.broadcasted_iota(jnp.int32, (SUBLANES, nbp), 0)
    meta_ref[...] = jnp.where(row8 == 0, bexp, jnp.where(row8 == 1, nact, 0.0)).astype(jnp.int32)


def _route(route):
    t = route.shape[1]
    nt = t // ROUTE_TILE
    nbp = -(-_num_blocks(t) // LANES) * LANES
    return pl.pallas_call(
        _route_kernel,
        grid=(1,),
        in_specs=[pl.BlockSpec(route.shape, lambda i: (0, 0))],
        out_specs=[pl.BlockSpec(route.shape, lambda i: (0, 0)),
                   pl.BlockSpec((nt, SUBLANES, 2 * TABLE_LANES), lambda i: (0, 0, 0)),
                   pl.BlockSpec((SUBLANES, LANES), lambda i: (0, 0)),
                   pl.BlockSpec((SUBLANES, nbp), lambda i: (0, 0))],
        out_shape=[jax.ShapeDtypeStruct(route.shape, jnp.int32),
                   jax.ShapeDtypeStruct((nt, SUBLANES, 2 * TABLE_LANES), jnp.int32),
                   jax.ShapeDtypeStruct((SUBLANES, LANES), jnp.int32),
                   jax.ShapeDtypeStruct((SUBLANES, nbp), jnp.int32)],
        compiler_params=pltpu.CompilerParams(
            dimension_semantics=("arbitrary",), vmem_limit_bytes=VMEM_LIMIT),
        name="route",
    )(route)


def _chunk_row(tab_ref, seg_ref, k):
    chunk = seg_ref[0, tab_ref[0, 0, TABLE_LANES + k]] + tab_ref[0, 0, k]
    return pl.multiple_of(chunk * RUN_ALIGN, RUN_ALIGN)


def _scatter_kernel(tab_ref, seg_ref, lpos_ref, x_ref, xs_ref, xsl, zrows, sem, zsem):
    lrows, ts = xsl.shape[1], x_ref.shape[0]
    n_blocks = (xs_ref.shape[0] - 2 * lrows) // EXPERT_BLOCK
    i = pl.program_id(0)
    last = pl.num_programs(0) - 1
    slot = lax.rem(i, 2)

    def drain(s):
        pltpu.make_async_copy(xsl.at[s], xs_ref.at[pl.ds(0, lrows)], sem.at[s]).wait()

    @pl.when(i >= 2)
    def _():
        drain(slot)

    j = lax.broadcasted_iota(jnp.int32, (lrows, ts), 0)
    perm = jnp.where((j == lpos_ref[0:1, :]) | (j == lpos_ref[1:2, :]), 1.0, 0.0).astype(bf16)
    xsl[slot] = jnp.dot(perm, x_ref[...], preferred_element_type=f32)

    for k in range(lrows // RUN_ALIGN):
        pltpu.make_async_copy(xsl.at[slot, pl.ds(k * RUN_ALIGN, RUN_ALIGN)],
                              xs_ref.at[pl.ds(_chunk_row(tab_ref, seg_ref, k), RUN_ALIGN)], sem.at[slot]).start()

    @pl.when(i == last)
    def _():
        drain(slot)

        @pl.when(i >= 1)
        def _():
            drain(1 - slot)

        zrows[...] = jnp.zeros(zrows.shape, zrows.dtype)

        def zcopy(row, n):
            return pltpu.make_async_copy(zrows.at[pl.ds(0, n)], xs_ref.at[pl.ds(row, n)], zsem)

        def for_each_unused_block(fn):
            def body(b, carry):
                fn(pl.multiple_of(b * EXPERT_BLOCK, EXPERT_BLOCK), EXPERT_BLOCK)
                return carry

            lax.fori_loop(seg_ref[0, SEG_NACT_LANE], n_blocks, body, 0)
            for off in range(0, 2 * lrows, EXPERT_BLOCK):
                fn(n_blocks * EXPERT_BLOCK + off, min(EXPERT_BLOCK, 2 * lrows - off))

        for_each_unused_block(lambda row, n: zcopy(row, n).start())
        for_each_unused_block(lambda row, n: zcopy(row, n).wait())

        def for_each_pad_chunk(fn):
            def seg(e, carry):
                g0 = seg_ref[0, e]
                used = seg_ref[0, N_EXPERTS + e]
                total = seg_ref[0, 2 * N_EXPERTS + e]

                def chunk(c, carry2):
                    fn(pl.multiple_of((g0 + c) * RUN_ALIGN, RUN_ALIGN), RUN_ALIGN)
                    return carry2

                lax.fori_loop(used, total, chunk, 0)
                return carry

            lax.fori_loop(0, N_EXPERTS, seg, 0)

        for_each_pad_chunk(lambda row, n: zcopy(row, n).start())
        for_each_pad_chunk(lambda row, n: zcopy(row, n).wait())


def _scatter(tab, seg, lpos, xn2, n_blocks):
    t = xn2.shape[0]
    ts = ROUTE_TILE
    lrows = _local_rows(ts)
    return pl.pallas_call(
        _scatter_kernel,
        grid=(t // ts,),
        in_specs=[
            pl.BlockSpec((1, SUBLANES, 2 * TABLE_LANES), lambda i: (i, 0, 0), memory_space=pltpu.SMEM),
            pl.BlockSpec((SUBLANES, LANES), lambda i: (0, 0), memory_space=pltpu.SMEM),
            pl.BlockSpec((SUBLANES, ts), lambda i: (0, i)),
            pl.BlockSpec((ts, D_MODEL), lambda i: (i, 0)),
        ],
        out_specs=pl.BlockSpec(memory_space=pl.ANY),
        out_shape=jax.ShapeDtypeStruct((n_blocks * EXPERT_BLOCK + 2 * lrows, D_MODEL), f32),
        scratch_shapes=[pltpu.VMEM((2, lrows, D_MODEL), f32), pltpu.VMEM((EXPERT_BLOCK, D_MODEL), f32),
                        pltpu.SemaphoreType.DMA((2,)), pltpu.SemaphoreType.DMA],
        compiler_params=pltpu.CompilerParams(
            dimension_semantics=("arbitrary",), vmem_limit_bytes=VMEM_LIMIT, has_side_effects=True),
        name="scatter",
    )(tab, seg, lpos, xn2)


def _expert_kernel(bexp_ref, nact_ref, xs_ref, wg_ref, wu_ref, wd_ref, ys_ref, wg_b, wu_b, wd_b):
    b = pl.program_id(0)
    active = b < nact_ref[0]
    new_expert = jnp.logical_or(b == 0, bexp_ref[b] != bexp_ref[jnp.maximum(b - 1, 0)])

    @pl.when(jnp.logical_not(active))
    def _():
        ys_ref[...] = jnp.zeros(ys_ref.shape, ys_ref.dtype)

    @pl.when(jnp.logical_and(active, new_expert))
    def _():
        wg_b[...] = wg_ref[0].astype(bf16)
        wu_b[...] = wu_ref[0].astype(bf16)
        wd_b[...] = wd_ref[0].astype(bf16)

    @pl.when(active)
    def _():
        x = xs_ref[...].astype(bf16)
        g = jnp.dot(x, wg_b[...], preferred_element_type=f32)
        u = jnp.dot(x, wu_b[...], preferred_element_type=f32)
        hmid = (g * _sigmoid(g) * u).astype(bf16)
        ys_ref[...] = jnp.dot(hmid, wd_b[...], preferred_element_type=f32)


def _experts(bexp, nact, xs, wg, wu, wd, nb):
    n_rows = nb * EXPERT_BLOCK

    def blk(b, bexp_ref, nact_ref):
        return jnp.minimum(b, nact_ref[0] - 1)

    grid_spec = pltpu.PrefetchScalarGridSpec(
        num_scalar_prefetch=2,
        grid=(nb,),
        in_specs=[
            pl.BlockSpec((EXPERT_BLOCK, D_MODEL), lambda b, e, n: (blk(b, e, n), 0)),
            pl.BlockSpec((1, D_MODEL, D_FF_EXPERT), lambda b, e, n: (e[blk(b, e, n)], 0, 0)),
            pl.BlockSpec((1, D_MODEL, D_FF_EXPERT), lambda b, e, n: (e[blk(b, e, n)], 0, 0)),
            pl.BlockSpec((1, D_FF_EXPERT, D_MODEL), lambda b, e, n: (e[blk(b, e, n)], 0, 0)),
        ],
        out_specs=pl.BlockSpec((EXPERT_BLOCK, D_MODEL), lambda b, e, n: (b, 0)),
        scratch_shapes=[pltpu.VMEM((D_MODEL, D_FF_EXPERT), bf16), pltpu.VMEM((D_MODEL, D_FF_EXPERT), bf16),
                        pltpu.VMEM((D_FF_EXPERT, D_MODEL), bf16)],
    )
    return pl.pallas_call(
        _expert_kernel,
        grid_spec=grid_spec,
        out_shape=jax.ShapeDtypeStruct((n_rows, D_MODEL), f32),
        compiler_params=pltpu.CompilerParams(
            dimension_semantics=("arbitrary",), vmem_limit_bytes=VMEM_LIMIT),
        name="experts",
    )(bexp, nact, xs, wg, wu, wd)


def _combine_kernel(tab_ref, tab_next_ref, seg_ref, lpos_ref, wts_ref, h_ref, gf_ref, ys_ref, out_ref, ybuf, sem):
    ts, lrows = h_ref.shape[0], ybuf.shape[1]
    i = pl.program_id(0)
    slot = lax.rem(i, 2)

    def fetch(t_ref, s):
        for k in range(lrows // RUN_ALIGN):
            pltpu.make_async_copy(ys_ref.at[pl.ds(_chunk_row(t_ref, seg_ref, k), RUN_ALIGN)],
                                  ybuf.at[s, pl.ds(k * RUN_ALIGN, RUN_ALIGN)], sem.at[s]).start()

    @pl.when(i == 0)
    def _():
        fetch(tab_ref, 0)

    @pl.when(i + 1 < pl.num_programs(0))
    def _():
        fetch(tab_next_ref, 1 - slot)

    pltpu.make_async_copy(ys_ref.at[pl.ds(0, lrows)], ybuf.at[slot], sem.at[slot]).wait()

    info = jnp.concatenate([lpos_ref[...].astype(f32), wts_ref[...],
                            jnp.zeros((LANES - 2 * SUBLANES, ts), f32)], axis=0).T
    jl = lax.broadcasted_iota(jnp.int32, (ts, lrows), 1).astype(f32)
    mix = (jnp.where(jl == info[:, 0:1], info[:, SUBLANES:SUBLANES + 1], 0.0)
           + jnp.where(jl == info[:, 1:2], info[:, SUBLANES + 1:SUBLANES + 2], 0.0)).astype(bf16)
    moe = jnp.dot(mix, ybuf[slot].astype(bf16), preferred_element_type=f32)
    out_ref[...] = _rms(h_ref[...] + moe, gf_ref[...])


def _combine(tab, seg, lpos, wts, h, g_final, ys):
    t = h.shape[0]
    ts = ROUTE_TILE
    nt = t // ts
    lrows = _local_rows(ts)
    assert ys.shape[0] >= lrows
    return pl.pallas_call(
        _combine_kernel,
        grid=(nt,),
        in_specs=[
            pl.BlockSpec((1, SUBLANES, 2 * TABLE_LANES), lambda i: (i, 0, 0), memory_space=pltpu.SMEM),
            pl.BlockSpec((1, SUBLANES, 2 * TABLE_LANES), lambda i: (jnp.minimum(i + 1, nt - 1), 0, 0),
                         memory_space=pltpu.SMEM),
            pl.BlockSpec((SUBLANES, LANES), lambda i: (0, 0), memory_space=pltpu.SMEM),
            pl.BlockSpec((SUBLANES, ts), lambda i: (0, i)),
            pl.BlockSpec((SUBLANES, ts), lambda i: (0, i)),
            pl.BlockSpec((ts, D_MODEL), lambda i: (i, 0)),
            pl.BlockSpec((1, D_MODEL), lambda i: (0, 0)),
            pl.BlockSpec(memory_space=pl.ANY),
        ],
        out_specs=pl.BlockSpec((ts, D_MODEL), lambda i: (i, 0)),
        out_shape=jax.ShapeDtypeStruct((t, D_MODEL), f32),
        scratch_shapes=[pltpu.VMEM((2, lrows, D_MODEL), f32), pltpu.SemaphoreType.DMA((2,))],
        compiler_params=pltpu.CompilerParams(
            dimension_semantics=("arbitrary",), vmem_limit_bytes=VMEM_LIMIT),
        name="combine",
    )(tab, tab, seg, lpos, wts, h, g_final, ys)


def _prep_w_in(w_in):
    c0 = 2 * CONV_CH
    c1 = c0 + Q_DIM
    c2 = c1 + KV_DIM
    c3 = c2 + KV_DIM
    wk = w_in[:, c1:c2]
    wv = w_in[:, c2:c3]

    def dup(w):
        return jnp.concatenate([w[:, h * HEAD_DIM:(h + 1) * HEAD_DIM] for h in range(N_KV_HEADS)
                                for _ in range(2)], axis=1)

    return jnp.concatenate(
        [w_in[:, :c0], w_in[:, c0:c1] * (HEAD_DIM ** -0.5), dup(wk), dup(wv), w_in[:, c3:]],
        axis=1).astype(bf16)


def _prep_router(w_group, b_group, w_expert, b_expert):
    d = w_group.shape[0]
    w = jnp.zeros((d, LANES), f32)
    w = w.at[:, :N_GROUPS].set(w_group).at[:, SUBLANES:ROUTER_ROWS].set(w_expert)
    b = jnp.full((1, LANES), NEG_BIG, f32)
    b = b.at[0, :N_GROUPS].set(b_group).at[0, SUBLANES:ROUTER_ROWS].set(b_expert)
    w_hi = w.astype(bf16)
    w_lo = (w - w_hi.astype(f32)).astype(bf16)
    return w_hi, w_lo, b


def kernel(x, g_mix, w_in, w_dw, b_dw, ln_conv_g, ln_conv_b, sinks, w_conv_out, w_attn_out, w_out, g_ffn,
           w_group, b_group, w_expert, b_expert, w_gate, w_up, w_down, g_final):
    batch, seq, d = x.shape
    assert d == D_MODEL and seq % TOKEN_TILE == 0 and g_mix.shape[0] == 1
    t = batch * seq
    x2 = x.reshape(t, d)

    act, q, kv, sgates = _inproj(x2, g_mix[0][None, :], _prep_w_in(w_in[0]), w_dw[0, :, 0, :], b_dw[0][None, :],
                                 ln_conv_g[0][None, :], ln_conv_b[0][None, :], seq)

    wr_hi, wr_lo, b_r = _prep_router(w_group[0], b_group[0], w_expert[0], b_expert[0])
    h, xn2, route, wts = _mixer(
        x2, act, q, kv, sgates, sinks[0], w_conv_out[0].astype(bf16), w_attn_out[0].astype(bf16),
        w_out[0].astype(bf16), g_ffn[0][None, :], wr_hi, wr_lo, b_r, batch, seq)

    lpos, tab, seg, meta = _route(route)
    nb = _num_blocks(t)
    xs = _scatter(tab, seg, lpos, xn2, nb)
    ys = _experts(meta[0, :nb], meta[1, :1], xs, w_gate[0], w_up[0], w_down[0], nb)
    seg_back = seg.at[:, SEG_SPARE_LANE:SEG_SPARE_LANE + 2].set(0)
    out = _combine(tab, seg_back, lpos, wts, h, g_final[None, :], ys)
    return out.reshape(batch, seq, d)
```

```python
import functools

import numpy as np
import jax
import jax.numpy as jnp
from jax import lax
from jax.experimental import pallas as pl
from jax.experimental.pallas import tpu as pltpu

D_MODEL = 1024
CONV_CH = 512
CONV_WIDTH = 31
N_HEADS = 8
N_KV_HEADS = 2
HEAD_DIM = 64
ATTN_BLOCK = 128
N_GROUPS = 4
EXPERTS_PER_GROUP = 8
N_EXPERTS = N_GROUPS * EXPERTS_PER_GROUP
D_FF_EXPERT = 512
NORM_EPS = 1e-6

Q_DIM = N_HEADS * HEAD_DIM
KV_DIM = N_KV_HEADS * HEAD_DIM

LANES = 128
SUBLANES = 8
CONV_HALO = 32
CONV_ROWS = 32
PROJ_PIECE = 256
ROUTER_ROWS = SUBLANES + N_EXPERTS
NEG_BIG = -1e30

TOKEN_TILE = 512
ROUTE_TILE = 512
RUN_ALIGN = SUBLANES
EXPERT_BLOCK = 512
VMEM_LIMIT = 56 * 1024 * 1024

f32 = jnp.float32
bf16 = jnp.bfloat16


def _rms(x, g):
    ms = jnp.mean(x * x, axis=-1, keepdims=True)
    return x * lax.rsqrt(ms + NORM_EPS) * g


def _sigmoid(x):
    return 1.0 / (1.0 + jnp.exp(-x))


def _inproj_kernel(tiles_per_seq, x_ref, g_ref, w_ref, wdw_ref, bdw_ref, lng_ref, lnb_ref,
                   act_ref, q_ref, kv_ref, sgate_ref, xn_s, vbuf):
    tm = x_ref.shape[0]
    xn_s[...] = _rms(x_ref[...], g_ref[...]).astype(bf16)

    def proj(lo, hi):
        return jnp.dot(xn_s[...], w_ref[:, lo:hi], preferred_element_type=f32)

    n_cb = CONV_CH // LANES
    first = lax.rem(pl.program_id(0), tiles_per_seq) == 0

    @pl.when(first)
    def _():
        vbuf[:, 0:CONV_HALO, :] = jnp.zeros((n_cb, CONV_HALO, LANES), f32)

    @pl.when(jnp.logical_not(first))
    def _():
        vbuf[:, 0:CONV_HALO, :] = vbuf[:, tm:tm + CONV_HALO, :]

    for b in range(n_cb):
        u = proj(2 * b * LANES, 2 * (b + 1) * LANES)
        vbuf[b, CONV_HALO:, :] = u[:, :LANES] * _sigmoid(u[:, LANES:])

    rows = CONV_ROWS
    tap0 = CONV_HALO - (CONV_WIDTH - 1)
    for c in range(tm // rows):
        accs = []
        for b in range(n_cb):
            acc = jnp.broadcast_to(bdw_ref[:, b * LANES:(b + 1) * LANES], (rows, LANES))
            for j in range(CONV_WIDTH):
                r0 = c * rows + tap0 + j
                acc = acc + wdw_ref[j:j + 1, b * LANES:(b + 1) * LANES] * vbuf[b, r0:r0 + rows, :]
            accs.append(acc)
        mu = sum(jnp.sum(a, axis=-1, keepdims=True) for a in accs) * (1.0 / CONV_CH)
        ds = [a - mu for a in accs]
        var = sum(jnp.sum(d * d, axis=-1, keepdims=True) for d in ds) * (1.0 / CONV_CH)
        inv = lax.rsqrt(var + NORM_EPS)
        for b in range(n_cb):
            y = ds[b] * inv * lng_ref[:, b * LANES:(b + 1) * LANES] + lnb_ref[:, b * LANES:(b + 1) * LANES]
            act_ref[c * rows:(c + 1) * rows, b * LANES:(b + 1) * LANES] = (y * _sigmoid(y)).astype(bf16)

    col = 2 * CONV_CH
    for out_ref, fn in ((q_ref, lambda z: z), (kv_ref, lambda z: z), (sgate_ref, _sigmoid)):
        for lo in range(0, out_ref.shape[1], PROJ_PIECE):
            out_ref[:, lo:lo + PROJ_PIECE] = fn(proj(col + lo, col + lo + PROJ_PIECE)).astype(bf16)
        col += out_ref.shape[1]


def _inproj(x2, g_mix, w_cat, w_dw, b_dw, ln_g, ln_b, seq):
    t = x2.shape[0]
    tm = TOKEN_TILE
    widths = (CONV_CH, Q_DIM, 4 * KV_DIM, 2 * D_MODEL)

    def full(a):
        return pl.BlockSpec(a.shape, lambda i: (0,) * a.ndim)

    return pl.pallas_call(
        functools.partial(_inproj_kernel, seq // tm),
        grid=(t // tm,),
        in_specs=[pl.BlockSpec((tm, D_MODEL), lambda i: (i, 0)),
                  full(g_mix), full(w_cat), full(w_dw), full(b_dw), full(ln_g), full(ln_b)],
        out_specs=[pl.BlockSpec((tm, w), lambda i: (i, 0)) for w in widths],
        out_shape=[jax.ShapeDtypeStruct((t, w), bf16) for w in widths],
        scratch_shapes=[pltpu.VMEM((tm, D_MODEL), bf16),
                        pltpu.VMEM((CONV_CH // LANES, tm + CONV_HALO, LANES), f32)],
        compiler_params=pltpu.CompilerParams(
            dimension_semantics=("arbitrary",), vmem_limit_bytes=VMEM_LIMIT),
        name="inproj",
    )(x2, g_mix, w_cat, w_dw, b_dw, ln_g, ln_b)


def _mixer_kernel(x_ref, act_ref, q_ref, kv_ref, kvp_ref, sgate_ref, sink_ref, bias_ref, wco_ref, wao_ref,
                  wo_ref, gffn_ref, wrh_ref, wrl_ref, br_ref,
                  h_ref, xn2_ref, route_ref, wts_ref,
                  kvall, attn, s_scr, m_scr):
    ts = x_ref.shape[0]
    first = pl.program_id(1) == 0

    kvall[0:ATTN_BLOCK, :] = jnp.where(first, jnp.zeros_like(kvp_ref[...]), kvp_ref[...])
    kvall[ATTN_BLOCK:, :] = kv_ref[...]
    nkeys = 2 * ATTN_BLOCK
    left_kv = lax.broadcasted_iota(jnp.int32, (nkeys, LANES), 1) < HEAD_DIM
    left_q = lax.broadcasted_iota(jnp.int32, (ATTN_BLOCK, LANES), 1) < HEAD_DIM
    key_lane = lax.broadcasted_iota(jnp.int32, (ATTN_BLOCK, 2 * nkeys), 1)
    prev_keys = (key_lane % nkeys) < ATTN_BLOCK
    no_prev = jnp.where(prev_keys, jnp.where(first, -jnp.inf, 0.0), 0.0)

    def block_diag(x):
        z = jnp.zeros_like(x)
        return jnp.concatenate([jnp.where(left_kv, x, z), jnp.where(left_kv, z, x)], axis=0)

    n_pairs = N_HEADS // 2
    for j in range(ts // ATTN_BLOCK):
        r0 = j * ATTN_BLOCK
        for kvh in range(N_KV_HEADS):
            kbd = block_diag(kvall[r0:r0 + nkeys, kvh * LANES:(kvh + 1) * LANES])
            for ii in range(2):
                i = 2 * kvh + ii
                qb = q_ref[r0:r0 + ATTN_BLOCK, i * LANES:(i + 1) * LANES]
                s = lax.dot_general(qb, kbd, (((1,), (1,)), ((), ())), preferred_element_type=f32)
                s = s + bias_ref[i]
                if j == 0:
                    s = s + no_prev
                s_scr[j * n_pairs + i] = s
                for half in range(2):
                    m = jnp.max(s[:, half * nkeys:(half + 1) * nkeys], axis=-1, keepdims=True)
                    m_scr[2 * (j * n_pairs + i) + half] = jnp.maximum(m, sink_ref[2 * i + half])

    for j in range(ts // ATTN_BLOCK):
        r0 = j * ATTN_BLOCK
        for kvh in range(N_KV_HEADS):
            vbd = block_diag(kvall[r0:r0 + nkeys, (N_KV_HEADS + kvh) * LANES:(N_KV_HEADS + kvh + 1) * LANES])
            for ii in range(2):
                i = 2 * kvh + ii
                s = s_scr[j * n_pairs + i]
                ps, ls = [], []
                for half in range(2):
                    m = m_scr[2 * (j * n_pairs + i) + half]
                    p = jnp.exp(s[:, half * nkeys:(half + 1) * nkeys] - m)
                    ps.append(p)
                    ls.append(jnp.sum(p, axis=-1, keepdims=True) + jnp.exp(sink_ref[2 * i + half] - m))
                p = jnp.concatenate(ps, axis=1).astype(bf16)
                o = jnp.dot(p, vbd, preferred_element_type=f32)
                o = o * jnp.where(left_q, 1.0 / ls[0], 1.0 / ls[1])
                attn[r0:r0 + ATTN_BLOCK, i * LANES:(i + 1) * LANES] = o.astype(bf16)

    conv_o = jnp.dot(act_ref[...], wco_ref[...], preferred_element_type=f32)
    attn_o = jnp.dot(attn[...], wao_ref[...], preferred_element_type=f32)
    merged = (sgate_ref[:, :D_MODEL].astype(f32) * conv_o
              + sgate_ref[:, D_MODEL:].astype(f32) * attn_o).astype(bf16)
    h = x_ref[...] + jnp.dot(merged, wo_ref[...], preferred_element_type=f32)
    h_ref[...] = h

    xn2 = _rms(h, gffn_ref[...])
    xn2_ref[...] = xn2.astype(bf16)
    xh = xn2.astype(bf16)
    xl = (xn2 - xh.astype(f32)).astype(bf16)
    logits = (jnp.dot(xh, wrh_ref[...], preferred_element_type=f32)
              + jnp.dot(xh, wrl_ref[...], preferred_element_type=f32)
              + jnp.dot(xl, wrh_ref[...], preferred_element_type=f32)) + br_ref[...]
    lt = logits.T
    sub = lax.broadcasted_iota(jnp.int32, (SUBLANES, ts), 0)

    gl = lt[0:SUBLANES]
    gmax = jnp.max(gl, axis=0, keepdims=True)
    gsel = jnp.min(jnp.where(gl == gmax, sub, SUBLANES), axis=0, keepdims=True)
    p_group = 1.0 / jnp.sum(jnp.exp(gl - gmax), axis=0, keepdims=True)

    e_in = lt[SUBLANES:2 * SUBLANES]
    for g in range(1, N_GROUPS):
        e_in = jnp.where(gsel == g, lt[(g + 1) * SUBLANES:(g + 2) * SUBLANES], e_in)
    m1 = jnp.max(e_in, axis=0, keepdims=True)
    i1 = jnp.min(jnp.where(e_in == m1, sub, SUBLANES), axis=0, keepdims=True)
    rest = jnp.where(sub == i1, -jnp.inf, e_in)
    m2 = jnp.max(rest, axis=0, keepdims=True)
    i2 = jnp.min(jnp.where(rest == m2, sub, SUBLANES), axis=0, keepdims=True)
    t2 = jnp.exp(m2 - m1)
    w1 = p_group / (1.0 + t2)
    w2 = p_group * t2 / (1.0 + t2)
    base = gsel * EXPERTS_PER_GROUP
    route_ref[...] = jnp.where(sub == 0, base + i1, jnp.where(sub == 1, base + i2, 0))
    wts_ref[...] = jnp.where(sub == 0, w1, jnp.where(sub == 1, w2, 0.0))


def _attn_bias():
    qi = np.arange(ATTN_BLOCK)[:, None]
    kj = np.arange(2 * ATTN_BLOCK)[None, :]
    rel = (ATTN_BLOCK + qi - kj).astype(np.float32)
    ok = (rel >= 0) & (rel < ATTN_BLOCK)
    slopes = np.array([2.0 ** (-8.0 * (h + 1) / N_HEADS) for h in range(N_HEADS)], np.float32)
    per_head = [np.where(ok, -(slopes[h] * rel), -np.inf).astype(np.float32) for h in range(N_HEADS)]
    return np.stack([np.concatenate([per_head[2 * i], per_head[2 * i + 1]], axis=1)
                     for i in range(N_HEADS // 2)])


def _mixer(x2, act, q, kv, sgates, sinks, wco, wao, wo, g_ffn, wr_hi, wr_lo, b_r, batch, seq):
    t = x2.shape[0]
    ts = TOKEN_TILE
    ns = seq // ts
    bias = jnp.asarray(_attn_bias())

    def row(b, s):
        return b * ns + s

    def full(a):
        return pl.BlockSpec(a.shape, lambda b, s: (0,) * a.ndim)

    in_specs = [
        pl.BlockSpec((ts, D_MODEL), lambda b, s: (row(b, s), 0)),
        pl.BlockSpec((ts, CONV_CH), lambda b, s: (row(b, s), 0)),
        pl.BlockSpec((ts, Q_DIM), lambda b, s: (row(b, s), 0)),
        pl.BlockSpec((ts, 4 * KV_DIM), lambda b, s: (row(b, s), 0)),
        pl.BlockSpec((ATTN_BLOCK, 4 * KV_DIM),
                     lambda b, s: (jnp.maximum(row(b, s) * (ts // ATTN_BLOCK) - 1, 0), 0)),
        pl.BlockSpec((ts, 2 * D_MODEL), lambda b, s: (row(b, s), 0)),
        pl.BlockSpec(memory_space=pltpu.SMEM),
        full(bias), full(wco), full(wao), full(wo), full(g_ffn), full(wr_hi), full(wr_lo), full(b_r),
    ]
    out_specs = [
        pl.BlockSpec((ts, D_MODEL), lambda b, s: (row(b, s), 0)),
        pl.BlockSpec((ts, D_MODEL), lambda b, s: (row(b, s), 0)),
        pl.BlockSpec((SUBLANES, ts), lambda b, s: (0, row(b, s))),
        pl.BlockSpec((SUBLANES, ts), lambda b, s: (0, row(b, s))),
    ]
    out_shape = [
        jax.ShapeDtypeStruct((t, D_MODEL), f32),
        jax.ShapeDtypeStruct((t, D_MODEL), bf16),
        jax.ShapeDtypeStruct((SUBLANES, t), jnp.int32),
        jax.ShapeDtypeStruct((SUBLANES, t), f32),
    ]
    return pl.pallas_call(
        _mixer_kernel,
        grid=(batch, ns),
        in_specs=in_specs,
        out_specs=out_specs,
        out_shape=out_shape,
        scratch_shapes=[
            pltpu.VMEM((ts + ATTN_BLOCK, 4 * KV_DIM), bf16),
            pltpu.VMEM((ts, Q_DIM), bf16),
            pltpu.VMEM((ts // ATTN_BLOCK * (N_HEADS // 2), ATTN_BLOCK, 4 * ATTN_BLOCK), f32),
            pltpu.VMEM((ts // ATTN_BLOCK * N_HEADS, ATTN_BLOCK, 1), f32),
        ],
        compiler_params=pltpu.CompilerParams(
            dimension_semantics=("arbitrary", "arbitrary"), vmem_limit_bytes=VMEM_LIMIT),
        name="mixer",
    )(x2, act, q, kv, kv, sgates, sinks, bias, wco, wao, wo, g_ffn, wr_hi, wr_lo, b_r)


def _local_rows(ts):
    return -(-(2 * ts + (RUN_ALIGN - 1) * N_EXPERTS) // LANES) * LANES


TABLE_LANES = 2 * LANES
SEG_NACT_LANE = 3 * N_EXPERTS
SEG_SPARE_LANE = 3 * N_EXPERTS + 1


def _num_blocks(t):
    run_rows = 2 * t + (RUN_ALIGN - 1) * N_EXPERTS * (t // ROUTE_TILE)
    return -(-(run_rows + N_EXPERTS * (EXPERT_BLOCK - RUN_ALIGN)) // EXPERT_BLOCK)


def _route_kernel(route_ref, lpos_ref, tab_ref, seg_ref, meta_ref):
    t = route_ref.shape[1]
    tr = ROUTE_TILE
    nbp = meta_ref.shape[1]
    chunks_per_block = EXPERT_BLOCK // RUN_ALIGN
    eiota = lax.broadcasted_iota(jnp.int32, (N_EXPERTS, tr), 0)
    before = (lax.broadcasted_iota(jnp.int32, (tr, tr), 0)
              < lax.broadcasted_iota(jnp.int32, (tr, tr), 1)).astype(bf16)
    lower = (lax.broadcasted_iota(jnp.int32, (N_EXPERTS, N_EXPERTS), 1)
             < lax.broadcasted_iota(jnp.int32, (N_EXPERTS, N_EXPERTS), 0)).astype(bf16)
    sub = lax.broadcasted_iota(jnp.int32, (N_EXPERTS, LANES), 0)
    lane = lax.broadcasted_iota(jnp.int32, (N_EXPERTS, LANES), 1)

    def to_lanes(col, offset):
        return jnp.sum(jnp.where(sub + offset == lane, col, 0.0), axis=0, keepdims=True)

    def expert_prefix(col):
        b = jnp.broadcast_to(col, (N_EXPERTS, LANES))
        hi = jnp.floor(b * (1.0 / 16.0))
        lo = b - 16.0 * hi
        return (16.0 * jnp.dot(lower, hi.astype(bf16), preferred_element_type=f32)
                + jnp.dot(lower, lo.astype(bf16), preferred_element_type=f32))[:, 0:1]

    lpos_ref[...] = jnp.zeros(lpos_ref.shape, jnp.int32)
    chunk_id = lax.broadcasted_iota(jnp.int32, (N_EXPERTS, TABLE_LANES), 1).astype(f32)
    chunk_expert = lax.broadcasted_iota(jnp.int32, (N_EXPERTS, TABLE_LANES), 0).astype(f32)

    def step(i, seen_chunks):
        off = pl.multiple_of(i * tr, tr)
        m1 = eiota == route_ref[0:1, pl.ds(off, tr)]
        m2 = eiota == route_ref[1:2, pl.ds(off, tr)]
        onehot = jnp.where(m1 | m2, 1.0, 0.0)
        within = jnp.dot(onehot.astype(bf16), before, preferred_element_type=f32)
        run_chunks = jnp.floor((jnp.sum(onehot, axis=1, keepdims=True) + (RUN_ALIGN - 1)) * (1.0 / RUN_ALIGN))
        run_start = expert_prefix(run_chunks)
        pos = within + RUN_ALIGN * run_start
        lpos_ref[0:1, pl.ds(off, tr)] = jnp.sum(jnp.where(m1, pos, 0.0), axis=0, keepdims=True).astype(jnp.int32)
        lpos_ref[1:2, pl.ds(off, tr)] = jnp.sum(jnp.where(m2, pos, 0.0), axis=0, keepdims=True).astype(jnp.int32)
        owner = (run_start <= chunk_id) & (chunk_id < run_start + run_chunks)
        rel = jnp.sum(jnp.where(owner, seen_chunks + chunk_id - run_start, 0.0), axis=0, keepdims=True)
        eid = jnp.sum(jnp.where(owner, chunk_expert, 0.0), axis=0, keepdims=True)
        n_used = jnp.sum(run_chunks, axis=0, keepdims=True)
        unused = chunk_id[0:1] >= n_used
        rel = jnp.where(unused, chunk_id[0:1] - n_used, rel)
        eid = jnp.where(unused, (SEG_SPARE_LANE + lax.rem(i, 2)).astype(f32), eid)
        row = jnp.concatenate([rel, eid], axis=1)
        tab_ref[i] = jnp.broadcast_to(row, (SUBLANES, 2 * TABLE_LANES)).astype(jnp.int32)
        return seen_chunks + run_chunks

    used_chunks = lax.fori_loop(0, t // tr, step, jnp.zeros((N_EXPERTS, 1), f32))

    nblk = jnp.floor((used_chunks + (chunks_per_block - 1)) * (1.0 / chunks_per_block))
    first_blk = expert_prefix(nblk)
    nact = jnp.sum(nblk, axis=0, keepdims=True)
    spare_chunk = _num_blocks(t) * chunks_per_block
    seg_row = (to_lanes(first_blk * chunks_per_block, 0) + to_lanes(used_chunks, N_EXPERTS)
               + to_lanes(nblk * chunks_per_block, 2 * N_EXPERTS)
               + jnp.where(lane[0:1] == SEG_NACT_LANE, nact, 0.0)
               + jnp.where(lane[0:1] == SEG_SPARE_LANE, float(spare_chunk), 0.0)
               + jnp.where(lane[0:1] == SEG_SPARE_LANE + 1, float(spare_chunk + _local_rows(tr) // RUN_ALIGN), 0.0))
    seg_ref[...] = jnp.broadcast_to(seg_row, (SUBLANES, LANES)).astype(jnp.int32)

    blk = lax.broadcasted_iota(jnp.int32, (N_EXPERTS, nbp), 1).astype(f32)
    owner = (first_blk <= blk) & (blk < first_blk + nblk)
    expert_id = lax.broadcasted_iota(jnp.int32, (N_EXPERTS, nbp), 0).astype(f32)
    bexp = jnp.sum(jnp.where(owner, expert_id, 0.0), axis=0, keepdims=True)
    row8 = lax.broadcasted_iota(jnp.int32, (SUBLANES, nbp), 0)
    meta_ref[...] = jnp.where(row8 == 0, bexp, jnp.where(row8 == 1, nact, 0.0)).astype(jnp.int32)


def _route(route):
    t = route.shape[1]
    nt = t // ROUTE_TILE
    nbp = -(-_num_blocks(t) // LANES) * LANES
    return pl.pallas_call(
        _route_kernel,
        grid=(1,),
        in_specs=[pl.BlockSpec(route.shape, lambda i: (0, 0))],
        out_specs=[pl.BlockSpec(route.shape, lambda i: (0, 0)),
                   pl.BlockSpec((nt, SUBLANES, 2 * TABLE_LANES), lambda i: (0, 0, 0)),
                   pl.BlockSpec((SUBLANES, LANES), lambda i: (0, 0)),
                   pl.BlockSpec((SUBLANES, nbp), lambda i: (0, 0))],
        out_shape=[jax.ShapeDtypeStruct(route.shape, jnp.int32),
                   jax.ShapeDtypeStruct((nt, SUBLANES, 2 * TABLE_LANES), jnp.int32),
                   jax.ShapeDtypeStruct((SUBLANES, LANES), jnp.int32),
                   jax.ShapeDtypeStruct((SUBLANES, nbp), jnp.int32)],
        compiler_params=pltpu.CompilerParams(
            dimension_semantics=("arbitrary",), vmem_limit_bytes=VMEM_LIMIT),
        name="route",
    )(route)


def _chunk_row(tab_ref, seg_ref, k):
    chunk = seg_ref[0, tab_ref[0, 0, TABLE_LANES + k]] + tab_ref[0, 0, k]
    return pl.multiple_of(chunk * RUN_ALIGN, RUN_ALIGN)


def _scatter_kernel(tab_ref, seg_ref, lpos_ref, x_ref, xs_ref, xsl, zrows, sem, zsem):
    lrows, ts = xsl.shape[1], x_ref.shape[0]
    n_blocks = (xs_ref.shape[0] - 2 * lrows) // EXPERT_BLOCK
    i = pl.program_id(0)
    last = pl.num_programs(0) - 1
    slot = lax.rem(i, 2)

    def drain(s):
        pltpu.make_async_copy(xsl.at[s], xs_ref.at[pl.ds(0, lrows)], sem.at[s]).wait()

    @pl.when(i >= 2)
    def _():
        drain(slot)

    j = lax.broadcasted_iota(jnp.int32, (lrows, ts), 0)
    perm = jnp.where((j == lpos_ref[0:1, :]) | (j == lpos_ref[1:2, :]), 1.0, 0.0).astype(bf16)
    xsl[slot] = jnp.dot(perm, x_ref[...], preferred_element_type=f32)

    for k in range(lrows // RUN_ALIGN):
        pltpu.make_async_copy(xsl.at[slot, pl.ds(k * RUN_ALIGN, RUN_ALIGN)],
                              xs_ref.at[pl.ds(_chunk_row(tab_ref, seg_ref, k), RUN_ALIGN)], sem.at[slot]).start()

    @pl.when(i == last)
    def _():
        drain(slot)

        @pl.when(i >= 1)
        def _():
            drain(1 - slot)

        zrows[...] = jnp.zeros(zrows.shape, zrows.dtype)

        def zcopy(row, n):
            return pltpu.make_async_copy(zrows.at[pl.ds(0, n)], xs_ref.at[pl.ds(row, n)], zsem)

        def for_each_unused_block(fn):
            def body(b, carry):
                fn(pl.multiple_of(b * EXPERT_BLOCK, EXPERT_BLOCK), EXPERT_BLOCK)
                return carry

            lax.fori_loop(seg_ref[0, SEG_NACT_LANE], n_blocks, body, 0)
            for off in range(0, 2 * lrows, EXPERT_BLOCK):
                fn(n_blocks * EXPERT_BLOCK + off, min(EXPERT_BLOCK, 2 * lrows - off))

        for_each_unused_block(lambda row, n: zcopy(row, n).start())
        for_each_unused_block(lambda row, n: zcopy(row, n).wait())

        def for_each_pad_chunk(fn):
            def seg(e, carry):
                g0 = seg_ref[0, e]
                used = seg_ref[0, N_EXPERTS + e]
                total = seg_ref[0, 2 * N_EXPERTS + e]

                def chunk(c, carry2):
                    fn(pl.multiple_of((g0 + c) * RUN_ALIGN, RUN_ALIGN), RUN_ALIGN)
                    return carry2

                lax.fori_loop(used, total, chunk, 0)
                return carry

            lax.fori_loop(0, N_EXPERTS, seg, 0)

        for_each_pad_chunk(lambda row, n: zcopy(row, n).start())
        for_each_pad_chunk(lambda row, n: zcopy(row, n).wait())


def _scatter(tab, seg, lpos, xn2, n_blocks):
    t = xn2.shape[0]
    ts = ROUTE_TILE
    lrows = _local_rows(ts)
    return pl.pallas_call(
        _scatter_kernel,
        grid=(t // ts,),
        in_specs=[
            pl.BlockSpec((1, SUBLANES, 2 * TABLE_LANES), lambda i: (i, 0, 0), memory_space=pltpu.SMEM),
            pl.BlockSpec((SUBLANES, LANES), lambda i: (0, 0), memory_space=pltpu.SMEM),
            pl.BlockSpec((SUBLANES, ts), lambda i: (0, i)),
            pl.BlockSpec((ts, D_MODEL), lambda i: (i, 0)),
        ],
        out_specs=pl.BlockSpec(memory_space=pl.ANY),
        out_shape=jax.ShapeDtypeStruct((n_blocks * EXPERT_BLOCK + 2 * lrows, D_MODEL), f32),
        scratch_shapes=[pltpu.VMEM((2, lrows, D_MODEL), f32), pltpu.VMEM((EXPERT_BLOCK, D_MODEL), f32),
                        pltpu.SemaphoreType.DMA((2,)), pltpu.SemaphoreType.DMA],
        compiler_params=pltpu.CompilerParams(
            dimension_semantics=("arbitrary",), vmem_limit_bytes=VMEM_LIMIT, has_side_effects=True),
        name="scatter",
    )(tab, seg, lpos, xn2)


def _expert_kernel(bexp_ref, nact_ref, xs_ref, wg_ref, wu_ref, wd_ref, ys_ref, wg_b, wu_b, wd_b):
    b = pl.program_id(0)
    active = b < nact_ref[0]
    new_expert = jnp.logical_or(b == 0, bexp_ref[b] != bexp_ref[jnp.maximum(b - 1, 0)])

    @pl.when(jnp.logical_not(active))
    def _():
        ys_ref[...] = jnp.zeros(ys_ref.shape, ys_ref.dtype)

    @pl.when(jnp.logical_and(active, new_expert))
    def _():
        wg_b[...] = wg_ref[0].astype(bf16)
        wu_b[...] = wu_ref[0].astype(bf16)
        wd_b[...] = wd_ref[0].astype(bf16)

    @pl.when(active)
    def _():
        x = xs_ref[...].astype(bf16)
        g = jnp.dot(x, wg_b[...], preferred_element_type=f32)
        u = jnp.dot(x, wu_b[...], preferred_element_type=f32)
        hmid = (g * _sigmoid(g) * u).astype(bf16)
        ys_ref[...] = jnp.dot(hmid, wd_b[...], preferred_element_type=f32)


def _experts(bexp, nact, xs, wg, wu, wd, nb):
    n_rows = nb * EXPERT_BLOCK

    def blk(b, bexp_ref, nact_ref):
        return jnp.minimum(b, nact_ref[0] - 1)

    grid_spec = pltpu.PrefetchScalarGridSpec(
        num_scalar_prefetch=2,
        grid=(nb,),
        in_specs=[
            pl.BlockSpec((EXPERT_BLOCK, D_MODEL), lambda b, e, n: (blk(b, e, n), 0)),
            pl.BlockSpec((1, D_MODEL, D_FF_EXPERT), lambda b, e, n: (e[blk(b, e, n)], 0, 0)),
            pl.BlockSpec((1, D_MODEL, D_FF_EXPERT), lambda b, e, n: (e[blk(b, e, n)], 0, 0)),
            pl.BlockSpec((1, D_FF_EXPERT, D_MODEL), lambda b, e, n: (e[blk(b, e, n)], 0, 0)),
        ],
        out_specs=pl.BlockSpec((EXPERT_BLOCK, D_MODEL), lambda b, e, n: (b, 0)),
        scratch_shapes=[pltpu.VMEM((D_MODEL, D_FF_EXPERT), bf16), pltpu.VMEM((D_MODEL, D_FF_EXPERT), bf16),
                        pltpu.VMEM((D_FF_EXPERT, D_MODEL), bf16)],
    )
    return pl.pallas_call(
        _expert_kernel,
        grid_spec=grid_spec,
        out_shape=jax.ShapeDtypeStruct((n_rows, D_MODEL), f32),
        compiler_params=pltpu.CompilerParams(
            dimension_semantics=("arbitrary",), vmem_limit_bytes=VMEM_LIMIT),
        name="experts",
    )(bexp, nact, xs, wg, wu, wd)


def _combine_kernel(tab_ref, tab_next_ref, seg_ref, lpos_ref, wts_ref, h_ref, gf_ref, ys_ref, out_ref, ybuf, sem):
    ts, lrows = h_ref.shape[0], ybuf.shape[1]
    i = pl.program_id(0)
    slot = lax.rem(i, 2)

    def fetch(t_ref, s):
        for k in range(lrows // RUN_ALIGN):
            pltpu.make_async_copy(ys_ref.at[pl.ds(_chunk_row(t_ref, seg_ref, k), RUN_ALIGN)],
                                  ybuf.at[s, pl.ds(k * RUN_ALIGN, RUN_ALIGN)], sem.at[s]).start()

    @pl.when(i == 0)
    def _():
        fetch(tab_ref, 0)

    @pl.when(i + 1 < pl.num_programs(0))
    def _():
        fetch(tab_next_ref, 1 - slot)

    pltpu.make_async_copy(ys_ref.at[pl.ds(0, lrows)], ybuf.at[slot], sem.at[slot]).wait()

    info = jnp.concatenate([lpos_ref[...].astype(f32), wts_ref[...],
                            jnp.zeros((LANES - 2 * SUBLANES, ts), f32)], axis=0).T
    jl = lax.broadcasted_iota(jnp.int32, (ts, lrows), 1).astype(f32)
    mix = (jnp.where(jl == info[:, 0:1], info[:, SUBLANES:SUBLANES + 1], 0.0)
           + jnp.where(jl == info[:, 1:2], info[:, SUBLANES + 1:SUBLANES + 2], 0.0)).astype(bf16)
    moe = jnp.dot(mix, ybuf[slot].astype(bf16), preferred_element_type=f32)
    out_ref[...] = _rms(h_ref[...] + moe, gf_ref[...])


def _combine(tab, seg, lpos, wts, h, g_final, ys):
    t = h.shape[0]
    ts = ROUTE_TILE
    nt = t // ts
    lrows = _local_rows(ts)
    assert ys.shape[0] >= lrows
    return pl.pallas_call(
        _combine_kernel,
        grid=(nt,),
        in_specs=[
            pl.BlockSpec((1, SUBLANES, 2 * TABLE_LANES), lambda i: (i, 0, 0), memory_space=pltpu.SMEM),
            pl.BlockSpec((1, SUBLANES, 2 * TABLE_LANES), lambda i: (jnp.minimum(i + 1, nt - 1), 0, 0),
                         memory_space=pltpu.SMEM),
            pl.BlockSpec((SUBLANES, LANES), lambda i: (0, 0), memory_space=pltpu.SMEM),
            pl.BlockSpec((SUBLANES, ts), lambda i: (0, i)),
            pl.BlockSpec((SUBLANES, ts), lambda i: (0, i)),
            pl.BlockSpec((ts, D_MODEL), lambda i: (i, 0)),
            pl.BlockSpec((1, D_MODEL), lambda i: (0, 0)),
            pl.BlockSpec(memory_space=pl.ANY),
        ],
        out_specs=pl.BlockSpec((ts, D_MODEL), lambda i: (i, 0)),
        out_shape=jax.ShapeDtypeStruct((t, D_MODEL), f32),
        scratch_shapes=[pltpu.VMEM((2, lrows, D_MODEL), f32), pltpu.SemaphoreType.DMA((2,))],
        compiler_params=pltpu.CompilerParams(
            dimension_semantics=("arbitrary",), vmem_limit_bytes=VMEM_LIMIT),
        name="combine",
    )(tab, tab, seg, lpos, wts, h, g_final, ys)


def _prep_w_in(w_in):
    c0 = 2 * CONV_CH
    c1 = c0 + Q_DIM
    c2 = c1 + KV_DIM
    c3 = c2 + KV_DIM
    wk = w_in[:, c1:c2]
    wv = w_in[:, c2:c3]

    def dup(w):
        return jnp.concatenate([w[:, h * HEAD_DIM:(h + 1) * HEAD_DIM] for h in range(N_KV_HEADS)
                                for _ in range(2)], axis=1)

    glu = jnp.concatenate([w_in[:, half + b * LANES:half + (b + 1) * LANES]
                           for b in range(CONV_CH // LANES) for half in (0, CONV_CH)], axis=1)
    return jnp.concatenate(
        [glu, w_in[:, c0:c1] * (HEAD_DIM ** -0.5), dup(wk), dup(wv), w_in[:, c3:]],
        axis=1).astype(bf16)


def _prep_router(w_group, b_group, w_expert, b_expert):
    d = w_group.shape[0]
    w = jnp.zeros((d, LANES), f32)
    w = w.at[:, :N_GROUPS].set(w_group).at[:, SUBLANES:ROUTER_ROWS].set(w_expert)
    b = jnp.full((1, LANES), NEG_BIG, f32)
    b = b.at[0, :N_GROUPS].set(b_group).at[0, SUBLANES:ROUTER_ROWS].set(b_expert)
    w_hi = w.astype(bf16)
    w_lo = (w - w_hi.astype(f32)).astype(bf16)
    return w_hi, w_lo, b


def kernel(x, g_mix, w_in, w_dw, b_dw, ln_conv_g, ln_conv_b, sinks, w_conv_out, w_attn_out, w_out, g_ffn,
           w_group, b_group, w_expert, b_expert, w_gate, w_up, w_down, g_final):
    batch, seq, d = x.shape
    assert d == D_MODEL and seq % TOKEN_TILE == 0 and g_mix.shape[0] == 1
    t = batch * seq
    x2 = x.reshape(t, d)

    act, q, kv, sgates = _inproj(x2, g_mix[0][None, :], _prep_w_in(w_in[0]), w_dw[0, :, 0, :], b_dw[0][None, :],
                                 ln_conv_g[0][None, :], ln_conv_b[0][None, :], seq)

    wr_hi, wr_lo, b_r = _prep_router(w_group[0], b_group[0], w_expert[0], b_expert[0])
    h, xn2, route, wts = _mixer(
        x2, act, q, kv, sgates, sinks[0], w_conv_out[0].astype(bf16), w_attn_out[0].astype(bf16),
        w_out[0].astype(bf16), g_ffn[0][None, :], wr_hi, wr_lo, b_r, batch, seq)

    lpos, tab, seg, meta = _route(route)
    nb = _num_blocks(t)
    xs = _scatter(tab, seg, lpos, xn2, nb)
    ys = _experts(meta[0, :nb], meta[1, :1], xs, w_gate[0], w_up[0], w_down[0], nb)
    seg_back = seg.at[:, SEG_SPARE_LANE:SEG_SPARE_LANE + 2].set(0)
    out = _combine(tab, seg_back, lpos, wts, h, g_final[None, :], ys)
    return out.reshape(batch, seq, d)
```

```python
import functools

import numpy as np
import jax
import jax.numpy as jnp
from jax import lax
from jax.experimental import pallas as pl
from jax.experimental.pallas import tpu as pltpu

D_MODEL = 1024
CONV_CH = 512
CONV_WIDTH = 31
N_HEADS = 8
N_KV_HEADS = 2
HEAD_DIM = 64
ATTN_BLOCK = 128
N_GROUPS = 4
EXPERTS_PER_GROUP = 8
N_EXPERTS = N_GROUPS * EXPERTS_PER_GROUP
D_FF_EXPERT = 512
NORM_EPS = 1e-6

Q_DIM = N_HEADS * HEAD_DIM
KV_DIM = N_KV_HEADS * HEAD_DIM

LANES = 128
SUBLANES = 8
CONV_HALO = 32
CONV_ROWS = 32
PROJ_PIECE = 256
ROUTER_ROWS = SUBLANES + N_EXPERTS
NEG_BIG = -1e30

TOKEN_TILE = 512
ROUTE_TILE = 512
RUN_ALIGN = SUBLANES
EXPERT_BLOCK = 512
VMEM_LIMIT = 56 * 1024 * 1024

f32 = jnp.float32
bf16 = jnp.bfloat16


def _rms(x, g):
    ms = jnp.mean(x * x, axis=-1, keepdims=True)
    return x * lax.rsqrt(ms + NORM_EPS) * g


def _sigmoid(x):
    return 1.0 / (1.0 + jnp.exp(-x))


PACKED = D_MODEL // 2
_HIGH_HALF = 0xFFFF0000


def _pack_rows(x):
    def bits(v):
        return pltpu.bitcast(v.astype(bf16).astype(f32), jnp.uint32)
    low = lax.shift_right_logical(bits(x[:, :PACKED]), jnp.uint32(16))
    return low | (bits(x[:, PACKED:]) & jnp.uint32(_HIGH_HALF))


def _unpack_rows(p):
    low = pltpu.bitcast(lax.shift_left(p, jnp.uint32(16)), f32).astype(bf16)
    high = pltpu.bitcast(p & jnp.uint32(_HIGH_HALF), f32).astype(bf16)
    return low, high


def _inproj_kernel(tiles_per_seq, x_ref, g_ref, w_ref, wdw_ref, bdw_ref, lng_ref, lnb_ref,
                   act_ref, q_ref, kv_ref, sgate_ref, xn_s, vbuf):
    tm = x_ref.shape[0]
    xn_s[...] = _rms(x_ref[...], g_ref[...]).astype(bf16)

    def proj(lo, hi):
        return jnp.dot(xn_s[...], w_ref[:, lo:hi], preferred_element_type=f32)

    n_cb = CONV_CH // LANES
    first = lax.rem(pl.program_id(0), tiles_per_seq) == 0

    @pl.when(first)
    def _():
        vbuf[:, 0:CONV_HALO, :] = jnp.zeros((n_cb, CONV_HALO, LANES), f32)

    @pl.when(jnp.logical_not(first))
    def _():
        vbuf[:, 0:CONV_HALO, :] = vbuf[:, tm:tm + CONV_HALO, :]

    for b in range(n_cb):
        u = proj(2 * b * LANES, 2 * (b + 1) * LANES)
        vbuf[b, CONV_HALO:, :] = u[:, :LANES] * _sigmoid(u[:, LANES:])

    rows = CONV_ROWS
    tap0 = CONV_HALO - (CONV_WIDTH - 1)
    for c in range(tm // rows):
        accs = []
        for b in range(n_cb):
            acc = jnp.broadcast_to(bdw_ref[:, b * LANES:(b + 1) * LANES], (rows, LANES))
            for j in range(CONV_WIDTH):
                r0 = c * rows + tap0 + j
                acc = acc + wdw_ref[j:j + 1, b * LANES:(b + 1) * LANES] * vbuf[b, r0:r0 + rows, :]
            accs.append(acc)
        mu = sum(jnp.sum(a, axis=-1, keepdims=True) for a in accs) * (1.0 / CONV_CH)
        ds = [a - mu for a in accs]
        var = sum(jnp.sum(d * d, axis=-1, keepdims=True) for d in ds) * (1.0 / CONV_CH)
        inv = lax.rsqrt(var + NORM_EPS)
        for b in range(n_cb):
            y = ds[b] * inv * lng_ref[:, b * LANES:(b + 1) * LANES] + lnb_ref[:, b * LANES:(b + 1) * LANES]
            act_ref[c * rows:(c + 1) * rows, b * LANES:(b + 1) * LANES] = (y * _sigmoid(y)).astype(bf16)

    col = 2 * CONV_CH
    for out_ref, fn in ((q_ref, lambda z: z), (kv_ref, lambda z: z), (sgate_ref, _sigmoid)):
        for lo in range(0, out_ref.shape[1], PROJ_PIECE):
            out_ref[:, lo:lo + PROJ_PIECE] = fn(proj(col + lo, col + lo + PROJ_PIECE)).astype(bf16)
        col += out_ref.shape[1]


def _inproj(x2, g_mix, w_cat, w_dw, b_dw, ln_g, ln_b, seq):
    t = x2.shape[0]
    tm = TOKEN_TILE
    widths = (CONV_CH, Q_DIM, 4 * KV_DIM, 2 * D_MODEL)

    def full(a):
        return pl.BlockSpec(a.shape, lambda i: (0,) * a.ndim)

    return pl.pallas_call(
        functools.partial(_inproj_kernel, seq // tm),
        grid=(t // tm,),
        in_specs=[pl.BlockSpec((tm, D_MODEL), lambda i: (i, 0)),
                  full(g_mix), full(w_cat), full(w_dw), full(b_dw), full(ln_g), full(ln_b)],
        out_specs=[pl.BlockSpec((tm, w), lambda i: (i, 0)) for w in widths],
        out_shape=[jax.ShapeDtypeStruct((t, w), bf16) for w in widths],
        scratch_shapes=[pltpu.VMEM((tm, D_MODEL), bf16),
                        pltpu.VMEM((CONV_CH // LANES, tm + CONV_HALO, LANES), f32)],
        compiler_params=pltpu.CompilerParams(
            dimension_semantics=("arbitrary",), vmem_limit_bytes=VMEM_LIMIT),
        name="inproj",
    )(x2, g_mix, w_cat, w_dw, b_dw, ln_g, ln_b)


def _mixer_kernel(x_ref, act_ref, q_ref, kv_ref, kvp_ref, sgate_ref, sink_ref, bias_ref, wco_ref, wao_ref,
                  wo_ref, gffn_ref, wrh_ref, wrl_ref, br_ref,
                  h_ref, xn2_ref, route_ref, wts_ref,
                  kvall, attn, s_scr, m_scr):
    ts = x_ref.shape[0]
    first = pl.program_id(1) == 0

    kvall[0:ATTN_BLOCK, :] = jnp.where(first, jnp.zeros_like(kvp_ref[...]), kvp_ref[...])
    kvall[ATTN_BLOCK:, :] = kv_ref[...]
    nkeys = 2 * ATTN_BLOCK
    left_kv = lax.broadcasted_iota(jnp.int32, (nkeys, LANES), 1) < HEAD_DIM
    left_q = lax.broadcasted_iota(jnp.int32, (ATTN_BLOCK, LANES), 1) < HEAD_DIM
    key_lane = lax.broadcasted_iota(jnp.int32, (ATTN_BLOCK, 2 * nkeys), 1)
    prev_keys = (key_lane % nkeys) < ATTN_BLOCK
    no_prev = jnp.where(prev_keys, jnp.where(first, -jnp.inf, 0.0), 0.0)

    def block_diag(x):
        z = jnp.zeros_like(x)
        return jnp.concatenate([jnp.where(left_kv, x, z), jnp.where(left_kv, z, x)], axis=0)

    n_pairs = N_HEADS // 2
    for j in range(ts // ATTN_BLOCK):
        r0 = j * ATTN_BLOCK
        for kvh in range(N_KV_HEADS):
            kbd = block_diag(kvall[r0:r0 + nkeys, kvh * LANES:(kvh + 1) * LANES])
            for ii in range(2):
                i = 2 * kvh + ii
                qb = q_ref[r0:r0 + ATTN_BLOCK, i * LANES:(i + 1) * LANES]
                s = lax.dot_general(qb, kbd, (((1,), (1,)), ((), ())), preferred_element_type=f32)
                s = s + bias_ref[i]
                if j == 0:
                    s = s + no_prev
                s_scr[j * n_pairs + i] = s
                for half in range(2):
                    m = jnp.max(s[:, half * nkeys:(half + 1) * nkeys], axis=-1, keepdims=True)
                    m_scr[2 * (j * n_pairs + i) + half] = jnp.maximum(m, sink_ref[2 * i + half])

    for j in range(ts // ATTN_BLOCK):
        r0 = j * ATTN_BLOCK
        for kvh in range(N_KV_HEADS):
            vbd = block_diag(kvall[r0:r0 + nkeys, (N_KV_HEADS + kvh) * LANES:(N_KV_HEADS + kvh + 1) * LANES])
            for ii in range(2):
                i = 2 * kvh + ii
                s = s_scr[j * n_pairs + i]
                ps, ls = [], []
                for half in range(2):
                    m = m_scr[2 * (j * n_pairs + i) + half]
                    p = jnp.exp(s[:, half * nkeys:(half + 1) * nkeys] - m)
                    ps.append(p)
                    ls.append(jnp.sum(p, axis=-1, keepdims=True) + jnp.exp(sink_ref[2 * i + half] - m))
                p = jnp.concatenate(ps, axis=1).astype(bf16)
                o = jnp.dot(p, vbd, preferred_element_type=f32)
                o = o * jnp.where(left_q, 1.0 / ls[0], 1.0 / ls[1])
                attn[r0:r0 + ATTN_BLOCK, i * LANES:(i + 1) * LANES] = o.astype(bf16)

    conv_o = jnp.dot(act_ref[...], wco_ref[...], preferred_element_type=f32)
    attn_o = jnp.dot(attn[...], wao_ref[...], preferred_element_type=f32)
    merged = (sgate_ref[:, :D_MODEL].astype(f32) * conv_o
              + sgate_ref[:, D_MODEL:].astype(f32) * attn_o).astype(bf16)
    h = x_ref[...] + jnp.dot(merged, wo_ref[...], preferred_element_type=f32)
    h_ref[...] = h

    xn2 = _rms(h, gffn_ref[...])
    xn2_ref[...] = xn2.astype(bf16)
    xh = xn2.astype(bf16)
    xl = (xn2 - xh.astype(f32)).astype(bf16)
    logits = (jnp.dot(xh, wrh_ref[...], preferred_element_type=f32)
              + jnp.dot(xh, wrl_ref[...], preferred_element_type=f32)
              + jnp.dot(xl, wrh_ref[...], preferred_element_type=f32)) + br_ref[...]
    lt = logits.T
    sub = lax.broadcasted_iota(jnp.int32, (SUBLANES, ts), 0)

    gl = lt[0:SUBLANES]
    gmax = jnp.max(gl, axis=0, keepdims=True)
    gsel = jnp.min(jnp.where(gl == gmax, sub, SUBLANES), axis=0, keepdims=True)
    p_group = 1.0 / jnp.sum(jnp.exp(gl - gmax), axis=0, keepdims=True)

    e_in = lt[SUBLANES:2 * SUBLANES]
    for g in range(1, N_GROUPS):
        e_in = jnp.where(gsel == g, lt[(g + 1) * SUBLANES:(g + 2) * SUBLANES], e_in)
    m1 = jnp.max(e_in, axis=0, keepdims=True)
    i1 = jnp.min(jnp.where(e_in == m1, sub, SUBLANES), axis=0, keepdims=True)
    rest = jnp.where(sub == i1, -jnp.inf, e_in)
    m2 = jnp.max(rest, axis=0, keepdims=True)
    i2 = jnp.min(jnp.where(rest == m2, sub, SUBLANES), axis=0, keepdims=True)
    t2 = jnp.exp(m2 - m1)
    w1 = p_group / (1.0 + t2)
    w2 = p_group * t2 / (1.0 + t2)
    base = gsel * EXPERTS_PER_GROUP
    route_ref[...] = jnp.where(sub == 0, base + i1, jnp.where(sub == 1, base + i2, 0))
    wts_ref[...] = jnp.where(sub == 0, w1, jnp.where(sub == 1, w2, 0.0))


def _attn_bias():
    qi = np.arange(ATTN_BLOCK)[:, None]
    kj = np.arange(2 * ATTN_BLOCK)[None, :]
    rel = (ATTN_BLOCK + qi - kj).astype(np.float32)
    ok = (rel >= 0) & (rel < ATTN_BLOCK)
    slopes = np.array([2.0 ** (-8.0 * (h + 1) / N_HEADS) for h in range(N_HEADS)], np.float32)
    per_head = [np.where(ok, -(slopes[h] * rel), -np.inf).astype(np.float32) for h in range(N_HEADS)]
    return np.stack([np.concatenate([per_head[2 * i], per_head[2 * i + 1]], axis=1)
                     for i in range(N_HEADS // 2)])


def _mixer(x2, act, q, kv, sgates, sinks, wco, wao, wo, g_ffn, wr_hi, wr_lo, b_r, batch, seq):
    t = x2.shape[0]
    ts = TOKEN_TILE
    ns = seq // ts
    bias = jnp.asarray(_attn_bias())

    def row(b, s):
        return b * ns + s

    def full(a):
        return pl.BlockSpec(a.shape, lambda b, s: (0,) * a.ndim)

    in_specs = [
        pl.BlockSpec((ts, D_MODEL), lambda b, s: (row(b, s), 0)),
        pl.BlockSpec((ts, CONV_CH), lambda b, s: (row(b, s), 0)),
        pl.BlockSpec((ts, Q_DIM), lambda b, s: (row(b, s), 0)),
        pl.BlockSpec((ts, 4 * KV_DIM), lambda b, s: (row(b, s), 0)),
        pl.BlockSpec((ATTN_BLOCK, 4 * KV_DIM),
                     lambda b, s: (jnp.maximum(row(b, s) * (ts // ATTN_BLOCK) - 1, 0), 0)),
        pl.BlockSpec((ts, 2 * D_MODEL), lambda b, s: (row(b, s), 0)),
        pl.BlockSpec(memory_space=pltpu.SMEM),
        full(bias), full(wco), full(wao), full(wo), full(g_ffn), full(wr_hi), full(wr_lo), full(b_r),
    ]
    out_specs = [
        pl.BlockSpec((ts, D_MODEL), lambda b, s: (row(b, s), 0)),
        pl.BlockSpec((ts, D_MODEL), lambda b, s: (row(b, s), 0)),
        pl.BlockSpec((SUBLANES, ts), lambda b, s: (0, row(b, s))),
        pl.BlockSpec((SUBLANES, ts), lambda b, s: (0, row(b, s))),
    ]
    out_shape = [
        jax.ShapeDtypeStruct((t, D_MODEL), f32),
        jax.ShapeDtypeStruct((t, D_MODEL), bf16),
        jax.ShapeDtypeStruct((SUBLANES, t), jnp.int32),
        jax.ShapeDtypeStruct((SUBLANES, t), f32),
    ]
    return pl.pallas_call(
        _mixer_kernel,
        grid=(batch, ns),
        in_specs=in_specs,
        out_specs=out_specs,
        out_shape=out_shape,
        scratch_shapes=[
            pltpu.VMEM((ts + ATTN_BLOCK, 4 * KV_DIM), bf16),
            pltpu.VMEM((ts, Q_DIM), bf16),
            pltpu.VMEM((ts // ATTN_BLOCK * (N_HEADS // 2), ATTN_BLOCK, 4 * ATTN_BLOCK), f32),
            pltpu.VMEM((ts // ATTN_BLOCK * N_HEADS, ATTN_BLOCK, 1), f32),
        ],
        compiler_params=pltpu.CompilerParams(
            dimension_semantics=("arbitrary", "arbitrary"), vmem_limit_bytes=VMEM_LIMIT),
        name="mixer",
    )(x2, act, q, kv, kv, sgates, sinks, bias, wco, wao, wo, g_ffn, wr_hi, wr_lo, b_r)


def _local_rows(ts):
    return -(-(2 * ts + (RUN_ALIGN - 1) * N_EXPERTS) // LANES) * LANES


TABLE_LANES = 2 * LANES
SEG_NACT_LANE = 3 * N_EXPERTS
SEG_SPARE_LANE = 3 * N_EXPERTS + 1


def _num_blocks(t):
    run_rows = 2 * t + (RUN_ALIGN - 1) * N_EXPERTS * (t // ROUTE_TILE)
    return -(-(run_rows + N_EXPERTS * (EXPERT_BLOCK - RUN_ALIGN)) // EXPERT_BLOCK)


def _route_kernel(route_ref, lpos_ref, tab_ref, seg_ref, meta_ref):
    t = route_ref.shape[1]
    tr = ROUTE_TILE
    nbp = meta_ref.shape[1]
    chunks_per_block = EXPERT_BLOCK // RUN_ALIGN
    eiota = lax.broadcasted_iota(jnp.int32, (N_EXPERTS, tr), 0)
    before = (lax.broadcasted_iota(jnp.int32, (tr, tr), 0)
              < lax.broadcasted_iota(jnp.int32, (tr, tr), 1)).astype(bf16)
    lower = (lax.broadcasted_iota(jnp.int32, (N_EXPERTS, N_EXPERTS), 1)
             < lax.broadcasted_iota(jnp.int32, (N_EXPERTS, N_EXPERTS), 0)).astype(bf16)
    sub = lax.broadcasted_iota(jnp.int32, (N_EXPERTS, LANES), 0)
    lane = lax.broadcasted_iota(jnp.int32, (N_EXPERTS, LANES), 1)

    def to_lanes(col, offset):
        return jnp.sum(jnp.where(sub + offset == lane, col, 0.0), axis=0, keepdims=True)

    def expert_prefix(col):
        b = jnp.broadcast_to(col, (N_EXPERTS, LANES))
        hi = jnp.floor(b * (1.0 / 16.0))
        lo = b - 16.0 * hi
        return (16.0 * jnp.dot(lower, hi.astype(bf16), preferred_element_type=f32)
                + jnp.dot(lower, lo.astype(bf16), preferred_element_type=f32))[:, 0:1]

    lpos_ref[...] = jnp.zeros(lpos_ref.shape, jnp.int32)
    chunk_id = lax.broadcasted_iota(jnp.int32, (N_EXPERTS, TABLE_LANES), 1).astype(f32)
    chunk_expert = lax.broadcasted_iota(jnp.int32, (N_EXPERTS, TABLE_LANES), 0).astype(f32)

    def step(i, seen_chunks):
        off = pl.multiple_of(i * tr, tr)
        m1 = eiota == route_ref[0:1, pl.ds(off, tr)]
        m2 = eiota == route_ref[1:2, pl.ds(off, tr)]
        onehot = jnp.where(m1 | m2, 1.0, 0.0)
        within = jnp.dot(onehot.astype(bf16), before, preferred_element_type=f32)
        run_chunks = jnp.floor((jnp.sum(onehot, axis=1, keepdims=True) + (RUN_ALIGN - 1)) * (1.0 / RUN_ALIGN))
        run_start = expert_prefix(run_chunks)
        pos = within + RUN_ALIGN * run_start
        lpos_ref[0:1, pl.ds(off, tr)] = jnp.sum(jnp.where(m1, pos, 0.0), axis=0, keepdims=True).astype(jnp.int32)
        lpos_ref[1:2, pl.ds(off, tr)] = jnp.sum(jnp.where(m2, pos, 0.0), axis=0, keepdims=True).astype(jnp.int32)
        owner = (run_start <= chunk_id) & (chunk_id < run_start + run_chunks)
        rel = jnp.sum(jnp.where(owner, seen_chunks + chunk_id - run_start, 0.0), axis=0, keepdims=True)
        eid = jnp.sum(jnp.where(owner, chunk_expert, 0.0), axis=0, keepdims=True)
        n_used = jnp.sum(run_chunks, axis=0, keepdims=True)
        unused = chunk_id[0:1] >= n_used
        rel = jnp.where(unused, chunk_id[0:1] - n_used, rel)
        eid = jnp.where(unused, (SEG_SPARE_LANE + lax.rem(i, 2)).astype(f32), eid)
        row = jnp.concatenate([rel, eid], axis=1)
        tab_ref[i] = jnp.broadcast_to(row, (SUBLANES, 2 * TABLE_LANES)).astype(jnp.int32)
        return seen_chunks + run_chunks

    used_chunks = lax.fori_loop(0, t // tr, step, jnp.zeros((N_EXPERTS, 1), f32))

    nblk = jnp.floor((used_chunks + (chunks_per_block - 1)) * (1.0 / chunks_per_block))
    first_blk = expert_prefix(nblk)
    nact = jnp.sum(nblk, axis=0, keepdims=True)
    spare_chunk = _num_blocks(t) * chunks_per_block
    seg_row = (to_lanes(first_blk * chunks_per_block, 0) + to_lanes(used_chunks, N_EXPERTS)
               + to_lanes(nblk * chunks_per_block, 2 * N_EXPERTS)
               + jnp.where(lane[0:1] == SEG_NACT_LANE, nact, 0.0)
               + jnp.where(lane[0:1] == SEG_SPARE_LANE, float(spare_chunk), 0.0)
               + jnp.where(lane[0:1] == SEG_SPARE_LANE + 1, float(spare_chunk + _local_rows(tr) // RUN_ALIGN), 0.0))
    seg_ref[...] = jnp.broadcast_to(seg_row, (SUBLANES, LANES)).astype(jnp.int32)

    blk = lax.broadcasted_iota(jnp.int32, (N_EXPERTS, nbp), 1).astype(f32)
    owner = (first_blk <= blk) & (blk < first_blk + nblk)
    expert_id = lax.broadcasted_iota(jnp.int32, (N_EXPERTS, nbp), 0).astype(f32)
    bexp = jnp.sum(jnp.where(owner, expert_id, 0.0), axis=0, keepdims=True)
    row8 = lax.broadcasted_iota(jnp.int32, (SUBLANES, nbp), 0)
    meta_ref[...] = jnp.where(row8 == 0, bexp, jnp.where(row8 == 1, nact, 0.0)).astype(jnp.int32)


def _route(route):
    t = route.shape[1]
    nt = t // ROUTE_TILE
    nbp = -(-_num_blocks(t) // LANES) * LANES
    return pl.pallas_call(
        _route_kernel,
        grid=(1,),
        in_specs=[pl.BlockSpec(route.shape, lambda i: (0, 0))],
        out_specs=[pl.BlockSpec(route.shape, lambda i: (0, 0)),
                   pl.BlockSpec((nt, SUBLANES, 2 * TABLE_LANES), lambda i: (0, 0, 0)),
                   pl.BlockSpec((SUBLANES, LANES), lambda i: (0, 0)),
                   pl.BlockSpec((SUBLANES, nbp), lambda i: (0, 0))],
        out_shape=[jax.ShapeDtypeStruct(route.shape, jnp.int32),
                   jax.ShapeDtypeStruct((nt, SUBLANES, 2 * TABLE_LANES), jnp.int32),
                   jax.ShapeDtypeStruct((SUBLANES, LANES), jnp.int32),
                   jax.ShapeDtypeStruct((SUBLANES, nbp), jnp.int32)],
        compiler_params=pltpu.CompilerParams(
            dimension_semantics=("arbitrary",), vmem_limit_bytes=VMEM_LIMIT),
        name="route",
    )(route)


def _chunk_row(tab_ref, seg_ref, k):
    chunk = seg_ref[0, tab_ref[0, 0, TABLE_LANES + k]] + tab_ref[0, 0, k]
    return pl.multiple_of(chunk * RUN_ALIGN, RUN_ALIGN)


def _scatter_kernel(tab_ref, seg_ref, lpos_ref, x_ref, xs_ref, xsl, zrows, sem, zsem):
    lrows, ts = xsl.shape[1], x_ref.shape[0]
    n_blocks = (xs_ref.shape[0] - 2 * lrows) // EXPERT_BLOCK
    i = pl.program_id(0)
    last = pl.num_programs(0) - 1
    slot = lax.rem(i, 2)

    def drain(s):
        pltpu.make_async_copy(xsl.at[s], xs_ref.at[pl.ds(0, lrows)], sem.at[s]).wait()

    @pl.when(i >= 2)
    def _():
        drain(slot)

    j = lax.broadcasted_iota(jnp.int32, (lrows, ts), 0)
    perm = jnp.where((j == lpos_ref[0:1, :]) | (j == lpos_ref[1:2, :]), 1.0, 0.0).astype(bf16)
    xsl[slot] = _pack_rows(jnp.dot(perm, x_ref[...], preferred_element_type=f32))

    for k in range(lrows // RUN_ALIGN):
        pltpu.make_async_copy(xsl.at[slot, pl.ds(k * RUN_ALIGN, RUN_ALIGN)],
                              xs_ref.at[pl.ds(_chunk_row(tab_ref, seg_ref, k), RUN_ALIGN)], sem.at[slot]).start()

    @pl.when(i == last)
    def _():
        drain(slot)

        @pl.when(i >= 1)
        def _():
            drain(1 - slot)

        zrows[...] = jnp.zeros(zrows.shape, zrows.dtype)

        def zcopy(row, n):
            return pltpu.make_async_copy(zrows.at[pl.ds(0, n)], xs_ref.at[pl.ds(row, n)], zsem)

        def for_each_unused_block(fn):
            def body(b, carry):
                fn(pl.multiple_of(b * EXPERT_BLOCK, EXPERT_BLOCK), EXPERT_BLOCK)
                return carry

            lax.fori_loop(seg_ref[0, SEG_NACT_LANE], n_blocks, body, 0)
            for off in range(0, 2 * lrows, EXPERT_BLOCK):
                fn(n_blocks * EXPERT_BLOCK + off, min(EXPERT_BLOCK, 2 * lrows - off))

        for_each_unused_block(lambda row, n: zcopy(row, n).start())
        for_each_unused_block(lambda row, n: zcopy(row, n).wait())

        def for_each_pad_chunk(fn):
            def seg(e, carry):
                g0 = seg_ref[0, e]
                used = seg_ref[0, N_EXPERTS + e]
                total = seg_ref[0, 2 * N_EXPERTS + e]

                def chunk(c, carry2):
                    fn(pl.multiple_of((g0 + c) * RUN_ALIGN, RUN_ALIGN), RUN_ALIGN)
                    return carry2

                lax.fori_loop(used, total, chunk, 0)
                return carry

            lax.fori_loop(0, N_EXPERTS, seg, 0)

        for_each_pad_chunk(lambda row, n: zcopy(row, n).start())
        for_each_pad_chunk(lambda row, n: zcopy(row, n).wait())


def _scatter(tab, seg, lpos, xn2, n_blocks):
    t = xn2.shape[0]
    ts = ROUTE_TILE
    lrows = _local_rows(ts)
    return pl.pallas_call(
        _scatter_kernel,
        grid=(t // ts,),
        in_specs=[
            pl.BlockSpec((1, SUBLANES, 2 * TABLE_LANES), lambda i: (i, 0, 0), memory_space=pltpu.SMEM),
            pl.BlockSpec((SUBLANES, LANES), lambda i: (0, 0), memory_space=pltpu.SMEM),
            pl.BlockSpec((SUBLANES, ts), lambda i: (0, i)),
            pl.BlockSpec((ts, D_MODEL), lambda i: (i, 0)),
        ],
        out_specs=pl.BlockSpec(memory_space=pl.ANY),
        out_shape=jax.ShapeDtypeStruct((n_blocks * EXPERT_BLOCK + 2 * lrows, PACKED), jnp.uint32),
        scratch_shapes=[pltpu.VMEM((2, lrows, PACKED), jnp.uint32), pltpu.VMEM((EXPERT_BLOCK, PACKED), jnp.uint32),
                        pltpu.SemaphoreType.DMA((2,)), pltpu.SemaphoreType.DMA],
        compiler_params=pltpu.CompilerParams(
            dimension_semantics=("arbitrary",), vmem_limit_bytes=VMEM_LIMIT, has_side_effects=True),
        name="scatter",
    )(tab, seg, lpos, xn2)


def _expert_kernel(bexp_ref, nact_ref, xs_ref, wg_ref, wu_ref, wd_ref, ys_ref, wg_b, wu_b, wd_b):
    b = pl.program_id(0)
    active = b < nact_ref[0]
    new_expert = jnp.logical_or(b == 0, bexp_ref[b] != bexp_ref[jnp.maximum(b - 1, 0)])

    @pl.when(jnp.logical_not(active))
    def _():
        ys_ref[...] = jnp.zeros(ys_ref.shape, ys_ref.dtype)

    @pl.when(jnp.logical_and(active, new_expert))
    def _():
        wg_b[...] = wg_ref[0].astype(bf16)
        wu_b[...] = wu_ref[0].astype(bf16)
        wd_b[...] = wd_ref[0].astype(bf16)

    @pl.when(active)
    def _():
        x_lo, x_hi = _unpack_rows(xs_ref[...])

        def up(w_ref):
            return (jnp.dot(x_lo, w_ref[:PACKED, :], preferred_element_type=f32)
                    + jnp.dot(x_hi, w_ref[PACKED:, :], preferred_element_type=f32))

        g = up(wg_b)
        hmid = (g * _sigmoid(g) * up(wu_b)).astype(bf16)
        ys_ref[...] = _pack_rows(jnp.dot(hmid, wd_b[...], preferred_element_type=f32))


def _experts(bexp, nact, xs, wg, wu, wd, nb):
    n_rows = nb * EXPERT_BLOCK

    def blk(b, bexp_ref, nact_ref):
        return jnp.minimum(b, nact_ref[0] - 1)

    grid_spec = pltpu.PrefetchScalarGridSpec(
        num_scalar_prefetch=2,
        grid=(nb,),
        in_specs=[
            pl.BlockSpec((EXPERT_BLOCK, PACKED), lambda b, e, n: (blk(b, e, n), 0)),
            pl.BlockSpec((1, D_MODEL, D_FF_EXPERT), lambda b, e, n: (e[blk(b, e, n)], 0, 0)),
            pl.BlockSpec((1, D_MODEL, D_FF_EXPERT), lambda b, e, n: (e[blk(b, e, n)], 0, 0)),
            pl.BlockSpec((1, D_FF_EXPERT, D_MODEL), lambda b, e, n: (e[blk(b, e, n)], 0, 0)),
        ],
        out_specs=pl.BlockSpec((EXPERT_BLOCK, PACKED), lambda b, e, n: (b, 0)),
        scratch_shapes=[pltpu.VMEM((D_MODEL, D_FF_EXPERT), bf16), pltpu.VMEM((D_MODEL, D_FF_EXPERT), bf16),
                        pltpu.VMEM((D_FF_EXPERT, D_MODEL), bf16)],
    )
    return pl.pallas_call(
        _expert_kernel,
        grid_spec=grid_spec,
        out_shape=jax.ShapeDtypeStruct((n_rows, PACKED), jnp.uint32),
        compiler_params=pltpu.CompilerParams(
            dimension_semantics=("arbitrary",), vmem_limit_bytes=VMEM_LIMIT),
        name="experts",
    )(bexp, nact, xs, wg, wu, wd)


def _combine_kernel(tab_ref, tab_next_ref, seg_ref, lpos_ref, wts_ref, h_ref, gf_ref, ys_ref, out_ref, ybuf, sem):
    ts, lrows = h_ref.shape[0], ybuf.shape[1]
    i = pl.program_id(0)
    slot = lax.rem(i, 2)

    def fetch(t_ref, s):
        for k in range(lrows // RUN_ALIGN):
            pltpu.make_async_copy(ys_ref.at[pl.ds(_chunk_row(t_ref, seg_ref, k), RUN_ALIGN)],
                                  ybuf.at[s, pl.ds(k * RUN_ALIGN, RUN_ALIGN)], sem.at[s]).start()

    @pl.when(i == 0)
    def _():
        fetch(tab_ref, 0)

    @pl.when(i + 1 < pl.num_programs(0))
    def _():
        fetch(tab_next_ref, 1 - slot)

    pltpu.make_async_copy(ys_ref.at[pl.ds(0, lrows)], ybuf.at[slot], sem.at[slot]).wait()

    info = jnp.concatenate([lpos_ref[...].astype(f32), wts_ref[...],
                            jnp.zeros((LANES - 2 * SUBLANES, ts), f32)], axis=0).T
    jl = lax.broadcasted_iota(jnp.int32, (ts, lrows), 1).astype(f32)
    mix = (jnp.where(jl == info[:, 0:1], info[:, SUBLANES:SUBLANES + 1], 0.0)
           + jnp.where(jl == info[:, 1:2], info[:, SUBLANES + 1:SUBLANES + 2], 0.0)).astype(bf16)
    moe = jnp.concatenate([jnp.dot(mix, y, preferred_element_type=f32) for y in _unpack_rows(ybuf[slot])],
                          axis=1)
    out_ref[...] = _rms(h_ref[...] + moe, gf_ref[...])


def _combine(tab, seg, lpos, wts, h, g_final, ys):
    t = h.shape[0]
    ts = ROUTE_TILE
    nt = t // ts
    lrows = _local_rows(ts)
    assert ys.shape[0] >= lrows
    return pl.pallas_call(
        _combine_kernel,
        grid=(nt,),
        in_specs=[
            pl.BlockSpec((1, SUBLANES, 2 * TABLE_LANES), lambda i: (i, 0, 0), memory_space=pltpu.SMEM),
            pl.BlockSpec((1, SUBLANES, 2 * TABLE_LANES), lambda i: (jnp.minimum(i + 1, nt - 1), 0, 0),
                         memory_space=pltpu.SMEM),
            pl.BlockSpec((SUBLANES, LANES), lambda i: (0, 0), memory_space=pltpu.SMEM),
            pl.BlockSpec((SUBLANES, ts), lambda i: (0, i)),
            pl.BlockSpec((SUBLANES, ts), lambda i: (0, i)),
            pl.BlockSpec((ts, D_MODEL), lambda i: (i, 0)),
            pl.BlockSpec((1, D_MODEL), lambda i: (0, 0)),
            pl.BlockSpec(memory_space=pl.ANY),
        ],
        out_specs=pl.BlockSpec((ts, D_MODEL), lambda i: (i, 0)),
        out_shape=jax.ShapeDtypeStruct((t, D_MODEL), f32),
        scratch_shapes=[pltpu.VMEM((2, lrows, PACKED), jnp.uint32), pltpu.SemaphoreType.DMA((2,))],
        compiler_params=pltpu.CompilerParams(
            dimension_semantics=("arbitrary",), vmem_limit_bytes=VMEM_LIMIT),
        name="combine",
    )(tab, tab, seg, lpos, wts, h, g_final, ys)


def _prep_w_in(w_in):
    c0 = 2 * CONV_CH
    c1 = c0 + Q_DIM
    c2 = c1 + KV_DIM
    c3 = c2 + KV_DIM
    wk = w_in[:, c1:c2]
    wv = w_in[:, c2:c3]

    def dup(w):
        return jnp.concatenate([w[:, h * HEAD_DIM:(h + 1) * HEAD_DIM] for h in range(N_KV_HEADS)
                                for _ in range(2)], axis=1)

    glu = jnp.concatenate([w_in[:, half + b * LANES:half + (b + 1) * LANES]
                           for b in range(CONV_CH // LANES) for half in (0, CONV_CH)], axis=1)
    return jnp.concatenate(
        [glu, w_in[:, c0:c1] * (HEAD_DIM ** -0.5), dup(wk), dup(wv), w_in[:, c3:]],
        axis=1).astype(bf16)


def _prep_router(w_group, b_group, w_expert, b_expert):
    d = w_group.shape[0]
    w = jnp.zeros((d, LANES), f32)
    w = w.at[:, :N_GROUPS].set(w_group).at[:, SUBLANES:ROUTER_ROWS].set(w_expert)
    b = jnp.full((1, LANES), NEG_BIG, f32)
    b = b.at[0, :N_GROUPS].set(b_group).at[0, SUBLANES:ROUTER_ROWS].set(b_expert)
    w_hi = w.astype(bf16)
    w_lo = (w - w_hi.astype(f32)).astype(bf16)
    return w_hi, w_lo, b


def kernel(x, g_mix, w_in, w_dw, b_dw, ln_conv_g, ln_conv_b, sinks, w_conv_out, w_attn_out, w_out, g_ffn,
           w_group, b_group, w_expert, b_expert, w_gate, w_up, w_down, g_final):
    batch, seq, d = x.shape
    assert d == D_MODEL and seq % TOKEN_TILE == 0 and g_mix.shape[0] == 1
    t = batch * seq
    x2 = x.reshape(t, d)

    act, q, kv, sgates = _inproj(x2, g_mix[0][None, :], _prep_w_in(w_in[0]), w_dw[0, :, 0, :], b_dw[0][None, :],
                                 ln_conv_g[0][None, :], ln_conv_b[0][None, :], seq)

    wr_hi, wr_lo, b_r = _prep_router(w_group[0], b_group[0], w_expert[0], b_expert[0])
    h, xn2, route, wts = _mixer(
        x2, act, q, kv, sgates, sinks[0], w_conv_out[0].astype(bf16), w_attn_out[0].astype(bf16),
        w_out[0].astype(bf16), g_ffn[0][None, :], wr_hi, wr_lo, b_r, batch, seq)

    lpos, tab, seg, meta = _route(route)
    nb = _num_blocks(t)
    xs = _scatter(tab, seg, lpos, xn2, nb)
    ys = _experts(meta[0, :nb], meta[1, :1], xs, w_gate[0], w_up[0], w_down[0], nb)
    seg_back = seg.at[:, SEG_SPARE_LANE:SEG_SPARE_LANE + 2].set(0)
    out = _combine(tab, seg_back, lpos, wts, h, g_final[None, :], ys)
    return out.reshape(batch, seq, d)
```

```python
import functools

import numpy as np
import jax
import jax.numpy as jnp
from jax import lax
from jax.experimental import pallas as pl
from jax.experimental.pallas import tpu as pltpu

D_MODEL = 1024
CONV_CH = 512
CONV_WIDTH = 31
N_HEADS = 8
N_KV_HEADS = 2
HEAD_DIM = 64
ATTN_BLOCK = 128
N_GROUPS = 4
EXPERTS_PER_GROUP = 8
N_EXPERTS = N_GROUPS * EXPERTS_PER_GROUP
D_FF_EXPERT = 512
NORM_EPS = 1e-6

Q_DIM = N_HEADS * HEAD_DIM
KV_DIM = N_KV_HEADS * HEAD_DIM

LANES = 128
SUBLANES = 8
CONV_HALO = 32
CONV_ROWS = 32
PROJ_PIECE = 256
ROUTER_ROWS = SUBLANES + N_EXPERTS
NEG_BIG = -1e30

TOKEN_TILE = 512
ROUTE_TILE = 512
RUN_ALIGN = SUBLANES
EXPERT_BLOCK = 1024
VMEM_LIMIT = 56 * 1024 * 1024

f32 = jnp.float32
bf16 = jnp.bfloat16


def _rms(x, g):
    ms = jnp.mean(x * x, axis=-1, keepdims=True)
    return x * lax.rsqrt(ms + NORM_EPS) * g


def _sigmoid(x):
    return 1.0 / (1.0 + jnp.exp(-x))


PACKED = D_MODEL // 2
_HIGH_HALF = 0xFFFF0000


def _pack_rows(x):
    def bits(v):
        return pltpu.bitcast(v.astype(bf16).astype(f32), jnp.uint32)
    low = lax.shift_right_logical(bits(x[:, :PACKED]), jnp.uint32(16))
    return low | (bits(x[:, PACKED:]) & jnp.uint32(_HIGH_HALF))


def _unpack_rows(p):
    low = pltpu.bitcast(lax.shift_left(p, jnp.uint32(16)), f32).astype(bf16)
    high = pltpu.bitcast(p & jnp.uint32(_HIGH_HALF), f32).astype(bf16)
    return low, high


def _inproj_kernel(tiles_per_seq, x_ref, g_ref, w_ref, wdw_ref, bdw_ref, lng_ref, lnb_ref,
                   act_ref, q_ref, kv_ref, sgate_ref, xn_s, vbuf):
    tm = x_ref.shape[0]
    xn_s[...] = _rms(x_ref[...], g_ref[...]).astype(bf16)

    def proj(lo, hi):
        return jnp.dot(xn_s[...], w_ref[:, lo:hi], preferred_element_type=f32)

    n_cb = CONV_CH // LANES
    first = lax.rem(pl.program_id(0), tiles_per_seq) == 0

    @pl.when(first)
    def _():
        vbuf[:, 0:CONV_HALO, :] = jnp.zeros((n_cb, CONV_HALO, LANES), f32)

    @pl.when(jnp.logical_not(first))
    def _():
        vbuf[:, 0:CONV_HALO, :] = vbuf[:, tm:tm + CONV_HALO, :]

    for b in range(n_cb):
        u = proj(2 * b * LANES, 2 * (b + 1) * LANES)
        vbuf[b, CONV_HALO:, :] = u[:, :LANES] * _sigmoid(u[:, LANES:])

    rows = CONV_ROWS
    tap0 = CONV_HALO - (CONV_WIDTH - 1)
    for c in range(tm // rows):
        accs = []
        for b in range(n_cb):
            acc = jnp.broadcast_to(bdw_ref[:, b * LANES:(b + 1) * LANES], (rows, LANES))
            for j in range(CONV_WIDTH):
                r0 = c * rows + tap0 + j
                acc = acc + wdw_ref[j:j + 1, b * LANES:(b + 1) * LANES] * vbuf[b, r0:r0 + rows, :]
            accs.append(acc)
        mu = sum(jnp.sum(a, axis=-1, keepdims=True) for a in accs) * (1.0 / CONV_CH)
        ds = [a - mu for a in accs]
        var = sum(jnp.sum(d * d, axis=-1, keepdims=True) for d in ds) * (1.0 / CONV_CH)
        inv = lax.rsqrt(var + NORM_EPS)
        for b in range(n_cb):
            y = ds[b] * inv * lng_ref[:, b * LANES:(b + 1) * LANES] + lnb_ref[:, b * LANES:(b + 1) * LANES]
            act_ref[c * rows:(c + 1) * rows, b * LANES:(b + 1) * LANES] = (y * _sigmoid(y)).astype(bf16)

    col = 2 * CONV_CH
    for out_ref, fn in ((q_ref, lambda z: z), (kv_ref, lambda z: z), (sgate_ref, _sigmoid)):
        for lo in range(0, out_ref.shape[1], PROJ_PIECE):
            out_ref[:, lo:lo + PROJ_PIECE] = fn(proj(col + lo, col + lo + PROJ_PIECE)).astype(bf16)
        col += out_ref.shape[1]


def _inproj(x2, g_mix, w_cat, w_dw, b_dw, ln_g, ln_b, seq):
    t = x2.shape[0]
    tm = TOKEN_TILE
    widths = (CONV_CH, Q_DIM, 4 * KV_DIM, 2 * D_MODEL)

    def full(a):
        return pl.BlockSpec(a.shape, lambda i: (0,) * a.ndim)

    return pl.pallas_call(
        functools.partial(_inproj_kernel, seq // tm),
        grid=(t // tm,),
        in_specs=[pl.BlockSpec((tm, D_MODEL), lambda i: (i, 0)),
                  full(g_mix), full(w_cat), full(w_dw), full(b_dw), full(ln_g), full(ln_b)],
        out_specs=[pl.BlockSpec((tm, w), lambda i: (i, 0)) for w in widths],
        out_shape=[jax.ShapeDtypeStruct((t, w), bf16) for w in widths],
        scratch_shapes=[pltpu.VMEM((tm, D_MODEL), bf16),
                        pltpu.VMEM((CONV_CH // LANES, tm + CONV_HALO, LANES), f32)],
        compiler_params=pltpu.CompilerParams(
            dimension_semantics=("arbitrary",), vmem_limit_bytes=VMEM_LIMIT),
        name="inproj",
    )(x2, g_mix, w_cat, w_dw, b_dw, ln_g, ln_b)


def _mixer_kernel(x_ref, act_ref, q_ref, kv_ref, kvp_ref, sgate_ref, sink_ref, bias_ref, wco_ref, wao_ref,
                  wo_ref, gffn_ref, wrh_ref, wrl_ref, br_ref,
                  h_ref, xn2_ref, route_ref, wts_ref,
                  kvall, attn, s_scr, m_scr):
    ts = x_ref.shape[0]
    first = pl.program_id(1) == 0

    kvall[0:ATTN_BLOCK, :] = jnp.where(first, jnp.zeros_like(kvp_ref[...]), kvp_ref[...])
    kvall[ATTN_BLOCK:, :] = kv_ref[...]
    nkeys = 2 * ATTN_BLOCK
    left_kv = lax.broadcasted_iota(jnp.int32, (nkeys, LANES), 1) < HEAD_DIM
    left_q = lax.broadcasted_iota(jnp.int32, (ATTN_BLOCK, LANES), 1) < HEAD_DIM
    key_lane = lax.broadcasted_iota(jnp.int32, (ATTN_BLOCK, 2 * nkeys), 1)
    prev_keys = (key_lane % nkeys) < ATTN_BLOCK
    no_prev = jnp.where(prev_keys, jnp.where(first, -jnp.inf, 0.0), 0.0)

    def block_diag(x):
        z = jnp.zeros_like(x)
        return jnp.concatenate([jnp.where(left_kv, x, z), jnp.where(left_kv, z, x)], axis=0)

    n_pairs = N_HEADS // 2
    for j in range(ts // ATTN_BLOCK):
        r0 = j * ATTN_BLOCK
        for kvh in range(N_KV_HEADS):
            kbd = block_diag(kvall[r0:r0 + nkeys, kvh * LANES:(kvh + 1) * LANES])
            for ii in range(2):
                i = 2 * kvh + ii
                qb = q_ref[r0:r0 + ATTN_BLOCK, i * LANES:(i + 1) * LANES]
                s = lax.dot_general(qb, kbd, (((1,), (1,)), ((), ())), preferred_element_type=f32)
                s = s + bias_ref[i]
                if j == 0:
                    s = s + no_prev
                s_scr[j * n_pairs + i] = s
                for half in range(2):
                    m = jnp.max(s[:, half * nkeys:(half + 1) * nkeys], axis=-1, keepdims=True)
                    m_scr[2 * (j * n_pairs + i) + half] = jnp.maximum(m, sink_ref[2 * i + half])

    for j in range(ts // ATTN_BLOCK):
        r0 = j * ATTN_BLOCK
        for kvh in range(N_KV_HEADS):
            vbd = block_diag(kvall[r0:r0 + nkeys, (N_KV_HEADS + kvh) * LANES:(N_KV_HEADS + kvh + 1) * LANES])
            for ii in range(2):
                i = 2 * kvh + ii
                s = s_scr[j * n_pairs + i]
                ps, ls = [], []
                for half in range(2):
                    m = m_scr[2 * (j * n_pairs + i) + half]
                    p = jnp.exp(s[:, half * nkeys:(half + 1) * nkeys] - m)
                    ps.append(p)
                    ls.append(jnp.sum(p, axis=-1, keepdims=True) + jnp.exp(sink_ref[2 * i + half] - m))
                p = jnp.concatenate(ps, axis=1).astype(bf16)
                o = jnp.dot(p, vbd, preferred_element_type=f32)
                o = o * jnp.where(left_q, 1.0 / ls[0], 1.0 / ls[1])
                attn[r0:r0 + ATTN_BLOCK, i * LANES:(i + 1) * LANES] = o.astype(bf16)

    conv_o = jnp.dot(act_ref[...], wco_ref[...], preferred_element_type=f32)
    attn_o = jnp.dot(attn[...], wao_ref[...], preferred_element_type=f32)
    merged = (sgate_ref[:, :D_MODEL].astype(f32) * conv_o
              + sgate_ref[:, D_MODEL:].astype(f32) * attn_o).astype(bf16)
    h = x_ref[...] + jnp.dot(merged, wo_ref[...], preferred_element_type=f32)
    h_ref[...] = h

    xn2 = _rms(h, gffn_ref[...])
    xn2_ref[...] = xn2.astype(bf16)
    xh = xn2.astype(bf16)
    xl = (xn2 - xh.astype(f32)).astype(bf16)
    logits = (jnp.dot(xh, wrh_ref[...], preferred_element_type=f32)
              + jnp.dot(xh, wrl_ref[...], preferred_element_type=f32)
              + jnp.dot(xl, wrh_ref[...], preferred_element_type=f32)) + br_ref[...]
    lt = logits.T
    sub = lax.broadcasted_iota(jnp.int32, (SUBLANES, ts), 0)

    gl = lt[0:SUBLANES]
    gmax = jnp.max(gl, axis=0, keepdims=True)
    gsel = jnp.min(jnp.where(gl == gmax, sub, SUBLANES), axis=0, keepdims=True)
    p_group = 1.0 / jnp.sum(jnp.exp(gl - gmax), axis=0, keepdims=True)

    e_in = lt[SUBLANES:2 * SUBLANES]
    for g in range(1, N_GROUPS):
        e_in = jnp.where(gsel == g, lt[(g + 1) * SUBLANES:(g + 2) * SUBLANES], e_in)
    m1 = jnp.max(e_in, axis=0, keepdims=True)
    i1 = jnp.min(jnp.where(e_in == m1, sub, SUBLANES), axis=0, keepdims=True)
    rest = jnp.where(sub == i1, -jnp.inf, e_in)
    m2 = jnp.max(rest, axis=0, keepdims=True)
    i2 = jnp.min(jnp.where(rest == m2, sub, SUBLANES), axis=0, keepdims=True)
    t2 = jnp.exp(m2 - m1)
    w1 = p_group / (1.0 + t2)
    w2 = p_group * t2 / (1.0 + t2)
    base = gsel * EXPERTS_PER_GROUP
    route_ref[...] = jnp.where(sub == 0, base + i1, jnp.where(sub == 1, base + i2, 0))
    wts_ref[...] = jnp.where(sub == 0, w1, jnp.where(sub == 1, w2, 0.0))


def _attn_bias():
    qi = np.arange(ATTN_BLOCK)[:, None]
    kj = np.arange(2 * ATTN_BLOCK)[None, :]
    rel = (ATTN_BLOCK + qi - kj).astype(np.float32)
    ok = (rel >= 0) & (rel < ATTN_BLOCK)
    slopes = np.array([2.0 ** (-8.0 * (h + 1) / N_HEADS) for h in range(N_HEADS)], np.float32)
    per_head = [np.where(ok, -(slopes[h] * rel), -np.inf).astype(np.float32) for h in range(N_HEADS)]
    return np.stack([np.concatenate([per_head[2 * i], per_head[2 * i + 1]], axis=1)
                     for i in range(N_HEADS // 2)])


def _mixer(x2, act, q, kv, sgates, sinks, wco, wao, wo, g_ffn, wr_hi, wr_lo, b_r, batch, seq):
    t = x2.shape[0]
    ts = TOKEN_TILE
    ns = seq // ts
    bias = jnp.asarray(_attn_bias())

    def row(b, s):
        return b * ns + s

    def full(a):
        return pl.BlockSpec(a.shape, lambda b, s: (0,) * a.ndim)

    in_specs = [
        pl.BlockSpec((ts, D_MODEL), lambda b, s: (row(b, s), 0)),
        pl.BlockSpec((ts, CONV_CH), lambda b, s: (row(b, s), 0)),
        pl.BlockSpec((ts, Q_DIM), lambda b, s: (row(b, s), 0)),
        pl.BlockSpec((ts, 4 * KV_DIM), lambda b, s: (row(b, s), 0)),
        pl.BlockSpec((ATTN_BLOCK, 4 * KV_DIM),
                     lambda b, s: (jnp.maximum(row(b, s) * (ts // ATTN_BLOCK) - 1, 0), 0)),
        pl.BlockSpec((ts, 2 * D_MODEL), lambda b, s: (row(b, s), 0)),
        pl.BlockSpec(memory_space=pltpu.SMEM),
        full(bias), full(wco), full(wao), full(wo), full(g_ffn), full(wr_hi), full(wr_lo), full(b_r),
    ]
    out_specs = [
        pl.BlockSpec((ts, D_MODEL), lambda b, s: (row(b, s), 0)),
        pl.BlockSpec((ts, D_MODEL), lambda b, s: (row(b, s), 0)),
        pl.BlockSpec((SUBLANES, ts), lambda b, s: (0, row(b, s))),
        pl.BlockSpec((SUBLANES, ts), lambda b, s: (0, row(b, s))),
    ]
    out_shape = [
        jax.ShapeDtypeStruct((t, D_MODEL), f32),
        jax.ShapeDtypeStruct((t, D_MODEL), bf16),
        jax.ShapeDtypeStruct((SUBLANES, t), jnp.int32),
        jax.ShapeDtypeStruct((SUBLANES, t), f32),
    ]
    return pl.pallas_call(
        _mixer_kernel,
        grid=(batch, ns),
        in_specs=in_specs,
        out_specs=out_specs,
        out_shape=out_shape,
        scratch_shapes=[
            pltpu.VMEM((ts + ATTN_BLOCK, 4 * KV_DIM), bf16),
            pltpu.VMEM((ts, Q_DIM), bf16),
            pltpu.VMEM((ts // ATTN_BLOCK * (N_HEADS // 2), ATTN_BLOCK, 4 * ATTN_BLOCK), f32),
            pltpu.VMEM((ts // ATTN_BLOCK * N_HEADS, ATTN_BLOCK, 1), f32),
        ],
        compiler_params=pltpu.CompilerParams(
            dimension_semantics=("arbitrary", "arbitrary"), vmem_limit_bytes=VMEM_LIMIT),
        name="mixer",
    )(x2, act, q, kv, kv, sgates, sinks, bias, wco, wao, wo, g_ffn, wr_hi, wr_lo, b_r)


def _local_rows(ts):
    return -(-(2 * ts + (RUN_ALIGN - 1) * N_EXPERTS) // LANES) * LANES


TABLE_LANES = 2 * LANES
SEG_NACT_LANE = 3 * N_EXPERTS
SEG_SPARE_LANE = 3 * N_EXPERTS + 1


def _num_blocks(t):
    run_rows = 2 * t + (RUN_ALIGN - 1) * N_EXPERTS * (t // ROUTE_TILE)
    return -(-(run_rows + N_EXPERTS * (EXPERT_BLOCK - RUN_ALIGN)) // EXPERT_BLOCK)


def _route_kernel(route_ref, lpos_ref, tab_ref, seg_ref, meta_ref):
    t = route_ref.shape[1]
    tr = ROUTE_TILE
    nbp = meta_ref.shape[1]
    chunks_per_block = EXPERT_BLOCK // RUN_ALIGN
    eiota = lax.broadcasted_iota(jnp.int32, (N_EXPERTS, tr), 0)
    before = (lax.broadcasted_iota(jnp.int32, (tr, tr), 0)
              < lax.broadcasted_iota(jnp.int32, (tr, tr), 1)).astype(bf16)
    lower = (lax.broadcasted_iota(jnp.int32, (N_EXPERTS, N_EXPERTS), 1)
             < lax.broadcasted_iota(jnp.int32, (N_EXPERTS, N_EXPERTS), 0)).astype(bf16)
    sub = lax.broadcasted_iota(jnp.int32, (N_EXPERTS, LANES), 0)
    lane = lax.broadcasted_iota(jnp.int32, (N_EXPERTS, LANES), 1)

    def to_lanes(col, offset):
        return jnp.sum(jnp.where(sub + offset == lane, col, 0.0), axis=0, keepdims=True)

    def expert_prefix(col):
        b = jnp.broadcast_to(col, (N_EXPERTS, LANES))
        hi = jnp.floor(b * (1.0 / 16.0))
        lo = b - 16.0 * hi
        return (16.0 * jnp.dot(lower, hi.astype(bf16), preferred_element_type=f32)
                + jnp.dot(lower, lo.astype(bf16), preferred_element_type=f32))[:, 0:1]

    lpos_ref[...] = jnp.zeros(lpos_ref.shape, jnp.int32)
    chunk_id = lax.broadcasted_iota(jnp.int32, (N_EXPERTS, TABLE_LANES), 1).astype(f32)
    chunk_expert = lax.broadcasted_iota(jnp.int32, (N_EXPERTS, TABLE_LANES), 0).astype(f32)

    def step(i, seen_chunks):
        off = pl.multiple_of(i * tr, tr)
        m1 = eiota == route_ref[0:1, pl.ds(off, tr)]
        m2 = eiota == route_ref[1:2, pl.ds(off, tr)]
        onehot = jnp.where(m1 | m2, 1.0, 0.0)
        within = jnp.dot(onehot.astype(bf16), before, preferred_element_type=f32)
        run_chunks = jnp.floor((jnp.sum(onehot, axis=1, keepdims=True) + (RUN_ALIGN - 1)) * (1.0 / RUN_ALIGN))
        run_start = expert_prefix(run_chunks)
        pos = within + RUN_ALIGN * run_start
        lpos_ref[0:1, pl.ds(off, tr)] = jnp.sum(jnp.where(m1, pos, 0.0), axis=0, keepdims=True).astype(jnp.int32)
        lpos_ref[1:2, pl.ds(off, tr)] = jnp.sum(jnp.where(m2, pos, 0.0), axis=0, keepdims=True).astype(jnp.int32)
        owner = (run_start <= chunk_id) & (chunk_id < run_start + run_chunks)
        rel = jnp.sum(jnp.where(owner, seen_chunks + chunk_id - run_start, 0.0), axis=0, keepdims=True)
        eid = jnp.sum(jnp.where(owner, chunk_expert, 0.0), axis=0, keepdims=True)
        n_used = jnp.sum(run_chunks, axis=0, keepdims=True)
        unused = chunk_id[0:1] >= n_used
        rel = jnp.where(unused, chunk_id[0:1] - n_used, rel)
        eid = jnp.where(unused, (SEG_SPARE_LANE + lax.rem(i, 2)).astype(f32), eid)
        row = jnp.concatenate([rel, eid], axis=1)
        tab_ref[i] = jnp.broadcast_to(row, (SUBLANES, 2 * TABLE_LANES)).astype(jnp.int32)
        return seen_chunks + run_chunks

    used_chunks = lax.fori_loop(0, t // tr, step, jnp.zeros((N_EXPERTS, 1), f32))

    nblk = jnp.floor((used_chunks + (chunks_per_block - 1)) * (1.0 / chunks_per_block))
    first_blk = expert_prefix(nblk)
    nact = jnp.sum(nblk, axis=0, keepdims=True)
    spare_chunk = _num_blocks(t) * chunks_per_block
    seg_row = (to_lanes(first_blk * chunks_per_block, 0) + to_lanes(used_chunks, N_EXPERTS)
               + to_lanes(nblk * chunks_per_block, 2 * N_EXPERTS)
               + jnp.where(lane[0:1] == SEG_NACT_LANE, nact, 0.0)
               + jnp.where(lane[0:1] == SEG_SPARE_LANE, float(spare_chunk), 0.0)
               + jnp.where(lane[0:1] == SEG_SPARE_LANE + 1, float(spare_chunk + _local_rows(tr) // RUN_ALIGN), 0.0))
    seg_ref[...] = jnp.broadcast_to(seg_row, (SUBLANES, LANES)).astype(jnp.int32)

    blk = lax.broadcasted_iota(jnp.int32, (N_EXPERTS, nbp), 1).astype(f32)
    owner = (first_blk <= blk) & (blk < first_blk + nblk)
    expert_id = lax.broadcasted_iota(jnp.int32, (N_EXPERTS, nbp), 0).astype(f32)
    bexp = jnp.sum(jnp.where(owner, expert_id, 0.0), axis=0, keepdims=True)
    row8 = lax.broadcasted_iota(jnp.int32, (SUBLANES, nbp), 0)
    meta_ref[...] = jnp.where(row8 == 0, bexp, jnp.where(row8 == 1, nact, 0.0)).astype(jnp.int32)


def _route(route):
    t = route.shape[1]
    nt = t // ROUTE_TILE
    nbp = -(-_num_blocks(t) // LANES) * LANES
    return pl.pallas_call(
        _route_kernel,
        grid=(1,),
        in_specs=[pl.BlockSpec(route.shape, lambda i: (0, 0))],
        out_specs=[pl.BlockSpec(route.shape, lambda i: (0, 0)),
                   pl.BlockSpec((nt, SUBLANES, 2 * TABLE_LANES), lambda i: (0, 0, 0)),
                   pl.BlockSpec((SUBLANES, LANES), lambda i: (0, 0)),
                   pl.BlockSpec((SUBLANES, nbp), lambda i: (0, 0))],
        out_shape=[jax.ShapeDtypeStruct(route.shape, jnp.int32),
                   jax.ShapeDtypeStruct((nt, SUBLANES, 2 * TABLE_LANES), jnp.int32),
                   jax.ShapeDtypeStruct((SUBLANES, LANES), jnp.int32),
                   jax.ShapeDtypeStruct((SUBLANES, nbp), jnp.int32)],
        compiler_params=pltpu.CompilerParams(
            dimension_semantics=("arbitrary",), vmem_limit_bytes=VMEM_LIMIT),
        name="route",
    )(route)


def _chunk_row(tab_ref, seg_ref, k):
    chunk = seg_ref[0, tab_ref[0, 0, TABLE_LANES + k]] + tab_ref[0, 0, k]
    return pl.multiple_of(chunk * RUN_ALIGN, RUN_ALIGN)


def _scatter_kernel(tab_ref, seg_ref, lpos_ref, x_ref, xs_ref, xsl, zrows, sem, zsem):
    lrows, ts = xsl.shape[1], x_ref.shape[0]
    n_blocks = (xs_ref.shape[0] - 2 * lrows) // EXPERT_BLOCK
    i = pl.program_id(0)
    last = pl.num_programs(0) - 1
    slot = lax.rem(i, 2)

    def drain(s):
        pltpu.make_async_copy(xsl.at[s], xs_ref.at[pl.ds(0, lrows)], sem.at[s]).wait()

    @pl.when(i >= 2)
    def _():
        drain(slot)

    j = lax.broadcasted_iota(jnp.int32, (lrows, ts), 0)
    perm = jnp.where((j == lpos_ref[0:1, :]) | (j == lpos_ref[1:2, :]), 1.0, 0.0).astype(bf16)
    xsl[slot] = _pack_rows(jnp.dot(perm, x_ref[...], preferred_element_type=f32))

    for k in range(lrows // RUN_ALIGN):
        pltpu.make_async_copy(xsl.at[slot, pl.ds(k * RUN_ALIGN, RUN_ALIGN)],
                              xs_ref.at[pl.ds(_chunk_row(tab_ref, seg_ref, k), RUN_ALIGN)], sem.at[slot]).start()

    @pl.when(i == last)
    def _():
        drain(slot)

        @pl.when(i >= 1)
        def _():
            drain(1 - slot)

        zrows[...] = jnp.zeros(zrows.shape, zrows.dtype)

        def zcopy(row, n):
            return pltpu.make_async_copy(zrows.at[pl.ds(0, n)], xs_ref.at[pl.ds(row, n)], zsem)

        def for_each_unused_block(fn):
            def body(b, carry):
                fn(pl.multiple_of(b * EXPERT_BLOCK, EXPERT_BLOCK), EXPERT_BLOCK)
                return carry

            lax.fori_loop(seg_ref[0, SEG_NACT_LANE], n_blocks, body, 0)
            for off in range(0, 2 * lrows, EXPERT_BLOCK):
                fn(n_blocks * EXPERT_BLOCK + off, min(EXPERT_BLOCK, 2 * lrows - off))

        for_each_unused_block(lambda row, n: zcopy(row, n).start())
        for_each_unused_block(lambda row, n: zcopy(row, n).wait())

        def for_each_pad_chunk(fn):
            def seg(e, carry):
                g0 = seg_ref[0, e]
                used = seg_ref[0, N_EXPERTS + e]
                total = seg_ref[0, 2 * N_EXPERTS + e]

                def chunk(c, carry2):
                    fn(pl.multiple_of((g0 + c) * RUN_ALIGN, RUN_ALIGN), RUN_ALIGN)
                    return carry2

                lax.fori_loop(used, total, chunk, 0)
                return carry

            lax.fori_loop(0, N_EXPERTS, seg, 0)

        for_each_pad_chunk(lambda row, n: zcopy(row, n).start())
        for_each_pad_chunk(lambda row, n: zcopy(row, n).wait())


def _scatter(tab, seg, lpos, xn2, n_blocks):
    t = xn2.shape[0]
    ts = ROUTE_TILE
    lrows = _local_rows(ts)
    return pl.pallas_call(
        _scatter_kernel,
        grid=(t // ts,),
        in_specs=[
            pl.BlockSpec((1, SUBLANES, 2 * TABLE_LANES), lambda i: (i, 0, 0), memory_space=pltpu.SMEM),
            pl.BlockSpec((SUBLANES, LANES), lambda i: (0, 0), memory_space=pltpu.SMEM),
            pl.BlockSpec((SUBLANES, ts), lambda i: (0, i)),
            pl.BlockSpec((ts, D_MODEL), lambda i: (i, 0)),
        ],
        out_specs=pl.BlockSpec(memory_space=pl.ANY),
        out_shape=jax.ShapeDtypeStruct((n_blocks * EXPERT_BLOCK + 2 * lrows, PACKED), jnp.uint32),
        scratch_shapes=[pltpu.VMEM((2, lrows, PACKED), jnp.uint32), pltpu.VMEM((EXPERT_BLOCK, PACKED), jnp.uint32),
                        pltpu.SemaphoreType.DMA((2,)), pltpu.SemaphoreType.DMA],
        compiler_params=pltpu.CompilerParams(
            dimension_semantics=("arbitrary",), vmem_limit_bytes=VMEM_LIMIT, has_side_effects=True),
        name="scatter",
    )(tab, seg, lpos, xn2)


def _expert_kernel(bexp_ref, nact_ref, xs_ref, wg_ref, wu_ref, wd_ref, ys_ref, wg_b, wu_b, wd_b):
    b = pl.program_id(0)
    active = b < nact_ref[0]
    new_expert = jnp.logical_or(b == 0, bexp_ref[b] != bexp_ref[jnp.maximum(b - 1, 0)])

    @pl.when(jnp.logical_not(active))
    def _():
        ys_ref[...] = jnp.zeros(ys_ref.shape, ys_ref.dtype)

    @pl.when(jnp.logical_and(active, new_expert))
    def _():
        wg_b[...] = wg_ref[0].astype(bf16)
        wu_b[...] = wu_ref[0].astype(bf16)
        wd_b[...] = wd_ref[0].astype(bf16)

    @pl.when(active)
    def _():
        x_lo, x_hi = _unpack_rows(xs_ref[...])

        def up(w_ref):
            return (jnp.dot(x_lo, w_ref[:PACKED, :], preferred_element_type=f32)
                    + jnp.dot(x_hi, w_ref[PACKED:, :], preferred_element_type=f32))

        g = up(wg_b)
        hmid = (g * _sigmoid(g) * up(wu_b)).astype(bf16)
        ys_ref[...] = _pack_rows(jnp.dot(hmid, wd_b[...], preferred_element_type=f32))


def _experts(bexp, nact, xs, wg, wu, wd, nb):
    n_rows = nb * EXPERT_BLOCK

    def blk(b, bexp_ref, nact_ref):
        return jnp.minimum(b, nact_ref[0] - 1)

    grid_spec = pltpu.PrefetchScalarGridSpec(
        num_scalar_prefetch=2,
        grid=(nb,),
        in_specs=[
            pl.BlockSpec((EXPERT_BLOCK, PACKED), lambda b, e, n: (blk(b, e, n), 0)),
            pl.BlockSpec((1, D_MODEL, D_FF_EXPERT), lambda b, e, n: (e[blk(b, e, n)], 0, 0)),
            pl.BlockSpec((1, D_MODEL, D_FF_EXPERT), lambda b, e, n: (e[blk(b, e, n)], 0, 0)),
            pl.BlockSpec((1, D_FF_EXPERT, D_MODEL), lambda b, e, n: (e[blk(b, e, n)], 0, 0)),
        ],
        out_specs=pl.BlockSpec((EXPERT_BLOCK, PACKED), lambda b, e, n: (b, 0)),
        scratch_shapes=[pltpu.VMEM((D_MODEL, D_FF_EXPERT), bf16), pltpu.VMEM((D_MODEL, D_FF_EXPERT), bf16),
                        pltpu.VMEM((D_FF_EXPERT, D_MODEL), bf16)],
    )
    return pl.pallas_call(
        _expert_kernel,
        grid_spec=grid_spec,
        out_shape=jax.ShapeDtypeStruct((n_rows, PACKED), jnp.uint32),
        compiler_params=pltpu.CompilerParams(
            dimension_semantics=("arbitrary",), vmem_limit_bytes=VMEM_LIMIT),
        name="experts",
    )(bexp, nact, xs, wg, wu, wd)


def _combine_kernel(tab_ref, tab_next_ref, seg_ref, lpos_ref, wts_ref, h_ref, gf_ref, ys_ref, out_ref, ybuf, sem):
    ts, lrows = h_ref.shape[0], ybuf.shape[1]
    i = pl.program_id(0)
    slot = lax.rem(i, 2)

    def fetch(t_ref, s):
        for k in range(lrows // RUN_ALIGN):
            pltpu.make_async_copy(ys_ref.at[pl.ds(_chunk_row(t_ref, seg_ref, k), RUN_ALIGN)],
                                  ybuf.at[s, pl.ds(k * RUN_ALIGN, RUN_ALIGN)], sem.at[s]).start()

    @pl.when(i == 0)
    def _():
        fetch(tab_ref, 0)

    @pl.when(i + 1 < pl.num_programs(0))
    def _():
        fetch(tab_next_ref, 1 - slot)

    pltpu.make_async_copy(ys_ref.at[pl.ds(0, lrows)], ybuf.at[slot], sem.at[slot]).wait()

    info = jnp.concatenate([lpos_ref[...].astype(f32), wts_ref[...],
                            jnp.zeros((LANES - 2 * SUBLANES, ts), f32)], axis=0).T
    jl = lax.broadcasted_iota(jnp.int32, (ts, lrows), 1).astype(f32)
    mix = (jnp.where(jl == info[:, 0:1], info[:, SUBLANES:SUBLANES + 1], 0.0)
           + jnp.where(jl == info[:, 1:2], info[:, SUBLANES + 1:SUBLANES + 2], 0.0)).astype(bf16)
    moe = jnp.concatenate([jnp.dot(mix, y, preferred_element_type=f32) for y in _unpack_rows(ybuf[slot])],
                          axis=1)
    out_ref[...] = _rms(h_ref[...] + moe, gf_ref[...])


def _combine(tab, seg, lpos, wts, h, g_final, ys):
    t = h.shape[0]
    ts = ROUTE_TILE
    nt = t // ts
    lrows = _local_rows(ts)
    assert ys.shape[0] >= lrows
    return pl.pallas_call(
        _combine_kernel,
        grid=(nt,),
        in_specs=[
            pl.BlockSpec((1, SUBLANES, 2 * TABLE_LANES), lambda i: (i, 0, 0), memory_space=pltpu.SMEM),
            pl.BlockSpec((1, SUBLANES, 2 * TABLE_LANES), lambda i: (jnp.minimum(i + 1, nt - 1), 0, 0),
                         memory_space=pltpu.SMEM),
            pl.BlockSpec((SUBLANES, LANES), lambda i: (0, 0), memory_space=pltpu.SMEM),
            pl.BlockSpec((SUBLANES, ts), lambda i: (0, i)),
            pl.BlockSpec((SUBLANES, ts), lambda i: (0, i)),
            pl.BlockSpec((ts, D_MODEL), lambda i: (i, 0)),
            pl.BlockSpec((1, D_MODEL), lambda i: (0, 0)),
            pl.BlockSpec(memory_space=pl.ANY),
        ],
        out_specs=pl.BlockSpec((ts, D_MODEL), lambda i: (i, 0)),
        out_shape=jax.ShapeDtypeStruct((t, D_MODEL), f32),
        scratch_shapes=[pltpu.VMEM((2, lrows, PACKED), jnp.uint32), pltpu.SemaphoreType.DMA((2,))],
        compiler_params=pltpu.CompilerParams(
            dimension_semantics=("arbitrary",), vmem_limit_bytes=VMEM_LIMIT),
        name="combine",
    )(tab, tab, seg, lpos, wts, h, g_final, ys)


def _prep_w_in(w_in):
    c0 = 2 * CONV_CH
    c1 = c0 + Q_DIM
    c2 = c1 + KV_DIM
    c3 = c2 + KV_DIM
    wk = w_in[:, c1:c2]
    wv = w_in[:, c2:c3]

    def dup(w):
        return jnp.concatenate([w[:, h * HEAD_DIM:(h + 1) * HEAD_DIM] for h in range(N_KV_HEADS)
                                for _ in range(2)], axis=1)

    glu = jnp.concatenate([w_in[:, half + b * LANES:half + (b + 1) * LANES]
                           for b in range(CONV_CH // LANES) for half in (0, CONV_CH)], axis=1)
    return jnp.concatenate(
        [glu, w_in[:, c0:c1] * (HEAD_DIM ** -0.5), dup(wk), dup(wv), w_in[:, c3:]],
        axis=1).astype(bf16)


def _prep_router(w_group, b_group, w_expert, b_expert):
    d = w_group.shape[0]
    w = jnp.zeros((d, LANES), f32)
    w = w.at[:, :N_GROUPS].set(w_group).at[:, SUBLANES:ROUTER_ROWS].set(w_expert)
    b = jnp.full((1, LANES), NEG_BIG, f32)
    b = b.at[0, :N_GROUPS].set(b_group).at[0, SUBLANES:ROUTER_ROWS].set(b_expert)
    w_hi = w.astype(bf16)
    w_lo = (w - w_hi.astype(f32)).astype(bf16)
    return w_hi, w_lo, b


def kernel(x, g_mix, w_in, w_dw, b_dw, ln_conv_g, ln_conv_b, sinks, w_conv_out, w_attn_out, w_out, g_ffn,
           w_group, b_group, w_expert, b_expert, w_gate, w_up, w_down, g_final):
    batch, seq, d = x.shape
    assert d == D_MODEL and seq % TOKEN_TILE == 0 and g_mix.shape[0] == 1
    t = batch * seq
    x2 = x.reshape(t, d)

    act, q, kv, sgates = _inproj(x2, g_mix[0][None, :], _prep_w_in(w_in[0]), w_dw[0, :, 0, :], b_dw[0][None, :],
                                 ln_conv_g[0][None, :], ln_conv_b[0][None, :], seq)

    wr_hi, wr_lo, b_r = _prep_router(w_group[0], b_group[0], w_expert[0], b_expert[0])
    h, xn2, route, wts = _mixer(
        x2, act, q, kv, sgates, sinks[0], w_conv_out[0].astype(bf16), w_attn_out[0].astype(bf16),
        w_out[0].astype(bf16), g_ffn[0][None, :], wr_hi, wr_lo, b_r, batch, seq)

    lpos, tab, seg, meta = _route(route)
    nb = _num_blocks(t)
    xs = _scatter(tab, seg, lpos, xn2, nb)
    ys = _experts(meta[0, :nb], meta[1, :1], xs, w_gate[0], w_up[0], w_down[0], nb)
    seg_back = seg.at[:, SEG_SPARE_LANE:SEG_SPARE_LANE + 2].set(0)
    out = _combine(tab, seg_back, lpos, wts, h, g_final[None, :], ys)
    return out.reshape(batch, seq, d)
```

```python
import functools

import numpy as np
import jax
import jax.numpy as jnp
from jax import lax
from jax.experimental import pallas as pl
from jax.experimental.pallas import tpu as pltpu

D_MODEL = 1024
CONV_CH = 512
CONV_WIDTH = 31
N_HEADS = 8
N_KV_HEADS = 2
HEAD_DIM = 64
ATTN_BLOCK = 128
N_GROUPS = 4
EXPERTS_PER_GROUP = 8
N_EXPERTS = N_GROUPS * EXPERTS_PER_GROUP
D_FF_EXPERT = 512
NORM_EPS = 1e-6

Q_DIM = N_HEADS * HEAD_DIM
KV_DIM = N_KV_HEADS * HEAD_DIM

LANES = 128
SUBLANES = 8
CONV_HALO = 32
CONV_ROWS = 32
PROJ_PIECE = 256
ROUTER_ROWS = SUBLANES + N_EXPERTS
NEG_BIG = -1e30

TOKEN_TILE = 512
ROUTE_TILE = 512
RUN_ALIGN = SUBLANES
EXPERT_BLOCK = 1024
VMEM_LIMIT = 56 * 1024 * 1024

f32 = jnp.float32
bf16 = jnp.bfloat16


def _rms(x, g):
    ms = jnp.mean(x * x, axis=-1, keepdims=True)
    return x * lax.rsqrt(ms + NORM_EPS) * g


def _sigmoid(x):
    return 1.0 / (1.0 + jnp.exp(-x))


PACKED = D_MODEL // 2
_HIGH_HALF = 0xFFFF0000


def _pack_rows(x, already_bf16=False):
    def bits(v):
        return pltpu.bitcast(v if already_bf16 else v.astype(bf16).astype(f32), jnp.uint32)
    low = lax.shift_right_logical(bits(x[:, :PACKED]), jnp.uint32(16))
    return low | (bits(x[:, PACKED:]) & jnp.uint32(_HIGH_HALF))


def _unpack_rows(p):
    low = pltpu.bitcast(lax.shift_left(p, jnp.uint32(16)), f32).astype(bf16)
    high = pltpu.bitcast(p & jnp.uint32(_HIGH_HALF), f32).astype(bf16)
    return low, high


def _inproj_kernel(tiles_per_seq, x_ref, g_ref, w_ref, wdw_ref, bdw_ref, lng_ref, lnb_ref,
                   act_ref, q_ref, kv_ref, sgate_ref, xn_s, vbuf):
    tm = x_ref.shape[0]
    xn_s[...] = _rms(x_ref[...], g_ref[...]).astype(bf16)

    def proj(lo, hi):
        return jnp.dot(xn_s[...], w_ref[:, lo:hi], preferred_element_type=f32)

    n_cb = CONV_CH // LANES
    first = lax.rem(pl.program_id(0), tiles_per_seq) == 0

    @pl.when(first)
    def _():
        vbuf[:, 0:CONV_HALO, :] = jnp.zeros((n_cb, CONV_HALO, LANES), f32)

    @pl.when(jnp.logical_not(first))
    def _():
        vbuf[:, 0:CONV_HALO, :] = vbuf[:, tm:tm + CONV_HALO, :]

    for b in range(n_cb):
        u = proj(2 * b * LANES, 2 * (b + 1) * LANES)
        vbuf[b, CONV_HALO:, :] = u[:, :LANES] * _sigmoid(u[:, LANES:])

    rows = CONV_ROWS
    tap0 = CONV_HALO - (CONV_WIDTH - 1)
    for c in range(tm // rows):
        accs = []
        for b in range(n_cb):
            acc = jnp.broadcast_to(bdw_ref[:, b * LANES:(b + 1) * LANES], (rows, LANES))
            for j in range(CONV_WIDTH):
                r0 = c * rows + tap0 + j
                acc = acc + wdw_ref[j:j + 1, b * LANES:(b + 1) * LANES] * vbuf[b, r0:r0 + rows, :]
            accs.append(acc)
        mu = sum(jnp.sum(a, axis=-1, keepdims=True) for a in accs) * (1.0 / CONV_CH)
        ds = [a - mu for a in accs]
        var = sum(jnp.sum(d * d, axis=-1, keepdims=True) for d in ds) * (1.0 / CONV_CH)
        inv = lax.rsqrt(var + NORM_EPS)
        for b in range(n_cb):
            y = ds[b] * inv * lng_ref[:, b * LANES:(b + 1) * LANES] + lnb_ref[:, b * LANES:(b + 1) * LANES]
            act_ref[c * rows:(c + 1) * rows, b * LANES:(b + 1) * LANES] = (y * _sigmoid(y)).astype(bf16)

    col = 2 * CONV_CH
    for out_ref, fn in ((q_ref, lambda z: z), (kv_ref, lambda z: z), (sgate_ref, _sigmoid)):
        for lo in range(0, out_ref.shape[1], PROJ_PIECE):
            out_ref[:, lo:lo + PROJ_PIECE] = fn(proj(col + lo, col + lo + PROJ_PIECE)).astype(bf16)
        col += out_ref.shape[1]


def _inproj(x2, g_mix, w_cat, w_dw, b_dw, ln_g, ln_b, seq):
    t = x2.shape[0]
    tm = TOKEN_TILE
    widths = (CONV_CH, Q_DIM, 4 * KV_DIM, 2 * D_MODEL)

    def full(a):
        return pl.BlockSpec(a.shape, lambda i: (0,) * a.ndim)

    return pl.pallas_call(
        functools.partial(_inproj_kernel, seq // tm),
        grid=(t // tm,),
        in_specs=[pl.BlockSpec((tm, D_MODEL), lambda i: (i, 0)),
                  full(g_mix), full(w_cat), full(w_dw), full(b_dw), full(ln_g), full(ln_b)],
        out_specs=[pl.BlockSpec((tm, w), lambda i: (i, 0)) for w in widths],
        out_shape=[jax.ShapeDtypeStruct((t, w), bf16) for w in widths],
        scratch_shapes=[pltpu.VMEM((tm, D_MODEL), bf16),
                        pltpu.VMEM((CONV_CH // LANES, tm + CONV_HALO, LANES), f32)],
        compiler_params=pltpu.CompilerParams(
            dimension_semantics=("arbitrary",), vmem_limit_bytes=VMEM_LIMIT),
        name="inproj",
    )(x2, g_mix, w_cat, w_dw, b_dw, ln_g, ln_b)


def _mixer_kernel(x_ref, act_ref, q_ref, kv_ref, kvp_ref, sgate_ref, sink_ref, bias_ref, wco_ref, wao_ref,
                  wo_ref, gffn_ref, wrh_ref, wrl_ref, br_ref,
                  h_ref, xn2_ref, route_ref, wts_ref,
                  kvall, attn, s_scr, m_scr):
    ts = x_ref.shape[0]
    first = pl.program_id(1) == 0

    kvall[0:ATTN_BLOCK, :] = jnp.where(first, jnp.zeros_like(kvp_ref[...]), kvp_ref[...])
    kvall[ATTN_BLOCK:, :] = kv_ref[...]
    nkeys = 2 * ATTN_BLOCK
    left_kv = lax.broadcasted_iota(jnp.int32, (nkeys, LANES), 1) < HEAD_DIM
    left_q = lax.broadcasted_iota(jnp.int32, (ATTN_BLOCK, LANES), 1) < HEAD_DIM
    key_lane = lax.broadcasted_iota(jnp.int32, (ATTN_BLOCK, 2 * nkeys), 1)
    prev_keys = (key_lane % nkeys) < ATTN_BLOCK
    no_prev = jnp.where(prev_keys, jnp.where(first, -jnp.inf, 0.0), 0.0)

    def block_diag(x):
        z = jnp.zeros_like(x)
        return jnp.concatenate([jnp.where(left_kv, x, z), jnp.where(left_kv, z, x)], axis=0)

    n_pairs = N_HEADS // 2
    for j in range(ts // ATTN_BLOCK):
        r0 = j * ATTN_BLOCK
        for kvh in range(N_KV_HEADS):
            kbd = block_diag(kvall[r0:r0 + nkeys, kvh * LANES:(kvh + 1) * LANES])
            for ii in range(2):
                i = 2 * kvh + ii
                qb = q_ref[r0:r0 + ATTN_BLOCK, i * LANES:(i + 1) * LANES]
                s = lax.dot_general(qb, kbd, (((1,), (1,)), ((), ())), preferred_element_type=f32)
                s = s + bias_ref[i]
                if j == 0:
                    s = s + no_prev
                s_scr[j * n_pairs + i] = s
                for half in range(2):
                    m = jnp.max(s[:, half * nkeys:(half + 1) * nkeys], axis=-1, keepdims=True)
                    m_scr[2 * (j * n_pairs + i) + half] = jnp.maximum(m, sink_ref[2 * i + half])

    for j in range(ts // ATTN_BLOCK):
        r0 = j * ATTN_BLOCK
        for kvh in range(N_KV_HEADS):
            vbd = block_diag(kvall[r0:r0 + nkeys, (N_KV_HEADS + kvh) * LANES:(N_KV_HEADS + kvh + 1) * LANES])
            for ii in range(2):
                i = 2 * kvh + ii
                s = s_scr[j * n_pairs + i]
                ps, ls = [], []
                for half in range(2):
                    m = m_scr[2 * (j * n_pairs + i) + half]
                    p = jnp.exp(s[:, half * nkeys:(half + 1) * nkeys] - m)
                    ps.append(p)
                    ls.append(jnp.sum(p, axis=-1, keepdims=True) + jnp.exp(sink_ref[2 * i + half] - m))
                p = jnp.concatenate(ps, axis=1).astype(bf16)
                o = jnp.dot(p, vbd, preferred_element_type=f32)
                o = o * jnp.where(left_q, 1.0 / ls[0], 1.0 / ls[1])
                attn[r0:r0 + ATTN_BLOCK, i * LANES:(i + 1) * LANES] = o.astype(bf16)

    conv_o = jnp.dot(act_ref[...], wco_ref[...], preferred_element_type=f32)
    attn_o = jnp.dot(attn[...], wao_ref[...], preferred_element_type=f32)
    merged = (sgate_ref[:, :D_MODEL].astype(f32) * conv_o
              + sgate_ref[:, D_MODEL:].astype(f32) * attn_o).astype(bf16)
    h = x_ref[...] + jnp.dot(merged, wo_ref[...], preferred_element_type=f32)
    h_ref[...] = h

    xn2 = _rms(h, gffn_ref[...])
    xn2_ref[...] = xn2.astype(bf16)
    xh = xn2.astype(bf16)
    xl = (xn2 - xh.astype(f32)).astype(bf16)
    logits = (jnp.dot(xh, wrh_ref[...], preferred_element_type=f32)
              + jnp.dot(xh, wrl_ref[...], preferred_element_type=f32)
              + jnp.dot(xl, wrh_ref[...], preferred_element_type=f32)) + br_ref[...]
    lt = logits.T
    sub = lax.broadcasted_iota(jnp.int32, (SUBLANES, ts), 0)

    gl = lt[0:SUBLANES]
    gmax = jnp.max(gl, axis=0, keepdims=True)
    gsel = jnp.min(jnp.where(gl == gmax, sub, SUBLANES), axis=0, keepdims=True)
    p_group = 1.0 / jnp.sum(jnp.exp(gl - gmax), axis=0, keepdims=True)

    e_in = lt[SUBLANES:2 * SUBLANES]
    for g in range(1, N_GROUPS):
        e_in = jnp.where(gsel == g, lt[(g + 1) * SUBLANES:(g + 2) * SUBLANES], e_in)
    m1 = jnp.max(e_in, axis=0, keepdims=True)
    i1 = jnp.min(jnp.where(e_in == m1, sub, SUBLANES), axis=0, keepdims=True)
    rest = jnp.where(sub == i1, -jnp.inf, e_in)
    m2 = jnp.max(rest, axis=0, keepdims=True)
    i2 = jnp.min(jnp.where(rest == m2, sub, SUBLANES), axis=0, keepdims=True)
    t2 = jnp.exp(m2 - m1)
    w1 = p_group / (1.0 + t2)
    w2 = p_group * t2 / (1.0 + t2)
    base = gsel * EXPERTS_PER_GROUP
    route_ref[...] = jnp.where(sub == 0, base + i1, jnp.where(sub == 1, base + i2, 0))
    wts_ref[...] = jnp.where(sub == 0, w1, jnp.where(sub == 1, w2, 0.0))


def _attn_bias():
    qi = np.arange(ATTN_BLOCK)[:, None]
    kj = np.arange(2 * ATTN_BLOCK)[None, :]
    rel = (ATTN_BLOCK + qi - kj).astype(np.float32)
    ok = (rel >= 0) & (rel < ATTN_BLOCK)
    slopes = np.array([2.0 ** (-8.0 * (h + 1) / N_HEADS) for h in range(N_HEADS)], np.float32)
    per_head = [np.where(ok, -(slopes[h] * rel), -np.inf).astype(np.float32) for h in range(N_HEADS)]
    return np.stack([np.concatenate([per_head[2 * i], per_head[2 * i + 1]], axis=1)
                     for i in range(N_HEADS // 2)])


def _mixer(x2, act, q, kv, sgates, sinks, wco, wao, wo, g_ffn, wr_hi, wr_lo, b_r, batch, seq):
    t = x2.shape[0]
    ts = TOKEN_TILE
    ns = seq // ts
    bias = jnp.asarray(_attn_bias())

    def row(b, s):
        return b * ns + s

    def full(a):
        return pl.BlockSpec(a.shape, lambda b, s: (0,) * a.ndim)

    in_specs = [
        pl.BlockSpec((ts, D_MODEL), lambda b, s: (row(b, s), 0)),
        pl.BlockSpec((ts, CONV_CH), lambda b, s: (row(b, s), 0)),
        pl.BlockSpec((ts, Q_DIM), lambda b, s: (row(b, s), 0)),
        pl.BlockSpec((ts, 4 * KV_DIM), lambda b, s: (row(b, s), 0)),
        pl.BlockSpec((ATTN_BLOCK, 4 * KV_DIM),
                     lambda b, s: (jnp.maximum(row(b, s) * (ts // ATTN_BLOCK) - 1, 0), 0)),
        pl.BlockSpec((ts, 2 * D_MODEL), lambda b, s: (row(b, s), 0)),
        pl.BlockSpec(memory_space=pltpu.SMEM),
        full(bias), full(wco), full(wao), full(wo), full(g_ffn), full(wr_hi), full(wr_lo), full(b_r),
    ]
    out_specs = [
        pl.BlockSpec((ts, D_MODEL), lambda b, s: (row(b, s), 0)),
        pl.BlockSpec((ts, D_MODEL), lambda b, s: (row(b, s), 0)),
        pl.BlockSpec((SUBLANES, ts), lambda b, s: (0, row(b, s))),
        pl.BlockSpec((SUBLANES, ts), lambda b, s: (0, row(b, s))),
    ]
    out_shape = [
        jax.ShapeDtypeStruct((t, D_MODEL), f32),
        jax.ShapeDtypeStruct((t, D_MODEL), bf16),
        jax.ShapeDtypeStruct((SUBLANES, t), jnp.int32),
        jax.ShapeDtypeStruct((SUBLANES, t), f32),
    ]
    return pl.pallas_call(
        _mixer_kernel,
        grid=(batch, ns),
        in_specs=in_specs,
        out_specs=out_specs,
        out_shape=out_shape,
        scratch_shapes=[
            pltpu.VMEM((ts + ATTN_BLOCK, 4 * KV_DIM), bf16),
            pltpu.VMEM((ts, Q_DIM), bf16),
            pltpu.VMEM((ts // ATTN_BLOCK * (N_HEADS // 2), ATTN_BLOCK, 4 * ATTN_BLOCK), f32),
            pltpu.VMEM((ts // ATTN_BLOCK * N_HEADS, ATTN_BLOCK, 1), f32),
        ],
        compiler_params=pltpu.CompilerParams(
            dimension_semantics=("arbitrary", "arbitrary"), vmem_limit_bytes=VMEM_LIMIT),
        name="mixer",
    )(x2, act, q, kv, kv, sgates, sinks, bias, wco, wao, wo, g_ffn, wr_hi, wr_lo, b_r)


def _local_rows(ts):
    return -(-(2 * ts + (RUN_ALIGN - 1) * N_EXPERTS) // LANES) * LANES


PAD_COARSE = 16
assert (EXPERT_BLOCK // RUN_ALIGN) % PAD_COARSE == 0
TABLE_LANES = 2 * LANES
SEG_NACT_LANE = 3 * N_EXPERTS
SEG_SPARE_LANE = 3 * N_EXPERTS + 1


def _num_blocks(t):
    run_rows = 2 * t + (RUN_ALIGN - 1) * N_EXPERTS * (t // ROUTE_TILE)
    return -(-(run_rows + N_EXPERTS * (EXPERT_BLOCK - RUN_ALIGN)) // EXPERT_BLOCK)


def _route_kernel(route_ref, lpos_ref, tab_ref, seg_ref, meta_ref):
    t = route_ref.shape[1]
    tr = ROUTE_TILE
    nbp = meta_ref.shape[1]
    chunks_per_block = EXPERT_BLOCK // RUN_ALIGN
    eiota = lax.broadcasted_iota(jnp.int32, (N_EXPERTS, tr), 0)
    before = (lax.broadcasted_iota(jnp.int32, (tr, tr), 0)
              < lax.broadcasted_iota(jnp.int32, (tr, tr), 1)).astype(bf16)
    lower = (lax.broadcasted_iota(jnp.int32, (N_EXPERTS, N_EXPERTS), 1)
             < lax.broadcasted_iota(jnp.int32, (N_EXPERTS, N_EXPERTS), 0)).astype(bf16)
    sub = lax.broadcasted_iota(jnp.int32, (N_EXPERTS, LANES), 0)
    lane = lax.broadcasted_iota(jnp.int32, (N_EXPERTS, LANES), 1)

    def to_lanes(col, offset):
        return jnp.sum(jnp.where(sub + offset == lane, col, 0.0), axis=0, keepdims=True)

    def expert_prefix(col):
        b = jnp.broadcast_to(col, (N_EXPERTS, LANES))
        hi = jnp.floor(b * (1.0 / 16.0))
        lo = b - 16.0 * hi
        return (16.0 * jnp.dot(lower, hi.astype(bf16), preferred_element_type=f32)
                + jnp.dot(lower, lo.astype(bf16), preferred_element_type=f32))[:, 0:1]

    lpos_ref[...] = jnp.zeros(lpos_ref.shape, jnp.int32)
    chunk_id = lax.broadcasted_iota(jnp.int32, (N_EXPERTS, TABLE_LANES), 1).astype(f32)
    chunk_expert = lax.broadcasted_iota(jnp.int32, (N_EXPERTS, TABLE_LANES), 0).astype(f32)

    def step(i, seen_chunks):
        off = pl.multiple_of(i * tr, tr)
        m1 = eiota == route_ref[0:1, pl.ds(off, tr)]
        m2 = eiota == route_ref[1:2, pl.ds(off, tr)]
        onehot = jnp.where(m1 | m2, 1.0, 0.0)
        within = jnp.dot(onehot.astype(bf16), before, preferred_element_type=f32)
        run_chunks = jnp.floor((jnp.sum(onehot, axis=1, keepdims=True) + (RUN_ALIGN - 1)) * (1.0 / RUN_ALIGN))
        run_start = expert_prefix(run_chunks)
        pos = within + RUN_ALIGN * run_start
        lpos_ref[0:1, pl.ds(off, tr)] = jnp.sum(jnp.where(m1, pos, 0.0), axis=0, keepdims=True).astype(jnp.int32)
        lpos_ref[1:2, pl.ds(off, tr)] = jnp.sum(jnp.where(m2, pos, 0.0), axis=0, keepdims=True).astype(jnp.int32)
        owner = (run_start <= chunk_id) & (chunk_id < run_start + run_chunks)
        rel = jnp.sum(jnp.where(owner, seen_chunks + chunk_id - run_start, 0.0), axis=0, keepdims=True)
        eid = jnp.sum(jnp.where(owner, chunk_expert, 0.0), axis=0, keepdims=True)
        n_used = jnp.sum(run_chunks, axis=0, keepdims=True)
        unused = chunk_id[0:1] >= n_used
        rel = jnp.where(unused, chunk_id[0:1] - n_used, rel)
        eid = jnp.where(unused, (SEG_SPARE_LANE + lax.rem(i, 2)).astype(f32), eid)
        row = jnp.concatenate([rel, eid], axis=1)
        tab_ref[i] = jnp.broadcast_to(row, (SUBLANES, 2 * TABLE_LANES)).astype(jnp.int32)
        return seen_chunks + run_chunks

    used_chunks = lax.fori_loop(0, t // tr, step, jnp.zeros((N_EXPERTS, 1), f32))

    nblk = jnp.floor((used_chunks + (chunks_per_block - 1)) * (1.0 / chunks_per_block))
    first_blk = expert_prefix(nblk)
    nact = jnp.sum(nblk, axis=0, keepdims=True)
    spare_chunk = _num_blocks(t) * chunks_per_block
    seg_row = (to_lanes(first_blk * chunks_per_block, 0) + to_lanes(used_chunks, N_EXPERTS)
               + to_lanes(nblk * chunks_per_block, 2 * N_EXPERTS)
               + jnp.where(lane[0:1] == SEG_NACT_LANE, nact, 0.0)
               + jnp.where(lane[0:1] == SEG_SPARE_LANE, float(spare_chunk), 0.0)
               + jnp.where(lane[0:1] == SEG_SPARE_LANE + 1, float(spare_chunk + _local_rows(tr) // RUN_ALIGN), 0.0))
    seg_ref[...] = jnp.broadcast_to(seg_row, (SUBLANES, LANES)).astype(jnp.int32)

    blk = lax.broadcasted_iota(jnp.int32, (N_EXPERTS, nbp), 1).astype(f32)
    owner = (first_blk <= blk) & (blk < first_blk + nblk)
    expert_id = lax.broadcasted_iota(jnp.int32, (N_EXPERTS, nbp), 0).astype(f32)
    bexp = jnp.sum(jnp.where(owner, expert_id, 0.0), axis=0, keepdims=True)
    row8 = lax.broadcasted_iota(jnp.int32, (SUBLANES, nbp), 0)
    meta_ref[...] = jnp.where(row8 == 0, bexp, jnp.where(row8 == 1, nact, 0.0)).astype(jnp.int32)


def _route(route):
    t = route.shape[1]
    nt = t // ROUTE_TILE
    nbp = -(-_num_blocks(t) // LANES) * LANES
    return pl.pallas_call(
        _route_kernel,
        grid=(1,),
        in_specs=[pl.BlockSpec(route.shape, lambda i: (0, 0))],
        out_specs=[pl.BlockSpec(route.shape, lambda i: (0, 0)),
                   pl.BlockSpec((nt, SUBLANES, 2 * TABLE_LANES), lambda i: (0, 0, 0)),
                   pl.BlockSpec((SUBLANES, LANES), lambda i: (0, 0)),
                   pl.BlockSpec((SUBLANES, nbp), lambda i: (0, 0))],
        out_shape=[jax.ShapeDtypeStruct(route.shape, jnp.int32),
                   jax.ShapeDtypeStruct((nt, SUBLANES, 2 * TABLE_LANES), jnp.int32),
                   jax.ShapeDtypeStruct((SUBLANES, LANES), jnp.int32),
                   jax.ShapeDtypeStruct((SUBLANES, nbp), jnp.int32)],
        compiler_params=pltpu.CompilerParams(
            dimension_semantics=("arbitrary",), vmem_limit_bytes=VMEM_LIMIT),
        name="route",
    )(route)


def _chunk_row(tab_ref, seg_ref, k):
    chunk = seg_ref[0, tab_ref[0, 0, TABLE_LANES + k]] + tab_ref[0, 0, k]
    return pl.multiple_of(chunk * RUN_ALIGN, RUN_ALIGN)


def _scatter_kernel(tab_ref, seg_ref, lpos_ref, x_ref, xs_ref, xsl, zrows, sem, zsem):
    lrows, ts = xsl.shape[1], x_ref.shape[0]
    n_blocks = (xs_ref.shape[0] - 2 * lrows) // EXPERT_BLOCK
    i = pl.program_id(0)
    last = pl.num_programs(0) - 1
    slot = lax.rem(i, 2)

    def drain(s):
        pltpu.make_async_copy(xsl.at[s], xs_ref.at[pl.ds(0, lrows)], sem.at[s]).wait()

    @pl.when(i >= 2)
    def _():
        drain(slot)

    j = lax.broadcasted_iota(jnp.int32, (lrows, ts), 0)
    perm = jnp.where((j == lpos_ref[0:1, :]) | (j == lpos_ref[1:2, :]), 1.0, 0.0).astype(bf16)
    xsl[slot] = _pack_rows(jnp.dot(perm, x_ref[...], preferred_element_type=f32), already_bf16=True)

    for k in range(lrows // RUN_ALIGN):
        pltpu.make_async_copy(xsl.at[slot, pl.ds(k * RUN_ALIGN, RUN_ALIGN)],
                              xs_ref.at[pl.ds(_chunk_row(tab_ref, seg_ref, k), RUN_ALIGN)], sem.at[slot]).start()

    @pl.when(i == last)
    def _():
        drain(slot)

        @pl.when(i >= 1)
        def _():
            drain(1 - slot)

        zrows[...] = jnp.zeros(zrows.shape, zrows.dtype)

        def zcopy(row, n):
            return pltpu.make_async_copy(zrows.at[pl.ds(0, n)], xs_ref.at[pl.ds(row, n)], zsem)

        def for_each_unused_block(fn):
            def body(b, carry):
                fn(pl.multiple_of(b * EXPERT_BLOCK, EXPERT_BLOCK), EXPERT_BLOCK)
                return carry

            lax.fori_loop(seg_ref[0, SEG_NACT_LANE], n_blocks, body, 0)
            for off in range(0, 2 * lrows, EXPERT_BLOCK):
                fn(n_blocks * EXPERT_BLOCK + off, min(EXPERT_BLOCK, 2 * lrows - off))

        for_each_unused_block(lambda row, n: zcopy(row, n).start())
        for_each_unused_block(lambda row, n: zcopy(row, n).wait())

        def for_each_pad_chunk(fn):
            def seg(e, carry):
                g0 = seg_ref[0, e]
                used = seg_ref[0, N_EXPERTS + e]
                total = seg_ref[0, 2 * N_EXPERTS + e]

                def chunk(c, carry2):
                    fn(pl.multiple_of((g0 + c) * RUN_ALIGN, RUN_ALIGN), RUN_ALIGN)
                    return carry2

                def coarse(c, carry2):
                    fn(pl.multiple_of((g0 + c * PAD_COARSE) * RUN_ALIGN, PAD_COARSE * RUN_ALIGN),
                       PAD_COARSE * RUN_ALIGN)
                    return carry2

                aligned = lax.div(used + (PAD_COARSE - 1), PAD_COARSE)
                lax.fori_loop(used, aligned * PAD_COARSE, chunk, 0)
                lax.fori_loop(aligned, lax.div(total, PAD_COARSE), coarse, 0)
                return carry

            lax.fori_loop(0, N_EXPERTS, seg, 0)

        for_each_pad_chunk(lambda row, n: zcopy(row, n).start())
        for_each_pad_chunk(lambda row, n: zcopy(row, n).wait())


def _scatter(tab, seg, lpos, xn2, n_blocks):
    t = xn2.shape[0]
    ts = ROUTE_TILE
    lrows = _local_rows(ts)
    return pl.pallas_call(
        _scatter_kernel,
        grid=(t // ts,),
        in_specs=[
            pl.BlockSpec((1, SUBLANES, 2 * TABLE_LANES), lambda i: (i, 0, 0), memory_space=pltpu.SMEM),
            pl.BlockSpec((SUBLANES, LANES), lambda i: (0, 0), memory_space=pltpu.SMEM),
            pl.BlockSpec((SUBLANES, ts), lambda i: (0, i)),
            pl.BlockSpec((ts, D_MODEL), lambda i: (i, 0)),
        ],
        out_specs=pl.BlockSpec(memory_space=pl.ANY),
        out_shape=jax.ShapeDtypeStruct((n_blocks * EXPERT_BLOCK + 2 * lrows, PACKED), jnp.uint32),
        scratch_shapes=[pltpu.VMEM((2, lrows, PACKED), jnp.uint32), pltpu.VMEM((EXPERT_BLOCK, PACKED), jnp.uint32),
                        pltpu.SemaphoreType.DMA((2,)), pltpu.SemaphoreType.DMA],
        compiler_params=pltpu.CompilerParams(
            dimension_semantics=("arbitrary",), vmem_limit_bytes=VMEM_LIMIT, has_side_effects=True),
        name="scatter",
    )(tab, seg, lpos, xn2)


def _expert_kernel(bexp_ref, nact_ref, xs_ref, wg_ref, wu_ref, wd_ref, ys_ref, wg_b, wu_b, wd_b):
    b = pl.program_id(0)
    active = b < nact_ref[0]
    new_expert = jnp.logical_or(b == 0, bexp_ref[b] != bexp_ref[jnp.maximum(b - 1, 0)])

    @pl.when(jnp.logical_not(active))
    def _():
        ys_ref[...] = jnp.zeros(ys_ref.shape, ys_ref.dtype)

    @pl.when(jnp.logical_and(active, new_expert))
    def _():
        wg_b[...] = wg_ref[0].astype(bf16)
        wu_b[...] = wu_ref[0].astype(bf16)
        wd_b[...] = wd_ref[0].astype(bf16)

    @pl.when(active)
    def _():
        x_lo, x_hi = _unpack_rows(xs_ref[...])

        def up(w_ref):
            return (jnp.dot(x_lo, w_ref[:PACKED, :], preferred_element_type=f32)
                    + jnp.dot(x_hi, w_ref[PACKED:, :], preferred_element_type=f32))

        g = up(wg_b)
        hmid = (g * _sigmoid(g) * up(wu_b)).astype(bf16)
        ys_ref[...] = _pack_rows(jnp.dot(hmid, wd_b[...], preferred_element_type=f32))


def _experts(bexp, nact, xs, wg, wu, wd, nb):
    n_rows = nb * EXPERT_BLOCK

    def blk(b, bexp_ref, nact_ref):
        return jnp.minimum(b, nact_ref[0] - 1)

    grid_spec = pltpu.PrefetchScalarGridSpec(
        num_scalar_prefetch=2,
        grid=(nb,),
        in_specs=[
            pl.BlockSpec((EXPERT_BLOCK, PACKED), lambda b, e, n: (blk(b, e, n), 0)),
            pl.BlockSpec((1, D_MODEL, D_FF_EXPERT), lambda b, e, n: (e[blk(b, e, n)], 0, 0)),
            pl.BlockSpec((1, D_MODEL, D_FF_EXPERT), lambda b, e, n: (e[blk(b, e, n)], 0, 0)),
            pl.BlockSpec((1, D_FF_EXPERT, D_MODEL), lambda b, e, n: (e[blk(b, e, n)], 0, 0)),
        ],
        out_specs=pl.BlockSpec((EXPERT_BLOCK, PACKED), lambda b, e, n: (b, 0)),
        scratch_shapes=[pltpu.VMEM((D_MODEL, D_FF_EXPERT), bf16), pltpu.VMEM((D_MODEL, D_FF_EXPERT), bf16),
                        pltpu.VMEM((D_FF_EXPERT, D_MODEL), bf16)],
    )
    return pl.pallas_call(
        _expert_kernel,
        grid_spec=grid_spec,
        out_shape=jax.ShapeDtypeStruct((n_rows, PACKED), jnp.uint32),
        compiler_params=pltpu.CompilerParams(
            dimension_semantics=("arbitrary",), vmem_limit_bytes=VMEM_LIMIT),
        name="experts",
    )(bexp, nact, xs, wg, wu, wd)


def _combine_kernel(tab_ref, tab_next_ref, seg_ref, lpos_ref, wts_ref, h_ref, gf_ref, ys_ref, out_ref, ybuf, sem):
    ts, lrows = h_ref.shape[0], ybuf.shape[1]
    i = pl.program_id(0)
    slot = lax.rem(i, 2)

    def fetch(t_ref, s):
        for k in range(lrows // RUN_ALIGN):
            pltpu.make_async_copy(ys_ref.at[pl.ds(_chunk_row(t_ref, seg_ref, k), RUN_ALIGN)],
                                  ybuf.at[s, pl.ds(k * RUN_ALIGN, RUN_ALIGN)], sem.at[s]).start()

    @pl.when(i == 0)
    def _():
        fetch(tab_ref, 0)

    @pl.when(i + 1 < pl.num_programs(0))
    def _():
        fetch(tab_next_ref, 1 - slot)

    pltpu.make_async_copy(ys_ref.at[pl.ds(0, lrows)], ybuf.at[slot], sem.at[slot]).wait()

    info = jnp.concatenate([lpos_ref[...].astype(f32), wts_ref[...],
                            jnp.zeros((LANES - 2 * SUBLANES, ts), f32)], axis=0).T
    jl = lax.broadcasted_iota(jnp.int32, (ts, lrows), 1).astype(f32)
    mix = (jnp.where(jl == info[:, 0:1], info[:, SUBLANES:SUBLANES + 1], 0.0)
           + jnp.where(jl == info[:, 1:2], info[:, SUBLANES + 1:SUBLANES + 2], 0.0)).astype(bf16)
    moe = jnp.concatenate([jnp.dot(mix, y, preferred_element_type=f32) for y in _unpack_rows(ybuf[slot])],
                          axis=1)
    out_ref[...] = _rms(h_ref[...] + moe, gf_ref[...])


def _combine(tab, seg, lpos, wts, h, g_final, ys):
    t = h.shape[0]
    ts = ROUTE_TILE
    nt = t // ts
    lrows = _local_rows(ts)
    assert ys.shape[0] >= lrows
    return pl.pallas_call(
        _combine_kernel,
        grid=(nt,),
        in_specs=[
            pl.BlockSpec((1, SUBLANES, 2 * TABLE_LANES), lambda i: (i, 0, 0), memory_space=pltpu.SMEM),
            pl.BlockSpec((1, SUBLANES, 2 * TABLE_LANES), lambda i: (jnp.minimum(i + 1, nt - 1), 0, 0),
                         memory_space=pltpu.SMEM),
            pl.BlockSpec((SUBLANES, LANES), lambda i: (0, 0), memory_space=pltpu.SMEM),
            pl.BlockSpec((SUBLANES, ts), lambda i: (0, i)),
            pl.BlockSpec((SUBLANES, ts), lambda i: (0, i)),
            pl.BlockSpec((ts, D_MODEL), lambda i: (i, 0)),
            pl.BlockSpec((1, D_MODEL), lambda i: (0, 0)),
            pl.BlockSpec(memory_space=pl.ANY),
        ],
        out_specs=pl.BlockSpec((ts, D_MODEL), lambda i: (i, 0)),
        out_shape=jax.ShapeDtypeStruct((t, D_MODEL), f32),
        scratch_shapes=[pltpu.VMEM((2, lrows, PACKED), jnp.uint32), pltpu.SemaphoreType.DMA((2,))],
        compiler_params=pltpu.CompilerParams(
            dimension_semantics=("arbitrary",), vmem_limit_bytes=VMEM_LIMIT),
        name="combine",
    )(tab, tab, seg, lpos, wts, h, g_final, ys)


def _prep_w_in(w_in):
    c0 = 2 * CONV_CH
    c1 = c0 + Q_DIM
    c2 = c1 + KV_DIM
    c3 = c2 + KV_DIM
    wk = w_in[:, c1:c2]
    wv = w_in[:, c2:c3]

    def dup(w):
        return jnp.concatenate([w[:, h * HEAD_DIM:(h + 1) * HEAD_DIM] for h in range(N_KV_HEADS)
                                for _ in range(2)], axis=1)

    glu = jnp.concatenate([w_in[:, half + b * LANES:half + (b + 1) * LANES]
                           for b in range(CONV_CH // LANES) for half in (0, CONV_CH)], axis=1)
    return jnp.concatenate(
        [glu, w_in[:, c0:c1] * (HEAD_DIM ** -0.5), dup(wk), dup(wv), w_in[:, c3:]],
        axis=1).astype(bf16)


def _prep_router(w_group, b_group, w_expert, b_expert):
    d = w_group.shape[0]
    w = jnp.zeros((d, LANES), f32)
    w = w.at[:, :N_GROUPS].set(w_group).at[:, SUBLANES:ROUTER_ROWS].set(w_expert)
    b = jnp.full((1, LANES), NEG_BIG, f32)
    b = b.at[0, :N_GROUPS].set(b_group).at[0, SUBLANES:ROUTER_ROWS].set(b_expert)
    w_hi = w.astype(bf16)
    w_lo = (w - w_hi.astype(f32)).astype(bf16)
    return w_hi, w_lo, b


def kernel(x, g_mix, w_in, w_dw, b_dw, ln_conv_g, ln_conv_b, sinks, w_conv_out, w_attn_out, w_out, g_ffn,
           w_group, b_group, w_expert, b_expert, w_gate, w_up, w_down, g_final):
    batch, seq, d = x.shape
    assert d == D_MODEL and seq % TOKEN_TILE == 0 and g_mix.shape[0] == 1
    t = batch * seq
    x2 = x.reshape(t, d)

    act, q, kv, sgates = _inproj(x2, g_mix[0][None, :], _prep_w_in(w_in[0]), w_dw[0, :, 0, :], b_dw[0][None, :],
                                 ln_conv_g[0][None, :], ln_conv_b[0][None, :], seq)

    wr_hi, wr_lo, b_r = _prep_router(w_group[0], b_group[0], w_expert[0], b_expert[0])
    h, xn2, route, wts = _mixer(
        x2, act, q, kv, sgates, sinks[0], w_conv_out[0].astype(bf16), w_attn_out[0].astype(bf16),
        w_out[0].astype(bf16), g_ffn[0][None, :], wr_hi, wr_lo, b_r, batch, seq)

    lpos, tab, seg, meta = _route(route)
    nb = _num_blocks(t)
    xs = _scatter(tab, seg, lpos, xn2, nb)
    ys = _experts(meta[0, :nb], meta[1, :1], xs, w_gate[0], w_up[0], w_down[0], nb)
    seg_back = seg.at[:, SEG_SPARE_LANE:SEG_SPARE_LANE + 2].set(0)
    out = _combine(tab, seg_back, lpos, wts, h, g_final[None, :], ys)
    return out.reshape(batch, seq, d)
```

```python
import functools

import numpy as np
import jax
import jax.numpy as jnp
from jax import lax
from jax.experimental import pallas as pl
from jax.experimental.pallas import tpu as pltpu

D_MODEL = 1024
CONV_CH = 512
CONV_WIDTH = 31
N_HEADS = 8
N_KV_HEADS = 2
HEAD_DIM = 64
ATTN_BLOCK = 128
N_GROUPS = 4
EXPERTS_PER_GROUP = 8
N_EXPERTS = N_GROUPS * EXPERTS_PER_GROUP
D_FF_EXPERT = 512
NORM_EPS = 1e-6

Q_DIM = N_HEADS * HEAD_DIM
KV_DIM = N_KV_HEADS * HEAD_DIM

LANES = 128
SUBLANES = 8
CONV_HALO = 32
CONV_ROWS = 32
PROJ_PIECE = 256
ROUTER_ROWS = SUBLANES + N_EXPERTS
NEG_BIG = -1e30

TOKEN_TILE = 512
MIXER_TILE = 1024
ROUTE_TILE = 512
RUN_ALIGN = SUBLANES
EXPERT_BLOCK = 1024
VMEM_LIMIT = 58 * 1024 * 1024

f32 = jnp.float32
bf16 = jnp.bfloat16


def _rms(x, g):
    ms = jnp.mean(x * x, axis=-1, keepdims=True)
    return x * lax.rsqrt(ms + NORM_EPS) * g


def _sigmoid(x):
    return 1.0 / (1.0 + jnp.exp(-x))


PACKED = D_MODEL // 2
_HIGH_HALF = 0xFFFF0000


def _pack_rows(x, already_bf16=False):
    def bits(v):
        return pltpu.bitcast(v if already_bf16 else v.astype(bf16).astype(f32), jnp.uint32)
    low = lax.shift_right_logical(bits(x[:, :PACKED]), jnp.uint32(16))
    return low | (bits(x[:, PACKED:]) & jnp.uint32(_HIGH_HALF))


def _unpack_rows(p):
    low = pltpu.bitcast(lax.shift_left(p, jnp.uint32(16)), f32).astype(bf16)
    high = pltpu.bitcast(p & jnp.uint32(_HIGH_HALF), f32).astype(bf16)
    return low, high


def _inproj_kernel(tiles_per_seq, x_ref, g_ref, w_ref, wdw_ref, bdw_ref, lng_ref, lnb_ref,
                   act_ref, q_ref, kv_ref, sgate_ref, xn_s, vbuf):
    tm = x_ref.shape[0]
    xn_s[...] = _rms(x_ref[...], g_ref[...]).astype(bf16)

    def proj(lo, hi):
        return jnp.dot(xn_s[...], w_ref[:, lo:hi], preferred_element_type=f32)

    n_cb = CONV_CH // LANES
    first = lax.rem(pl.program_id(0), tiles_per_seq) == 0

    @pl.when(first)
    def _():
        vbuf[:, 0:CONV_HALO, :] = jnp.zeros((n_cb, CONV_HALO, LANES), f32)

    @pl.when(jnp.logical_not(first))
    def _():
        vbuf[:, 0:CONV_HALO, :] = vbuf[:, tm:tm + CONV_HALO, :]

    for b in range(n_cb):
        u = proj(2 * b * LANES, 2 * (b + 1) * LANES)
        vbuf[b, CONV_HALO:, :] = u[:, :LANES] * _sigmoid(u[:, LANES:])

    rows = CONV_ROWS
    tap0 = CONV_HALO - (CONV_WIDTH - 1)
    for c in range(tm // rows):
        accs = []
        for b in range(n_cb):
            acc = jnp.broadcast_to(bdw_ref[:, b * LANES:(b + 1) * LANES], (rows, LANES))
            for j in range(CONV_WIDTH):
                r0 = c * rows + tap0 + j
                acc = acc + wdw_ref[j:j + 1, b * LANES:(b + 1) * LANES] * vbuf[b, r0:r0 + rows, :]
            accs.append(acc)
        mu = sum(jnp.sum(a, axis=-1, keepdims=True) for a in accs) * (1.0 / CONV_CH)
        ds = [a - mu for a in accs]
        var = sum(jnp.sum(d * d, axis=-1, keepdims=True) for d in ds) * (1.0 / CONV_CH)
        inv = lax.rsqrt(var + NORM_EPS)
        for b in range(n_cb):
            y = ds[b] * inv * lng_ref[:, b * LANES:(b + 1) * LANES] + lnb_ref[:, b * LANES:(b + 1) * LANES]
            act_ref[c * rows:(c + 1) * rows, b * LANES:(b + 1) * LANES] = (y * _sigmoid(y)).astype(bf16)

    col = 2 * CONV_CH
    for out_ref, fn in ((q_ref, lambda z: z), (kv_ref, lambda z: z), (sgate_ref, _sigmoid)):
        for lo in range(0, out_ref.shape[1], PROJ_PIECE):
            out_ref[:, lo:lo + PROJ_PIECE] = fn(proj(col + lo, col + lo + PROJ_PIECE)).astype(bf16)
        col += out_ref.shape[1]


def _inproj(x2, g_mix, w_cat, w_dw, b_dw, ln_g, ln_b, seq):
    t = x2.shape[0]
    tm = TOKEN_TILE
    widths = (CONV_CH, Q_DIM, 4 * KV_DIM, 2 * D_MODEL)

    def full(a):
        return pl.BlockSpec(a.shape, lambda i: (0,) * a.ndim)

    return pl.pallas_call(
        functools.partial(_inproj_kernel, seq // tm),
        grid=(t // tm,),
        in_specs=[pl.BlockSpec((tm, D_MODEL), lambda i: (i, 0)),
                  full(g_mix), full(w_cat), full(w_dw), full(b_dw), full(ln_g), full(ln_b)],
        out_specs=[pl.BlockSpec((tm, w), lambda i: (i, 0)) for w in widths],
        out_shape=[jax.ShapeDtypeStruct((t, w), bf16) for w in widths],
        scratch_shapes=[pltpu.VMEM((tm, D_MODEL), bf16),
                        pltpu.VMEM((CONV_CH // LANES, tm + CONV_HALO, LANES), f32)],
        compiler_params=pltpu.CompilerParams(
            dimension_semantics=("arbitrary",), vmem_limit_bytes=VMEM_LIMIT),
        name="inproj",
    )(x2, g_mix, w_cat, w_dw, b_dw, ln_g, ln_b)


def _mixer_kernel(x_ref, act_ref, q_ref, kv_ref, kvp_ref, sgate_ref, sink_ref, bias_ref, wco_ref, wao_ref,
                  wo_ref, gffn_ref, wrh_ref, wrl_ref, br_ref,
                  h_ref, xn2_ref, route_ref, wts_ref,
                  kvall, attn, s_scr, m_scr):
    ts = x_ref.shape[0]
    first = pl.program_id(1) == 0

    kvall[0:ATTN_BLOCK, :] = jnp.where(first, jnp.zeros_like(kvp_ref[...]), kvp_ref[...])
    kvall[ATTN_BLOCK:, :] = kv_ref[...]
    nkeys = 2 * ATTN_BLOCK
    left_kv = lax.broadcasted_iota(jnp.int32, (nkeys, LANES), 1) < HEAD_DIM
    left_q = lax.broadcasted_iota(jnp.int32, (ATTN_BLOCK, LANES), 1) < HEAD_DIM
    key_lane = lax.broadcasted_iota(jnp.int32, (ATTN_BLOCK, 2 * nkeys), 1)
    prev_keys = (key_lane % nkeys) < ATTN_BLOCK
    no_prev = jnp.where(prev_keys, jnp.where(first, -jnp.inf, 0.0), 0.0)

    def block_diag(x):
        z = jnp.zeros_like(x)
        return jnp.concatenate([jnp.where(left_kv, x, z), jnp.where(left_kv, z, x)], axis=0)

    n_pairs = N_HEADS // 2
    for j in range(ts // ATTN_BLOCK):
        r0 = j * ATTN_BLOCK
        for kvh in range(N_KV_HEADS):
            kbd = block_diag(kvall[r0:r0 + nkeys, kvh * LANES:(kvh + 1) * LANES])
            for ii in range(2):
                i = 2 * kvh + ii
                qb = q_ref[r0:r0 + ATTN_BLOCK, i * LANES:(i + 1) * LANES]
                s = lax.dot_general(qb, kbd, (((1,), (1,)), ((), ())), preferred_element_type=f32)
                s = s + bias_ref[i]
                if j == 0:
                    s = s + no_prev
                s_scr[j * n_pairs + i] = s
                for half in range(2):
                    m = jnp.max(s[:, half * nkeys:(half + 1) * nkeys], axis=-1, keepdims=True)
                    m_scr[2 * (j * n_pairs + i) + half] = jnp.maximum(m, sink_ref[2 * i + half])

    for j in range(ts // ATTN_BLOCK):
        r0 = j * ATTN_BLOCK
        for kvh in range(N_KV_HEADS):
            vbd = block_diag(kvall[r0:r0 + nkeys, (N_KV_HEADS + kvh) * LANES:(N_KV_HEADS + kvh + 1) * LANES])
            for ii in range(2):
                i = 2 * kvh + ii
                s = s_scr[j * n_pairs + i]
                ps, ls = [], []
                for half in range(2):
                    m = m_scr[2 * (j * n_pairs + i) + half]
                    p = jnp.exp(s[:, half * nkeys:(half + 1) * nkeys] - m)
                    ps.append(p)
                    ls.append(jnp.sum(p, axis=-1, keepdims=True) + jnp.exp(sink_ref[2 * i + half] - m))
                p = jnp.concatenate(ps, axis=1).astype(bf16)
                o = jnp.dot(p, vbd, preferred_element_type=f32)
                o = o * jnp.where(left_q, 1.0 / ls[0], 1.0 / ls[1])
                attn[r0:r0 + ATTN_BLOCK, i * LANES:(i + 1) * LANES] = o.astype(bf16)

    conv_o = jnp.dot(act_ref[...], wco_ref[...], preferred_element_type=f32)
    attn_o = jnp.dot(attn[...], wao_ref[...], preferred_element_type=f32)
    merged = (sgate_ref[:, :D_MODEL].astype(f32) * conv_o
              + sgate_ref[:, D_MODEL:].astype(f32) * attn_o).astype(bf16)
    h = x_ref[...] + jnp.dot(merged, wo_ref[...], preferred_element_type=f32)
    h_ref[...] = h

    xn2 = _rms(h, gffn_ref[...])
    xn2_ref[...] = xn2.astype(bf16)
    xh = xn2.astype(bf16)
    xl = (xn2 - xh.astype(f32)).astype(bf16)
    logits = (jnp.dot(xh, wrh_ref[...], preferred_element_type=f32)
              + jnp.dot(xh, wrl_ref[...], preferred_element_type=f32)
              + jnp.dot(xl, wrh_ref[...], preferred_element_type=f32)) + br_ref[...]
    lt = logits.T
    sub = lax.broadcasted_iota(jnp.int32, (SUBLANES, ts), 0)

    gl = lt[0:SUBLANES]
    gmax = jnp.max(gl, axis=0, keepdims=True)
    gsel = jnp.min(jnp.where(gl == gmax, sub, SUBLANES), axis=0, keepdims=True)
    p_group = 1.0 / jnp.sum(jnp.exp(gl - gmax), axis=0, keepdims=True)

    e_in = lt[SUBLANES:2 * SUBLANES]
    for g in range(1, N_GROUPS):
        e_in = jnp.where(gsel == g, lt[(g + 1) * SUBLANES:(g + 2) * SUBLANES], e_in)
    m1 = jnp.max(e_in, axis=0, keepdims=True)
    i1 = jnp.min(jnp.where(e_in == m1, sub, SUBLANES), axis=0, keepdims=True)
    rest = jnp.where(sub == i1, -jnp.inf, e_in)
    m2 = jnp.max(rest, axis=0, keepdims=True)
    i2 = jnp.min(jnp.where(rest == m2, sub, SUBLANES), axis=0, keepdims=True)
    t2 = jnp.exp(m2 - m1)
    w1 = p_group / (1.0 + t2)
    w2 = p_group * t2 / (1.0 + t2)
    base = gsel * EXPERTS_PER_GROUP
    route_ref[...] = jnp.where(sub == 0, base + i1, jnp.where(sub == 1, base + i2, 0))
    wts_ref[...] = jnp.where(sub == 0, w1, jnp.where(sub == 1, w2, 0.0))


def _attn_bias():
    qi = np.arange(ATTN_BLOCK)[:, None]
    kj = np.arange(2 * ATTN_BLOCK)[None, :]
    rel = (ATTN_BLOCK + qi - kj).astype(np.float32)
    ok = (rel >= 0) & (rel < ATTN_BLOCK)
    slopes = np.array([2.0 ** (-8.0 * (h + 1) / N_HEADS) for h in range(N_HEADS)], np.float32)
    per_head = [np.where(ok, -(slopes[h] * rel), -np.inf).astype(np.float32) for h in range(N_HEADS)]
    return np.stack([np.concatenate([per_head[2 * i], per_head[2 * i + 1]], axis=1)
                     for i in range(N_HEADS // 2)])


def _mixer(x2, act, q, kv, sgates, sinks, wco, wao, wo, g_ffn, wr_hi, wr_lo, b_r, batch, seq):
    t = x2.shape[0]
    ts = MIXER_TILE
    ns = seq // ts
    bias = jnp.asarray(_attn_bias())

    def row(b, s):
        return b * ns + s

    def full(a):
        return pl.BlockSpec(a.shape, lambda b, s: (0,) * a.ndim, pipeline_mode=pl.Buffered(1))

    in_specs = [
        pl.BlockSpec((ts, D_MODEL), lambda b, s: (row(b, s), 0)),
        pl.BlockSpec((ts, CONV_CH), lambda b, s: (row(b, s), 0)),
        pl.BlockSpec((ts, Q_DIM), lambda b, s: (row(b, s), 0)),
        pl.BlockSpec((ts, 4 * KV_DIM), lambda b, s: (row(b, s), 0)),
        pl.BlockSpec((ATTN_BLOCK, 4 * KV_DIM),
                     lambda b, s: (jnp.maximum(row(b, s) * (ts // ATTN_BLOCK) - 1, 0), 0)),
        pl.BlockSpec((ts, 2 * D_MODEL), lambda b, s: (row(b, s), 0)),
        pl.BlockSpec(memory_space=pltpu.SMEM),
        full(bias), full(wco), full(wao), full(wo), full(g_ffn), full(wr_hi), full(wr_lo), full(b_r),
    ]
    out_specs = [
        pl.BlockSpec((ts, D_MODEL), lambda b, s: (row(b, s), 0)),
        pl.BlockSpec((ts, D_MODEL), lambda b, s: (row(b, s), 0)),
        pl.BlockSpec((SUBLANES, ts), lambda b, s: (0, row(b, s))),
        pl.BlockSpec((SUBLANES, ts), lambda b, s: (0, row(b, s))),
    ]
    out_shape = [
        jax.ShapeDtypeStruct((t, D_MODEL), f32),
        jax.ShapeDtypeStruct((t, D_MODEL), bf16),
        jax.ShapeDtypeStruct((SUBLANES, t), jnp.int32),
        jax.ShapeDtypeStruct((SUBLANES, t), f32),
    ]
    return pl.pallas_call(
        _mixer_kernel,
        grid=(batch, ns),
        in_specs=in_specs,
        out_specs=out_specs,
        out_shape=out_shape,
        scratch_shapes=[
            pltpu.VMEM((ts + ATTN_BLOCK, 4 * KV_DIM), bf16),
            pltpu.VMEM((ts, Q_DIM), bf16),
            pltpu.VMEM((ts // ATTN_BLOCK * (N_HEADS // 2), ATTN_BLOCK, 4 * ATTN_BLOCK), f32),
            pltpu.VMEM((ts // ATTN_BLOCK * N_HEADS, ATTN_BLOCK, 1), f32),
        ],
        compiler_params=pltpu.CompilerParams(
            dimension_semantics=("arbitrary", "arbitrary"), vmem_limit_bytes=VMEM_LIMIT),
        name="mixer",
    )(x2, act, q, kv, kv, sgates, sinks, bias, wco, wao, wo, g_ffn, wr_hi, wr_lo, b_r)


def _local_rows(ts):
    return -(-(2 * ts + (RUN_ALIGN - 1) * N_EXPERTS) // LANES) * LANES


PAD_COARSE = 16
assert (EXPERT_BLOCK // RUN_ALIGN) % PAD_COARSE == 0
TABLE_LANES = 2 * LANES
SEG_NACT_LANE = 3 * N_EXPERTS
SEG_SPARE_LANE = 3 * N_EXPERTS + 1


def _num_blocks(t):
    run_rows = 2 * t + (RUN_ALIGN - 1) * N_EXPERTS * (t // ROUTE_TILE)
    return -(-(run_rows + N_EXPERTS * (EXPERT_BLOCK - RUN_ALIGN)) // EXPERT_BLOCK)


def _route_kernel(route_ref, lpos_ref, tab_ref, seg_ref, meta_ref):
    t = route_ref.shape[1]
    tr = ROUTE_TILE
    nbp = meta_ref.shape[1]
    chunks_per_block = EXPERT_BLOCK // RUN_ALIGN
    eiota = lax.broadcasted_iota(jnp.int32, (N_EXPERTS, tr), 0)
    before = (lax.broadcasted_iota(jnp.int32, (tr, tr), 0)
              < lax.broadcasted_iota(jnp.int32, (tr, tr), 1)).astype(bf16)
    lower = (lax.broadcasted_iota(jnp.int32, (N_EXPERTS, N_EXPERTS), 1)
             < lax.broadcasted_iota(jnp.int32, (N_EXPERTS, N_EXPERTS), 0)).astype(bf16)
    sub = lax.broadcasted_iota(jnp.int32, (N_EXPERTS, LANES), 0)
    lane = lax.broadcasted_iota(jnp.int32, (N_EXPERTS, LANES), 1)

    def to_lanes(col, offset):
        return jnp.sum(jnp.where(sub + offset == lane, col, 0.0), axis=0, keepdims=True)

    def expert_prefix(col):
        b = jnp.broadcast_to(col, (N_EXPERTS, LANES))
        hi = jnp.floor(b * (1.0 / 16.0))
        lo = b - 16.0 * hi
        return (16.0 * jnp.dot(lower, hi.astype(bf16), preferred_element_type=f32)
                + jnp.dot(lower, lo.astype(bf16), preferred_element_type=f32))[:, 0:1]

    lpos_ref[...] = jnp.zeros(lpos_ref.shape, jnp.int32)
    chunk_id = lax.broadcasted_iota(jnp.int32, (N_EXPERTS, TABLE_LANES), 1).astype(f32)
    chunk_expert = lax.broadcasted_iota(jnp.int32, (N_EXPERTS, TABLE_LANES), 0).astype(f32)

    def step(i, seen_chunks):
        off = pl.multiple_of(i * tr, tr)
        m1 = eiota == route_ref[0:1, pl.ds(off, tr)]
        m2 = eiota == route_ref[1:2, pl.ds(off, tr)]
        onehot = jnp.where(m1 | m2, 1.0, 0.0)
        within = jnp.dot(onehot.astype(bf16), before, preferred_element_type=f32)
        run_chunks = jnp.floor((jnp.sum(onehot, axis=1, keepdims=True) + (RUN_ALIGN - 1)) * (1.0 / RUN_ALIGN))
        run_start = expert_prefix(run_chunks)
        pos = within + RUN_ALIGN * run_start
        lpos_ref[0:1, pl.ds(off, tr)] = jnp.sum(jnp.where(m1, pos, 0.0), axis=0, keepdims=True).astype(jnp.int32)
        lpos_ref[1:2, pl.ds(off, tr)] = jnp.sum(jnp.where(m2, pos, 0.0), axis=0, keepdims=True).astype(jnp.int32)
        owner = (run_start <= chunk_id) & (chunk_id < run_start + run_chunks)
        rel = jnp.sum(jnp.where(owner, seen_chunks + chunk_id - run_start, 0.0), axis=0, keepdims=True)
        eid = jnp.sum(jnp.where(owner, chunk_expert, 0.0), axis=0, keepdims=True)
        n_used = jnp.sum(run_chunks, axis=0, keepdims=True)
        unused = chunk_id[0:1] >= n_used
        rel = jnp.where(unused, chunk_id[0:1] - n_used, rel)
        eid = jnp.where(unused, (SEG_SPARE_LANE + lax.rem(i, 2)).astype(f32), eid)
        row = jnp.concatenate([rel, eid], axis=1)
        tab_ref[i] = jnp.broadcast_to(row, (SUBLANES, 2 * TABLE_LANES)).astype(jnp.int32)
        return seen_chunks + run_chunks

    used_chunks = lax.fori_loop(0, t // tr, step, jnp.zeros((N_EXPERTS, 1), f32))

    nblk = jnp.floor((used_chunks + (chunks_per_block - 1)) * (1.0 / chunks_per_block))
    first_blk = expert_prefix(nblk)
    nact = jnp.sum(nblk, axis=0, keepdims=True)
    spare_chunk = _num_blocks(t) * chunks_per_block
    seg_row = (to_lanes(first_blk * chunks_per_block, 0) + to_lanes(used_chunks, N_EXPERTS)
               + to_lanes(nblk * chunks_per_block, 2 * N_EXPERTS)
               + jnp.where(lane[0:1] == SEG_NACT_LANE, nact, 0.0)
               + jnp.where(lane[0:1] == SEG_SPARE_LANE, float(spare_chunk), 0.0)
               + jnp.where(lane[0:1] == SEG_SPARE_LANE + 1, float(spare_chunk + _local_rows(tr) // RUN_ALIGN), 0.0))
    seg_ref[...] = jnp.broadcast_to(seg_row, (SUBLANES, LANES)).astype(jnp.int32)

    blk = lax.broadcasted_iota(jnp.int32, (N_EXPERTS, nbp), 1).astype(f32)
    owner = (first_blk <= blk) & (blk < first_blk + nblk)
    expert_id = lax.broadcasted_iota(jnp.int32, (N_EXPERTS, nbp), 0).astype(f32)
    bexp = jnp.sum(jnp.where(owner, expert_id, 0.0), axis=0, keepdims=True)
    left = jnp.clip(used_chunks - (blk - first_blk) * chunks_per_block, 0.0, float(chunks_per_block))
    brows = jnp.sum(jnp.where(owner, left, 0.0), axis=0, keepdims=True) * RUN_ALIGN
    row8 = lax.broadcasted_iota(jnp.int32, (SUBLANES, nbp), 0)
    meta = jnp.where(row8 == 0, bexp, jnp.where(row8 == 1, nact, jnp.where(row8 == 2, brows, 0.0)))
    meta_ref[...] = meta.astype(jnp.int32)


def _route(route):
    t = route.shape[1]
    nt = t // ROUTE_TILE
    nbp = -(-_num_blocks(t) // LANES) * LANES
    return pl.pallas_call(
        _route_kernel,
        grid=(1,),
        in_specs=[pl.BlockSpec(route.shape, lambda i: (0, 0))],
        out_specs=[pl.BlockSpec(route.shape, lambda i: (0, 0)),
                   pl.BlockSpec((nt, SUBLANES, 2 * TABLE_LANES), lambda i: (0, 0, 0)),
                   pl.BlockSpec((SUBLANES, LANES), lambda i: (0, 0)),
                   pl.BlockSpec((SUBLANES, nbp), lambda i: (0, 0))],
        out_shape=[jax.ShapeDtypeStruct(route.shape, jnp.int32),
                   jax.ShapeDtypeStruct((nt, SUBLANES, 2 * TABLE_LANES), jnp.int32),
                   jax.ShapeDtypeStruct((SUBLANES, LANES), jnp.int32),
                   jax.ShapeDtypeStruct((SUBLANES, nbp), jnp.int32)],
        compiler_params=pltpu.CompilerParams(
            dimension_semantics=("arbitrary",), vmem_limit_bytes=VMEM_LIMIT),
        name="route",
    )(route)


def _chunk_row(tab_ref, seg_ref, k):
    chunk = seg_ref[0, tab_ref[0, 0, TABLE_LANES + k]] + tab_ref[0, 0, k]
    return pl.multiple_of(chunk * RUN_ALIGN, RUN_ALIGN)


def _scatter_kernel(tab_ref, seg_ref, lpos_ref, x_ref, xs_ref, xsl, zrows, sem, zsem):
    lrows, ts = xsl.shape[1], x_ref.shape[0]
    n_blocks = (xs_ref.shape[0] - 2 * lrows) // EXPERT_BLOCK
    i = pl.program_id(0)
    last = pl.num_programs(0) - 1
    slot = lax.rem(i, 2)

    def drain(s):
        pltpu.make_async_copy(xsl.at[s], xs_ref.at[pl.ds(0, lrows)], sem.at[s]).wait()

    @pl.when(i >= 2)
    def _():
        drain(slot)

    j = lax.broadcasted_iota(jnp.int32, (lrows, ts), 0)
    perm = jnp.where((j == lpos_ref[0:1, :]) | (j == lpos_ref[1:2, :]), 1.0, 0.0).astype(bf16)
    xsl[slot] = _pack_rows(jnp.dot(perm, x_ref[...], preferred_element_type=f32), already_bf16=True)

    for k in range(lrows // RUN_ALIGN):
        pltpu.make_async_copy(xsl.at[slot, pl.ds(k * RUN_ALIGN, RUN_ALIGN)],
                              xs_ref.at[pl.ds(_chunk_row(tab_ref, seg_ref, k), RUN_ALIGN)], sem.at[slot]).start()

    @pl.when(i == last)
    def _():
        drain(slot)

        @pl.when(i >= 1)
        def _():
            drain(1 - slot)

        zrows[...] = jnp.zeros(zrows.shape, zrows.dtype)

        def zcopy(row, n):
            return pltpu.make_async_copy(zrows.at[pl.ds(0, n)], xs_ref.at[pl.ds(row, n)], zsem)

        def for_each_unused_block(fn):
            def body(b, carry):
                fn(pl.multiple_of(b * EXPERT_BLOCK, EXPERT_BLOCK), EXPERT_BLOCK)
                return carry

            lax.fori_loop(seg_ref[0, SEG_NACT_LANE], n_blocks, body, 0)
            for off in range(0, 2 * lrows, EXPERT_BLOCK):
                fn(n_blocks * EXPERT_BLOCK + off, min(EXPERT_BLOCK, 2 * lrows - off))

        for_each_unused_block(lambda row, n: zcopy(row, n).start())
        for_each_unused_block(lambda row, n: zcopy(row, n).wait())

        def for_each_pad_chunk(fn):
            def seg(e, carry):
                g0 = seg_ref[0, e]
                used = seg_ref[0, N_EXPERTS + e]
                total = seg_ref[0, 2 * N_EXPERTS + e]

                def chunk(c, carry2):
                    fn(pl.multiple_of((g0 + c) * RUN_ALIGN, RUN_ALIGN), RUN_ALIGN)
                    return carry2

                def coarse(c, carry2):
                    fn(pl.multiple_of((g0 + c * PAD_COARSE) * RUN_ALIGN, PAD_COARSE * RUN_ALIGN),
                       PAD_COARSE * RUN_ALIGN)
                    return carry2

                aligned = lax.div(used + (PAD_COARSE - 1), PAD_COARSE)
                lax.fori_loop(used, aligned * PAD_COARSE, chunk, 0)
                lax.fori_loop(aligned, lax.div(total, PAD_COARSE), coarse, 0)
                return carry

            lax.fori_loop(0, N_EXPERTS, seg, 0)

        for_each_pad_chunk(lambda row, n: zcopy(row, n).start())
        for_each_pad_chunk(lambda row, n: zcopy(row, n).wait())


def _scatter(tab, seg, lpos, xn2, n_blocks):
    t = xn2.shape[0]
    ts = ROUTE_TILE
    lrows = _local_rows(ts)
    return pl.pallas_call(
        _scatter_kernel,
        grid=(t // ts,),
        in_specs=[
            pl.BlockSpec((1, SUBLANES, 2 * TABLE_LANES), lambda i: (i, 0, 0), memory_space=pltpu.SMEM),
            pl.BlockSpec((SUBLANES, LANES), lambda i: (0, 0), memory_space=pltpu.SMEM),
            pl.BlockSpec((SUBLANES, ts), lambda i: (0, i)),
            pl.BlockSpec((ts, D_MODEL), lambda i: (i, 0)),
        ],
        out_specs=pl.BlockSpec(memory_space=pl.ANY),
        out_shape=jax.ShapeDtypeStruct((n_blocks * EXPERT_BLOCK + 2 * lrows, PACKED), jnp.uint32),
        scratch_shapes=[pltpu.VMEM((2, lrows, PACKED), jnp.uint32), pltpu.VMEM((EXPERT_BLOCK, PACKED), jnp.uint32),
                        pltpu.SemaphoreType.DMA((2,)), pltpu.SemaphoreType.DMA],
        compiler_params=pltpu.CompilerParams(
            dimension_semantics=("arbitrary",), vmem_limit_bytes=VMEM_LIMIT, has_side_effects=True),
        name="scatter",
    )(tab, seg, lpos, xn2)


def _expert_kernel(bexp_ref, nact_ref, brows_ref, xs_ref, wg_ref, wu_ref, wd_ref, ys_ref, wg_b, wu_b, wd_b):
    b = pl.program_id(0)
    active = b < nact_ref[0]
    half_full = brows_ref[b] <= EXPERT_BLOCK // 2
    new_expert = jnp.logical_or(b == 0, bexp_ref[b] != bexp_ref[jnp.maximum(b - 1, 0)])

    @pl.when(jnp.logical_not(active))
    def _():
        ys_ref[...] = jnp.zeros(ys_ref.shape, ys_ref.dtype)

    @pl.when(jnp.logical_and(active, new_expert))
    def _():
        wg_b[...] = wg_ref[0].astype(bf16)
        wu_b[...] = wu_ref[0].astype(bf16)
        wd_b[...] = wd_ref[0].astype(bf16)

    def swiglu(rows):
        x_lo, x_hi = _unpack_rows(xs_ref[0:rows, :])

        def up(w_ref):
            return (jnp.dot(x_lo, w_ref[:PACKED, :], preferred_element_type=f32)
                    + jnp.dot(x_hi, w_ref[PACKED:, :], preferred_element_type=f32))

        g = up(wg_b)
        hmid = (g * _sigmoid(g) * up(wu_b)).astype(bf16)
        ys_ref[0:rows, :] = _pack_rows(jnp.dot(hmid, wd_b[...], preferred_element_type=f32))
        if rows < EXPERT_BLOCK:
            ys_ref[rows:, :] = jnp.zeros((EXPERT_BLOCK - rows, PACKED), ys_ref.dtype)

    @pl.when(jnp.logical_and(active, jnp.logical_not(half_full)))
    def _():
        swiglu(EXPERT_BLOCK)

    @pl.when(jnp.logical_and(active, half_full))
    def _():
        swiglu(EXPERT_BLOCK // 2)


def _experts(bexp, nact, brows, xs, wg, wu, wd, nb):
    n_rows = nb * EXPERT_BLOCK

    def blk(b, nact_ref):
        return jnp.minimum(b, nact_ref[0] - 1)

    grid_spec = pltpu.PrefetchScalarGridSpec(
        num_scalar_prefetch=3,
        grid=(nb,),
        in_specs=[
            pl.BlockSpec((EXPERT_BLOCK, PACKED), lambda b, e, n, r: (blk(b, n), 0)),
            pl.BlockSpec((1, D_MODEL, D_FF_EXPERT), lambda b, e, n, r: (e[blk(b, n)], 0, 0)),
            pl.BlockSpec((1, D_MODEL, D_FF_EXPERT), lambda b, e, n, r: (e[blk(b, n)], 0, 0)),
            pl.BlockSpec((1, D_FF_EXPERT, D_MODEL), lambda b, e, n, r: (e[blk(b, n)], 0, 0)),
        ],
        out_specs=pl.BlockSpec((EXPERT_BLOCK, PACKED), lambda b, e, n, r: (b, 0)),
        scratch_shapes=[pltpu.VMEM((D_MODEL, D_FF_EXPERT), bf16), pltpu.VMEM((D_MODEL, D_FF_EXPERT), bf16),
                        pltpu.VMEM((D_FF_EXPERT, D_MODEL), bf16)],
    )
    return pl.pallas_call(
        _expert_kernel,
        grid_spec=grid_spec,
        out_shape=jax.ShapeDtypeStruct((n_rows, PACKED), jnp.uint32),
        compiler_params=pltpu.CompilerParams(
            dimension_semantics=("arbitrary",), vmem_limit_bytes=VMEM_LIMIT),
        name="experts",
    )(bexp, nact, brows, xs, wg, wu, wd)


def _combine_kernel(tab_ref, tab_next_ref, seg_ref, lpos_ref, wts_ref, h_ref, gf_ref, ys_ref, out_ref, ybuf, sem):
    ts, lrows = h_ref.shape[0], ybuf.shape[1]
    i = pl.program_id(0)
    slot = lax.rem(i, 2)

    def fetch(t_ref, s):
        for k in range(lrows // RUN_ALIGN):
            pltpu.make_async_copy(ys_ref.at[pl.ds(_chunk_row(t_ref, seg_ref, k), RUN_ALIGN)],
                                  ybuf.at[s, pl.ds(k * RUN_ALIGN, RUN_ALIGN)], sem.at[s]).start()

    @pl.when(i == 0)
    def _():
        fetch(tab_ref, 0)

    @pl.when(i + 1 < pl.num_programs(0))
    def _():
        fetch(tab_next_ref, 1 - slot)

    pltpu.make_async_copy(ys_ref.at[pl.ds(0, lrows)], ybuf.at[slot], sem.at[slot]).wait()

    info = jnp.concatenate([lpos_ref[...].astype(f32), wts_ref[...],
                            jnp.zeros((LANES - 2 * SUBLANES, ts), f32)], axis=0).T
    jl = lax.broadcasted_iota(jnp.int32, (ts, lrows), 1).astype(f32)
    mix = (jnp.where(jl == info[:, 0:1], info[:, SUBLANES:SUBLANES + 1], 0.0)
           + jnp.where(jl == info[:, 1:2], info[:, SUBLANES + 1:SUBLANES + 2], 0.0)).astype(bf16)
    moe = jnp.concatenate([jnp.dot(mix, y, preferred_element_type=f32) for y in _unpack_rows(ybuf[slot])],
                          axis=1)
    out_ref[...] = _rms(h_ref[...] + moe, gf_ref[...])


def _combine(tab, seg, lpos, wts, h, g_final, ys):
    t = h.shape[0]
    ts = ROUTE_TILE
    nt = t // ts
    lrows = _local_rows(ts)
    assert ys.shape[0] >= lrows
    return pl.pallas_call(
        _combine_kernel,
        grid=(nt,),
        in_specs=[
            pl.BlockSpec((1, SUBLANES, 2 * TABLE_LANES), lambda i: (i, 0, 0), memory_space=pltpu.SMEM),
            pl.BlockSpec((1, SUBLANES, 2 * TABLE_LANES), lambda i: (jnp.minimum(i + 1, nt - 1), 0, 0),
                         memory_space=pltpu.SMEM),
            pl.BlockSpec((SUBLANES, LANES), lambda i: (0, 0), memory_space=pltpu.SMEM),
            pl.BlockSpec((SUBLANES, ts), lambda i: (0, i)),
            pl.BlockSpec((SUBLANES, ts), lambda i: (0, i)),
            pl.BlockSpec((ts, D_MODEL), lambda i: (i, 0)),
            pl.BlockSpec((1, D_MODEL), lambda i: (0, 0)),
            pl.BlockSpec(memory_space=pl.ANY),
        ],
        out_specs=pl.BlockSpec((ts, D_MODEL), lambda i: (i, 0)),
        out_shape=jax.ShapeDtypeStruct((t, D_MODEL), f32),
        scratch_shapes=[pltpu.VMEM((2, lrows, PACKED), jnp.uint32), pltpu.SemaphoreType.DMA((2,))],
        compiler_params=pltpu.CompilerParams(
            dimension_semantics=("arbitrary",), vmem_limit_bytes=VMEM_LIMIT),
        name="combine",
    )(tab, tab, seg, lpos, wts, h, g_final, ys)


def _prep_w_in(w_in):
    c0 = 2 * CONV_CH
    c1 = c0 + Q_DIM
    c2 = c1 + KV_DIM
    c3 = c2 + KV_DIM
    wk = w_in[:, c1:c2]
    wv = w_in[:, c2:c3]

    def dup(w):
        return jnp.concatenate([w[:, h * HEAD_DIM:(h + 1) * HEAD_DIM] for h in range(N_KV_HEADS)
                                for _ in range(2)], axis=1)

    glu = jnp.concatenate([w_in[:, half + b * LANES:half + (b + 1) * LANES]
                           for b in range(CONV_CH // LANES) for half in (0, CONV_CH)], axis=1)
    return jnp.concatenate(
        [glu, w_in[:, c0:c1] * (HEAD_DIM ** -0.5), dup(wk), dup(wv), w_in[:, c3:]],
        axis=1).astype(bf16)


def _prep_router(w_group, b_group, w_expert, b_expert):
    d = w_group.shape[0]
    w = jnp.zeros((d, LANES), f32)
    w = w.at[:, :N_GROUPS].set(w_group).at[:, SUBLANES:ROUTER_ROWS].set(w_expert)
    b = jnp.full((1, LANES), NEG_BIG, f32)
    b = b.at[0, :N_GROUPS].set(b_group).at[0, SUBLANES:ROUTER_ROWS].set(b_expert)
    w_hi = w.astype(bf16)
    w_lo = (w - w_hi.astype(f32)).astype(bf16)
    return w_hi, w_lo, b


def kernel(x, g_mix, w_in, w_dw, b_dw, ln_conv_g, ln_conv_b, sinks, w_conv_out, w_attn_out, w_out, g_ffn,
           w_group, b_group, w_expert, b_expert, w_gate, w_up, w_down, g_final):
    batch, seq, d = x.shape
    assert d == D_MODEL and seq % MIXER_TILE == 0 and seq % TOKEN_TILE == 0 and g_mix.shape[0] == 1
    t = batch * seq
    x2 = x.reshape(t, d)

    act, q, kv, sgates = _inproj(x2, g_mix[0][None, :], _prep_w_in(w_in[0]), w_dw[0, :, 0, :], b_dw[0][None, :],
                                 ln_conv_g[0][None, :], ln_conv_b[0][None, :], seq)

    wr_hi, wr_lo, b_r = _prep_router(w_group[0], b_group[0], w_expert[0], b_expert[0])
    h, xn2, route, wts = _mixer(
        x2, act, q, kv, sgates, sinks[0], w_conv_out[0].astype(bf16), w_attn_out[0].astype(bf16),
        w_out[0].astype(bf16), g_ffn[0][None, :], wr_hi, wr_lo, b_r, batch, seq)

    lpos, tab, seg, meta = _route(route)
    nb = _num_blocks(t)
    xs = _scatter(tab, seg, lpos, xn2, nb)
    ys = _experts(meta[0, :nb], meta[1, :1], meta[2, :nb], xs, w_gate[0], w_up[0], w_down[0], nb)
    seg_back = seg.at[:, SEG_SPARE_LANE:SEG_SPARE_LANE + 2].set(0)
    out = _combine(tab, seg_back, lpos, wts, h, g_final[None, :], ys)
    return out.reshape(batch, seq, d)
```

```python
import functools

import numpy as np
import jax
import jax.numpy as jnp
from jax import lax
from jax.experimental import pallas as pl
from jax.experimental.pallas import tpu as pltpu

D_MODEL = 1024
CONV_CH = 512
CONV_WIDTH = 31
N_HEADS = 8
N_KV_HEADS = 2
HEAD_DIM = 64
ATTN_BLOCK = 128
N_GROUPS = 4
EXPERTS_PER_GROUP = 8
N_EXPERTS = N_GROUPS * EXPERTS_PER_GROUP
D_FF_EXPERT = 512
NORM_EPS = 1e-6

Q_DIM = N_HEADS * HEAD_DIM
KV_DIM = N_KV_HEADS * HEAD_DIM

LANES = 128
SUBLANES = 8
CONV_HALO = 32
CONV_ROWS = 32
PROJ_PIECE = 256
ROUTER_ROWS = SUBLANES + N_EXPERTS
NEG_BIG = -1e30

TOKEN_TILE = 512
MIXER_TILE = 1024
ROUTE_TILE = 512
RUN_ALIGN = SUBLANES
EXPERT_BLOCK = 1024
VMEM_LIMIT = 58 * 1024 * 1024

f32 = jnp.float32
bf16 = jnp.bfloat16


def _rms(x, g):
    ms = jnp.mean(x * x, axis=-1, keepdims=True)
    return x * lax.rsqrt(ms + NORM_EPS) * g


def _sigmoid(x):
    return 1.0 / (1.0 + jnp.exp(-x))


PACKED = D_MODEL // 2
_HIGH_HALF = 0xFFFF0000


def _pack_rows(x, already_bf16=False):
    def bits(v):
        return pltpu.bitcast(v if already_bf16 else v.astype(bf16).astype(f32), jnp.uint32)
    low = lax.shift_right_logical(bits(x[:, :PACKED]), jnp.uint32(16))
    return low | (bits(x[:, PACKED:]) & jnp.uint32(_HIGH_HALF))


def _unpack_rows(p):
    low = pltpu.bitcast(lax.shift_left(p, jnp.uint32(16)), f32).astype(bf16)
    high = pltpu.bitcast(p & jnp.uint32(_HIGH_HALF), f32).astype(bf16)
    return low, high


def _load_w_in(w_hbm, stage, w_ref, sem):
    copy = pltpu.make_async_copy(w_hbm, stage, sem)
    copy.start()
    copy.wait()

    def put(dst, value):
        w_ref[:, dst:dst + value.shape[1]] = value.astype(bf16)

    for b in range(CONV_CH // LANES):
        put(2 * b * LANES, stage[:, b * LANES:(b + 1) * LANES])
        put((2 * b + 1) * LANES, stage[:, CONV_CH + b * LANES:CONV_CH + (b + 1) * LANES])
    src = dst = 2 * CONV_CH
    for lo in range(0, Q_DIM, LANES):
        put(dst + lo, stage[:, src + lo:src + lo + LANES] * (HEAD_DIM ** -0.5))
    src, dst = src + Q_DIM, dst + Q_DIM
    for _ in range(2):
        for h in range(N_KV_HEADS):
            head = stage[:, src + h * HEAD_DIM:src + (h + 1) * HEAD_DIM]
            put(dst + 2 * h * HEAD_DIM, jnp.concatenate([head, head], axis=1))
        src, dst = src + KV_DIM, dst + 2 * KV_DIM
    for lo in range(0, 2 * D_MODEL, LANES):
        put(dst + lo, stage[:, src + lo:src + lo + LANES])


def _inproj_kernel(tiles_per_seq, x_ref, g_ref, w_hbm, wdw_ref, bdw_ref, lng_ref, lnb_ref,
                   act_ref, q_ref, kv_ref, sgate_ref, xn_s, vbuf, w_stage, w_ref, w_sem):
    tm = x_ref.shape[0]

    @pl.when(pl.program_id(0) == 0)
    def _():
        _load_w_in(w_hbm, w_stage, w_ref, w_sem)

    xn_s[...] = _rms(x_ref[...], g_ref[...]).astype(bf16)

    def proj(lo, hi):
        return jnp.dot(xn_s[...], w_ref[:, lo:hi], preferred_element_type=f32)

    n_cb = CONV_CH // LANES
    first = lax.rem(pl.program_id(0), tiles_per_seq) == 0

    @pl.when(first)
    def _():
        vbuf[:, 0:CONV_HALO, :] = jnp.zeros((n_cb, CONV_HALO, LANES), f32)

    @pl.when(jnp.logical_not(first))
    def _():
        vbuf[:, 0:CONV_HALO, :] = vbuf[:, tm:tm + CONV_HALO, :]

    for b in range(n_cb):
        u = proj(2 * b * LANES, 2 * (b + 1) * LANES)
        vbuf[b, CONV_HALO:, :] = u[:, :LANES] * _sigmoid(u[:, LANES:])

    rows = CONV_ROWS
    tap0 = CONV_HALO - (CONV_WIDTH - 1)
    for c in range(tm // rows):
        accs = []
        for b in range(n_cb):
            acc = jnp.broadcast_to(bdw_ref[:, b * LANES:(b + 1) * LANES], (rows, LANES))
            for j in range(CONV_WIDTH):
                r0 = c * rows + tap0 + j
                acc = acc + wdw_ref[j:j + 1, b * LANES:(b + 1) * LANES] * vbuf[b, r0:r0 + rows, :]
            accs.append(acc)
        mu = sum(jnp.sum(a, axis=-1, keepdims=True) for a in accs) * (1.0 / CONV_CH)
        ds = [a - mu for a in accs]
        var = sum(jnp.sum(d * d, axis=-1, keepdims=True) for d in ds) * (1.0 / CONV_CH)
        inv = lax.rsqrt(var + NORM_EPS)
        for b in range(n_cb):
            y = ds[b] * inv * lng_ref[:, b * LANES:(b + 1) * LANES] + lnb_ref[:, b * LANES:(b + 1) * LANES]
            act_ref[c * rows:(c + 1) * rows, b * LANES:(b + 1) * LANES] = (y * _sigmoid(y)).astype(bf16)

    col = 2 * CONV_CH
    for out_ref, fn in ((q_ref, lambda z: z), (kv_ref, lambda z: z), (sgate_ref, _sigmoid)):
        for lo in range(0, out_ref.shape[1], PROJ_PIECE):
            out_ref[:, lo:lo + PROJ_PIECE] = fn(proj(col + lo, col + lo + PROJ_PIECE)).astype(bf16)
        col += out_ref.shape[1]


def _inproj(x2, g_mix, w_in, w_dw, b_dw, ln_g, ln_b, seq):
    t = x2.shape[0]
    tm = TOKEN_TILE
    widths = (CONV_CH, Q_DIM, 4 * KV_DIM, 2 * D_MODEL)

    def full(a):
        return pl.BlockSpec(a.shape, lambda i: (0,) * a.ndim)

    return pl.pallas_call(
        functools.partial(_inproj_kernel, seq // tm),
        grid=(t // tm,),
        in_specs=[pl.BlockSpec((tm, D_MODEL), lambda i: (i, 0)),
                  full(g_mix), pl.BlockSpec(memory_space=pl.ANY), full(w_dw), full(b_dw), full(ln_g), full(ln_b)],
        out_specs=[pl.BlockSpec((tm, w), lambda i: (i, 0)) for w in widths],
        out_shape=[jax.ShapeDtypeStruct((t, w), bf16) for w in widths],
        scratch_shapes=[pltpu.VMEM((tm, D_MODEL), bf16),
                        pltpu.VMEM((CONV_CH // LANES, tm + CONV_HALO, LANES), f32),
                        pltpu.VMEM(w_in.shape, f32),
                        pltpu.VMEM((D_MODEL, sum(widths) + CONV_CH), bf16),
                        pltpu.SemaphoreType.DMA],
        compiler_params=pltpu.CompilerParams(
            dimension_semantics=("arbitrary",), vmem_limit_bytes=VMEM_LIMIT),
        name="inproj",
    )(x2, g_mix, w_in, w_dw, b_dw, ln_g, ln_b)


def _mixer_kernel(x_ref, act_ref, q_ref, kv_ref, kvp_ref, sgate_ref, sink_ref, bias_ref, wco_ref, wao_ref,
                  wo_ref, gffn_ref, wrh_ref, wrl_ref, br_ref,
                  h_ref, xn2_ref, route_ref, wts_ref,
                  kvall, attn, s_scr, m_scr):
    ts = x_ref.shape[0]
    first = pl.program_id(1) == 0

    kvall[0:ATTN_BLOCK, :] = jnp.where(first, jnp.zeros_like(kvp_ref[...]), kvp_ref[...])
    kvall[ATTN_BLOCK:, :] = kv_ref[...]
    nkeys = 2 * ATTN_BLOCK
    left_kv = lax.broadcasted_iota(jnp.int32, (nkeys, LANES), 1) < HEAD_DIM
    left_q = lax.broadcasted_iota(jnp.int32, (ATTN_BLOCK, LANES), 1) < HEAD_DIM
    key_lane = lax.broadcasted_iota(jnp.int32, (ATTN_BLOCK, 2 * nkeys), 1)
    prev_keys = (key_lane % nkeys) < ATTN_BLOCK
    no_prev = jnp.where(prev_keys, jnp.where(first, -jnp.inf, 0.0), 0.0)

    def block_diag(x):
        z = jnp.zeros_like(x)
        return jnp.concatenate([jnp.where(left_kv, x, z), jnp.where(left_kv, z, x)], axis=0)

    n_pairs = N_HEADS // 2
    for j in range(ts // ATTN_BLOCK):
        r0 = j * ATTN_BLOCK
        for kvh in range(N_KV_HEADS):
            kbd = block_diag(kvall[r0:r0 + nkeys, kvh * LANES:(kvh + 1) * LANES])
            for ii in range(2):
                i = 2 * kvh + ii
                qb = q_ref[r0:r0 + ATTN_BLOCK, i * LANES:(i + 1) * LANES]
                s = lax.dot_general(qb, kbd, (((1,), (1,)), ((), ())), preferred_element_type=f32)
                s = s + bias_ref[i]
                if j == 0:
                    s = s + no_prev
                s_scr[j * n_pairs + i] = s
                for half in range(2):
                    m = jnp.max(s[:, half * nkeys:(half + 1) * nkeys], axis=-1, keepdims=True)
                    m_scr[2 * (j * n_pairs + i) + half] = jnp.maximum(m, sink_ref[2 * i + half])

    for j in range(ts // ATTN_BLOCK):
        r0 = j * ATTN_BLOCK
        for kvh in range(N_KV_HEADS):
            vbd = block_diag(kvall[r0:r0 + nkeys, (N_KV_HEADS + kvh) * LANES:(N_KV_HEADS + kvh + 1) * LANES])
            for ii in range(2):
                i = 2 * kvh + ii
                s = s_scr[j * n_pairs + i]
                ps, ls = [], []
                for half in range(2):
                    m = m_scr[2 * (j * n_pairs + i) + half]
                    p = jnp.exp(s[:, half * nkeys:(half + 1) * nkeys] - m)
                    ps.append(p)
                    ls.append(jnp.sum(p, axis=-1, keepdims=True) + jnp.exp(sink_ref[2 * i + half] - m))
                p = jnp.concatenate(ps, axis=1).astype(bf16)
                o = jnp.dot(p, vbd, preferred_element_type=f32)
                o = o * jnp.where(left_q, 1.0 / ls[0], 1.0 / ls[1])
                attn[r0:r0 + ATTN_BLOCK, i * LANES:(i + 1) * LANES] = o.astype(bf16)

    conv_o = jnp.dot(act_ref[...], wco_ref[...], preferred_element_type=f32)
    attn_o = jnp.dot(attn[...], wao_ref[...], preferred_element_type=f32)
    merged = (sgate_ref[:, :D_MODEL].astype(f32) * conv_o
              + sgate_ref[:, D_MODEL:].astype(f32) * attn_o).astype(bf16)
    h = x_ref[...] + jnp.dot(merged, wo_ref[...], preferred_element_type=f32)
    h_ref[...] = h

    xn2 = _rms(h, gffn_ref[...])
    xn2_ref[...] = xn2.astype(bf16)
    xh = xn2.astype(bf16)
    xl = (xn2 - xh.astype(f32)).astype(bf16)
    logits = (jnp.dot(xh, wrh_ref[...], preferred_element_type=f32)
              + jnp.dot(xh, wrl_ref[...], preferred_element_type=f32)
              + jnp.dot(xl, wrh_ref[...], preferred_element_type=f32)) + br_ref[...]
    lt = logits.T
    sub = lax.broadcasted_iota(jnp.int32, (SUBLANES, ts), 0)

    gl = lt[0:SUBLANES]
    gmax = jnp.max(gl, axis=0, keepdims=True)
    gsel = jnp.min(jnp.where(gl == gmax, sub, SUBLANES), axis=0, keepdims=True)
    p_group = 1.0 / jnp.sum(jnp.exp(gl - gmax), axis=0, keepdims=True)

    e_in = lt[SUBLANES:2 * SUBLANES]
    for g in range(1, N_GROUPS):
        e_in = jnp.where(gsel == g, lt[(g + 1) * SUBLANES:(g + 2) * SUBLANES], e_in)
    m1 = jnp.max(e_in, axis=0, keepdims=True)
    i1 = jnp.min(jnp.where(e_in == m1, sub, SUBLANES), axis=0, keepdims=True)
    rest = jnp.where(sub == i1, -jnp.inf, e_in)
    m2 = jnp.max(rest, axis=0, keepdims=True)
    i2 = jnp.min(jnp.where(rest == m2, sub, SUBLANES), axis=0, keepdims=True)
    t2 = jnp.exp(m2 - m1)
    w1 = p_group / (1.0 + t2)
    w2 = p_group * t2 / (1.0 + t2)
    base = gsel * EXPERTS_PER_GROUP
    route_ref[...] = jnp.where(sub == 0, base + i1, jnp.where(sub == 1, base + i2, 0))
    wts_ref[...] = jnp.where(sub == 0, w1, jnp.where(sub == 1, w2, 0.0))


def _attn_bias():
    qi = np.arange(ATTN_BLOCK)[:, None]
    kj = np.arange(2 * ATTN_BLOCK)[None, :]
    rel = (ATTN_BLOCK + qi - kj).astype(np.float32)
    ok = (rel >= 0) & (rel < ATTN_BLOCK)
    slopes = np.array([2.0 ** (-8.0 * (h + 1) / N_HEADS) for h in range(N_HEADS)], np.float32)
    per_head = [np.where(ok, -(slopes[h] * rel), -np.inf).astype(np.float32) for h in range(N_HEADS)]
    return np.stack([np.concatenate([per_head[2 * i], per_head[2 * i + 1]], axis=1)
                     for i in range(N_HEADS // 2)])


def _mixer(x2, act, q, kv, sgates, sinks, wco, wao, wo, g_ffn, wr_hi, wr_lo, b_r, batch, seq):
    t = x2.shape[0]
    ts = MIXER_TILE
    ns = seq // ts
    bias = jnp.asarray(_attn_bias())

    def row(b, s):
        return b * ns + s

    def full(a):
        return pl.BlockSpec(a.shape, lambda b, s: (0,) * a.ndim, pipeline_mode=pl.Buffered(1))

    in_specs = [
        pl.BlockSpec((ts, D_MODEL), lambda b, s: (row(b, s), 0)),
        pl.BlockSpec((ts, CONV_CH), lambda b, s: (row(b, s), 0)),
        pl.BlockSpec((ts, Q_DIM), lambda b, s: (row(b, s), 0)),
        pl.BlockSpec((ts, 4 * KV_DIM), lambda b, s: (row(b, s), 0)),
        pl.BlockSpec((ATTN_BLOCK, 4 * KV_DIM),
                     lambda b, s: (jnp.maximum(row(b, s) * (ts // ATTN_BLOCK) - 1, 0), 0)),
        pl.BlockSpec((ts, 2 * D_MODEL), lambda b, s: (row(b, s), 0)),
        pl.BlockSpec(memory_space=pltpu.SMEM),
        full(bias), full(wco), full(wao), full(wo), full(g_ffn), full(wr_hi), full(wr_lo), full(b_r),
    ]
    out_specs = [
        pl.BlockSpec((ts, D_MODEL), lambda b, s: (row(b, s), 0)),
        pl.BlockSpec((ts, D_MODEL), lambda b, s: (row(b, s), 0)),
        pl.BlockSpec((SUBLANES, ts), lambda b, s: (0, row(b, s))),
        pl.BlockSpec((SUBLANES, ts), lambda b, s: (0, row(b, s))),
    ]
    out_shape = [
        jax.ShapeDtypeStruct((t, D_MODEL), f32),
        jax.ShapeDtypeStruct((t, D_MODEL), bf16),
        jax.ShapeDtypeStruct((SUBLANES, t), jnp.int32),
        jax.ShapeDtypeStruct((SUBLANES, t), f32),
    ]
    return pl.pallas_call(
        _mixer_kernel,
        grid=(batch, ns),
        in_specs=in_specs,
        out_specs=out_specs,
        out_shape=out_shape,
        scratch_shapes=[
            pltpu.VMEM((ts + ATTN_BLOCK, 4 * KV_DIM), bf16),
            pltpu.VMEM((ts, Q_DIM), bf16),
            pltpu.VMEM((ts // ATTN_BLOCK * (N_HEADS // 2), ATTN_BLOCK, 4 * ATTN_BLOCK), f32),
            pltpu.VMEM((ts // ATTN_BLOCK * N_HEADS, ATTN_BLOCK, 1), f32),
        ],
        compiler_params=pltpu.CompilerParams(
            dimension_semantics=("arbitrary", "arbitrary"), vmem_limit_bytes=VMEM_LIMIT),
        name="mixer",
    )(x2, act, q, kv, kv, sgates, sinks, bias, wco, wao, wo, g_ffn, wr_hi, wr_lo, b_r)


def _local_rows(ts):
    return -(-(2 * ts + (RUN_ALIGN - 1) * N_EXPERTS) // LANES) * LANES


PAD_COARSE = 16
assert (EXPERT_BLOCK // RUN_ALIGN) % PAD_COARSE == 0
TABLE_LANES = 2 * LANES
SEG_NACT_LANE = 3 * N_EXPERTS
SEG_SPARE_LANE = 3 * N_EXPERTS + 1


def _num_blocks(t):
    run_rows = 2 * t + (RUN_ALIGN - 1) * N_EXPERTS * (t // ROUTE_TILE)
    return -(-(run_rows + N_EXPERTS * (EXPERT_BLOCK - RUN_ALIGN)) // EXPERT_BLOCK)


def _route_kernel(route_ref, lpos_ref, tab_ref, seg_ref, meta_ref):
    t = route_ref.shape[1]
    tr = ROUTE_TILE
    nbp = meta_ref.shape[1]
    chunks_per_block = EXPERT_BLOCK // RUN_ALIGN
    eiota = lax.broadcasted_iota(jnp.int32, (N_EXPERTS, tr), 0)
    before = (lax.broadcasted_iota(jnp.int32, (tr, tr), 0)
              < lax.broadcasted_iota(jnp.int32, (tr, tr), 1)).astype(bf16)
    lower = (lax.broadcasted_iota(jnp.int32, (N_EXPERTS, N_EXPERTS), 1)
             < lax.broadcasted_iota(jnp.int32, (N_EXPERTS, N_EXPERTS), 0)).astype(bf16)
    sub = lax.broadcasted_iota(jnp.int32, (N_EXPERTS, LANES), 0)
    lane = lax.broadcasted_iota(jnp.int32, (N_EXPERTS, LANES), 1)

    def to_lanes(col, offset):
        return jnp.sum(jnp.where(sub + offset == lane, col, 0.0), axis=0, keepdims=True)

    def expert_prefix(col):
        b = jnp.broadcast_to(col, (N_EXPERTS, LANES))
        hi = jnp.floor(b * (1.0 / 16.0))
        lo = b - 16.0 * hi
        return (16.0 * jnp.dot(lower, hi.astype(bf16), preferred_element_type=f32)
                + jnp.dot(lower, lo.astype(bf16), preferred_element_type=f32))[:, 0:1]

    lpos_ref[...] = jnp.zeros(lpos_ref.shape, jnp.int32)
    chunk_id = lax.broadcasted_iota(jnp.int32, (N_EXPERTS, TABLE_LANES), 1).astype(f32)
    chunk_expert = lax.broadcasted_iota(jnp.int32, (N_EXPERTS, TABLE_LANES), 0).astype(f32)

    def step(i, seen_chunks):
        off = pl.multiple_of(i * tr, tr)
        m1 = eiota == route_ref[0:1, pl.ds(off, tr)]
        m2 = eiota == route_ref[1:2, pl.ds(off, tr)]
        onehot = jnp.where(m1 | m2, 1.0, 0.0)
        within = jnp.dot(onehot.astype(bf16), before, preferred_element_type=f32)
        run_chunks = jnp.floor((jnp.sum(onehot, axis=1, keepdims=True) + (RUN_ALIGN - 1)) * (1.0 / RUN_ALIGN))
        run_start = expert_prefix(run_chunks)
        pos = within + RUN_ALIGN * run_start
        lpos_ref[0:1, pl.ds(off, tr)] = jnp.sum(jnp.where(m1, pos, 0.0), axis=0, keepdims=True).astype(jnp.int32)
        lpos_ref[1:2, pl.ds(off, tr)] = jnp.sum(jnp.where(m2, pos, 0.0), axis=0, keepdims=True).astype(jnp.int32)
        owner = (run_start <= chunk_id) & (chunk_id < run_start + run_chunks)
        rel = jnp.sum(jnp.where(owner, seen_chunks + chunk_id - run_start, 0.0), axis=0, keepdims=True)
        eid = jnp.sum(jnp.where(owner, chunk_expert, 0.0), axis=0, keepdims=True)
        n_used = jnp.sum(run_chunks, axis=0, keepdims=True)
        unused = chunk_id[0:1] >= n_used
        rel = jnp.where(unused, chunk_id[0:1] - n_used, rel)
        eid = jnp.where(unused, (SEG_SPARE_LANE + lax.rem(i, 2)).astype(f32), eid)
        row = jnp.concatenate([rel, eid], axis=1)
        tab_ref[i] = jnp.broadcast_to(row, (SUBLANES, 2 * TABLE_LANES)).astype(jnp.int32)
        return seen_chunks + run_chunks

    used_chunks = lax.fori_loop(0, t // tr, step, jnp.zeros((N_EXPERTS, 1), f32))

    nblk = jnp.floor((used_chunks + (chunks_per_block - 1)) * (1.0 / chunks_per_block))
    first_blk = expert_prefix(nblk)
    nact = jnp.sum(nblk, axis=0, keepdims=True)
    spare_chunk = _num_blocks(t) * chunks_per_block
    seg_row = (to_lanes(first_blk * chunks_per_block, 0) + to_lanes(used_chunks, N_EXPERTS)
               + to_lanes(nblk * chunks_per_block, 2 * N_EXPERTS)
               + jnp.where(lane[0:1] == SEG_NACT_LANE, nact, 0.0)
               + jnp.where(lane[0:1] == SEG_SPARE_LANE, float(spare_chunk), 0.0)
               + jnp.where(lane[0:1] == SEG_SPARE_LANE + 1, float(spare_chunk + _local_rows(tr) // RUN_ALIGN), 0.0))
    seg_ref[...] = jnp.broadcast_to(seg_row, (SUBLANES, LANES)).astype(jnp.int32)

    blk = lax.broadcasted_iota(jnp.int32, (N_EXPERTS, nbp), 1).astype(f32)
    owner = (first_blk <= blk) & (blk < first_blk + nblk)
    expert_id = lax.broadcasted_iota(jnp.int32, (N_EXPERTS, nbp), 0).astype(f32)
    bexp = jnp.sum(jnp.where(owner, expert_id, 0.0), axis=0, keepdims=True)
    row8 = lax.broadcasted_iota(jnp.int32, (SUBLANES, nbp), 0)
    meta_ref[...] = jnp.where(row8 == 0, bexp, jnp.where(row8 == 1, nact, 0.0)).astype(jnp.int32)


def _route(route):
    t = route.shape[1]
    nt = t // ROUTE_TILE
    nbp = -(-_num_blocks(t) // LANES) * LANES
    return pl.pallas_call(
        _route_kernel,
        grid=(1,),
        in_specs=[pl.BlockSpec(route.shape, lambda i: (0, 0))],
        out_specs=[pl.BlockSpec(route.shape, lambda i: (0, 0)),
                   pl.BlockSpec((nt, SUBLANES, 2 * TABLE_LANES), lambda i: (0, 0, 0)),
                   pl.BlockSpec((SUBLANES, LANES), lambda i: (0, 0)),
                   pl.BlockSpec((SUBLANES, nbp), lambda i: (0, 0))],
        out_shape=[jax.ShapeDtypeStruct(route.shape, jnp.int32),
                   jax.ShapeDtypeStruct((nt, SUBLANES, 2 * TABLE_LANES), jnp.int32),
                   jax.ShapeDtypeStruct((SUBLANES, LANES), jnp.int32),
                   jax.ShapeDtypeStruct((SUBLANES, nbp), jnp.int32)],
        compiler_params=pltpu.CompilerParams(
            dimension_semantics=("arbitrary",), vmem_limit_bytes=VMEM_LIMIT),
        name="route",
    )(route)


def _chunk_row(tab_ref, seg_ref, k):
    chunk = seg_ref[0, tab_ref[0, 0, TABLE_LANES + k]] + tab_ref[0, 0, k]
    return pl.multiple_of(chunk * RUN_ALIGN, RUN_ALIGN)


def _scatter_kernel(tab_ref, seg_ref, lpos_ref, x_ref, xs_ref, xsl, zrows, sem, zsem):
    lrows, ts = xsl.shape[1], x_ref.shape[0]
    n_blocks = (xs_ref.shape[0] - 2 * lrows) // EXPERT_BLOCK
    i = pl.program_id(0)
    last = pl.num_programs(0) - 1
    slot = lax.rem(i, 2)

    def drain(s):
        pltpu.make_async_copy(xsl.at[s], xs_ref.at[pl.ds(0, lrows)], sem.at[s]).wait()

    @pl.when(i >= 2)
    def _():
        drain(slot)

    j = lax.broadcasted_iota(jnp.int32, (lrows, ts), 0)
    perm = jnp.where((j == lpos_ref[0:1, :]) | (j == lpos_ref[1:2, :]), 1.0, 0.0).astype(bf16)
    xsl[slot] = _pack_rows(jnp.dot(perm, x_ref[...], preferred_element_type=f32), already_bf16=True)

    for k in range(lrows // RUN_ALIGN):
        pltpu.make_async_copy(xsl.at[slot, pl.ds(k * RUN_ALIGN, RUN_ALIGN)],
                              xs_ref.at[pl.ds(_chunk_row(tab_ref, seg_ref, k), RUN_ALIGN)], sem.at[slot]).start()

    @pl.when(i == last)
    def _():
        drain(slot)

        @pl.when(i >= 1)
        def _():
            drain(1 - slot)

        zrows[...] = jnp.zeros(zrows.shape, zrows.dtype)

        def zcopy(row, n):
            return pltpu.make_async_copy(zrows.at[pl.ds(0, n)], xs_ref.at[pl.ds(row, n)], zsem)

        def for_each_unused_block(fn):
            def body(b, carry):
                fn(pl.multiple_of(b * EXPERT_BLOCK, EXPERT_BLOCK), EXPERT_BLOCK)
                return carry

            lax.fori_loop(seg_ref[0, SEG_NACT_LANE], n_blocks, body, 0)
            for off in range(0, 2 * lrows, EXPERT_BLOCK):
                fn(n_blocks * EXPERT_BLOCK + off, min(EXPERT_BLOCK, 2 * lrows - off))

        for_each_unused_block(lambda row, n: zcopy(row, n).start())
        for_each_unused_block(lambda row, n: zcopy(row, n).wait())

        def for_each_pad_chunk(fn):
            def seg(e, carry):
                g0 = seg_ref[0, e]
                used = seg_ref[0, N_EXPERTS + e]
                total = seg_ref[0, 2 * N_EXPERTS + e]

                def chunk(c, carry2):
                    fn(pl.multiple_of((g0 + c) * RUN_ALIGN, RUN_ALIGN), RUN_ALIGN)
                    return carry2

                def coarse(c, carry2):
                    fn(pl.multiple_of((g0 + c * PAD_COARSE) * RUN_ALIGN, PAD_COARSE * RUN_ALIGN),
                       PAD_COARSE * RUN_ALIGN)
                    return carry2

                aligned = lax.div(used + (PAD_COARSE - 1), PAD_COARSE)
                lax.fori_loop(used, aligned * PAD_COARSE, chunk, 0)
                lax.fori_loop(aligned, lax.div(total, PAD_COARSE), coarse, 0)
                return carry

            lax.fori_loop(0, N_EXPERTS, seg, 0)

        for_each_pad_chunk(lambda row, n: zcopy(row, n).start())
        for_each_pad_chunk(lambda row, n: zcopy(row, n).wait())


def _scatter(tab, seg, lpos, xn2, n_blocks):
    t = xn2.shape[0]
    ts = ROUTE_TILE
    lrows = _local_rows(ts)
    return pl.pallas_call(
        _scatter_kernel,
        grid=(t // ts,),
        in_specs=[
            pl.BlockSpec((1, SUBLANES, 2 * TABLE_LANES), lambda i: (i, 0, 0), memory_space=pltpu.SMEM),
            pl.BlockSpec((SUBLANES, LANES), lambda i: (0, 0), memory_space=pltpu.SMEM),
            pl.BlockSpec((SUBLANES, ts), lambda i: (0, i)),
            pl.BlockSpec((ts, D_MODEL), lambda i: (i, 0)),
        ],
        out_specs=pl.BlockSpec(memory_space=pl.ANY),
        out_shape=jax.ShapeDtypeStruct((n_blocks * EXPERT_BLOCK + 2 * lrows, PACKED), jnp.uint32),
        scratch_shapes=[pltpu.VMEM((2, lrows, PACKED), jnp.uint32), pltpu.VMEM((EXPERT_BLOCK, PACKED), jnp.uint32),
                        pltpu.SemaphoreType.DMA((2,)), pltpu.SemaphoreType.DMA],
        compiler_params=pltpu.CompilerParams(
            dimension_semantics=("arbitrary",), vmem_limit_bytes=VMEM_LIMIT, has_side_effects=True),
        name="scatter",
    )(tab, seg, lpos, xn2)


def _expert_kernel(bexp_ref, nact_ref, xs_ref, wg_ref, wu_ref, wd_ref, ys_ref, wg_b, wu_b, wd_b):
    b = pl.program_id(0)
    active = b < nact_ref[0]
    new_expert = jnp.logical_or(b == 0, bexp_ref[b] != bexp_ref[jnp.maximum(b - 1, 0)])

    @pl.when(jnp.logical_not(active))
    def _():
        ys_ref[...] = jnp.zeros(ys_ref.shape, ys_ref.dtype)

    @pl.when(jnp.logical_and(active, new_expert))
    def _():
        wg_b[...] = wg_ref[0].astype(bf16)
        wu_b[...] = wu_ref[0].astype(bf16)
        wd_b[...] = wd_ref[0].astype(bf16)

    @pl.when(active)
    def _():
        x_lo, x_hi = _unpack_rows(xs_ref[...])

        def up(w_ref):
            return (jnp.dot(x_lo, w_ref[:PACKED, :], preferred_element_type=f32)
                    + jnp.dot(x_hi, w_ref[PACKED:, :], preferred_element_type=f32))

        g = up(wg_b)
        hmid = (g * _sigmoid(g) * up(wu_b)).astype(bf16)
        ys_ref[...] = _pack_rows(jnp.dot(hmid, wd_b[...], preferred_element_type=f32))


def _experts(bexp, nact, xs, wg, wu, wd, nb):
    n_rows = nb * EXPERT_BLOCK

    def blk(b, nact_ref):
        return jnp.minimum(b, nact_ref[0] - 1)

    grid_spec = pltpu.PrefetchScalarGridSpec(
        num_scalar_prefetch=2,
        grid=(nb,),
        in_specs=[
            pl.BlockSpec((EXPERT_BLOCK, PACKED), lambda b, e, n: (blk(b, n), 0)),
            pl.BlockSpec((1, D_MODEL, D_FF_EXPERT), lambda b, e, n: (e[blk(b, n)], 0, 0)),
            pl.BlockSpec((1, D_MODEL, D_FF_EXPERT), lambda b, e, n: (e[blk(b, n)], 0, 0)),
            pl.BlockSpec((1, D_FF_EXPERT, D_MODEL), lambda b, e, n: (e[blk(b, n)], 0, 0)),
        ],
        out_specs=pl.BlockSpec((EXPERT_BLOCK, PACKED), lambda b, e, n: (b, 0)),
        scratch_shapes=[pltpu.VMEM((D_MODEL, D_FF_EXPERT), bf16), pltpu.VMEM((D_MODEL, D_FF_EXPERT), bf16),
                        pltpu.VMEM((D_FF_EXPERT, D_MODEL), bf16)],
    )
    return pl.pallas_call(
        _expert_kernel,
        grid_spec=grid_spec,
        out_shape=jax.ShapeDtypeStruct((n_rows, PACKED), jnp.uint32),
        compiler_params=pltpu.CompilerParams(
            dimension_semantics=("arbitrary",), vmem_limit_bytes=VMEM_LIMIT),
        name="experts",
    )(bexp, nact, xs, wg, wu, wd)


def _combine_kernel(tab_ref, tab_next_ref, seg_ref, lpos_ref, wts_ref, h_ref, gf_ref, ys_ref, out_ref, ybuf, sem):
    ts, lrows = h_ref.shape[0], ybuf.shape[1]
    i = pl.program_id(0)
    slot = lax.rem(i, 2)

    def fetch(t_ref, s):
        for k in range(lrows // RUN_ALIGN):
            pltpu.make_async_copy(ys_ref.at[pl.ds(_chunk_row(t_ref, seg_ref, k), RUN_ALIGN)],
                                  ybuf.at[s, pl.ds(k * RUN_ALIGN, RUN_ALIGN)], sem.at[s]).start()

    @pl.when(i == 0)
    def _():
        fetch(tab_ref, 0)

    @pl.when(i + 1 < pl.num_programs(0))
    def _():
        fetch(tab_next_ref, 1 - slot)

    pltpu.make_async_copy(ys_ref.at[pl.ds(0, lrows)], ybuf.at[slot], sem.at[slot]).wait()

    info = jnp.concatenate([lpos_ref[...].astype(f32), wts_ref[...],
                            jnp.zeros((LANES - 2 * SUBLANES, ts), f32)], axis=0).T
    jl = lax.broadcasted_iota(jnp.int32, (ts, lrows), 1).astype(f32)
    mix = (jnp.where(jl == info[:, 0:1], info[:, SUBLANES:SUBLANES + 1], 0.0)
           + jnp.where(jl == info[:, 1:2], info[:, SUBLANES + 1:SUBLANES + 2], 0.0)).astype(bf16)
    moe = jnp.concatenate([jnp.dot(mix, y, preferred_element_type=f32) for y in _unpack_rows(ybuf[slot])],
                          axis=1)
    out_ref[...] = _rms(h_ref[...] + moe, gf_ref[...])


def _combine(tab, seg, lpos, wts, h, g_final, ys):
    t = h.shape[0]
    ts = ROUTE_TILE
    nt = t // ts
    lrows = _local_rows(ts)
    assert ys.shape[0] >= lrows
    return pl.pallas_call(
        _combine_kernel,
        grid=(nt,),
        in_specs=[
            pl.BlockSpec((1, SUBLANES, 2 * TABLE_LANES), lambda i: (i, 0, 0), memory_space=pltpu.SMEM),
            pl.BlockSpec((1, SUBLANES, 2 * TABLE_LANES), lambda i: (jnp.minimum(i + 1, nt - 1), 0, 0),
                         memory_space=pltpu.SMEM),
            pl.BlockSpec((SUBLANES, LANES), lambda i: (0, 0), memory_space=pltpu.SMEM),
            pl.BlockSpec((SUBLANES, ts), lambda i: (0, i)),
            pl.BlockSpec((SUBLANES, ts), lambda i: (0, i)),
            pl.BlockSpec((ts, D_MODEL), lambda i: (i, 0)),
            pl.BlockSpec((1, D_MODEL), lambda i: (0, 0)),
            pl.BlockSpec(memory_space=pl.ANY),
        ],
        out_specs=pl.BlockSpec((ts, D_MODEL), lambda i: (i, 0)),
        out_shape=jax.ShapeDtypeStruct((t, D_MODEL), f32),
        scratch_shapes=[pltpu.VMEM((2, lrows, PACKED), jnp.uint32), pltpu.SemaphoreType.DMA((2,))],
        compiler_params=pltpu.CompilerParams(
            dimension_semantics=("arbitrary",), vmem_limit_bytes=VMEM_LIMIT),
        name="combine",
    )(tab, tab, seg, lpos, wts, h, g_final, ys)


def _prep_router(w_group, b_group, w_expert, b_expert):
    d = w_group.shape[0]
    w = jnp.zeros((d, LANES), f32)
    w = w.at[:, :N_GROUPS].set(w_group).at[:, SUBLANES:ROUTER_ROWS].set(w_expert)
    b = jnp.full((1, LANES), NEG_BIG, f32)
    b = b.at[0, :N_GROUPS].set(b_group).at[0, SUBLANES:ROUTER_ROWS].set(b_expert)
    w_hi = w.astype(bf16)
    w_lo = (w - w_hi.astype(f32)).astype(bf16)
    return w_hi, w_lo, b


def kernel(x, g_mix, w_in, w_dw, b_dw, ln_conv_g, ln_conv_b, sinks, w_conv_out, w_attn_out, w_out, g_ffn,
           w_group, b_group, w_expert, b_expert, w_gate, w_up, w_down, g_final):
    batch, seq, d = x.shape
    assert d == D_MODEL and seq % MIXER_TILE == 0 and seq % TOKEN_TILE == 0 and g_mix.shape[0] == 1
    t = batch * seq
    x2 = x.reshape(t, d)

    act, q, kv, sgates = _inproj(x2, g_mix[0][None, :], w_in[0], w_dw[0, :, 0, :], b_dw[0][None, :],
                                 ln_conv_g[0][None, :], ln_conv_b[0][None, :], seq)

    wr_hi, wr_lo, b_r = _prep_router(w_group[0], b_group[0], w_expert[0], b_expert[0])
    h, xn2, route, wts = _mixer(
        x2, act, q, kv, sgates, sinks[0], w_conv_out[0].astype(bf16), w_attn_out[0].astype(bf16),
        w_out[0].astype(bf16), g_ffn[0][None, :], wr_hi, wr_lo, b_r, batch, seq)

    lpos, tab, seg, meta = _route(route)
    nb = _num_blocks(t)
    xs = _scatter(tab, seg, lpos, xn2, nb)
    ys = _experts(meta[0, :nb], meta[1, :1], xs, w_gate[0], w_up[0], w_down[0], nb)
    seg_back = seg.at[:, SEG_SPARE_LANE:SEG_SPARE_LANE + 2].set(0)
    out = _combine(tab, seg_back, lpos, wts, h, g_final[None, :], ys)
    return out.reshape(batch, seq, d)
```

```python
import functools

import numpy as np
import jax
import jax.numpy as jnp
from jax import lax
from jax.experimental import pallas as pl
from jax.experimental.pallas import tpu as pltpu

D_MODEL = 1024
CONV_CH = 512
CONV_WIDTH = 31
N_HEADS = 8
N_KV_HEADS = 2
HEAD_DIM = 64
ATTN_BLOCK = 128
N_GROUPS = 4
EXPERTS_PER_GROUP = 8
N_EXPERTS = N_GROUPS * EXPERTS_PER_GROUP
D_FF_EXPERT = 512
NORM_EPS = 1e-6

Q_DIM = N_HEADS * HEAD_DIM
KV_DIM = N_KV_HEADS * HEAD_DIM

LANES = 128
SUBLANES = 8
CONV_HALO = 32
CONV_ROWS = 32
PROJ_PIECE = 256
ROUTER_ROWS = SUBLANES + N_EXPERTS
NEG_BIG = -1e30

TOKEN_TILE = 512
MIXER_TILE = 1024
ROUTE_TILE = 512
RUN_ALIGN = SUBLANES
EXPERT_BLOCK = 1024
VMEM_LIMIT = 58 * 1024 * 1024

f32 = jnp.float32
bf16 = jnp.bfloat16


def _rms(x, g):
    ms = jnp.mean(x * x, axis=-1, keepdims=True)
    return x * lax.rsqrt(ms + NORM_EPS) * g


def _sigmoid(x):
    return 1.0 / (1.0 + jnp.exp(-x))


PACKED = D_MODEL // 2
_HIGH_HALF = 0xFFFF0000


def _pack_rows(x, already_bf16=False):
    def bits(v):
        return pltpu.bitcast(v if already_bf16 else v.astype(bf16).astype(f32), jnp.uint32)
    low = lax.shift_right_logical(bits(x[:, :PACKED]), jnp.uint32(16))
    return low | (bits(x[:, PACKED:]) & jnp.uint32(_HIGH_HALF))


def _unpack_rows(p):
    low = pltpu.bitcast(lax.shift_left(p, jnp.uint32(16)), f32).astype(bf16)
    high = pltpu.bitcast(p & jnp.uint32(_HIGH_HALF), f32).astype(bf16)
    return low, high


def _load_w_in(w_hbm, stage, w_ref, sem):
    copy = pltpu.make_async_copy(w_hbm, stage, sem)
    copy.start()
    copy.wait()

    def put(dst, value):
        w_ref[:, dst:dst + value.shape[1]] = value.astype(bf16)

    for b in range(CONV_CH // LANES):
        put(2 * b * LANES, stage[:, b * LANES:(b + 1) * LANES])
        put((2 * b + 1) * LANES, stage[:, CONV_CH + b * LANES:CONV_CH + (b + 1) * LANES])
    src = dst = 2 * CONV_CH
    for lo in range(0, Q_DIM, LANES):
        put(dst + lo, stage[:, src + lo:src + lo + LANES] * (HEAD_DIM ** -0.5))
    src, dst = src + Q_DIM, dst + Q_DIM
    for _ in range(2):
        for h in range(N_KV_HEADS):
            head = stage[:, src + h * HEAD_DIM:src + (h + 1) * HEAD_DIM]
            put(dst + 2 * h * HEAD_DIM, jnp.concatenate([head, head], axis=1))
        src, dst = src + KV_DIM, dst + 2 * KV_DIM
    for lo in range(0, 2 * D_MODEL, LANES):
        put(dst + lo, stage[:, src + lo:src + lo + LANES])


def _inproj_kernel(tiles_per_seq, x_ref, g_ref, w_hbm, wdw_ref, bdw_ref, lng_ref, lnb_ref,
                   act_ref, q_ref, kv_ref, sgate_ref, xn_s, vbuf, w_stage, w_ref, w_sem):
    tm = x_ref.shape[0]

    @pl.when(pl.program_id(0) == 0)
    def _():
        _load_w_in(w_hbm, w_stage, w_ref, w_sem)

    xn_s[...] = _rms(x_ref[...], g_ref[...]).astype(bf16)

    def proj(lo, hi):
        return jnp.dot(xn_s[...], w_ref[:, lo:hi], preferred_element_type=f32)

    n_cb = CONV_CH // LANES
    first = lax.rem(pl.program_id(0), tiles_per_seq) == 0

    @pl.when(first)
    def _():
        vbuf[:, 0:CONV_HALO, :] = jnp.zeros((n_cb, CONV_HALO, LANES), f32)

    @pl.when(jnp.logical_not(first))
    def _():
        vbuf[:, 0:CONV_HALO, :] = vbuf[:, tm:tm + CONV_HALO, :]

    for b in range(n_cb):
        u = proj(2 * b * LANES, 2 * (b + 1) * LANES)
        vbuf[b, CONV_HALO:, :] = u[:, :LANES] * _sigmoid(u[:, LANES:])

    rows = CONV_ROWS
    tap0 = CONV_HALO - (CONV_WIDTH - 1)
    for c in range(tm // rows):
        accs = []
        for b in range(n_cb):
            acc = jnp.broadcast_to(bdw_ref[:, b * LANES:(b + 1) * LANES], (rows, LANES))
            for j in range(CONV_WIDTH):
                r0 = c * rows + tap0 + j
                acc = acc + wdw_ref[j:j + 1, b * LANES:(b + 1) * LANES] * vbuf[b, r0:r0 + rows, :]
            accs.append(acc)
        mu = sum(jnp.sum(a, axis=-1, keepdims=True) for a in accs) * (1.0 / CONV_CH)
        ds = [a - mu for a in accs]
        var = sum(jnp.sum(d * d, axis=-1, keepdims=True) for d in ds) * (1.0 / CONV_CH)
        inv = lax.rsqrt(var + NORM_EPS)
        for b in range(n_cb):
            y = ds[b] * inv * lng_ref[:, b * LANES:(b + 1) * LANES] + lnb_ref[:, b * LANES:(b + 1) * LANES]
            act_ref[c * rows:(c + 1) * rows, b * LANES:(b + 1) * LANES] = (y * _sigmoid(y)).astype(bf16)

    col = 2 * CONV_CH
    for out_ref, fn in ((q_ref, lambda z: z), (kv_ref, lambda z: z), (sgate_ref, _sigmoid)):
        for lo in range(0, out_ref.shape[1], PROJ_PIECE):
            out_ref[:, lo:lo + PROJ_PIECE] = fn(proj(col + lo, col + lo + PROJ_PIECE)).astype(bf16)
        col += out_ref.shape[1]


def _inproj(x2, g_mix, w_in, w_dw, b_dw, ln_g, ln_b, seq):
    t = x2.shape[0]
    tm = TOKEN_TILE
    widths = (CONV_CH, Q_DIM, 4 * KV_DIM, 2 * D_MODEL)

    def full(a):
        return pl.BlockSpec(a.shape, lambda i: (0,) * a.ndim)

    return pl.pallas_call(
        functools.partial(_inproj_kernel, seq // tm),
        grid=(t // tm,),
        in_specs=[pl.BlockSpec((tm, D_MODEL), lambda i: (i, 0)),
                  full(g_mix), pl.BlockSpec(memory_space=pl.ANY), full(w_dw), full(b_dw), full(ln_g), full(ln_b)],
        out_specs=[pl.BlockSpec((tm, w), lambda i: (i, 0)) for w in widths],
        out_shape=[jax.ShapeDtypeStruct((t, w), bf16) for w in widths],
        scratch_shapes=[pltpu.VMEM((tm, D_MODEL), bf16),
                        pltpu.VMEM((CONV_CH // LANES, tm + CONV_HALO, LANES), f32),
                        pltpu.VMEM(w_in.shape, f32),
                        pltpu.VMEM((D_MODEL, sum(widths) + CONV_CH), bf16),
                        pltpu.SemaphoreType.DMA],
        compiler_params=pltpu.CompilerParams(
            dimension_semantics=("arbitrary",), vmem_limit_bytes=VMEM_LIMIT),
        name="inproj",
    )(x2, g_mix, w_in, w_dw, b_dw, ln_g, ln_b)


def _mixer_kernel(x_ref, act_ref, q_ref, kv_ref, kvp_ref, sgate_ref, sink_ref, bias_ref, wco_ref, wao_ref,
                  wo_ref, gffn_ref, wrh_ref, wrl_ref, br_ref,
                  h_ref, xn2_ref, route_ref, wts_ref,
                  kvall, attn, s_scr, m_scr):
    ts = x_ref.shape[0]
    first = pl.program_id(1) == 0

    kvall[0:ATTN_BLOCK, :] = jnp.where(first, jnp.zeros_like(kvp_ref[...]), kvp_ref[...])
    kvall[ATTN_BLOCK:, :] = kv_ref[...]
    nkeys = 2 * ATTN_BLOCK
    left_kv = lax.broadcasted_iota(jnp.int32, (nkeys, LANES), 1) < HEAD_DIM
    left_q = lax.broadcasted_iota(jnp.int32, (ATTN_BLOCK, LANES), 1) < HEAD_DIM
    key_lane = lax.broadcasted_iota(jnp.int32, (ATTN_BLOCK, 2 * nkeys), 1)
    prev_keys = (key_lane % nkeys) < ATTN_BLOCK
    no_prev = jnp.where(prev_keys, jnp.where(first, -jnp.inf, 0.0), 0.0)

    def block_diag(x):
        z = jnp.zeros_like(x)
        return jnp.concatenate([jnp.where(left_kv, x, z), jnp.where(left_kv, z, x)], axis=0)

    n_pairs = N_HEADS // 2
    for j in range(ts // ATTN_BLOCK):
        r0 = j * ATTN_BLOCK
        for kvh in range(N_KV_HEADS):
            kbd = block_diag(kvall[r0:r0 + nkeys, kvh * LANES:(kvh + 1) * LANES])
            for ii in range(2):
                i = 2 * kvh + ii
                qb = q_ref[r0:r0 + ATTN_BLOCK, i * LANES:(i + 1) * LANES]
                s = lax.dot_general(qb, kbd, (((1,), (1,)), ((), ())), preferred_element_type=f32)
                s = s + bias_ref[i]
                if j == 0:
                    s = s + no_prev
                s_scr[j * n_pairs + i] = s
                for half in range(2):
                    m = jnp.max(s[:, half * nkeys:(half + 1) * nkeys], axis=-1, keepdims=True)
                    m_scr[2 * (j * n_pairs + i) + half] = jnp.maximum(m, sink_ref[2 * i + half])

    for j in range(ts // ATTN_BLOCK):
        r0 = j * ATTN_BLOCK
        for kvh in range(N_KV_HEADS):
            vbd = block_diag(kvall[r0:r0 + nkeys, (N_KV_HEADS + kvh) * LANES:(N_KV_HEADS + kvh + 1) * LANES])
            for ii in range(2):
                i = 2 * kvh + ii
                s = s_scr[j * n_pairs + i]
                ps, ls = [], []
                for half in range(2):
                    m = m_scr[2 * (j * n_pairs + i) + half]
                    p = jnp.exp(s[:, half * nkeys:(half + 1) * nkeys] - m)
                    ps.append(p)
                    ls.append(jnp.sum(p, axis=-1, keepdims=True) + jnp.exp(sink_ref[2 * i + half] - m))
                p = jnp.concatenate(ps, axis=1).astype(bf16)
                o = jnp.dot(p, vbd, preferred_element_type=f32)
                o = o * jnp.where(left_q, 1.0 / ls[0], 1.0 / ls[1])
                attn[r0:r0 + ATTN_BLOCK, i * LANES:(i + 1) * LANES] = o.astype(bf16)

    conv_o = jnp.dot(act_ref[...], wco_ref[...], preferred_element_type=f32)
    attn_o = jnp.dot(attn[...], wao_ref[...], preferred_element_type=f32)
    merged = (sgate_ref[:, :D_MODEL].astype(f32) * conv_o
              + sgate_ref[:, D_MODEL:].astype(f32) * attn_o).astype(bf16)
    h = x_ref[...] + jnp.dot(merged, wo_ref[...], preferred_element_type=f32)
    h_ref[...] = h

    xn2 = _rms(h, gffn_ref[...])
    xn2_ref[...] = xn2.astype(bf16)
    xh = xn2.astype(bf16)
    xl = (xn2 - xh.astype(f32)).astype(bf16)
    logits = (jnp.dot(xh, wrh_ref[...], preferred_element_type=f32)
              + jnp.dot(xh, wrl_ref[...], preferred_element_type=f32)
              + jnp.dot(xl, wrh_ref[...], preferred_element_type=f32)) + br_ref[...]
    lt = logits.T
    sub = lax.broadcasted_iota(jnp.int32, (SUBLANES, ts), 0)

    gl = lt[0:SUBLANES]
    gmax = jnp.max(gl, axis=0, keepdims=True)
    gsel = jnp.min(jnp.where(gl == gmax, sub, SUBLANES), axis=0, keepdims=True)
    p_group = 1.0 / jnp.sum(jnp.exp(gl - gmax), axis=0, keepdims=True)

    e_in = lt[SUBLANES:2 * SUBLANES]
    for g in range(1, N_GROUPS):
        e_in = jnp.where(gsel == g, lt[(g + 1) * SUBLANES:(g + 2) * SUBLANES], e_in)
    m1 = jnp.max(e_in, axis=0, keepdims=True)
    i1 = jnp.min(jnp.where(e_in == m1, sub, SUBLANES), axis=0, keepdims=True)
    rest = jnp.where(sub == i1, -jnp.inf, e_in)
    m2 = jnp.max(rest, axis=0, keepdims=True)
    i2 = jnp.min(jnp.where(rest == m2, sub, SUBLANES), axis=0, keepdims=True)
    t2 = jnp.exp(m2 - m1)
    w1 = p_group / (1.0 + t2)
    w2 = p_group * t2 / (1.0 + t2)
    base = gsel * EXPERTS_PER_GROUP
    route_ref[...] = jnp.where(sub == 0, base + i1, jnp.where(sub == 1, base + i2, 0))
    wts_ref[...] = jnp.where(sub == 0, w1, jnp.where(sub == 1, w2, 0.0))


def _attn_bias():
    qi = np.arange(ATTN_BLOCK)[:, None]
    kj = np.arange(2 * ATTN_BLOCK)[None, :]
    rel = (ATTN_BLOCK + qi - kj).astype(np.float32)
    ok = (rel >= 0) & (rel < ATTN_BLOCK)
    slopes = np.array([2.0 ** (-8.0 * (h + 1) / N_HEADS) for h in range(N_HEADS)], np.float32)
    per_head = [np.where(ok, -(slopes[h] * rel), -np.inf).astype(np.float32) for h in range(N_HEADS)]
    return np.stack([np.concatenate([per_head[2 * i], per_head[2 * i + 1]], axis=1)
                     for i in range(N_HEADS // 2)])


def _mixer(x2, act, q, kv, sgates, sinks, wco, wao, wo, g_ffn, wr_hi, wr_lo, b_r, batch, seq):
    t = x2.shape[0]
    ts = MIXER_TILE
    ns = seq // ts
    bias = jnp.asarray(_attn_bias())

    def row(b, s):
        return b * ns + s

    def full(a):
        return pl.BlockSpec(a.shape, lambda b, s: (0,) * a.ndim, pipeline_mode=pl.Buffered(1))

    in_specs = [
        pl.BlockSpec((ts, D_MODEL), lambda b, s: (row(b, s), 0)),
        pl.BlockSpec((ts, CONV_CH), lambda b, s: (row(b, s), 0)),
        pl.BlockSpec((ts, Q_DIM), lambda b, s: (row(b, s), 0)),
        pl.BlockSpec((ts, 4 * KV_DIM), lambda b, s: (row(b, s), 0)),
        pl.BlockSpec((ATTN_BLOCK, 4 * KV_DIM),
                     lambda b, s: (jnp.maximum(row(b, s) * (ts // ATTN_BLOCK) - 1, 0), 0)),
        pl.BlockSpec((ts, 2 * D_MODEL), lambda b, s: (row(b, s), 0)),
        pl.BlockSpec(memory_space=pltpu.SMEM),
        full(bias), full(wco), full(wao), full(wo), full(g_ffn), full(wr_hi), full(wr_lo), full(b_r),
    ]
    out_specs = [
        pl.BlockSpec((ts, D_MODEL), lambda b, s: (row(b, s), 0)),
        pl.BlockSpec((ts, D_MODEL), lambda b, s: (row(b, s), 0)),
        pl.BlockSpec((SUBLANES, ts), lambda b, s: (0, row(b, s))),
        pl.BlockSpec((SUBLANES, ts), lambda b, s: (0, row(b, s))),
    ]
    out_shape = [
        jax.ShapeDtypeStruct((t, D_MODEL), f32),
        jax.ShapeDtypeStruct((t, D_MODEL), bf16),
        jax.ShapeDtypeStruct((SUBLANES, t), jnp.int32),
        jax.ShapeDtypeStruct((SUBLANES, t), f32),
    ]
    return pl.pallas_call(
        _mixer_kernel,
        grid=(batch, ns),
        in_specs=in_specs,
        out_specs=out_specs,
        out_shape=out_shape,
        scratch_shapes=[
            pltpu.VMEM((ts + ATTN_BLOCK, 4 * KV_DIM), bf16),
            pltpu.VMEM((ts, Q_DIM), bf16),
            pltpu.VMEM((ts // ATTN_BLOCK * (N_HEADS // 2), ATTN_BLOCK, 4 * ATTN_BLOCK), f32),
            pltpu.VMEM((ts // ATTN_BLOCK * N_HEADS, ATTN_BLOCK, 1), f32),
        ],
        compiler_params=pltpu.CompilerParams(
            dimension_semantics=("arbitrary", "arbitrary"), vmem_limit_bytes=VMEM_LIMIT),
        name="mixer",
    )(x2, act, q, kv, kv, sgates, sinks, bias, wco, wao, wo, g_ffn, wr_hi, wr_lo, b_r)


def _local_rows(ts):
    return -(-(2 * ts + (RUN_ALIGN - 1) * N_EXPERTS) // LANES) * LANES


PAD_COARSE = 16
assert (EXPERT_BLOCK // RUN_ALIGN) % PAD_COARSE == 0
TABLE_LANES = 2 * LANES
SEG_NACT_LANE = 3 * N_EXPERTS
SEG_SPARE_LANE = 3 * N_EXPERTS + 1


def _num_blocks(t):
    run_rows = 2 * t + (RUN_ALIGN - 1) * N_EXPERTS * (t // ROUTE_TILE)
    return -(-(run_rows + N_EXPERTS * (EXPERT_BLOCK - RUN_ALIGN)) // EXPERT_BLOCK)


def _route_kernel(route_ref, lpos_ref, tab_ref, seg_ref, meta_ref):
    t = route_ref.shape[1]
    tr = ROUTE_TILE
    nbp = meta_ref.shape[1]
    chunks_per_block = EXPERT_BLOCK // RUN_ALIGN
    eiota = lax.broadcasted_iota(jnp.int32, (N_EXPERTS, tr), 0)
    before = (lax.broadcasted_iota(jnp.int32, (tr, tr), 0)
              < lax.broadcasted_iota(jnp.int32, (tr, tr), 1)).astype(bf16)
    lower = (lax.broadcasted_iota(jnp.int32, (N_EXPERTS, N_EXPERTS), 1)
             < lax.broadcasted_iota(jnp.int32, (N_EXPERTS, N_EXPERTS), 0)).astype(bf16)
    sub = lax.broadcasted_iota(jnp.int32, (N_EXPERTS, LANES), 0)
    lane = lax.broadcasted_iota(jnp.int32, (N_EXPERTS, LANES), 1)

    def to_lanes(col, offset):
        return jnp.sum(jnp.where(sub + offset == lane, col, 0.0), axis=0, keepdims=True)

    def expert_prefix(col):
        b = jnp.broadcast_to(col, (N_EXPERTS, LANES))
        hi = jnp.floor(b * (1.0 / 16.0))
        lo = b - 16.0 * hi
        return (16.0 * jnp.dot(lower, hi.astype(bf16), preferred_element_type=f32)
                + jnp.dot(lower, lo.astype(bf16), preferred_element_type=f32))[:, 0:1]

    lpos_ref[...] = jnp.zeros(lpos_ref.shape, jnp.int32)
    chunk_id = lax.broadcasted_iota(jnp.int32, (N_EXPERTS, TABLE_LANES), 1).astype(f32)
    chunk_expert = lax.broadcasted_iota(jnp.int32, (N_EXPERTS, TABLE_LANES), 0).astype(f32)

    def step(i, seen_chunks):
        off = pl.multiple_of(i * tr, tr)
        m1 = eiota == route_ref[0:1, pl.ds(off, tr)]
        m2 = eiota == route_ref[1:2, pl.ds(off, tr)]
        onehot = jnp.where(m1 | m2, 1.0, 0.0)
        within = jnp.dot(onehot.astype(bf16), before, preferred_element_type=f32)
        run_chunks = jnp.floor((jnp.sum(onehot, axis=1, keepdims=True) + (RUN_ALIGN - 1)) * (1.0 / RUN_ALIGN))
        run_start = expert_prefix(run_chunks)
        pos = within + RUN_ALIGN * run_start
        lpos_ref[0:1, pl.ds(off, tr)] = jnp.sum(jnp.where(m1, pos, 0.0), axis=0, keepdims=True).astype(jnp.int32)
        lpos_ref[1:2, pl.ds(off, tr)] = jnp.sum(jnp.where(m2, pos, 0.0), axis=0, keepdims=True).astype(jnp.int32)
        owner = (run_start <= chunk_id) & (chunk_id < run_start + run_chunks)
        rel = jnp.sum(jnp.where(owner, seen_chunks + chunk_id - run_start, 0.0), axis=0, keepdims=True)
        eid = jnp.sum(jnp.where(owner, chunk_expert, 0.0), axis=0, keepdims=True)
        n_used = jnp.sum(run_chunks, axis=0, keepdims=True)
        unused = chunk_id[0:1] >= n_used
        rel = jnp.where(unused, chunk_id[0:1] - n_used, rel)
        eid = jnp.where(unused, (SEG_SPARE_LANE + lax.rem(i, 2)).astype(f32), eid)
        row = jnp.concatenate([rel, eid], axis=1)
        tab_ref[i] = jnp.broadcast_to(row, (SUBLANES, 2 * TABLE_LANES)).astype(jnp.int32)
        return seen_chunks + run_chunks

    used_chunks = lax.fori_loop(0, t // tr, step, jnp.zeros((N_EXPERTS, 1), f32))

    nblk = jnp.floor((used_chunks + (chunks_per_block - 1)) * (1.0 / chunks_per_block))
    first_blk = expert_prefix(nblk)
    nact = jnp.sum(nblk, axis=0, keepdims=True)
    spare_chunk = _num_blocks(t) * chunks_per_block
    seg_row = (to_lanes(first_blk * chunks_per_block, 0) + to_lanes(used_chunks, N_EXPERTS)
               + to_lanes(nblk * chunks_per_block, 2 * N_EXPERTS)
               + jnp.where(lane[0:1] == SEG_NACT_LANE, nact, 0.0)
               + jnp.where(lane[0:1] == SEG_SPARE_LANE, float(spare_chunk), 0.0)
               + jnp.where(lane[0:1] == SEG_SPARE_LANE + 1, float(spare_chunk + _local_rows(tr) // RUN_ALIGN), 0.0))
    seg_ref[...] = jnp.broadcast_to(seg_row, (SUBLANES, LANES)).astype(jnp.int32)

    tab = tab_ref[...].astype(f32)
    rel, eid = tab[:, :, :TABLE_LANES], tab[:, :, TABLE_LANES:]
    start = jnp.zeros_like(rel)
    for seg_lane in list(range(N_EXPERTS)) + [SEG_SPARE_LANE, SEG_SPARE_LANE + 1]:
        start = start + jnp.where(eid == seg_lane, seg_row[:, seg_lane:seg_lane + 1], 0.0)
    write_row = (start + rel) * RUN_ALIGN
    read_row = jnp.where(eid >= SEG_SPARE_LANE, rel, start + rel) * RUN_ALIGN
    tab_ref[...] = jnp.concatenate([write_row, read_row], axis=2).astype(jnp.int32)

    blk = lax.broadcasted_iota(jnp.int32, (N_EXPERTS, nbp), 1).astype(f32)
    owner = (first_blk <= blk) & (blk < first_blk + nblk)
    expert_id = lax.broadcasted_iota(jnp.int32, (N_EXPERTS, nbp), 0).astype(f32)
    bexp = jnp.sum(jnp.where(owner, expert_id, 0.0), axis=0, keepdims=True)
    row8 = lax.broadcasted_iota(jnp.int32, (SUBLANES, nbp), 0)
    meta_ref[...] = jnp.where(row8 == 0, bexp, jnp.where(row8 == 1, nact, 0.0)).astype(jnp.int32)


def _route(route):
    t = route.shape[1]
    nt = t // ROUTE_TILE
    nbp = -(-_num_blocks(t) // LANES) * LANES
    return pl.pallas_call(
        _route_kernel,
        grid=(1,),
        in_specs=[pl.BlockSpec(route.shape, lambda i: (0, 0))],
        out_specs=[pl.BlockSpec(route.shape, lambda i: (0, 0)),
                   pl.BlockSpec((nt, SUBLANES, 2 * TABLE_LANES), lambda i: (0, 0, 0)),
                   pl.BlockSpec((SUBLANES, LANES), lambda i: (0, 0)),
                   pl.BlockSpec((SUBLANES, nbp), lambda i: (0, 0))],
        out_shape=[jax.ShapeDtypeStruct(route.shape, jnp.int32),
                   jax.ShapeDtypeStruct((nt, SUBLANES, 2 * TABLE_LANES), jnp.int32),
                   jax.ShapeDtypeStruct((SUBLANES, LANES), jnp.int32),
                   jax.ShapeDtypeStruct((SUBLANES, nbp), jnp.int32)],
        compiler_params=pltpu.CompilerParams(
            dimension_semantics=("arbitrary",), vmem_limit_bytes=VMEM_LIMIT),
        name="route",
    )(route)


def _chunk_row(tab_ref, k, reading):
    return pl.multiple_of(tab_ref[0, 0, (TABLE_LANES if reading else 0) + k], RUN_ALIGN)


def _scatter_kernel(tab_ref, seg_ref, lpos_ref, x_ref, xs_ref, xsl, zrows, sem, zsem):
    lrows, ts = xsl.shape[1], x_ref.shape[0]
    n_blocks = (xs_ref.shape[0] - 2 * lrows) // EXPERT_BLOCK
    i = pl.program_id(0)
    last = pl.num_programs(0) - 1
    slot = lax.rem(i, 2)

    def drain(s):
        pltpu.make_async_copy(xsl.at[s], xs_ref.at[pl.ds(0, lrows)], sem.at[s]).wait()

    @pl.when(i >= 2)
    def _():
        drain(slot)

    j = lax.broadcasted_iota(jnp.int32, (lrows, ts), 0)
    perm = jnp.where((j == lpos_ref[0:1, :]) | (j == lpos_ref[1:2, :]), 1.0, 0.0).astype(bf16)
    xsl[slot] = _pack_rows(jnp.dot(perm, x_ref[...], preferred_element_type=f32), already_bf16=True)

    for k in range(lrows // RUN_ALIGN):
        pltpu.make_async_copy(xsl.at[slot, pl.ds(k * RUN_ALIGN, RUN_ALIGN)],
                              xs_ref.at[pl.ds(_chunk_row(tab_ref, k, False), RUN_ALIGN)], sem.at[slot]).start()

    @pl.when(i == last)
    def _():
        drain(slot)

        @pl.when(i >= 1)
        def _():
            drain(1 - slot)

        zrows[...] = jnp.zeros(zrows.shape, zrows.dtype)

        def zcopy(row, n):
            return pltpu.make_async_copy(zrows.at[pl.ds(0, n)], xs_ref.at[pl.ds(row, n)], zsem)

        def for_each_unused_block(fn):
            def body(b, carry):
                fn(pl.multiple_of(b * EXPERT_BLOCK, EXPERT_BLOCK), EXPERT_BLOCK)
                return carry

            lax.fori_loop(seg_ref[0, SEG_NACT_LANE], n_blocks, body, 0)
            for off in range(0, 2 * lrows, EXPERT_BLOCK):
                fn(n_blocks * EXPERT_BLOCK + off, min(EXPERT_BLOCK, 2 * lrows - off))

        for_each_unused_block(lambda row, n: zcopy(row, n).start())
        for_each_unused_block(lambda row, n: zcopy(row, n).wait())

        def for_each_pad_chunk(fn):
            def seg(e, carry):
                g0 = seg_ref[0, e]
                used = seg_ref[0, N_EXPERTS + e]
                total = seg_ref[0, 2 * N_EXPERTS + e]

                def chunk(c, carry2):
                    fn(pl.multiple_of((g0 + c) * RUN_ALIGN, RUN_ALIGN), RUN_ALIGN)
                    return carry2

                def coarse(c, carry2):
                    fn(pl.multiple_of((g0 + c * PAD_COARSE) * RUN_ALIGN, PAD_COARSE * RUN_ALIGN),
                       PAD_COARSE * RUN_ALIGN)
                    return carry2

                aligned = lax.div(used + (PAD_COARSE - 1), PAD_COARSE)
                lax.fori_loop(used, aligned * PAD_COARSE, chunk, 0)
                lax.fori_loop(aligned, lax.div(total, PAD_COARSE), coarse, 0)
                return carry

            lax.fori_loop(0, N_EXPERTS, seg, 0)

        for_each_pad_chunk(lambda row, n: zcopy(row, n).start())
        for_each_pad_chunk(lambda row, n: zcopy(row, n).wait())


def _scatter(tab, seg, lpos, xn2, n_blocks):
    t = xn2.shape[0]
    ts = ROUTE_TILE
    lrows = _local_rows(ts)
    return pl.pallas_call(
        _scatter_kernel,
        grid=(t // ts,),
        in_specs=[
            pl.BlockSpec((1, SUBLANES, 2 * TABLE_LANES), lambda i: (i, 0, 0), memory_space=pltpu.SMEM),
            pl.BlockSpec((SUBLANES, LANES), lambda i: (0, 0), memory_space=pltpu.SMEM),
            pl.BlockSpec((SUBLANES, ts), lambda i: (0, i)),
            pl.BlockSpec((ts, D_MODEL), lambda i: (i, 0)),
        ],
        out_specs=pl.BlockSpec(memory_space=pl.ANY),
        out_shape=jax.ShapeDtypeStruct((n_blocks * EXPERT_BLOCK + 2 * lrows, PACKED), jnp.uint32),
        scratch_shapes=[pltpu.VMEM((2, lrows, PACKED), jnp.uint32), pltpu.VMEM((EXPERT_BLOCK, PACKED), jnp.uint32),
                        pltpu.SemaphoreType.DMA((2,)), pltpu.SemaphoreType.DMA],
        compiler_params=pltpu.CompilerParams(
            dimension_semantics=("arbitrary",), vmem_limit_bytes=VMEM_LIMIT, has_side_effects=True),
        name="scatter",
    )(tab, seg, lpos, xn2)


def _expert_kernel(bexp_ref, nact_ref, xs_ref, wg_ref, wu_ref, wd_ref, ys_ref, wg_b, wu_b, wd_b):
    b = pl.program_id(0)
    active = b < nact_ref[0]
    new_expert = jnp.logical_or(b == 0, bexp_ref[b] != bexp_ref[jnp.maximum(b - 1, 0)])

    @pl.when(jnp.logical_not(active))
    def _():
        ys_ref[...] = jnp.zeros(ys_ref.shape, ys_ref.dtype)

    @pl.when(jnp.logical_and(active, new_expert))
    def _():
        wg_b[...] = wg_ref[0].astype(bf16)
        wu_b[...] = wu_ref[0].astype(bf16)
        wd_b[...] = wd_ref[0].astype(bf16)

    @pl.when(active)
    def _():
        x_lo, x_hi = _unpack_rows(xs_ref[...])

        def up(w_ref):
            return (jnp.dot(x_lo, w_ref[:PACKED, :], preferred_element_type=f32)
                    + jnp.dot(x_hi, w_ref[PACKED:, :], preferred_element_type=f32))

        g = up(wg_b)
        hmid = (g * _sigmoid(g) * up(wu_b)).astype(bf16)
        ys_ref[...] = _pack_rows(jnp.dot(hmid, wd_b[...], preferred_element_type=f32))


def _experts(bexp, nact, xs, wg, wu, wd, nb):
    n_rows = nb * EXPERT_BLOCK

    def blk(b, nact_ref):
        return jnp.minimum(b, nact_ref[0] - 1)

    grid_spec = pltpu.PrefetchScalarGridSpec(
        num_scalar_prefetch=2,
        grid=(nb,),
        in_specs=[
            pl.BlockSpec((EXPERT_BLOCK, PACKED), lambda b, e, n: (blk(b, n), 0)),
            pl.BlockSpec((1, D_MODEL, D_FF_EXPERT), lambda b, e, n: (e[blk(b, n)], 0, 0)),
            pl.BlockSpec((1, D_MODEL, D_FF_EXPERT), lambda b, e, n: (e[blk(b, n)], 0, 0)),
            pl.BlockSpec((1, D_FF_EXPERT, D_MODEL), lambda b, e, n: (e[blk(b, n)], 0, 0)),
        ],
        out_specs=pl.BlockSpec((EXPERT_BLOCK, PACKED), lambda b, e, n: (b, 0)),
        scratch_shapes=[pltpu.VMEM((D_MODEL, D_FF_EXPERT), bf16), pltpu.VMEM((D_MODEL, D_FF_EXPERT), bf16),
                        pltpu.VMEM((D_FF_EXPERT, D_MODEL), bf16)],
    )
    return pl.pallas_call(
        _expert_kernel,
        grid_spec=grid_spec,
        out_shape=jax.ShapeDtypeStruct((n_rows, PACKED), jnp.uint32),
        compiler_params=pltpu.CompilerParams(
            dimension_semantics=("arbitrary",), vmem_limit_bytes=VMEM_LIMIT),
        name="experts",
    )(bexp, nact, xs, wg, wu, wd)


def _combine_kernel(tab_ref, tab_next_ref, lpos_ref, wts_ref, h_ref, gf_ref, ys_ref, out_ref, ybuf, sem):
    ts, lrows = h_ref.shape[0], ybuf.shape[1]
    i = pl.program_id(0)
    slot = lax.rem(i, 2)

    def fetch(t_ref, s):
        for k in range(lrows // RUN_ALIGN):
            pltpu.make_async_copy(ys_ref.at[pl.ds(_chunk_row(t_ref, k, True), RUN_ALIGN)],
                                  ybuf.at[s, pl.ds(k * RUN_ALIGN, RUN_ALIGN)], sem.at[s]).start()

    @pl.when(i == 0)
    def _():
        fetch(tab_ref, 0)

    @pl.when(i + 1 < pl.num_programs(0))
    def _():
        fetch(tab_next_ref, 1 - slot)

    pltpu.make_async_copy(ys_ref.at[pl.ds(0, lrows)], ybuf.at[slot], sem.at[slot]).wait()

    info = jnp.concatenate([lpos_ref[...].astype(f32), wts_ref[...],
                            jnp.zeros((LANES - 2 * SUBLANES, ts), f32)], axis=0).T
    jl = lax.broadcasted_iota(jnp.int32, (ts, lrows), 1).astype(f32)
    mix = (jnp.where(jl == info[:, 0:1], info[:, SUBLANES:SUBLANES + 1], 0.0)
           + jnp.where(jl == info[:, 1:2], info[:, SUBLANES + 1:SUBLANES + 2], 0.0)).astype(bf16)
    moe = jnp.concatenate([jnp.dot(mix, y, preferred_element_type=f32) for y in _unpack_rows(ybuf[slot])],
                          axis=1)
    out_ref[...] = _rms(h_ref[...] + moe, gf_ref[...])


def _combine(tab, lpos, wts, h, g_final, ys):
    t = h.shape[0]
    ts = ROUTE_TILE
    nt = t // ts
    lrows = _local_rows(ts)
    assert ys.shape[0] >= lrows
    return pl.pallas_call(
        _combine_kernel,
        grid=(nt,),
        in_specs=[
            pl.BlockSpec((1, SUBLANES, 2 * TABLE_LANES), lambda i: (i, 0, 0), memory_space=pltpu.SMEM),
            pl.BlockSpec((1, SUBLANES, 2 * TABLE_LANES), lambda i: (jnp.minimum(i + 1, nt - 1), 0, 0),
                         memory_space=pltpu.SMEM),
            pl.BlockSpec((SUBLANES, ts), lambda i: (0, i)),
            pl.BlockSpec((SUBLANES, ts), lambda i: (0, i)),
            pl.BlockSpec((ts, D_MODEL), lambda i: (i, 0)),
            pl.BlockSpec((1, D_MODEL), lambda i: (0, 0)),
            pl.BlockSpec(memory_space=pl.ANY),
        ],
        out_specs=pl.BlockSpec((ts, D_MODEL), lambda i: (i, 0)),
        out_shape=jax.ShapeDtypeStruct((t, D_MODEL), f32),
        scratch_shapes=[pltpu.VMEM((2, lrows, PACKED), jnp.uint32), pltpu.SemaphoreType.DMA((2,))],
        compiler_params=pltpu.CompilerParams(
            dimension_semantics=("arbitrary",), vmem_limit_bytes=VMEM_LIMIT),
        name="combine",
    )(tab, tab, lpos, wts, h, g_final, ys)


def _prep_router(w_group, b_group, w_expert, b_expert):
    d = w_group.shape[0]
    w = jnp.zeros((d, LANES), f32)
    w = w.at[:, :N_GROUPS].set(w_group).at[:, SUBLANES:ROUTER_ROWS].set(w_expert)
    b = jnp.full((1, LANES), NEG_BIG, f32)
    b = b.at[0, :N_GROUPS].set(b_group).at[0, SUBLANES:ROUTER_ROWS].set(b_expert)
    w_hi = w.astype(bf16)
    w_lo = (w - w_hi.astype(f32)).astype(bf16)
    return w_hi, w_lo, b


def kernel(x, g_mix, w_in, w_dw, b_dw, ln_conv_g, ln_conv_b, sinks, w_conv_out, w_attn_out, w_out, g_ffn,
           w_group, b_group, w_expert, b_expert, w_gate, w_up, w_down, g_final):
    batch, seq, d = x.shape
    assert d == D_MODEL and seq % MIXER_TILE == 0 and seq % TOKEN_TILE == 0 and g_mix.shape[0] == 1
    t = batch * seq
    x2 = x.reshape(t, d)

    act, q, kv, sgates = _inproj(x2, g_mix[0][None, :], w_in[0], w_dw[0, :, 0, :], b_dw[0][None, :],
                                 ln_conv_g[0][None, :], ln_conv_b[0][None, :], seq)

    wr_hi, wr_lo, b_r = _prep_router(w_group[0], b_group[0], w_expert[0], b_expert[0])
    h, xn2, route, wts = _mixer(
        x2, act, q, kv, sgates, sinks[0], w_conv_out[0].astype(bf16), w_attn_out[0].astype(bf16),
        w_out[0].astype(bf16), g_ffn[0][None, :], wr_hi, wr_lo, b_r, batch, seq)

    lpos, tab, seg, meta = _route(route)
    nb = _num_blocks(t)
    xs = _scatter(tab, seg, lpos, xn2, nb)
    ys = _experts(meta[0, :nb], meta[1, :1], xs, w_gate[0], w_up[0], w_down[0], nb)
    out = _combine(tab, lpos, wts, h, g_final[None, :], ys)
    return out.reshape(batch, seq, d)
```

```python
import functools

import numpy as np
import jax
import jax.numpy as jnp
from jax import lax
from jax.experimental import pallas as pl
from jax.experimental.pallas import tpu as pltpu

D_MODEL = 1024
CONV_CH = 512
CONV_WIDTH = 31
N_HEADS = 8
N_KV_HEADS = 2
HEAD_DIM = 64
ATTN_BLOCK = 128
N_GROUPS = 4
EXPERTS_PER_GROUP = 8
N_EXPERTS = N_GROUPS * EXPERTS_PER_GROUP
D_FF_EXPERT = 512
NORM_EPS = 1e-6

Q_DIM = N_HEADS * HEAD_DIM
KV_DIM = N_KV_HEADS * HEAD_DIM

LANES = 128
SUBLANES = 8
CONV_HALO = 32
CONV_ROWS = 32
PROJ_PIECE = 256
ROUTER_ROWS = SUBLANES + N_EXPERTS
NEG_BIG = -1e30

TOKEN_TILE = 512
MIXER_TILE = 1024
ROUTE_TILE = 512
RUN_ALIGN = SUBLANES
EXPERT_BLOCK = 1024
VMEM_LIMIT = 58 * 1024 * 1024

f32 = jnp.float32
bf16 = jnp.bfloat16


def _rms(x, g):
    ms = jnp.mean(x * x, axis=-1, keepdims=True)
    return x * lax.rsqrt(ms + NORM_EPS) * g


def _sigmoid(x):
    return 1.0 / (1.0 + jnp.exp(-x))


def _exact_zero(x):
    bits = pltpu.bitcast(x, jnp.uint32)
    sixteen = jnp.uint32(16)
    return pltpu.bitcast(lax.shift_right_logical(lax.shift_right_logical(bits, sixteen), sixteen), f32)


PACKED = D_MODEL // 2
_HIGH_HALF = 0xFFFF0000


def _pack_rows(x, already_bf16=False):
    def bits(v):
        return pltpu.bitcast(v if already_bf16 else v.astype(bf16).astype(f32), jnp.uint32)
    low = lax.shift_right_logical(bits(x[:, :PACKED]), jnp.uint32(16))
    return low | (bits(x[:, PACKED:]) & jnp.uint32(_HIGH_HALF))


def _unpack_rows(p):
    low = pltpu.bitcast(lax.shift_left(p, jnp.uint32(16)), f32).astype(bf16)
    high = pltpu.bitcast(p & jnp.uint32(_HIGH_HALF), f32).astype(bf16)
    return low, high


def _load_w_in(w_hbm, stage, w_ref, sem):
    copy = pltpu.make_async_copy(w_hbm, stage, sem)
    copy.start()
    copy.wait()

    def put(dst, value):
        w_ref[:, dst:dst + value.shape[1]] = value.astype(bf16)

    for b in range(CONV_CH // LANES):
        put(2 * b * LANES, stage[:, b * LANES:(b + 1) * LANES])
        put((2 * b + 1) * LANES, stage[:, CONV_CH + b * LANES:CONV_CH + (b + 1) * LANES])
    src = dst = 2 * CONV_CH
    for lo in range(0, Q_DIM, LANES):
        put(dst + lo, stage[:, src + lo:src + lo + LANES] * (HEAD_DIM ** -0.5))
    src, dst = src + Q_DIM, dst + Q_DIM
    for _ in range(2):
        for h in range(N_KV_HEADS):
            head = stage[:, src + h * HEAD_DIM:src + (h + 1) * HEAD_DIM]
            put(dst + 2 * h * HEAD_DIM, jnp.concatenate([head, head], axis=1))
        src, dst = src + KV_DIM, dst + 2 * KV_DIM
    for lo in range(0, 2 * D_MODEL, LANES):
        put(dst + lo, stage[:, src + lo:src + lo + LANES])


def _inproj_kernel(tiles_per_seq, x_ref, g_ref, w_hbm, wdw_ref, bdw_ref, lng_ref, lnb_ref,
                   act_ref, q_ref, kv_ref, sgate_ref, xn_s, vbuf, w_stage, w_ref, w_sem):
    tm = x_ref.shape[0]

    @pl.when(pl.program_id(0) == 0)
    def _():
        _load_w_in(w_hbm, w_stage, w_ref, w_sem)

    xn_s[...] = _rms(x_ref[...], g_ref[...]).astype(bf16)

    def proj(lo, hi):
        return jnp.dot(xn_s[...], w_ref[:, lo:hi], preferred_element_type=f32)

    n_cb = CONV_CH // LANES
    first = lax.rem(pl.program_id(0), tiles_per_seq) == 0

    @pl.when(first)
    def _():
        vbuf[:, 0:CONV_HALO, :] = jnp.zeros((n_cb, CONV_HALO, LANES), f32)

    @pl.when(jnp.logical_not(first))
    def _():
        vbuf[:, 0:CONV_HALO, :] = vbuf[:, tm:tm + CONV_HALO, :]

    for b in range(n_cb):
        u = proj(2 * b * LANES, 2 * (b + 1) * LANES)
        vbuf[b, CONV_HALO:, :] = u[:, :LANES] * _sigmoid(u[:, LANES:])

    pieces, col = [], 2 * CONV_CH
    for out_ref, fn in ((q_ref, lambda z: z), (kv_ref, lambda z: z), (sgate_ref, _sigmoid)):
        pieces += [(out_ref, col, lo, fn) for lo in range(0, out_ref.shape[1], PROJ_PIECE)]
        col += out_ref.shape[1]

    rows = CONV_ROWS
    n_steps = tm // rows
    lag = n_steps - len(pieces)
    assert lag >= 0
    tap0 = CONV_HALO - (CONV_WIDTH - 1)

    def tie_rows(x):
        return jnp.concatenate([_exact_zero(x[0:SUBLANES, 0:LANES])] * (rows // SUBLANES), axis=0)

    tie = None
    for c in range(n_steps):
        if c >= lag:
            out_ref, col, lo, fn = pieces[c - lag]
            z = proj(col + lo, col + lo + PROJ_PIECE)
            out_ref[:, lo:lo + PROJ_PIECE] = fn(z).astype(bf16)
            tie = tie_rows(z) if tie is None else tie + tie_rows(z)
        accs = []
        for b in range(n_cb):
            acc = jnp.broadcast_to(bdw_ref[:, b * LANES:(b + 1) * LANES], (rows, LANES))
            if tie is not None:
                acc = acc + tie
            for j in range(CONV_WIDTH):
                r0 = c * rows + tap0 + j
                acc = acc + wdw_ref[j:j + 1, b * LANES:(b + 1) * LANES] * vbuf[b, r0:r0 + rows, :]
            accs.append(acc)
            tie = tie_rows(sum(acc[k:k + SUBLANES] for k in range(0, rows, SUBLANES)))
        mu = sum(jnp.sum(a, axis=-1, keepdims=True) for a in accs) * (1.0 / CONV_CH)
        ds = [a - mu for a in accs]
        var = sum(jnp.sum(d * d, axis=-1, keepdims=True) for d in ds) * (1.0 / CONV_CH)
        inv = lax.rsqrt(var + NORM_EPS)
        for b in range(n_cb):
            y = ds[b] * inv * lng_ref[:, b * LANES:(b + 1) * LANES] + lnb_ref[:, b * LANES:(b + 1) * LANES]
            act_ref[c * rows:(c + 1) * rows, b * LANES:(b + 1) * LANES] = (y * _sigmoid(y)).astype(bf16)


def _inproj(x2, g_mix, w_in, w_dw, b_dw, ln_g, ln_b, seq):
    t = x2.shape[0]
    tm = TOKEN_TILE
    widths = (CONV_CH, Q_DIM, 4 * KV_DIM, 2 * D_MODEL)

    def full(a):
        return pl.BlockSpec(a.shape, lambda i: (0,) * a.ndim)

    return pl.pallas_call(
        functools.partial(_inproj_kernel, seq // tm),
        grid=(t // tm,),
        in_specs=[pl.BlockSpec((tm, D_MODEL), lambda i: (i, 0)),
                  full(g_mix), pl.BlockSpec(memory_space=pl.ANY), full(w_dw), full(b_dw), full(ln_g), full(ln_b)],
        out_specs=[pl.BlockSpec((tm, w), lambda i: (i, 0)) for w in widths],
        out_shape=[jax.ShapeDtypeStruct((t, w), bf16) for w in widths],
        scratch_shapes=[pltpu.VMEM((tm, D_MODEL), bf16),
                        pltpu.VMEM((CONV_CH // LANES, tm + CONV_HALO, LANES), f32),
                        pltpu.VMEM(w_in.shape, f32),
                        pltpu.VMEM((D_MODEL, sum(widths) + CONV_CH), bf16),
                        pltpu.SemaphoreType.DMA],
        compiler_params=pltpu.CompilerParams(
            dimension_semantics=("arbitrary",), vmem_limit_bytes=VMEM_LIMIT),
        name="inproj",
    )(x2, g_mix, w_in, w_dw, b_dw, ln_g, ln_b)


def _mixer_kernel(x_ref, act_ref, q_ref, kv_ref, kvp_ref, sgate_ref, sink_ref, bias_ref, wco_ref, wao_ref,
                  wo_ref, gffn_ref, wrh_ref, wrl_ref, br_ref,
                  h_ref, xn2_ref, route_ref, wts_ref,
                  kvall, attn, s_scr, m_scr):
    ts = x_ref.shape[0]
    first = pl.program_id(1) == 0

    kvall[0:ATTN_BLOCK, :] = jnp.where(first, jnp.zeros_like(kvp_ref[...]), kvp_ref[...])
    kvall[ATTN_BLOCK:, :] = kv_ref[...]
    nkeys = 2 * ATTN_BLOCK
    left_kv = lax.broadcasted_iota(jnp.int32, (nkeys, LANES), 1) < HEAD_DIM
    left_q = lax.broadcasted_iota(jnp.int32, (ATTN_BLOCK, LANES), 1) < HEAD_DIM
    key_lane = lax.broadcasted_iota(jnp.int32, (ATTN_BLOCK, 2 * nkeys), 1)
    prev_keys = (key_lane % nkeys) < ATTN_BLOCK
    no_prev = jnp.where(prev_keys, jnp.where(first, -jnp.inf, 0.0), 0.0)

    def block_diag(x):
        z = jnp.zeros_like(x)
        return jnp.concatenate([jnp.where(left_kv, x, z), jnp.where(left_kv, z, x)], axis=0)

    n_pairs = N_HEADS // 2
    for j in range(ts // ATTN_BLOCK):
        r0 = j * ATTN_BLOCK
        for kvh in range(N_KV_HEADS):
            kbd = block_diag(kvall[r0:r0 + nkeys, kvh * LANES:(kvh + 1) * LANES])
            for ii in range(2):
                i = 2 * kvh + ii
                qb = q_ref[r0:r0 + ATTN_BLOCK, i * LANES:(i + 1) * LANES]
                s = lax.dot_general(qb, kbd, (((1,), (1,)), ((), ())), preferred_element_type=f32)
                s = s + bias_ref[i]
                if j == 0:
                    s = s + no_prev
                s_scr[j * n_pairs + i] = s
                for half in range(2):
                    m = jnp.max(s[:, half * nkeys:(half + 1) * nkeys], axis=-1, keepdims=True)
                    m_scr[2 * (j * n_pairs + i) + half] = jnp.maximum(m, sink_ref[2 * i + half])

    for j in range(ts // ATTN_BLOCK):
        r0 = j * ATTN_BLOCK
        for kvh in range(N_KV_HEADS):
            vbd = block_diag(kvall[r0:r0 + nkeys, (N_KV_HEADS + kvh) * LANES:(N_KV_HEADS + kvh + 1) * LANES])
            for ii in range(2):
                i = 2 * kvh + ii
                s = s_scr[j * n_pairs + i]
                ps, ls = [], []
                for half in range(2):
                    m = m_scr[2 * (j * n_pairs + i) + half]
                    p = jnp.exp(s[:, half * nkeys:(half + 1) * nkeys] - m)
                    ps.append(p)
                    ls.append(jnp.sum(p, axis=-1, keepdims=True) + jnp.exp(sink_ref[2 * i + half] - m))
                p = jnp.concatenate(ps, axis=1).astype(bf16)
                o = jnp.dot(p, vbd, preferred_element_type=f32)
                o = o * jnp.where(left_q, 1.0 / ls[0], 1.0 / ls[1])
                attn[r0:r0 + ATTN_BLOCK, i * LANES:(i + 1) * LANES] = o.astype(bf16)

    conv_o = jnp.dot(act_ref[...], wco_ref[...], preferred_element_type=f32)
    attn_o = jnp.dot(attn[...], wao_ref[...], preferred_element_type=f32)
    merged = (sgate_ref[:, :D_MODEL].astype(f32) * conv_o
              + sgate_ref[:, D_MODEL:].astype(f32) * attn_o).astype(bf16)
    h = x_ref[...] + jnp.dot(merged, wo_ref[...], preferred_element_type=f32)
    h_ref[...] = h

    xn2 = _rms(h, gffn_ref[...])
    xn2_ref[...] = xn2.astype(bf16)
    xh = xn2.astype(bf16)
    xl = (xn2 - xh.astype(f32)).astype(bf16)
    logits = (jnp.dot(xh, wrh_ref[...], preferred_element_type=f32)
              + jnp.dot(xh, wrl_ref[...], preferred_element_type=f32)
              + jnp.dot(xl, wrh_ref[...], preferred_element_type=f32)) + br_ref[...]
    lt = logits.T
    sub = lax.broadcasted_iota(jnp.int32, (SUBLANES, ts), 0)

    gl = lt[0:SUBLANES]
    gmax = jnp.max(gl, axis=0, keepdims=True)
    gsel = jnp.min(jnp.where(gl == gmax, sub, SUBLANES), axis=0, keepdims=True)
    p_group = 1.0 / jnp.sum(jnp.exp(gl - gmax), axis=0, keepdims=True)

    e_in = lt[SUBLANES:2 * SUBLANES]
    for g in range(1, N_GROUPS):
        e_in = jnp.where(gsel == g, lt[(g + 1) * SUBLANES:(g + 2) * SUBLANES], e_in)
    m1 = jnp.max(e_in, axis=0, keepdims=True)
    i1 = jnp.min(jnp.where(e_in == m1, sub, SUBLANES), axis=0, keepdims=True)
    rest = jnp.where(sub == i1, -jnp.inf, e_in)
    m2 = jnp.max(rest, axis=0, keepdims=True)
    i2 = jnp.min(jnp.where(rest == m2, sub, SUBLANES), axis=0, keepdims=True)
    t2 = jnp.exp(m2 - m1)
    w1 = p_group / (1.0 + t2)
    w2 = p_group * t2 / (1.0 + t2)
    base = gsel * EXPERTS_PER_GROUP
    route_ref[...] = jnp.where(sub == 0, base + i1, jnp.where(sub == 1, base + i2, 0))
    wts_ref[...] = jnp.where(sub == 0, w1, jnp.where(sub == 1, w2, 0.0))


def _attn_bias():
    qi = np.arange(ATTN_BLOCK)[:, None]
    kj = np.arange(2 * ATTN_BLOCK)[None, :]
    rel = (ATTN_BLOCK + qi - kj).astype(np.float32)
    ok = (rel >= 0) & (rel < ATTN_BLOCK)
    slopes = np.array([2.0 ** (-8.0 * (h + 1) / N_HEADS) for h in range(N_HEADS)], np.float32)
    per_head = [np.where(ok, -(slopes[h] * rel), -np.inf).astype(np.float32) for h in range(N_HEADS)]
    return np.stack([np.concatenate([per_head[2 * i], per_head[2 * i + 1]], axis=1)
                     for i in range(N_HEADS // 2)])


def _mixer(x2, act, q, kv, sgates, sinks, wco, wao, wo, g_ffn, wr_hi, wr_lo, b_r, batch, seq):
    t = x2.shape[0]
    ts = MIXER_TILE
    ns = seq // ts
    bias = jnp.asarray(_attn_bias())

    def row(b, s):
        return b * ns + s

    def full(a):
        return pl.BlockSpec(a.shape, lambda b, s: (0,) * a.ndim, pipeline_mode=pl.Buffered(1))

    in_specs = [
        pl.BlockSpec((ts, D_MODEL), lambda b, s: (row(b, s), 0)),
        pl.BlockSpec((ts, CONV_CH), lambda b, s: (row(b, s), 0)),
        pl.BlockSpec((ts, Q_DIM), lambda b, s: (row(b, s), 0)),
        pl.BlockSpec((ts, 4 * KV_DIM), lambda b, s: (row(b, s), 0)),
        pl.BlockSpec((ATTN_BLOCK, 4 * KV_DIM),
                     lambda b, s: (jnp.maximum(row(b, s) * (ts // ATTN_BLOCK) - 1, 0), 0)),
        pl.BlockSpec((ts, 2 * D_MODEL), lambda b, s: (row(b, s), 0)),
        pl.BlockSpec(memory_space=pltpu.SMEM),
        full(bias), full(wco), full(wao), full(wo), full(g_ffn), full(wr_hi), full(wr_lo), full(b_r),
    ]
    out_specs = [
        pl.BlockSpec((ts, D_MODEL), lambda b, s: (row(b, s), 0)),
        pl.BlockSpec((ts, D_MODEL), lambda b, s: (row(b, s), 0)),
        pl.BlockSpec((SUBLANES, ts), lambda b, s: (0, row(b, s))),
        pl.BlockSpec((SUBLANES, ts), lambda b, s: (0, row(b, s))),
    ]
    out_shape = [
        jax.ShapeDtypeStruct((t, D_MODEL), f32),
        jax.ShapeDtypeStruct((t, D_MODEL), bf16),
        jax.ShapeDtypeStruct((SUBLANES, t), jnp.int32),
        jax.ShapeDtypeStruct((SUBLANES, t), f32),
    ]
    return pl.pallas_call(
        _mixer_kernel,
        grid=(batch, ns),
        in_specs=in_specs,
        out_specs=out_specs,
        out_shape=out_shape,
        scratch_shapes=[
            pltpu.VMEM((ts + ATTN_BLOCK, 4 * KV_DIM), bf16),
            pltpu.VMEM((ts, Q_DIM), bf16),
            pltpu.VMEM((ts // ATTN_BLOCK * (N_HEADS // 2), ATTN_BLOCK, 4 * ATTN_BLOCK), f32),
            pltpu.VMEM((ts // ATTN_BLOCK * N_HEADS, ATTN_BLOCK, 1), f32),
        ],
        compiler_params=pltpu.CompilerParams(
            dimension_semantics=("arbitrary", "arbitrary"), vmem_limit_bytes=VMEM_LIMIT),
        name="mixer",
    )(x2, act, q, kv, kv, sgates, sinks, bias, wco, wao, wo, g_ffn, wr_hi, wr_lo, b_r)


def _local_rows(ts):
    return -(-(2 * ts + (RUN_ALIGN - 1) * N_EXPERTS) // LANES) * LANES


PAD_COARSE = 16
assert (EXPERT_BLOCK // RUN_ALIGN) % PAD_COARSE == 0
TABLE_LANES = 2 * LANES
SEG_NACT_LANE = 3 * N_EXPERTS
SEG_SPARE_LANE = 3 * N_EXPERTS + 1


def _num_blocks(t):
    run_rows = 2 * t + (RUN_ALIGN - 1) * N_EXPERTS * (t // ROUTE_TILE)
    return -(-(run_rows + N_EXPERTS * (EXPERT_BLOCK - RUN_ALIGN)) // EXPERT_BLOCK)


def _route_kernel(route_ref, lpos_ref, tab_ref, seg_ref, meta_ref):
    t = route_ref.shape[1]
    tr = ROUTE_TILE
    nbp = meta_ref.shape[1]
    chunks_per_block = EXPERT_BLOCK // RUN_ALIGN
    eiota = lax.broadcasted_iota(jnp.int32, (N_EXPERTS, tr), 0)
    before = (lax.broadcasted_iota(jnp.int32, (tr, tr), 0)
              < lax.broadcasted_iota(jnp.int32, (tr, tr), 1)).astype(bf16)
    lower = (lax.broadcasted_iota(jnp.int32, (N_EXPERTS, N_EXPERTS), 1)
             < lax.broadcasted_iota(jnp.int32, (N_EXPERTS, N_EXPERTS), 0)).astype(bf16)
    sub = lax.broadcasted_iota(jnp.int32, (N_EXPERTS, LANES), 0)
    lane = lax.broadcasted_iota(jnp.int32, (N_EXPERTS, LANES), 1)

    def to_lanes(col, offset):
        return jnp.sum(jnp.where(sub + offset == lane, col, 0.0), axis=0, keepdims=True)

    def expert_prefix(col):
        b = jnp.broadcast_to(col, (N_EXPERTS, LANES))
        hi = jnp.floor(b * (1.0 / 16.0))
        lo = b - 16.0 * hi
        return (16.0 * jnp.dot(lower, hi.astype(bf16), preferred_element_type=f32)
                + jnp.dot(lower, lo.astype(bf16), preferred_element_type=f32))[:, 0:1]

    lpos_ref[...] = jnp.zeros(lpos_ref.shape, jnp.int32)
    chunk_id = lax.broadcasted_iota(jnp.int32, (N_EXPERTS, TABLE_LANES), 1).astype(f32)
    chunk_expert = lax.broadcasted_iota(jnp.int32, (N_EXPERTS, TABLE_LANES), 0).astype(f32)

    def step(i, seen_chunks):
        off = pl.multiple_of(i * tr, tr)
        m1 = eiota == route_ref[0:1, pl.ds(off, tr)]
        m2 = eiota == route_ref[1:2, pl.ds(off, tr)]
        onehot = jnp.where(m1 | m2, 1.0, 0.0)
        within = jnp.dot(onehot.astype(bf16), before, preferred_element_type=f32)
        run_chunks = jnp.floor((jnp.sum(onehot, axis=1, keepdims=True) + (RUN_ALIGN - 1)) * (1.0 / RUN_ALIGN))
        run_start = expert_prefix(run_chunks)
        pos = within + RUN_ALIGN * run_start
        lpos_ref[0:1, pl.ds(off, tr)] = jnp.sum(jnp.where(m1, pos, 0.0), axis=0, keepdims=True).astype(jnp.int32)
        lpos_ref[1:2, pl.ds(off, tr)] = jnp.sum(jnp.where(m2, pos, 0.0), axis=0, keepdims=True).astype(jnp.int32)
        owner = (run_start <= chunk_id) & (chunk_id < run_start + run_chunks)
        rel = jnp.sum(jnp.where(owner, seen_chunks + chunk_id - run_start, 0.0), axis=0, keepdims=True)
        eid = jnp.sum(jnp.where(owner, chunk_expert, 0.0), axis=0, keepdims=True)
        n_used = jnp.sum(run_chunks, axis=0, keepdims=True)
        unused = chunk_id[0:1] >= n_used
        rel = jnp.where(unused, chunk_id[0:1] - n_used, rel)
        eid = jnp.where(unused, (SEG_SPARE_LANE + lax.rem(i, 2)).astype(f32), eid)
        row = jnp.concatenate([rel, eid], axis=1)
        tab_ref[i] = jnp.broadcast_to(row, (SUBLANES, 2 * TABLE_LANES)).astype(jnp.int32)
        return seen_chunks + run_chunks

    used_chunks = lax.fori_loop(0, t // tr, step, jnp.zeros((N_EXPERTS, 1), f32))

    nblk = jnp.floor((used_chunks + (chunks_per_block - 1)) * (1.0 / chunks_per_block))
    first_blk = expert_prefix(nblk)
    nact = jnp.sum(nblk, axis=0, keepdims=True)
    spare_chunk = _num_blocks(t) * chunks_per_block
    seg_row = (to_lanes(first_blk * chunks_per_block, 0) + to_lanes(used_chunks, N_EXPERTS)
               + to_lanes(nblk * chunks_per_block, 2 * N_EXPERTS)
               + jnp.where(lane[0:1] == SEG_NACT_LANE, nact, 0.0)
               + jnp.where(lane[0:1] == SEG_SPARE_LANE, float(spare_chunk), 0.0)
               + jnp.where(lane[0:1] == SEG_SPARE_LANE + 1, float(spare_chunk + _local_rows(tr) // RUN_ALIGN), 0.0))
    seg_ref[...] = jnp.broadcast_to(seg_row, (SUBLANES, LANES)).astype(jnp.int32)

    tab = tab_ref[...].astype(f32)
    rel, eid = tab[:, :, :TABLE_LANES], tab[:, :, TABLE_LANES:]
    start = jnp.zeros_like(rel)
    for seg_lane in list(range(N_EXPERTS)) + [SEG_SPARE_LANE, SEG_SPARE_LANE + 1]:
        start = start + jnp.where(eid == seg_lane, seg_row[:, seg_lane:seg_lane + 1], 0.0)
    write_row = (start + rel) * RUN_ALIGN
    read_row = jnp.where(eid >= SEG_SPARE_LANE, rel, start + rel) * RUN_ALIGN
    tab_ref[...] = jnp.concatenate([write_row, read_row], axis=2).astype(jnp.int32)

    blk = lax.broadcasted_iota(jnp.int32, (N_EXPERTS, nbp), 1).astype(f32)
    owner = (first_blk <= blk) & (blk < first_blk + nblk)
    expert_id = lax.broadcasted_iota(jnp.int32, (N_EXPERTS, nbp), 0).astype(f32)
    bexp = jnp.sum(jnp.where(owner, expert_id, 0.0), axis=0, keepdims=True)
    row8 = lax.broadcasted_iota(jnp.int32, (SUBLANES, nbp), 0)
    meta_ref[...] = jnp.where(row8 == 0, bexp, jnp.where(row8 == 1, nact, 0.0)).astype(jnp.int32)


def _route(route):
    t = route.shape[1]
    nt = t // ROUTE_TILE
    nbp = -(-_num_blocks(t) // LANES) * LANES
    return pl.pallas_call(
        _route_kernel,
        grid=(1,),
        in_specs=[pl.BlockSpec(route.shape, lambda i: (0, 0))],
        out_specs=[pl.BlockSpec(route.shape, lambda i: (0, 0)),
                   pl.BlockSpec((nt, SUBLANES, 2 * TABLE_LANES), lambda i: (0, 0, 0)),
                   pl.BlockSpec((SUBLANES, LANES), lambda i: (0, 0)),
                   pl.BlockSpec((SUBLANES, nbp), lambda i: (0, 0))],
        out_shape=[jax.ShapeDtypeStruct(route.shape, jnp.int32),
                   jax.ShapeDtypeStruct((nt, SUBLANES, 2 * TABLE_LANES), jnp.int32),
                   jax.ShapeDtypeStruct((SUBLANES, LANES), jnp.int32),
                   jax.ShapeDtypeStruct((SUBLANES, nbp), jnp.int32)],
        compiler_params=pltpu.CompilerParams(
            dimension_semantics=("arbitrary",), vmem_limit_bytes=VMEM_LIMIT),
        name="route",
    )(route)


def _chunk_row(tab_ref, k, reading):
    return pl.multiple_of(tab_ref[0, 0, (TABLE_LANES if reading else 0) + k], RUN_ALIGN)


def _scatter_kernel(tab_ref, seg_ref, lpos_ref, x_ref, xs_ref, xsl, zrows, sem, zsem):
    lrows, ts = xsl.shape[1], x_ref.shape[0]
    n_blocks = (xs_ref.shape[0] - 2 * lrows) // EXPERT_BLOCK
    i = pl.program_id(0)
    last = pl.num_programs(0) - 1
    slot = lax.rem(i, 2)

    def drain(s):
        pltpu.make_async_copy(xsl.at[s], xs_ref.at[pl.ds(0, lrows)], sem.at[s]).wait()

    @pl.when(i >= 2)
    def _():
        drain(slot)

    j = lax.broadcasted_iota(jnp.int32, (lrows, ts), 0)
    perm = jnp.where((j == lpos_ref[0:1, :]) | (j == lpos_ref[1:2, :]), 1.0, 0.0).astype(bf16)
    xsl[slot] = _pack_rows(jnp.dot(perm, x_ref[...], preferred_element_type=f32), already_bf16=True)

    for k in range(lrows // RUN_ALIGN):
        pltpu.make_async_copy(xsl.at[slot, pl.ds(k * RUN_ALIGN, RUN_ALIGN)],
                              xs_ref.at[pl.ds(_chunk_row(tab_ref, k, False), RUN_ALIGN)], sem.at[slot]).start()

    @pl.when(i == last)
    def _():
        drain(slot)

        @pl.when(i >= 1)
        def _():
            drain(1 - slot)

        zrows[...] = jnp.zeros(zrows.shape, zrows.dtype)

        def zcopy(row, n):
            return pltpu.make_async_copy(zrows.at[pl.ds(0, n)], xs_ref.at[pl.ds(row, n)], zsem)

        def for_each_unused_block(fn):
            def body(b, carry):
                fn(pl.multiple_of(b * EXPERT_BLOCK, EXPERT_BLOCK), EXPERT_BLOCK)
                return carry

            lax.fori_loop(seg_ref[0, SEG_NACT_LANE], n_blocks, body, 0)
            for off in range(0, 2 * lrows, EXPERT_BLOCK):
                fn(n_blocks * EXPERT_BLOCK + off, min(EXPERT_BLOCK, 2 * lrows - off))

        for_each_unused_block(lambda row, n: zcopy(row, n).start())
        for_each_unused_block(lambda row, n: zcopy(row, n).wait())

        def for_each_pad_chunk(fn):
            def seg(e, carry):
                g0 = seg_ref[0, e]
                used = seg_ref[0, N_EXPERTS + e]
                total = seg_ref[0, 2 * N_EXPERTS + e]

                def chunk(c, carry2):
                    fn(pl.multiple_of((g0 + c) * RUN_ALIGN, RUN_ALIGN), RUN_ALIGN)
                    return carry2

                def coarse(c, carry2):
                    fn(pl.multiple_of((g0 + c * PAD_COARSE) * RUN_ALIGN, PAD_COARSE * RUN_ALIGN),
                       PAD_COARSE * RUN_ALIGN)
                    return carry2

                aligned = lax.div(used + (PAD_COARSE - 1), PAD_COARSE)
                lax.fori_loop(used, aligned * PAD_COARSE, chunk, 0)
                lax.fori_loop(aligned, lax.div(total, PAD_COARSE), coarse, 0)
                return carry

            lax.fori_loop(0, N_EXPERTS, seg, 0)

        for_each_pad_chunk(lambda row, n: zcopy(row, n).start())
        for_each_pad_chunk(lambda row, n: zcopy(row, n).wait())


def _scatter(tab, seg, lpos, xn2, n_blocks):
    t = xn2.shape[0]
    ts = ROUTE_TILE
    lrows = _local_rows(ts)
    return pl.pallas_call(
        _scatter_kernel,
        grid=(t // ts,),
        in_specs=[
            pl.BlockSpec((1, SUBLANES, 2 * TABLE_LANES), lambda i: (i, 0, 0), memory_space=pltpu.SMEM),
            pl.BlockSpec((SUBLANES, LANES), lambda i: (0, 0), memory_space=pltpu.SMEM),
            pl.BlockSpec((SUBLANES, ts), lambda i: (0, i)),
            pl.BlockSpec((ts, D_MODEL), lambda i: (i, 0)),
        ],
        out_specs=pl.BlockSpec(memory_space=pl.ANY),
        out_shape=jax.ShapeDtypeStruct((n_blocks * EXPERT_BLOCK + 2 * lrows, PACKED), jnp.uint32),
        scratch_shapes=[pltpu.VMEM((2, lrows, PACKED), jnp.uint32), pltpu.VMEM((EXPERT_BLOCK, PACKED), jnp.uint32),
                        pltpu.SemaphoreType.DMA((2,)), pltpu.SemaphoreType.DMA],
        compiler_params=pltpu.CompilerParams(
            dimension_semantics=("arbitrary",), vmem_limit_bytes=VMEM_LIMIT, has_side_effects=True),
        name="scatter",
    )(tab, seg, lpos, xn2)


def _expert_kernel(bexp_ref, nact_ref, xs_ref, wg_ref, wu_ref, wd_ref, ys_ref, wg_b, wu_b, wd_b):
    b = pl.program_id(0)
    active = b < nact_ref[0]
    new_expert = jnp.logical_or(b == 0, bexp_ref[b] != bexp_ref[jnp.maximum(b - 1, 0)])

    @pl.when(jnp.logical_not(active))
    def _():
        ys_ref[...] = jnp.zeros(ys_ref.shape, ys_ref.dtype)

    @pl.when(jnp.logical_and(active, new_expert))
    def _():
        wg_b[...] = wg_ref[0].astype(bf16)
        wu_b[...] = wu_ref[0].astype(bf16)
        wd_b[...] = wd_ref[0].astype(bf16)

    @pl.when(active)
    def _():
        x_lo, x_hi = _unpack_rows(xs_ref[...])

        def up(w_ref):
            return (jnp.dot(x_lo, w_ref[:PACKED, :], preferred_element_type=f32)
                    + jnp.dot(x_hi, w_ref[PACKED:, :], preferred_element_type=f32))

        g = up(wg_b)
        hmid = (g * _sigmoid(g) * up(wu_b)).astype(bf16)
        ys_ref[...] = _pack_rows(jnp.dot(hmid, wd_b[...], preferred_element_type=f32))


def _experts(bexp, nact, xs, wg, wu, wd, nb):
    n_rows = nb * EXPERT_BLOCK

    def blk(b, nact_ref):
        return jnp.minimum(b, nact_ref[0] - 1)

    grid_spec = pltpu.PrefetchScalarGridSpec(
        num_scalar_prefetch=2,
        grid=(nb,),
        in_specs=[
            pl.BlockSpec((EXPERT_BLOCK, PACKED), lambda b, e, n: (blk(b, n), 0)),
            pl.BlockSpec((1, D_MODEL, D_FF_EXPERT), lambda b, e, n: (e[blk(b, n)], 0, 0)),
            pl.BlockSpec((1, D_MODEL, D_FF_EXPERT), lambda b, e, n: (e[blk(b, n)], 0, 0)),
            pl.BlockSpec((1, D_FF_EXPERT, D_MODEL), lambda b, e, n: (e[blk(b, n)], 0, 0)),
        ],
        out_specs=pl.BlockSpec((EXPERT_BLOCK, PACKED), lambda b, e, n: (b, 0)),
        scratch_shapes=[pltpu.VMEM((D_MODEL, D_FF_EXPERT), bf16), pltpu.VMEM((D_MODEL, D_FF_EXPERT), bf16),
                        pltpu.VMEM((D_FF_EXPERT, D_MODEL), bf16)],
    )
    return pl.pallas_call(
        _expert_kernel,
        grid_spec=grid_spec,
        out_shape=jax.ShapeDtypeStruct((n_rows, PACKED), jnp.uint32),
        compiler_params=pltpu.CompilerParams(
            dimension_semantics=("arbitrary",), vmem_limit_bytes=VMEM_LIMIT),
        name="experts",
    )(bexp, nact, xs, wg, wu, wd)


def _combine_kernel(tab_ref, tab_next_ref, lpos_ref, wts_ref, h_ref, gf_ref, ys_ref, out_ref, ybuf, sem):
    ts, lrows = h_ref.shape[0], ybuf.shape[1]
    i = pl.program_id(0)
    slot = lax.rem(i, 2)

    def fetch(t_ref, s):
        for k in range(lrows // RUN_ALIGN):
            pltpu.make_async_copy(ys_ref.at[pl.ds(_chunk_row(t_ref, k, True), RUN_ALIGN)],
                                  ybuf.at[s, pl.ds(k * RUN_ALIGN, RUN_ALIGN)], sem.at[s]).start()

    @pl.when(i == 0)
    def _():
        fetch(tab_ref, 0)

    @pl.when(i + 1 < pl.num_programs(0))
    def _():
        fetch(tab_next_ref, 1 - slot)

    pltpu.make_async_copy(ys_ref.at[pl.ds(0, lrows)], ybuf.at[slot], sem.at[slot]).wait()

    info = jnp.concatenate([lpos_ref[...].astype(f32), wts_ref[...],
                            jnp.zeros((LANES - 2 * SUBLANES, ts), f32)], axis=0).T
    jl = lax.broadcasted_iota(jnp.int32, (ts, lrows), 1).astype(f32)
    mix = (jnp.where(jl == info[:, 0:1], info[:, SUBLANES:SUBLANES + 1], 0.0)
           + jnp.where(jl == info[:, 1:2], info[:, SUBLANES + 1:SUBLANES + 2], 0.0)).astype(bf16)
    moe = jnp.concatenate([jnp.dot(mix, y, preferred_element_type=f32) for y in _unpack_rows(ybuf[slot])],
                          axis=1)
    out_ref[...] = _rms(h_ref[...] + moe, gf_ref[...])


def _combine(tab, lpos, wts, h, g_final, ys):
    t = h.shape[0]
    ts = ROUTE_TILE
    nt = t // ts
    lrows = _local_rows(ts)
    assert ys.shape[0] >= lrows
    return pl.pallas_call(
        _combine_kernel,
        grid=(nt,),
        in_specs=[
            pl.BlockSpec((1, SUBLANES, 2 * TABLE_LANES), lambda i: (i, 0, 0), memory_space=pltpu.SMEM),
            pl.BlockSpec((1, SUBLANES, 2 * TABLE_LANES), lambda i: (jnp.minimum(i + 1, nt - 1), 0, 0),
                         memory_space=pltpu.SMEM),
            pl.BlockSpec((SUBLANES, ts), lambda i: (0, i)),
            pl.BlockSpec((SUBLANES, ts), lambda i: (0, i)),
            pl.BlockSpec((ts, D_MODEL), lambda i: (i, 0)),
            pl.BlockSpec((1, D_MODEL), lambda i: (0, 0)),
            pl.BlockSpec(memory_space=pl.ANY),
        ],
        out_specs=pl.BlockSpec((ts, D_MODEL), lambda i: (i, 0)),
        out_shape=jax.ShapeDtypeStruct((t, D_MODEL), f32),
        scratch_shapes=[pltpu.VMEM((2, lrows, PACKED), jnp.uint32), pltpu.SemaphoreType.DMA((2,))],
        compiler_params=pltpu.CompilerParams(
            dimension_semantics=("arbitrary",), vmem_limit_bytes=VMEM_LIMIT),
        name="combine",
    )(tab, tab, lpos, wts, h, g_final, ys)


def _prep_router(w_group, b_group, w_expert, b_expert):
    d = w_group.shape[0]
    w = jnp.zeros((d, LANES), f32)
    w = w.at[:, :N_GROUPS].set(w_group).at[:, SUBLANES:ROUTER_ROWS].set(w_expert)
    b = jnp.full((1, LANES), NEG_BIG, f32)
    b = b.at[0, :N_GROUPS].set(b_group).at[0, SUBLANES:ROUTER_ROWS].set(b_expert)
    w_hi = w.astype(bf16)
    w_lo = (w - w_hi.astype(f32)).astype(bf16)
    return w_hi, w_lo, b


def kernel(x, g_mix, w_in, w_dw, b_dw, ln_conv_g, ln_conv_b, sinks, w_conv_out, w_attn_out, w_out, g_ffn,
           w_group, b_group, w_expert, b_expert, w_gate, w_up, w_down, g_final):
    batch, seq, d = x.shape
    assert d == D_MODEL and seq % MIXER_TILE == 0 and seq % TOKEN_TILE == 0 and g_mix.shape[0] == 1
    t = batch * seq
    x2 = x.reshape(t, d)

    act, q, kv, sgates = _inproj(x2, g_mix[0][None, :], w_in[0], w_dw[0, :, 0, :], b_dw[0][None, :],
                                 ln_conv_g[0][None, :], ln_conv_b[0][None, :], seq)

    wr_hi, wr_lo, b_r = _prep_router(w_group[0], b_group[0], w_expert[0], b_expert[0])
    h, xn2, route, wts = _mixer(
        x2, act, q, kv, sgates, sinks[0], w_conv_out[0].astype(bf16), w_attn_out[0].astype(bf16),
        w_out[0].astype(bf16), g_ffn[0][None, :], wr_hi, wr_lo, b_r, batch, seq)

    lpos, tab, seg, meta = _route(route)
    nb = _num_blocks(t)
    xs = _scatter(tab, seg, lpos, xn2, nb)
    ys = _experts(meta[0, :nb], meta[1, :1], xs, w_gate[0], w_up[0], w_down[0], nb)
    out = _combine(tab, lpos, wts, h, g_final[None, :], ys)
    return out.reshape(batch, seq, d)
```

```python
import functools

import numpy as np
import jax
import jax.numpy as jnp
from jax import lax
from jax.experimental import pallas as pl
from jax.experimental.pallas import tpu as pltpu

D_MODEL = 1024
CONV_CH = 512
CONV_WIDTH = 31
N_HEADS = 8
N_KV_HEADS = 2
HEAD_DIM = 64
ATTN_BLOCK = 128
N_GROUPS = 4
EXPERTS_PER_GROUP = 8
N_EXPERTS = N_GROUPS * EXPERTS_PER_GROUP
D_FF_EXPERT = 512
NORM_EPS = 1e-6

Q_DIM = N_HEADS * HEAD_DIM
KV_DIM = N_KV_HEADS * HEAD_DIM

LANES = 128
SUBLANES = 8
CONV_HALO = 32
CONV_ROWS = 32
PROJ_PIECE = 256
ROUTER_ROWS = SUBLANES + N_EXPERTS
NEG_BIG = -1e30

TOKEN_TILE = 512
MIXER_TILE = 1024
ROUTE_TILE = 512
RUN_ALIGN = SUBLANES
EXPERT_BLOCK = 1024
VMEM_LIMIT = 58 * 1024 * 1024

f32 = jnp.float32
bf16 = jnp.bfloat16


def _rms(x, g):
    ms = jnp.mean(x * x, axis=-1, keepdims=True)
    return x * lax.rsqrt(ms + NORM_EPS) * g


def _sigmoid(x):
    return 1.0 / (1.0 + jnp.exp(-x))


def _exact_zero(x):
    bits = pltpu.bitcast(x, jnp.uint32)
    sixteen = jnp.uint32(16)
    return pltpu.bitcast(lax.shift_right_logical(lax.shift_right_logical(bits, sixteen), sixteen), f32)


PACKED = D_MODEL // 2
_HIGH_HALF = 0xFFFF0000


def _pack_rows(x, already_bf16=False):
    def bits(v):
        return pltpu.bitcast(v if already_bf16 else v.astype(bf16).astype(f32), jnp.uint32)
    low = lax.shift_right_logical(bits(x[:, :PACKED]), jnp.uint32(16))
    return low | (bits(x[:, PACKED:]) & jnp.uint32(_HIGH_HALF))


def _unpack_rows(p):
    low = pltpu.bitcast(lax.shift_left(p, jnp.uint32(16)), f32).astype(bf16)
    high = pltpu.bitcast(p & jnp.uint32(_HIGH_HALF), f32).astype(bf16)
    return low, high


def _load_w_in(w_hbm, stage, w_ref, sem):
    copy = pltpu.make_async_copy(w_hbm, stage, sem)
    copy.start()
    copy.wait()

    def put(dst, value):
        w_ref[:, dst:dst + value.shape[1]] = value.astype(bf16)

    for b in range(CONV_CH // LANES):
        put(2 * b * LANES, stage[:, b * LANES:(b + 1) * LANES])
        put((2 * b + 1) * LANES, stage[:, CONV_CH + b * LANES:CONV_CH + (b + 1) * LANES])
    src = dst = 2 * CONV_CH
    for lo in range(0, Q_DIM, LANES):
        put(dst + lo, stage[:, src + lo:src + lo + LANES] * (HEAD_DIM ** -0.5))
    src, dst = src + Q_DIM, dst + Q_DIM
    for _ in range(2):
        for h in range(N_KV_HEADS):
            head = stage[:, src + h * HEAD_DIM:src + (h + 1) * HEAD_DIM]
            put(dst + 2 * h * HEAD_DIM, jnp.concatenate([head, head], axis=1))
        src, dst = src + KV_DIM, dst + 2 * KV_DIM
    for lo in range(0, 2 * D_MODEL, LANES):
        put(dst + lo, stage[:, src + lo:src + lo + LANES])


def _inproj_kernel(tiles_per_seq, x_ref, g_ref, w_hbm, wdw_ref, bdw_ref, lng_ref, lnb_ref,
                   act_ref, q_ref, kv_ref, sgate_ref, xn_s, vbuf, w_stage, w_ref, w_sem):
    tm = x_ref.shape[0]

    @pl.when(pl.program_id(0) == 0)
    def _():
        _load_w_in(w_hbm, w_stage, w_ref, w_sem)

    xn_s[...] = _rms(x_ref[...], g_ref[...]).astype(bf16)

    def proj(lo, hi):
        return jnp.dot(xn_s[...], w_ref[:, lo:hi], preferred_element_type=f32)

    n_cb = CONV_CH // LANES
    first = lax.rem(pl.program_id(0), tiles_per_seq) == 0

    @pl.when(first)
    def _():
        vbuf[:, 0:CONV_HALO, :] = jnp.zeros((n_cb, CONV_HALO, LANES), f32)

    @pl.when(jnp.logical_not(first))
    def _():
        vbuf[:, 0:CONV_HALO, :] = vbuf[:, tm:tm + CONV_HALO, :]

    for b in range(n_cb):
        u = proj(2 * b * LANES, 2 * (b + 1) * LANES)
        vbuf[b, CONV_HALO:, :] = u[:, :LANES] * _sigmoid(u[:, LANES:])

    pieces, col = [], 2 * CONV_CH
    for out_ref, fn in ((q_ref, lambda z: z), (kv_ref, lambda z: z), (sgate_ref, _sigmoid)):
        pieces += [(out_ref, col, lo, fn) for lo in range(0, out_ref.shape[1], PROJ_PIECE)]
        col += out_ref.shape[1]

    rows = CONV_ROWS
    n_steps = tm // rows
    lag = n_steps - len(pieces)
    assert lag >= 0
    tap0 = CONV_HALO - (CONV_WIDTH - 1)

    def tie_rows(x):
        return jnp.concatenate([_exact_zero(x[0:SUBLANES, 0:LANES])] * (rows // SUBLANES), axis=0)

    tie = None
    for c in range(n_steps):
        if c >= lag:
            out_ref, col, lo, fn = pieces[c - lag]
            z = proj(col + lo, col + lo + PROJ_PIECE)
            out_ref[:, lo:lo + PROJ_PIECE] = fn(z).astype(bf16)
            tie = tie_rows(z) if tie is None else tie + tie_rows(z)
        accs = []
        for b in range(n_cb):
            acc = jnp.broadcast_to(bdw_ref[:, b * LANES:(b + 1) * LANES], (rows, LANES))
            if tie is not None:
                acc = acc + tie
            for j in range(CONV_WIDTH):
                r0 = c * rows + tap0 + j
                acc = acc + wdw_ref[j:j + 1, b * LANES:(b + 1) * LANES] * vbuf[b, r0:r0 + rows, :]
            accs.append(acc)
            tie = tie_rows(sum(acc[k:k + SUBLANES] for k in range(0, rows, SUBLANES)))
        mu = sum(jnp.sum(a, axis=-1, keepdims=True) for a in accs) * (1.0 / CONV_CH)
        ds = [a - mu for a in accs]
        var = sum(jnp.sum(d * d, axis=-1, keepdims=True) for d in ds) * (1.0 / CONV_CH)
        inv = lax.rsqrt(var + NORM_EPS)
        for b in range(n_cb):
            y = ds[b] * inv * lng_ref[:, b * LANES:(b + 1) * LANES] + lnb_ref[:, b * LANES:(b + 1) * LANES]
            act_ref[c * rows:(c + 1) * rows, b * LANES:(b + 1) * LANES] = (y * _sigmoid(y)).astype(bf16)


def _inproj(x2, g_mix, w_in, w_dw, b_dw, ln_g, ln_b, seq):
    t = x2.shape[0]
    tm = TOKEN_TILE
    widths = (CONV_CH, Q_DIM, 4 * KV_DIM, 2 * D_MODEL)

    def full(a):
        return pl.BlockSpec(a.shape, lambda i: (0,) * a.ndim)

    return pl.pallas_call(
        functools.partial(_inproj_kernel, seq // tm),
        grid=(t // tm,),
        in_specs=[pl.BlockSpec((tm, D_MODEL), lambda i: (i, 0)),
                  full(g_mix), pl.BlockSpec(memory_space=pl.ANY), full(w_dw), full(b_dw), full(ln_g), full(ln_b)],
        out_specs=[pl.BlockSpec((tm, w), lambda i: (i, 0)) for w in widths],
        out_shape=[jax.ShapeDtypeStruct((t, w), bf16) for w in widths],
        scratch_shapes=[pltpu.VMEM((tm, D_MODEL), bf16),
                        pltpu.VMEM((CONV_CH // LANES, tm + CONV_HALO, LANES), f32),
                        pltpu.VMEM(w_in.shape, f32),
                        pltpu.VMEM((D_MODEL, sum(widths) + CONV_CH), bf16),
                        pltpu.SemaphoreType.DMA],
        compiler_params=pltpu.CompilerParams(
            dimension_semantics=("arbitrary",), vmem_limit_bytes=VMEM_LIMIT),
        name="inproj",
    )(x2, g_mix, w_in, w_dw, b_dw, ln_g, ln_b)


def _mixer_kernel(x_ref, act_ref, q_ref, kv_ref, kvp_ref, sgate_ref, sink_ref, bias_ref, wco_ref, wao_ref,
                  wo_ref, gffn_ref, wr_ref, br_ref,
                  h_ref, xn2_ref, route_ref, wts_ref,
                  kvall, attn, s_scr, m_scr):
    ts = x_ref.shape[0]
    first = pl.program_id(1) == 0

    kvall[0:ATTN_BLOCK, :] = jnp.where(first, jnp.zeros_like(kvp_ref[...]), kvp_ref[...])
    kvall[ATTN_BLOCK:, :] = kv_ref[...]
    nkeys = 2 * ATTN_BLOCK
    left_kv = lax.broadcasted_iota(jnp.int32, (nkeys, LANES), 1) < HEAD_DIM
    left_q = lax.broadcasted_iota(jnp.int32, (ATTN_BLOCK, LANES), 1) < HEAD_DIM
    key_lane = lax.broadcasted_iota(jnp.int32, (ATTN_BLOCK, 2 * nkeys), 1)
    prev_keys = (key_lane % nkeys) < ATTN_BLOCK
    no_prev = jnp.where(prev_keys, jnp.where(first, -jnp.inf, 0.0), 0.0)

    def block_diag(x):
        z = jnp.zeros_like(x)
        return jnp.concatenate([jnp.where(left_kv, x, z), jnp.where(left_kv, z, x)], axis=0)

    n_pairs = N_HEADS // 2
    for j in range(ts // ATTN_BLOCK):
        r0 = j * ATTN_BLOCK
        for kvh in range(N_KV_HEADS):
            kbd = block_diag(kvall[r0:r0 + nkeys, kvh * LANES:(kvh + 1) * LANES])
            for ii in range(2):
                i = 2 * kvh + ii
                qb = q_ref[r0:r0 + ATTN_BLOCK, i * LANES:(i + 1) * LANES]
                s = lax.dot_general(qb, kbd, (((1,), (1,)), ((), ())), preferred_element_type=f32)
                s = s + bias_ref[i]
                if j == 0:
                    s = s + no_prev
                s_scr[j * n_pairs + i] = s
                for half in range(2):
                    m = jnp.max(s[:, half * nkeys:(half + 1) * nkeys], axis=-1, keepdims=True)
                    m_scr[2 * (j * n_pairs + i) + half] = jnp.maximum(m, sink_ref[2 * i + half])

    for j in range(ts // ATTN_BLOCK):
        r0 = j * ATTN_BLOCK
        for kvh in range(N_KV_HEADS):
            vbd = block_diag(kvall[r0:r0 + nkeys, (N_KV_HEADS + kvh) * LANES:(N_KV_HEADS + kvh + 1) * LANES])
            for ii in range(2):
                i = 2 * kvh + ii
                s = s_scr[j * n_pairs + i]
                ps, ls = [], []
                for half in range(2):
                    m = m_scr[2 * (j * n_pairs + i) + half]
                    p = jnp.exp(s[:, half * nkeys:(half + 1) * nkeys] - m)
                    ps.append(p)
                    ls.append(jnp.sum(p, axis=-1, keepdims=True) + jnp.exp(sink_ref[2 * i + half] - m))
                p = jnp.concatenate(ps, axis=1).astype(bf16)
                o = jnp.dot(p, vbd, preferred_element_type=f32)
                o = o * jnp.where(left_q, 1.0 / ls[0], 1.0 / ls[1])
                attn[r0:r0 + ATTN_BLOCK, i * LANES:(i + 1) * LANES] = o.astype(bf16)

    conv_o = jnp.dot(act_ref[...], wco_ref[...], preferred_element_type=f32)
    attn_o = jnp.dot(attn[...], wao_ref[...], preferred_element_type=f32)
    merged = (sgate_ref[:, :D_MODEL].astype(f32) * conv_o
              + sgate_ref[:, D_MODEL:].astype(f32) * attn_o).astype(bf16)
    h = x_ref[...] + jnp.dot(merged, wo_ref[...], preferred_element_type=f32)
    h_ref[...] = h

    xn2 = _rms(h, gffn_ref[...]).astype(bf16)
    xn2_ref[...] = xn2
    logits = jnp.dot(xn2, wr_ref[...], preferred_element_type=f32) + br_ref[...]
    lt = logits.T
    sub = lax.broadcasted_iota(jnp.int32, (SUBLANES, ts), 0)

    gl = lt[0:SUBLANES]
    gmax = jnp.max(gl, axis=0, keepdims=True)
    gsel = jnp.min(jnp.where(gl == gmax, sub, SUBLANES), axis=0, keepdims=True)
    p_group = 1.0 / jnp.sum(jnp.exp(gl - gmax), axis=0, keepdims=True)

    e_in = lt[SUBLANES:2 * SUBLANES]
    for g in range(1, N_GROUPS):
        e_in = jnp.where(gsel == g, lt[(g + 1) * SUBLANES:(g + 2) * SUBLANES], e_in)
    m1 = jnp.max(e_in, axis=0, keepdims=True)
    i1 = jnp.min(jnp.where(e_in == m1, sub, SUBLANES), axis=0, keepdims=True)
    rest = jnp.where(sub == i1, -jnp.inf, e_in)
    m2 = jnp.max(rest, axis=0, keepdims=True)
    i2 = jnp.min(jnp.where(rest == m2, sub, SUBLANES), axis=0, keepdims=True)
    t2 = jnp.exp(m2 - m1)
    w1 = p_group / (1.0 + t2)
    w2 = p_group * t2 / (1.0 + t2)
    base = gsel * EXPERTS_PER_GROUP
    route_ref[...] = jnp.where(sub == 0, base + i1, jnp.where(sub == 1, base + i2, 0))
    wts_ref[...] = jnp.where(sub == 0, w1, jnp.where(sub == 1, w2, 0.0))


def _attn_bias():
    qi = np.arange(ATTN_BLOCK)[:, None]
    kj = np.arange(2 * ATTN_BLOCK)[None, :]
    rel = (ATTN_BLOCK + qi - kj).astype(np.float32)
    ok = (rel >= 0) & (rel < ATTN_BLOCK)
    slopes = np.array([2.0 ** (-8.0 * (h + 1) / N_HEADS) for h in range(N_HEADS)], np.float32)
    per_head = [np.where(ok, -(slopes[h] * rel), -np.inf).astype(np.float32) for h in range(N_HEADS)]
    return np.stack([np.concatenate([per_head[2 * i], per_head[2 * i + 1]], axis=1)
                     for i in range(N_HEADS // 2)])


def _mixer(x2, act, q, kv, sgates, sinks, wco, wao, wo, g_ffn, w_r, b_r, batch, seq):
    t = x2.shape[0]
    ts = MIXER_TILE
    ns = seq // ts
    bias = jnp.asarray(_attn_bias())

    def row(b, s):
        return b * ns + s

    def full(a):
        return pl.BlockSpec(a.shape, lambda b, s: (0,) * a.ndim, pipeline_mode=pl.Buffered(1))

    in_specs = [
        pl.BlockSpec((ts, D_MODEL), lambda b, s: (row(b, s), 0)),
        pl.BlockSpec((ts, CONV_CH), lambda b, s: (row(b, s), 0)),
        pl.BlockSpec((ts, Q_DIM), lambda b, s: (row(b, s), 0)),
        pl.BlockSpec((ts, 4 * KV_DIM), lambda b, s: (row(b, s), 0)),
        pl.BlockSpec((ATTN_BLOCK, 4 * KV_DIM),
                     lambda b, s: (jnp.maximum(row(b, s) * (ts // ATTN_BLOCK) - 1, 0), 0)),
        pl.BlockSpec((ts, 2 * D_MODEL), lambda b, s: (row(b, s), 0)),
        pl.BlockSpec(memory_space=pltpu.SMEM),
        full(bias), full(wco), full(wao), full(wo), full(g_ffn), full(w_r), full(b_r),
    ]
    out_specs = [
        pl.BlockSpec((ts, D_MODEL), lambda b, s: (row(b, s), 0)),
        pl.BlockSpec((ts, D_MODEL), lambda b, s: (row(b, s), 0)),
        pl.BlockSpec((SUBLANES, ts), lambda b, s: (0, row(b, s))),
        pl.BlockSpec((SUBLANES, ts), lambda b, s: (0, row(b, s))),
    ]
    out_shape = [
        jax.ShapeDtypeStruct((t, D_MODEL), f32),
        jax.ShapeDtypeStruct((t, D_MODEL), bf16),
        jax.ShapeDtypeStruct((SUBLANES, t), jnp.int32),
        jax.ShapeDtypeStruct((SUBLANES, t), f32),
    ]
    return pl.pallas_call(
        _mixer_kernel,
        grid=(batch, ns),
        in_specs=in_specs,
        out_specs=out_specs,
        out_shape=out_shape,
        scratch_shapes=[
            pltpu.VMEM((ts + ATTN_BLOCK, 4 * KV_DIM), bf16),
            pltpu.VMEM((ts, Q_DIM), bf16),
            pltpu.VMEM((ts // ATTN_BLOCK * (N_HEADS // 2), ATTN_BLOCK, 4 * ATTN_BLOCK), f32),
            pltpu.VMEM((ts // ATTN_BLOCK * N_HEADS, ATTN_BLOCK, 1), f32),
        ],
        compiler_params=pltpu.CompilerParams(
            dimension_semantics=("arbitrary", "arbitrary"), vmem_limit_bytes=VMEM_LIMIT),
        name="mixer",
    )(x2, act, q, kv, kv, sgates, sinks, bias, wco, wao, wo, g_ffn, w_r, b_r)


def _local_rows(ts):
    return -(-(2 * ts + (RUN_ALIGN - 1) * N_EXPERTS) // LANES) * LANES


PAD_COARSE = 16
assert (EXPERT_BLOCK // RUN_ALIGN) % PAD_COARSE == 0
TABLE_LANES = 2 * LANES
SEG_NACT_LANE = 3 * N_EXPERTS
SEG_SPARE_LANE = 3 * N_EXPERTS + 1


def _num_blocks(t):
    run_rows = 2 * t + (RUN_ALIGN - 1) * N_EXPERTS * (t // ROUTE_TILE)
    return -(-(run_rows + N_EXPERTS * (EXPERT_BLOCK - RUN_ALIGN)) // EXPERT_BLOCK)


def _route_kernel(route_ref, lpos_ref, tab_ref, seg_ref, meta_ref):
    t = route_ref.shape[1]
    tr = ROUTE_TILE
    nbp = meta_ref.shape[1]
    chunks_per_block = EXPERT_BLOCK // RUN_ALIGN
    eiota = lax.broadcasted_iota(jnp.int32, (N_EXPERTS, tr), 0)
    before = (lax.broadcasted_iota(jnp.int32, (tr, tr), 0)
              < lax.broadcasted_iota(jnp.int32, (tr, tr), 1)).astype(bf16)
    lower = (lax.broadcasted_iota(jnp.int32, (N_EXPERTS, N_EXPERTS), 1)
             < lax.broadcasted_iota(jnp.int32, (N_EXPERTS, N_EXPERTS), 0)).astype(bf16)
    sub = lax.broadcasted_iota(jnp.int32, (N_EXPERTS, LANES), 0)
    lane = lax.broadcasted_iota(jnp.int32, (N_EXPERTS, LANES), 1)

    def to_lanes(col, offset):
        return jnp.sum(jnp.where(sub + offset == lane, col, 0.0), axis=0, keepdims=True)

    def expert_prefix(col):
        b = jnp.broadcast_to(col, (N_EXPERTS, LANES))
        hi = jnp.floor(b * (1.0 / 16.0))
        lo = b - 16.0 * hi
        return (16.0 * jnp.dot(lower, hi.astype(bf16), preferred_element_type=f32)
                + jnp.dot(lower, lo.astype(bf16), preferred_element_type=f32))[:, 0:1]

    lpos_ref[...] = jnp.zeros(lpos_ref.shape, jnp.int32)
    chunk_id = lax.broadcasted_iota(jnp.int32, (N_EXPERTS, TABLE_LANES), 1).astype(f32)
    chunk_expert = lax.broadcasted_iota(jnp.int32, (N_EXPERTS, TABLE_LANES), 0).astype(f32)

    def step(i, seen_chunks):
        off = pl.multiple_of(i * tr, tr)
        m1 = eiota == route_ref[0:1, pl.ds(off, tr)]
        m2 = eiota == route_ref[1:2, pl.ds(off, tr)]
        onehot = jnp.where(m1 | m2, 1.0, 0.0)
        within = jnp.dot(onehot.astype(bf16), before, preferred_element_type=f32)
        run_chunks = jnp.floor((jnp.sum(onehot, axis=1, keepdims=True) + (RUN_ALIGN - 1)) * (1.0 / RUN_ALIGN))
        run_start = expert_prefix(run_chunks)
        pos = within + RUN_ALIGN * run_start
        lpos_ref[0:1, pl.ds(off, tr)] = jnp.sum(jnp.where(m1, pos, 0.0), axis=0, keepdims=True).astype(jnp.int32)
        lpos_ref[1:2, pl.ds(off, tr)] = jnp.sum(jnp.where(m2, pos, 0.0), axis=0, keepdims=True).astype(jnp.int32)
        owner = (run_start <= chunk_id) & (chunk_id < run_start + run_chunks)
        rel = jnp.sum(jnp.where(owner, seen_chunks + chunk_id - run_start, 0.0), axis=0, keepdims=True)
        eid = jnp.sum(jnp.where(owner, chunk_expert, 0.0), axis=0, keepdims=True)
        n_used = jnp.sum(run_chunks, axis=0, keepdims=True)
        unused = chunk_id[0:1] >= n_used
        rel = jnp.where(unused, chunk_id[0:1] - n_used, rel)
        eid = jnp.where(unused, (SEG_SPARE_LANE + lax.rem(i, 2)).astype(f32), eid)
        row = jnp.concatenate([rel, eid], axis=1)
        tab_ref[i] = jnp.broadcast_to(row, (SUBLANES, 2 * TABLE_LANES)).astype(jnp.int32)
        return seen_chunks + run_chunks

    used_chunks = lax.fori_loop(0, t // tr, step, jnp.zeros((N_EXPERTS, 1), f32))

    nblk = jnp.floor((used_chunks + (chunks_per_block - 1)) * (1.0 / chunks_per_block))
    first_blk = expert_prefix(nblk)
    nact = jnp.sum(nblk, axis=0, keepdims=True)
    spare_chunk = _num_blocks(t) * chunks_per_block
    seg_row = (to_lanes(first_blk * chunks_per_block, 0) + to_lanes(used_chunks, N_EXPERTS)
               + to_lanes(nblk * chunks_per_block, 2 * N_EXPERTS)
               + jnp.where(lane[0:1] == SEG_NACT_LANE, nact, 0.0)
               + jnp.where(lane[0:1] == SEG_SPARE_LANE, float(spare_chunk), 0.0)
               + jnp.where(lane[0:1] == SEG_SPARE_LANE + 1, float(spare_chunk + _local_rows(tr) // RUN_ALIGN), 0.0))
    seg_ref[...] = jnp.broadcast_to(seg_row, (SUBLANES, LANES)).astype(jnp.int32)

    tab = tab_ref[...].astype(f32)
    rel, eid = tab[:, :, :TABLE_LANES], tab[:, :, TABLE_LANES:]
    start = jnp.zeros_like(rel)
    for seg_lane in list(range(N_EXPERTS)) + [SEG_SPARE_LANE, SEG_SPARE_LANE + 1]:
        start = start + jnp.where(eid == seg_lane, seg_row[:, seg_lane:seg_lane + 1], 0.0)
    write_row = (start + rel) * RUN_ALIGN
    read_row = jnp.where(eid >= SEG_SPARE_LANE, rel, start + rel) * RUN_ALIGN
    tab_ref[...] = jnp.concatenate([write_row, read_row], axis=2).astype(jnp.int32)

    blk = lax.broadcasted_iota(jnp.int32, (N_EXPERTS, nbp), 1).astype(f32)
    owner = (first_blk <= blk) & (blk < first_blk + nblk)
    expert_id = lax.broadcasted_iota(jnp.int32, (N_EXPERTS, nbp), 0).astype(f32)
    bexp = jnp.sum(jnp.where(owner, expert_id, 0.0), axis=0, keepdims=True)
    row8 = lax.broadcasted_iota(jnp.int32, (SUBLANES, nbp), 0)
    meta_ref[...] = jnp.where(row8 == 0, bexp, jnp.where(row8 == 1, nact, 0.0)).astype(jnp.int32)


def _route(route):
    t = route.shape[1]
    nt = t // ROUTE_TILE
    nbp = -(-_num_blocks(t) // LANES) * LANES
    return pl.pallas_call(
        _route_kernel,
        grid=(1,),
        in_specs=[pl.BlockSpec(route.shape, lambda i: (0, 0))],
        out_specs=[pl.BlockSpec(route.shape, lambda i: (0, 0)),
                   pl.BlockSpec((nt, SUBLANES, 2 * TABLE_LANES), lambda i: (0, 0, 0)),
                   pl.BlockSpec((SUBLANES, LANES), lambda i: (0, 0)),
                   pl.BlockSpec((SUBLANES, nbp), lambda i: (0, 0))],
        out_shape=[jax.ShapeDtypeStruct(route.shape, jnp.int32),
                   jax.ShapeDtypeStruct((nt, SUBLANES, 2 * TABLE_LANES), jnp.int32),
                   jax.ShapeDtypeStruct((SUBLANES, LANES), jnp.int32),
                   jax.ShapeDtypeStruct((SUBLANES, nbp), jnp.int32)],
        compiler_params=pltpu.CompilerParams(
            dimension_semantics=("arbitrary",), vmem_limit_bytes=VMEM_LIMIT),
        name="route",
    )(route)


def _chunk_row(tab_ref, k, reading):
    return pl.multiple_of(tab_ref[0, 0, (TABLE_LANES if reading else 0) + k], RUN_ALIGN)


def _scatter_kernel(tab_ref, seg_ref, lpos_ref, x_ref, xs_ref, xsl, zrows, sem, zsem):
    lrows, ts = xsl.shape[1], x_ref.shape[0]
    n_blocks = (xs_ref.shape[0] - 2 * lrows) // EXPERT_BLOCK
    i = pl.program_id(0)
    last = pl.num_programs(0) - 1
    slot = lax.rem(i, 2)

    def drain(s):
        pltpu.make_async_copy(xsl.at[s], xs_ref.at[pl.ds(0, lrows)], sem.at[s]).wait()

    @pl.when(i >= 2)
    def _():
        drain(slot)

    j = lax.broadcasted_iota(jnp.int32, (lrows, ts), 0)
    perm = jnp.where((j == lpos_ref[0:1, :]) | (j == lpos_ref[1:2, :]), 1.0, 0.0).astype(bf16)
    xsl[slot] = _pack_rows(jnp.dot(perm, x_ref[...], preferred_element_type=f32), already_bf16=True)

    for k in range(lrows // RUN_ALIGN):
        pltpu.make_async_copy(xsl.at[slot, pl.ds(k * RUN_ALIGN, RUN_ALIGN)],
                              xs_ref.at[pl.ds(_chunk_row(tab_ref, k, False), RUN_ALIGN)], sem.at[slot]).start()

    @pl.when(i == last)
    def _():
        drain(slot)

        @pl.when(i >= 1)
        def _():
            drain(1 - slot)

        zrows[...] = jnp.zeros(zrows.shape, zrows.dtype)

        def zcopy(row, n):
            return pltpu.make_async_copy(zrows.at[pl.ds(0, n)], xs_ref.at[pl.ds(row, n)], zsem)

        def for_each_unused_block(fn):
            def body(b, carry):
                fn(pl.multiple_of(b * EXPERT_BLOCK, EXPERT_BLOCK), EXPERT_BLOCK)
                return carry

            lax.fori_loop(seg_ref[0, SEG_NACT_LANE], n_blocks, body, 0)
            for off in range(0, 2 * lrows, EXPERT_BLOCK):
                fn(n_blocks * EXPERT_BLOCK + off, min(EXPERT_BLOCK, 2 * lrows - off))

        for_each_unused_block(lambda row, n: zcopy(row, n).start())
        for_each_unused_block(lambda row, n: zcopy(row, n).wait())

        def for_each_pad_chunk(fn):
            def seg(e, carry):
                g0 = seg_ref[0, e]
                used = seg_ref[0, N_EXPERTS + e]
                total = seg_ref[0, 2 * N_EXPERTS + e]

                def chunk(c, carry2):
                    fn(pl.multiple_of((g0 + c) * RUN_ALIGN, RUN_ALIGN), RUN_ALIGN)
                    return carry2

                def coarse(c, carry2):
                    fn(pl.multiple_of((g0 + c * PAD_COARSE) * RUN_ALIGN, PAD_COARSE * RUN_ALIGN),
                       PAD_COARSE * RUN_ALIGN)
                    return carry2

                aligned = lax.div(used + (PAD_COARSE - 1), PAD_COARSE)
                lax.fori_loop(used, aligned * PAD_COARSE, chunk, 0)
                lax.fori_loop(aligned, lax.div(total, PAD_COARSE), coarse, 0)
                return carry

            lax.fori_loop(0, N_EXPERTS, seg, 0)

        for_each_pad_chunk(lambda row, n: zcopy(row, n).start())
        for_each_pad_chunk(lambda row, n: zcopy(row, n).wait())


def _scatter(tab, seg, lpos, xn2, n_blocks):
    t = xn2.shape[0]
    ts = ROUTE_TILE
    lrows = _local_rows(ts)
    return pl.pallas_call(
        _scatter_kernel,
        grid=(t // ts,),
        in_specs=[
            pl.BlockSpec((1, SUBLANES, 2 * TABLE_LANES), lambda i: (i, 0, 0), memory_space=pltpu.SMEM),
            pl.BlockSpec((SUBLANES, LANES), lambda i: (0, 0), memory_space=pltpu.SMEM),
            pl.BlockSpec((SUBLANES, ts), lambda i: (0, i)),
            pl.BlockSpec((ts, D_MODEL), lambda i: (i, 0)),
        ],
        out_specs=pl.BlockSpec(memory_space=pl.ANY),
        out_shape=jax.ShapeDtypeStruct((n_blocks * EXPERT_BLOCK + 2 * lrows, PACKED), jnp.uint32),
        scratch_shapes=[pltpu.VMEM((2, lrows, PACKED), jnp.uint32), pltpu.VMEM((EXPERT_BLOCK, PACKED), jnp.uint32),
                        pltpu.SemaphoreType.DMA((2,)), pltpu.SemaphoreType.DMA],
        compiler_params=pltpu.CompilerParams(
            dimension_semantics=("arbitrary",), vmem_limit_bytes=VMEM_LIMIT, has_side_effects=True),
        name="scatter",
    )(tab, seg, lpos, xn2)


def _expert_kernel(bexp_ref, nact_ref, xs_ref, wg_ref, wu_ref, wd_ref, ys_ref, wg_b, wu_b, wd_b):
    b = pl.program_id(0)
    active = b < nact_ref[0]
    new_expert = jnp.logical_or(b == 0, bexp_ref[b] != bexp_ref[jnp.maximum(b - 1, 0)])

    @pl.when(jnp.logical_not(active))
    def _():
        ys_ref[...] = jnp.zeros(ys_ref.shape, ys_ref.dtype)

    @pl.when(jnp.logical_and(active, new_expert))
    def _():
        wg_b[...] = wg_ref[0].astype(bf16)
        wu_b[...] = wu_ref[0].astype(bf16)
        wd_b[...] = wd_ref[0].astype(bf16)

    @pl.when(active)
    def _():
        x_lo, x_hi = _unpack_rows(xs_ref[...])

        def up(w_ref):
            return (jnp.dot(x_lo, w_ref[:PACKED, :], preferred_element_type=f32)
                    + jnp.dot(x_hi, w_ref[PACKED:, :], preferred_element_type=f32))

        g = up(wg_b)
        hmid = (g * _sigmoid(g) * up(wu_b)).astype(bf16)
        ys_ref[...] = _pack_rows(jnp.dot(hmid, wd_b[...], preferred_element_type=f32))


def _experts(bexp, nact, xs, wg, wu, wd, nb):
    n_rows = nb * EXPERT_BLOCK

    def blk(b, nact_ref):
        return jnp.minimum(b, nact_ref[0] - 1)

    grid_spec = pltpu.PrefetchScalarGridSpec(
        num_scalar_prefetch=2,
        grid=(nb,),
        in_specs=[
            pl.BlockSpec((EXPERT_BLOCK, PACKED), lambda b, e, n: (blk(b, n), 0)),
            pl.BlockSpec((1, D_MODEL, D_FF_EXPERT), lambda b, e, n: (e[blk(b, n)], 0, 0)),
            pl.BlockSpec((1, D_MODEL, D_FF_EXPERT), lambda b, e, n: (e[blk(b, n)], 0, 0)),
            pl.BlockSpec((1, D_FF_EXPERT, D_MODEL), lambda b, e, n: (e[blk(b, n)], 0, 0)),
        ],
        out_specs=pl.BlockSpec((EXPERT_BLOCK, PACKED), lambda b, e, n: (b, 0)),
        scratch_shapes=[pltpu.VMEM((D_MODEL, D_FF_EXPERT), bf16), pltpu.VMEM((D_MODEL, D_FF_EXPERT), bf16),
                        pltpu.VMEM((D_FF_EXPERT, D_MODEL), bf16)],
    )
    return pl.pallas_call(
        _expert_kernel,
        grid_spec=grid_spec,
        out_shape=jax.ShapeDtypeStruct((n_rows, PACKED), jnp.uint32),
        compiler_params=pltpu.CompilerParams(
            dimension_semantics=("arbitrary",), vmem_limit_bytes=VMEM_LIMIT),
        name="experts",
    )(bexp, nact, xs, wg, wu, wd)


def _combine_kernel(tab_ref, tab_next_ref, lpos_ref, wts_ref, h_ref, gf_ref, ys_ref, out_ref, ybuf, sem):
    ts, lrows = h_ref.shape[0], ybuf.shape[1]
    i = pl.program_id(0)
    slot = lax.rem(i, 2)

    def fetch(t_ref, s):
        for k in range(lrows // RUN_ALIGN):
            pltpu.make_async_copy(ys_ref.at[pl.ds(_chunk_row(t_ref, k, True), RUN_ALIGN)],
                                  ybuf.at[s, pl.ds(k * RUN_ALIGN, RUN_ALIGN)], sem.at[s]).start()

    @pl.when(i == 0)
    def _():
        fetch(tab_ref, 0)

    @pl.when(i + 1 < pl.num_programs(0))
    def _():
        fetch(tab_next_ref, 1 - slot)

    pltpu.make_async_copy(ys_ref.at[pl.ds(0, lrows)], ybuf.at[slot], sem.at[slot]).wait()

    info = jnp.concatenate([lpos_ref[...].astype(f32), wts_ref[...],
                            jnp.zeros((LANES - 2 * SUBLANES, ts), f32)], axis=0).T
    jl = lax.broadcasted_iota(jnp.int32, (ts, lrows), 1).astype(f32)
    mix = (jnp.where(jl == info[:, 0:1], info[:, SUBLANES:SUBLANES + 1], 0.0)
           + jnp.where(jl == info[:, 1:2], info[:, SUBLANES + 1:SUBLANES + 2], 0.0)).astype(bf16)
    moe = jnp.concatenate([jnp.dot(mix, y, preferred_element_type=f32) for y in _unpack_rows(ybuf[slot])],
                          axis=1)
    out_ref[...] = _rms(h_ref[...] + moe, gf_ref[...])


def _combine(tab, lpos, wts, h, g_final, ys):
    t = h.shape[0]
    ts = ROUTE_TILE
    nt = t // ts
    lrows = _local_rows(ts)
    assert ys.shape[0] >= lrows
    return pl.pallas_call(
        _combine_kernel,
        grid=(nt,),
        in_specs=[
            pl.BlockSpec((1, SUBLANES, 2 * TABLE_LANES), lambda i: (i, 0, 0), memory_space=pltpu.SMEM),
            pl.BlockSpec((1, SUBLANES, 2 * TABLE_LANES), lambda i: (jnp.minimum(i + 1, nt - 1), 0, 0),
                         memory_space=pltpu.SMEM),
            pl.BlockSpec((SUBLANES, ts), lambda i: (0, i)),
            pl.BlockSpec((SUBLANES, ts), lambda i: (0, i)),
            pl.BlockSpec((ts, D_MODEL), lambda i: (i, 0)),
            pl.BlockSpec((1, D_MODEL), lambda i: (0, 0)),
            pl.BlockSpec(memory_space=pl.ANY),
        ],
        out_specs=pl.BlockSpec((ts, D_MODEL), lambda i: (i, 0)),
        out_shape=jax.ShapeDtypeStruct((t, D_MODEL), f32),
        scratch_shapes=[pltpu.VMEM((2, lrows, PACKED), jnp.uint32), pltpu.SemaphoreType.DMA((2,))],
        compiler_params=pltpu.CompilerParams(
            dimension_semantics=("arbitrary",), vmem_limit_bytes=VMEM_LIMIT),
        name="combine",
    )(tab, tab, lpos, wts, h, g_final, ys)


def _prep_router(w_group, b_group, w_expert, b_expert):
    d = w_group.shape[0]
    w = jnp.zeros((d, LANES), f32)
    w = w.at[:, :N_GROUPS].set(w_group).at[:, SUBLANES:ROUTER_ROWS].set(w_expert)
    b = jnp.full((1, LANES), NEG_BIG, f32)
    b = b.at[0, :N_GROUPS].set(b_group).at[0, SUBLANES:ROUTER_ROWS].set(b_expert)
    return w.astype(bf16), b


def kernel(x, g_mix, w_in, w_dw, b_dw, ln_conv_g, ln_conv_b, sinks, w_conv_out, w_attn_out, w_out, g_ffn,
           w_group, b_group, w_expert, b_expert, w_gate, w_up, w_down, g_final):
    batch, seq, d = x.shape
    assert d == D_MODEL and seq % MIXER_TILE == 0 and seq % TOKEN_TILE == 0 and g_mix.shape[0] == 1
    t = batch * seq
    x2 = x.reshape(t, d)

    act, q, kv, sgates = _inproj(x2, g_mix[0][None, :], w_in[0], w_dw[0, :, 0, :], b_dw[0][None, :],
                                 ln_conv_g[0][None, :], ln_conv_b[0][None, :], seq)

    w_r, b_r = _prep_router(w_group[0], b_group[0], w_expert[0], b_expert[0])
    h, xn2, route, wts = _mixer(
        x2, act, q, kv, sgates, sinks[0], w_conv_out[0].astype(bf16), w_attn_out[0].astype(bf16),
        w_out[0].astype(bf16), g_ffn[0][None, :], w_r, b_r, batch, seq)

    lpos, tab, seg, meta = _route(route)
    nb = _num_blocks(t)
    xs = _scatter(tab, seg, lpos, xn2, nb)
    ys = _experts(meta[0, :nb], meta[1, :1], xs, w_gate[0], w_up[0], w_down[0], nb)
    out = _combine(tab, lpos, wts, h, g_final[None, :], ys)
    return out.reshape(batch, seq, d)
```

```python
import functools

import numpy as np
import jax
import jax.numpy as jnp
from jax import lax
from jax.experimental import pallas as pl
from jax.experimental.pallas import tpu as pltpu

D_MODEL = 1024
CONV_CH = 512
CONV_WIDTH = 31
N_HEADS = 8
N_KV_HEADS = 2
HEAD_DIM = 64
ATTN_BLOCK = 128
N_GROUPS = 4
EXPERTS_PER_GROUP = 8
N_EXPERTS = N_GROUPS * EXPERTS_PER_GROUP
D_FF_EXPERT = 512
NORM_EPS = 1e-6

Q_DIM = N_HEADS * HEAD_DIM
KV_DIM = N_KV_HEADS * HEAD_DIM

LANES = 128
SUBLANES = 8
CONV_HALO = 32
CONV_ROWS = 32
PROJ_PIECE = 256
ROUTER_ROWS = SUBLANES + N_EXPERTS
NEG_BIG = -1e30

TOKEN_TILE = 512
MIXER_TILE = 1024
ROUTE_TILE = 512
RUN_ALIGN = SUBLANES
EXPERT_BLOCK = 1024
VMEM_LIMIT = 58 * 1024 * 1024

f32 = jnp.float32
bf16 = jnp.bfloat16


def _rms(x, g):
    ms = jnp.mean(x * x, axis=-1, keepdims=True)
    return x * lax.rsqrt(ms + NORM_EPS) * g


def _sigmoid(x):
    return 1.0 / (1.0 + jnp.exp(-x))


def _exact_zero(x):
    bits = pltpu.bitcast(x, jnp.uint32)
    sixteen = jnp.uint32(16)
    return pltpu.bitcast(lax.shift_right_logical(lax.shift_right_logical(bits, sixteen), sixteen), f32)


PACKED = D_MODEL // 2
_HIGH_HALF = 0xFFFF0000


def _pack_rows(x, already_bf16=False):
    def bits(v):
        return pltpu.bitcast(v if already_bf16 else v.astype(bf16).astype(f32), jnp.uint32)
    low = lax.shift_right_logical(bits(x[:, :PACKED]), jnp.uint32(16))
    return low | (bits(x[:, PACKED:]) & jnp.uint32(_HIGH_HALF))


def _unpack_rows(p):
    low = pltpu.bitcast(lax.shift_left(p, jnp.uint32(16)), f32).astype(bf16)
    high = pltpu.bitcast(p & jnp.uint32(_HIGH_HALF), f32).astype(bf16)
    return low, high


def _load_w_in(w_hbm, stage, w_ref, sem):
    copy = pltpu.make_async_copy(w_hbm, stage, sem)
    copy.start()
    copy.wait()

    def put(dst, value):
        w_ref[:, dst:dst + value.shape[1]] = value.astype(bf16)

    for b in range(CONV_CH // LANES):
        put(2 * b * LANES, stage[:, b * LANES:(b + 1) * LANES])
        put((2 * b + 1) * LANES, stage[:, CONV_CH + b * LANES:CONV_CH + (b + 1) * LANES])
    src = dst = 2 * CONV_CH
    for lo in range(0, Q_DIM, LANES):
        put(dst + lo, stage[:, src + lo:src + lo + LANES] * (HEAD_DIM ** -0.5))
    src, dst = src + Q_DIM, dst + Q_DIM
    for _ in range(2):
        for h in range(N_KV_HEADS):
            head = stage[:, src + h * HEAD_DIM:src + (h + 1) * HEAD_DIM]
            put(dst + 2 * h * HEAD_DIM, jnp.concatenate([head, head], axis=1))
        src, dst = src + KV_DIM, dst + 2 * KV_DIM
    for lo in range(0, 2 * D_MODEL, LANES):
        put(dst + lo, stage[:, src + lo:src + lo + LANES])


def _inproj_kernel(tiles_per_seq, x_ref, g_ref, w_hbm, wdw_ref, bdw_ref, lng_ref, lnb_ref,
                   act_ref, q_ref, kv_ref, sgate_ref, xn_s, vbuf, w_stage, w_ref, w_sem):
    tm = x_ref.shape[0]

    @pl.when(pl.program_id(0) == 0)
    def _():
        _load_w_in(w_hbm, w_stage, w_ref, w_sem)

    xn_s[...] = _rms(x_ref[...], g_ref[...]).astype(bf16)

    def proj(lo, hi):
        return jnp.dot(xn_s[...], w_ref[:, lo:hi], preferred_element_type=f32)

    n_cb = CONV_CH // LANES
    first = lax.rem(pl.program_id(0), tiles_per_seq) == 0

    @pl.when(first)
    def _():
        vbuf[:, 0:CONV_HALO, :] = jnp.zeros((n_cb, CONV_HALO, LANES), f32)

    @pl.when(jnp.logical_not(first))
    def _():
        vbuf[:, 0:CONV_HALO, :] = vbuf[:, tm:tm + CONV_HALO, :]

    for b in range(n_cb):
        u = proj(2 * b * LANES, 2 * (b + 1) * LANES)
        vbuf[b, CONV_HALO:, :] = u[:, :LANES] * _sigmoid(u[:, LANES:])

    pieces, col = [], 2 * CONV_CH
    for out_ref, fn in ((q_ref, lambda z: z), (kv_ref, lambda z: z), (sgate_ref, _sigmoid)):
        pieces += [(out_ref, col, lo, fn) for lo in range(0, out_ref.shape[1], PROJ_PIECE)]
        col += out_ref.shape[1]

    rows = CONV_ROWS
    n_steps = tm // rows
    lag = n_steps - len(pieces)
    assert lag >= 0
    tap0 = CONV_HALO - (CONV_WIDTH - 1)

    def tie_rows(x):
        return jnp.concatenate([_exact_zero(x[0:SUBLANES, 0:LANES])] * (rows // SUBLANES), axis=0)

    tie = None
    for c in range(n_steps):
        if c >= lag:
            out_ref, col, lo, fn = pieces[c - lag]
            z = proj(col + lo, col + lo + PROJ_PIECE)
            out_ref[:, lo:lo + PROJ_PIECE] = fn(z).astype(bf16)
            tie = tie_rows(z) if tie is None else tie + tie_rows(z)
        accs = []
        for b in range(n_cb):
            acc = jnp.broadcast_to(bdw_ref[:, b * LANES:(b + 1) * LANES], (rows, LANES))
            if tie is not None:
                acc = acc + tie
            for j in range(CONV_WIDTH):
                r0 = c * rows + tap0 + j
                acc = acc + wdw_ref[j:j + 1, b * LANES:(b + 1) * LANES] * vbuf[b, r0:r0 + rows, :]
            accs.append(acc)
            tie = tie_rows(sum(acc[k:k + SUBLANES] for k in range(0, rows, SUBLANES)))
        mu = sum(jnp.sum(a, axis=-1, keepdims=True) for a in accs) * (1.0 / CONV_CH)
        ds = [a - mu for a in accs]
        var = sum(jnp.sum(d * d, axis=-1, keepdims=True) for d in ds) * (1.0 / CONV_CH)
        inv = lax.rsqrt(var + NORM_EPS)
        for b in range(n_cb):
            y = ds[b] * inv * lng_ref[:, b * LANES:(b + 1) * LANES] + lnb_ref[:, b * LANES:(b + 1) * LANES]
            act_ref[c * rows:(c + 1) * rows, b * LANES:(b + 1) * LANES] = (y * _sigmoid(y)).astype(bf16)


def _inproj(x2, g_mix, w_in, w_dw, b_dw, ln_g, ln_b, seq):
    t = x2.shape[0]
    tm = TOKEN_TILE
    widths = (CONV_CH, Q_DIM, 4 * KV_DIM, 2 * D_MODEL)

    def full(a):
        return pl.BlockSpec(a.shape, lambda i: (0,) * a.ndim)

    return pl.pallas_call(
        functools.partial(_inproj_kernel, seq // tm),
        grid=(t // tm,),
        in_specs=[pl.BlockSpec((tm, D_MODEL), lambda i: (i, 0)),
                  full(g_mix), pl.BlockSpec(memory_space=pl.ANY), full(w_dw), full(b_dw), full(ln_g), full(ln_b)],
        out_specs=[pl.BlockSpec((tm, w), lambda i: (i, 0)) for w in widths],
        out_shape=[jax.ShapeDtypeStruct((t, w), bf16) for w in widths],
        scratch_shapes=[pltpu.VMEM((tm, D_MODEL), bf16),
                        pltpu.VMEM((CONV_CH // LANES, tm + CONV_HALO, LANES), f32),
                        pltpu.VMEM(w_in.shape, f32),
                        pltpu.VMEM((D_MODEL, sum(widths) + CONV_CH), bf16),
                        pltpu.SemaphoreType.DMA],
        compiler_params=pltpu.CompilerParams(
            dimension_semantics=("arbitrary",), vmem_limit_bytes=VMEM_LIMIT),
        name="inproj",
    )(x2, g_mix, w_in, w_dw, b_dw, ln_g, ln_b)


def _mixer_kernel(x_ref, act_ref, q_ref, kv_ref, kvp_ref, sgate_ref, sink_ref, bias_ref, wco_ref, wao_ref,
                  wo_ref, gffn_ref, wr_ref, br_ref,
                  h_ref, xn2_ref, route_ref, wts_ref,
                  kvall, attn, s_scr, m_scr):
    ts = x_ref.shape[0]
    first = pl.program_id(1) == 0

    kvall[0:ATTN_BLOCK, :] = jnp.where(first, jnp.zeros_like(kvp_ref[...]), kvp_ref[...])
    kvall[ATTN_BLOCK:, :] = kv_ref[...]
    nkeys = 2 * ATTN_BLOCK
    left_kv = lax.broadcasted_iota(jnp.int32, (nkeys, LANES), 1) < HEAD_DIM
    left_q = lax.broadcasted_iota(jnp.int32, (ATTN_BLOCK, LANES), 1) < HEAD_DIM
    key_lane = lax.broadcasted_iota(jnp.int32, (ATTN_BLOCK, 2 * nkeys), 1)
    prev_keys = (key_lane % nkeys) < ATTN_BLOCK
    no_prev = jnp.where(prev_keys, jnp.where(first, -jnp.inf, 0.0), 0.0)

    def block_diag(x):
        z = jnp.zeros_like(x)
        return jnp.concatenate([jnp.where(left_kv, x, z), jnp.where(left_kv, z, x)], axis=0)

    n_pairs = N_HEADS // 2
    for j in range(ts // ATTN_BLOCK):
        r0 = j * ATTN_BLOCK
        for kvh in range(N_KV_HEADS):
            kbd = block_diag(kvall[r0:r0 + nkeys, kvh * LANES:(kvh + 1) * LANES])
            for ii in range(2):
                i = 2 * kvh + ii
                qb = q_ref[r0:r0 + ATTN_BLOCK, i * LANES:(i + 1) * LANES]
                s = lax.dot_general(qb, kbd, (((1,), (1,)), ((), ())), preferred_element_type=f32)
                s = s + bias_ref[i]
                if j == 0:
                    s = s + no_prev
                s_scr[j * n_pairs + i] = s
                for half in range(2):
                    m = jnp.max(s[:, half * nkeys:(half + 1) * nkeys], axis=-1, keepdims=True)
                    m_scr[2 * (j * n_pairs + i) + half] = jnp.maximum(m, sink_ref[2 * i + half])

    for j in range(ts // ATTN_BLOCK):
        r0 = j * ATTN_BLOCK
        for kvh in range(N_KV_HEADS):
            vbd = block_diag(kvall[r0:r0 + nkeys, (N_KV_HEADS + kvh) * LANES:(N_KV_HEADS + kvh + 1) * LANES])
            for ii in range(2):
                i = 2 * kvh + ii
                s = s_scr[j * n_pairs + i]
                ps, ls = [], []
                for half in range(2):
                    m = m_scr[2 * (j * n_pairs + i) + half]
                    p = jnp.exp(s[:, half * nkeys:(half + 1) * nkeys] - m)
                    ps.append(p)
                    ls.append(jnp.sum(p, axis=-1, keepdims=True) + jnp.exp(sink_ref[2 * i + half] - m))
                p = jnp.concatenate(ps, axis=1).astype(bf16)
                o = jnp.dot(p, vbd, preferred_element_type=f32)
                o = o * jnp.where(left_q, 1.0 / ls[0], 1.0 / ls[1])
                attn[r0:r0 + ATTN_BLOCK, i * LANES:(i + 1) * LANES] = o.astype(bf16)

    conv_o = jnp.dot(act_ref[...], wco_ref[...], preferred_element_type=f32)
    attn_o = jnp.dot(attn[...], wao_ref[...], preferred_element_type=f32)
    merged = (sgate_ref[:, :D_MODEL].astype(f32) * conv_o
              + sgate_ref[:, D_MODEL:].astype(f32) * attn_o).astype(bf16)
    h = x_ref[...] + jnp.dot(merged, wo_ref[...], preferred_element_type=f32)
    h_ref[...] = h

    xn2 = _rms(h, gffn_ref[...]).astype(bf16)
    xn2_ref[...] = xn2
    logits = jnp.dot(xn2, wr_ref[...], preferred_element_type=f32) + br_ref[...]
    lt = logits.T
    sub = lax.broadcasted_iota(jnp.int32, (SUBLANES, ts), 0)

    gl = lt[0:SUBLANES]
    gmax = jnp.max(gl, axis=0, keepdims=True)
    gsel = jnp.min(jnp.where(gl == gmax, sub, SUBLANES), axis=0, keepdims=True)
    p_group = 1.0 / jnp.sum(jnp.exp(gl - gmax), axis=0, keepdims=True)

    e_in = lt[SUBLANES:2 * SUBLANES]
    for g in range(1, N_GROUPS):
        e_in = jnp.where(gsel == g, lt[(g + 1) * SUBLANES:(g + 2) * SUBLANES], e_in)
    m1 = jnp.max(e_in, axis=0, keepdims=True)
    i1 = jnp.min(jnp.where(e_in == m1, sub, SUBLANES), axis=0, keepdims=True)
    rest = jnp.where(sub == i1, -jnp.inf, e_in)
    m2 = jnp.max(rest, axis=0, keepdims=True)
    i2 = jnp.min(jnp.where(rest == m2, sub, SUBLANES), axis=0, keepdims=True)
    t2 = jnp.exp(m2 - m1)
    w1 = p_group / (1.0 + t2)
    w2 = p_group * t2 / (1.0 + t2)
    base = gsel * EXPERTS_PER_GROUP
    route_ref[...] = jnp.where(sub == 0, base + i1, jnp.where(sub == 1, base + i2, 0))
    wts_ref[...] = jnp.where(sub == 0, w1, jnp.where(sub == 1, w2, 0.0))


def _attn_bias():
    qi = np.arange(ATTN_BLOCK)[:, None]
    kj = np.arange(2 * ATTN_BLOCK)[None, :]
    rel = (ATTN_BLOCK + qi - kj).astype(np.float32)
    ok = (rel >= 0) & (rel < ATTN_BLOCK)
    slopes = np.array([2.0 ** (-8.0 * (h + 1) / N_HEADS) for h in range(N_HEADS)], np.float32)
    per_head = [np.where(ok, -(slopes[h] * rel), -np.inf).astype(np.float32) for h in range(N_HEADS)]
    return np.stack([np.concatenate([per_head[2 * i], per_head[2 * i + 1]], axis=1)
                     for i in range(N_HEADS // 2)])


def _mixer(x2, act, q, kv, sgates, sinks, wco, wao, wo, g_ffn, w_r, b_r, batch, seq):
    t = x2.shape[0]
    ts = MIXER_TILE
    ns = seq // ts
    bias = jnp.asarray(_attn_bias())

    def row(b, s):
        return b * ns + s

    def full(a):
        return pl.BlockSpec(a.shape, lambda b, s: (0,) * a.ndim, pipeline_mode=pl.Buffered(1))

    in_specs = [
        pl.BlockSpec((ts, D_MODEL), lambda b, s: (row(b, s), 0)),
        pl.BlockSpec((ts, CONV_CH), lambda b, s: (row(b, s), 0)),
        pl.BlockSpec((ts, Q_DIM), lambda b, s: (row(b, s), 0)),
        pl.BlockSpec((ts, 4 * KV_DIM), lambda b, s: (row(b, s), 0)),
        pl.BlockSpec((ATTN_BLOCK, 4 * KV_DIM),
                     lambda b, s: (jnp.maximum(row(b, s) * (ts // ATTN_BLOCK) - 1, 0), 0)),
        pl.BlockSpec((ts, 2 * D_MODEL), lambda b, s: (row(b, s), 0)),
        pl.BlockSpec(memory_space=pltpu.SMEM),
        full(bias), full(wco), full(wao), full(wo), full(g_ffn), full(w_r), full(b_r),
    ]
    out_specs = [
        pl.BlockSpec((ts, D_MODEL), lambda b, s: (row(b, s), 0)),
        pl.BlockSpec((ts, D_MODEL), lambda b, s: (row(b, s), 0)),
        pl.BlockSpec((SUBLANES, ts), lambda b, s: (0, row(b, s))),
        pl.BlockSpec((SUBLANES, ts), lambda b, s: (0, row(b, s))),
    ]
    out_shape = [
        jax.ShapeDtypeStruct((t, D_MODEL), f32),
        jax.ShapeDtypeStruct((t, D_MODEL), bf16),
        jax.ShapeDtypeStruct((SUBLANES, t), jnp.int32),
        jax.ShapeDtypeStruct((SUBLANES, t), f32),
    ]
    return pl.pallas_call(
        _mixer_kernel,
        grid=(batch, ns),
        in_specs=in_specs,
        out_specs=out_specs,
        out_shape=out_shape,
        scratch_shapes=[
            pltpu.VMEM((ts + ATTN_BLOCK, 4 * KV_DIM), bf16),
            pltpu.VMEM((ts, Q_DIM), bf16),
            pltpu.VMEM((ts // ATTN_BLOCK * (N_HEADS // 2), ATTN_BLOCK, 4 * ATTN_BLOCK), f32),
            pltpu.VMEM((ts // ATTN_BLOCK * N_HEADS, ATTN_BLOCK, 1), f32),
        ],
        compiler_params=pltpu.CompilerParams(
            dimension_semantics=("arbitrary", "arbitrary"), vmem_limit_bytes=VMEM_LIMIT),
        name="mixer",
    )(x2, act, q, kv, kv, sgates, sinks, bias, wco, wao, wo, g_ffn, w_r, b_r)


def _local_rows(ts):
    return -(-(2 * ts + (RUN_ALIGN - 1) * N_EXPERTS) // LANES) * LANES


PAD_COARSE = 16
assert (EXPERT_BLOCK // RUN_ALIGN) % PAD_COARSE == 0
TABLE_LANES = 2 * LANES
SEG_NACT_LANE = 3 * N_EXPERTS
SEG_SPARE_LANE = 3 * N_EXPERTS + 1


def _num_blocks(t):
    run_rows = 2 * t + (RUN_ALIGN - 1) * N_EXPERTS * (t // ROUTE_TILE)
    return -(-(run_rows + N_EXPERTS * (EXPERT_BLOCK - RUN_ALIGN)) // EXPERT_BLOCK)


def _route_kernel(route_ref, lpos_ref, tab_ref, seg_ref, meta_ref):
    t = route_ref.shape[1]
    tr = ROUTE_TILE
    nbp = meta_ref.shape[1]
    chunks_per_block = EXPERT_BLOCK // RUN_ALIGN
    eiota = lax.broadcasted_iota(jnp.int32, (N_EXPERTS, tr), 0)
    before = (lax.broadcasted_iota(jnp.int32, (tr, tr), 0)
              < lax.broadcasted_iota(jnp.int32, (tr, tr), 1)).astype(bf16)
    lower = (lax.broadcasted_iota(jnp.int32, (N_EXPERTS, N_EXPERTS), 1)
             < lax.broadcasted_iota(jnp.int32, (N_EXPERTS, N_EXPERTS), 0)).astype(bf16)
    sub = lax.broadcasted_iota(jnp.int32, (N_EXPERTS, LANES), 0)
    lane = lax.broadcasted_iota(jnp.int32, (N_EXPERTS, LANES), 1)

    def to_lanes(col, offset):
        return jnp.sum(jnp.where(sub + offset == lane, col, 0.0), axis=0, keepdims=True)

    def expert_prefix(col):
        b = jnp.broadcast_to(col, (N_EXPERTS, LANES))
        hi = jnp.floor(b * (1.0 / 16.0))
        lo = b - 16.0 * hi
        return (16.0 * jnp.dot(lower, hi.astype(bf16), preferred_element_type=f32)
                + jnp.dot(lower, lo.astype(bf16), preferred_element_type=f32))[:, 0:1]

    lpos_ref[...] = jnp.zeros(lpos_ref.shape, jnp.int32)
    chunk_id = lax.broadcasted_iota(jnp.int32, (N_EXPERTS, TABLE_LANES), 1).astype(f32)
    chunk_expert = lax.broadcasted_iota(jnp.int32, (N_EXPERTS, TABLE_LANES), 0).astype(f32)

    def step(i, seen_chunks):
        off = pl.multiple_of(i * tr, tr)
        m1 = eiota == route_ref[0:1, pl.ds(off, tr)]
        m2 = eiota == route_ref[1:2, pl.ds(off, tr)]
        onehot = jnp.where(m1 | m2, 1.0, 0.0)
        within = jnp.dot(onehot.astype(bf16), before, preferred_element_type=f32)
        run_chunks = jnp.floor((jnp.sum(onehot, axis=1, keepdims=True) + (RUN_ALIGN - 1)) * (1.0 / RUN_ALIGN))
        run_start = expert_prefix(run_chunks)
        pos = within + RUN_ALIGN * run_start
        lpos_ref[0:1, pl.ds(off, tr)] = jnp.sum(jnp.where(m1, pos, 0.0), axis=0, keepdims=True).astype(jnp.int32)
        lpos_ref[1:2, pl.ds(off, tr)] = jnp.sum(jnp.where(m2, pos, 0.0), axis=0, keepdims=True).astype(jnp.int32)
        owner = (run_start <= chunk_id) & (chunk_id < run_start + run_chunks)
        rel = jnp.sum(jnp.where(owner, seen_chunks + chunk_id - run_start, 0.0), axis=0, keepdims=True)
        eid = jnp.sum(jnp.where(owner, chunk_expert, 0.0), axis=0, keepdims=True)
        n_used = jnp.sum(run_chunks, axis=0, keepdims=True)
        unused = chunk_id[0:1] >= n_used
        rel = jnp.where(unused, chunk_id[0:1] - n_used, rel)
        eid = jnp.where(unused, (SEG_SPARE_LANE + lax.rem(i, 2)).astype(f32), eid)
        row = jnp.concatenate([rel, eid], axis=1)
        tab_ref[i] = jnp.broadcast_to(row, (SUBLANES, 2 * TABLE_LANES)).astype(jnp.int32)
        return seen_chunks + run_chunks

    used_chunks = lax.fori_loop(0, t // tr, step, jnp.zeros((N_EXPERTS, 1), f32))

    nblk = jnp.floor((used_chunks + (chunks_per_block - 1)) * (1.0 / chunks_per_block))
    first_blk = expert_prefix(nblk)
    nact = jnp.sum(nblk, axis=0, keepdims=True)
    spare_chunk = _num_blocks(t) * chunks_per_block
    seg_row = (to_lanes(first_blk * chunks_per_block, 0) + to_lanes(used_chunks, N_EXPERTS)
               + to_lanes(nblk * chunks_per_block, 2 * N_EXPERTS)
               + jnp.where(lane[0:1] == SEG_NACT_LANE, nact, 0.0)
               + jnp.where(lane[0:1] == SEG_SPARE_LANE, float(spare_chunk), 0.0)
               + jnp.where(lane[0:1] == SEG_SPARE_LANE + 1, float(spare_chunk + _local_rows(tr) // RUN_ALIGN), 0.0))
    seg_ref[...] = jnp.broadcast_to(seg_row, (SUBLANES, LANES)).astype(jnp.int32)

    tab = tab_ref[...].astype(f32)
    rel, eid = tab[:, :, :TABLE_LANES], tab[:, :, TABLE_LANES:]
    start = jnp.zeros_like(rel)
    for seg_lane in list(range(N_EXPERTS)) + [SEG_SPARE_LANE, SEG_SPARE_LANE + 1]:
        start = start + jnp.where(eid == seg_lane, seg_row[:, seg_lane:seg_lane + 1], 0.0)
    write_row = (start + rel) * RUN_ALIGN
    read_row = jnp.where(eid >= SEG_SPARE_LANE, rel, start + rel) * RUN_ALIGN
    tab_ref[...] = jnp.concatenate([write_row, read_row], axis=2).astype(jnp.int32)

    blk = lax.broadcasted_iota(jnp.int32, (N_EXPERTS, nbp), 1).astype(f32)
    owner = (first_blk <= blk) & (blk < first_blk + nblk)
    expert_id = lax.broadcasted_iota(jnp.int32, (N_EXPERTS, nbp), 0).astype(f32)
    bexp = jnp.sum(jnp.where(owner, expert_id, 0.0), axis=0, keepdims=True)
    row8 = lax.broadcasted_iota(jnp.int32, (SUBLANES, nbp), 0)
    meta_ref[...] = jnp.where(row8 == 0, bexp, jnp.where(row8 == 1, nact, 0.0)).astype(jnp.int32)


def _route(route):
    t = route.shape[1]
    nt = t // ROUTE_TILE
    nbp = -(-_num_blocks(t) // LANES) * LANES
    return pl.pallas_call(
        _route_kernel,
        grid=(1,),
        in_specs=[pl.BlockSpec(route.shape, lambda i: (0, 0))],
        out_specs=[pl.BlockSpec(route.shape, lambda i: (0, 0)),
                   pl.BlockSpec((nt, SUBLANES, 2 * TABLE_LANES), lambda i: (0, 0, 0)),
                   pl.BlockSpec((SUBLANES, LANES), lambda i: (0, 0)),
                   pl.BlockSpec((SUBLANES, nbp), lambda i: (0, 0))],
        out_shape=[jax.ShapeDtypeStruct(route.shape, jnp.int32),
                   jax.ShapeDtypeStruct((nt, SUBLANES, 2 * TABLE_LANES), jnp.int32),
                   jax.ShapeDtypeStruct((SUBLANES, LANES), jnp.int32),
                   jax.ShapeDtypeStruct((SUBLANES, nbp), jnp.int32)],
        compiler_params=pltpu.CompilerParams(
            dimension_semantics=("arbitrary",), vmem_limit_bytes=VMEM_LIMIT),
        name="route",
    )(route)


def _chunk_row(tab_ref, k, reading):
    return pl.multiple_of(tab_ref[0, 0, (TABLE_LANES if reading else 0) + k], RUN_ALIGN)


def _scatter_kernel(tab_ref, seg_ref, lpos_ref, x_ref, xs_ref, xsl, zrows, sem, zsem):
    lrows, ts = xsl.shape[1], x_ref.shape[0]
    n_blocks = (xs_ref.shape[0] - 2 * lrows) // EXPERT_BLOCK
    i = pl.program_id(0)
    last = pl.num_programs(0) - 1
    slot = lax.rem(i, 2)

    def drain(s):
        pltpu.make_async_copy(xsl.at[s], xs_ref.at[pl.ds(0, lrows)], sem.at[s]).wait()

    @pl.when(i >= 2)
    def _():
        drain(slot)

    j = lax.broadcasted_iota(jnp.int32, (lrows, ts), 0)
    perm = jnp.where((j == lpos_ref[0:1, :]) | (j == lpos_ref[1:2, :]), 1.0, 0.0).astype(bf16)
    xsl[slot] = _pack_rows(jnp.dot(perm, x_ref[...], preferred_element_type=f32), already_bf16=True)

    for k in range(lrows // RUN_ALIGN):
        pltpu.make_async_copy(xsl.at[slot, pl.ds(k * RUN_ALIGN, RUN_ALIGN)],
                              xs_ref.at[pl.ds(_chunk_row(tab_ref, k, False), RUN_ALIGN)], sem.at[slot]).start()

    @pl.when(i == last)
    def _():
        drain(slot)

        @pl.when(i >= 1)
        def _():
            drain(1 - slot)

        zrows[...] = jnp.zeros(zrows.shape, zrows.dtype)

        def zcopy(row, n):
            return pltpu.make_async_copy(zrows.at[pl.ds(0, n)], xs_ref.at[pl.ds(row, n)], zsem)

        def for_each_unused_block(fn):
            def body(b, carry):
                fn(pl.multiple_of(b * EXPERT_BLOCK, EXPERT_BLOCK), EXPERT_BLOCK)
                return carry

            lax.fori_loop(seg_ref[0, SEG_NACT_LANE], n_blocks, body, 0)
            for off in range(0, 2 * lrows, EXPERT_BLOCK):
                fn(n_blocks * EXPERT_BLOCK + off, min(EXPERT_BLOCK, 2 * lrows - off))

        for_each_unused_block(lambda row, n: zcopy(row, n).start())
        for_each_unused_block(lambda row, n: zcopy(row, n).wait())

        def for_each_pad_chunk(fn):
            def seg(e, carry):
                g0 = seg_ref[0, e]
                used = seg_ref[0, N_EXPERTS + e]
                total = seg_ref[0, 2 * N_EXPERTS + e]

                def chunk(c, carry2):
                    fn(pl.multiple_of((g0 + c) * RUN_ALIGN, RUN_ALIGN), RUN_ALIGN)
                    return carry2

                def coarse(c, carry2):
                    fn(pl.multiple_of((g0 + c * PAD_COARSE) * RUN_ALIGN, PAD_COARSE * RUN_ALIGN),
                       PAD_COARSE * RUN_ALIGN)
                    return carry2

                aligned = lax.div(used + (PAD_COARSE - 1), PAD_COARSE)
                lax.fori_loop(used, aligned * PAD_COARSE, chunk, 0)
                lax.fori_loop(aligned, lax.div(total, PAD_COARSE), coarse, 0)
                return carry

            lax.fori_loop(0, N_EXPERTS, seg, 0)

        for_each_pad_chunk(lambda row, n: zcopy(row, n).start())
        for_each_pad_chunk(lambda row, n: zcopy(row, n).wait())


def _scatter(tab, seg, lpos, xn2, n_blocks):
    t = xn2.shape[0]
    ts = ROUTE_TILE
    lrows = _local_rows(ts)
    return pl.pallas_call(
        _scatter_kernel,
        grid=(t // ts,),
        in_specs=[
            pl.BlockSpec((1, SUBLANES, 2 * TABLE_LANES), lambda i: (i, 0, 0), memory_space=pltpu.SMEM),
            pl.BlockSpec((SUBLANES, LANES), lambda i: (0, 0), memory_space=pltpu.SMEM),
            pl.BlockSpec((SUBLANES, ts), lambda i: (0, i)),
            pl.BlockSpec((ts, D_MODEL), lambda i: (i, 0)),
        ],
        out_specs=pl.BlockSpec(memory_space=pl.ANY),
        out_shape=jax.ShapeDtypeStruct((n_blocks * EXPERT_BLOCK + 2 * lrows, PACKED), jnp.uint32),
        scratch_shapes=[pltpu.VMEM((2, lrows, PACKED), jnp.uint32), pltpu.VMEM((EXPERT_BLOCK, PACKED), jnp.uint32),
                        pltpu.SemaphoreType.DMA((2,)), pltpu.SemaphoreType.DMA],
        compiler_params=pltpu.CompilerParams(
            dimension_semantics=("arbitrary",), vmem_limit_bytes=VMEM_LIMIT, has_side_effects=True),
        name="scatter",
    )(tab, seg, lpos, xn2)


def _expert_kernel(bexp_ref, nact_ref, xs_ref, wg_ref, wu_ref, wd_ref, ys_ref, wg_b, wu_b, wd_b):
    b = pl.program_id(0)
    active = b < nact_ref[0]
    new_expert = jnp.logical_or(b == 0, bexp_ref[b] != bexp_ref[jnp.maximum(b - 1, 0)])

    @pl.when(jnp.logical_not(active))
    def _():
        ys_ref[...] = jnp.zeros(ys_ref.shape, ys_ref.dtype)

    @pl.when(jnp.logical_and(active, new_expert))
    def _():
        wg_b[...] = wg_ref[0].astype(bf16)
        wu_b[...] = wu_ref[0].astype(bf16)
        wd_b[...] = wd_ref[0].astype(bf16)

    @pl.when(active)
    def _():
        x_lo, x_hi = _unpack_rows(xs_ref[...])

        def up(w_ref):
            return (jnp.dot(x_lo, w_ref[:PACKED, :], preferred_element_type=f32)
                    + jnp.dot(x_hi, w_ref[PACKED:, :], preferred_element_type=f32))

        g = up(wg_b)
        hmid = (g * _sigmoid(g) * up(wu_b)).astype(bf16)
        ys_ref[...] = _pack_rows(jnp.dot(hmid, wd_b[...], preferred_element_type=f32))


def _experts(bexp, nact, xs, wg, wu, wd, nb):
    n_rows = nb * EXPERT_BLOCK

    def blk(b, nact_ref):
        return jnp.minimum(b, nact_ref[0] - 1)

    grid_spec = pltpu.PrefetchScalarGridSpec(
        num_scalar_prefetch=2,
        grid=(nb,),
        in_specs=[
            pl.BlockSpec((EXPERT_BLOCK, PACKED), lambda b, e, n: (blk(b, n), 0)),
            pl.BlockSpec((1, D_MODEL, D_FF_EXPERT), lambda b, e, n: (e[blk(b, n)], 0, 0)),
            pl.BlockSpec((1, D_MODEL, D_FF_EXPERT), lambda b, e, n: (e[blk(b, n)], 0, 0)),
            pl.BlockSpec((1, D_FF_EXPERT, D_MODEL), lambda b, e, n: (e[blk(b, n)], 0, 0)),
        ],
        out_specs=pl.BlockSpec((EXPERT_BLOCK, PACKED), lambda b, e, n: (b, 0)),
        scratch_shapes=[pltpu.VMEM((D_MODEL, D_FF_EXPERT), bf16), pltpu.VMEM((D_MODEL, D_FF_EXPERT), bf16),
                        pltpu.VMEM((D_FF_EXPERT, D_MODEL), bf16)],
    )
    return pl.pallas_call(
        _expert_kernel,
        grid_spec=grid_spec,
        out_shape=jax.ShapeDtypeStruct((n_rows, PACKED), jnp.uint32),
        compiler_params=pltpu.CompilerParams(
            dimension_semantics=("arbitrary",), vmem_limit_bytes=VMEM_LIMIT),
        name="experts",
    )(bexp, nact, xs, wg, wu, wd)


def _combine_kernel(tab_ref, tab_next_ref, lpos_ref, wts_ref, h_ref, gf_ref, ys_ref, out_ref, ybuf, sem):
    ts, lrows = h_ref.shape[0], ybuf.shape[1]
    i = pl.program_id(0)
    slot = lax.rem(i, 2)

    def fetch(t_ref, s):
        for k in range(lrows // RUN_ALIGN):
            pltpu.make_async_copy(ys_ref.at[pl.ds(_chunk_row(t_ref, k, True), RUN_ALIGN)],
                                  ybuf.at[s, pl.ds(k * RUN_ALIGN, RUN_ALIGN)], sem.at[s]).start()

    @pl.when(i == 0)
    def _():
        fetch(tab_ref, 0)

    @pl.when(i + 1 < pl.num_programs(0))
    def _():
        fetch(tab_next_ref, 1 - slot)

    pltpu.make_async_copy(ys_ref.at[pl.ds(0, lrows)], ybuf.at[slot], sem.at[slot]).wait()

    info = jnp.concatenate([lpos_ref[...].astype(f32), wts_ref[...],
                            jnp.zeros((LANES - 2 * SUBLANES, ts), f32)], axis=0).T
    jl = lax.broadcasted_iota(jnp.int16, (ts, lrows), 1)
    zero = jnp.zeros((), bf16)
    mix = (jnp.where(jl == info[:, 0:1].astype(jnp.int16), info[:, SUBLANES:SUBLANES + 1].astype(bf16), zero)
           + jnp.where(jl == info[:, 1:2].astype(jnp.int16), info[:, SUBLANES + 1:SUBLANES + 2].astype(bf16), zero))
    moe = jnp.concatenate([jnp.dot(mix, y, preferred_element_type=f32) for y in _unpack_rows(ybuf[slot])],
                          axis=1)
    out_ref[...] = _rms(h_ref[...] + moe, gf_ref[...])


def _combine(tab, lpos, wts, h, g_final, ys):
    t = h.shape[0]
    ts = ROUTE_TILE
    nt = t // ts
    lrows = _local_rows(ts)
    assert ys.shape[0] >= lrows
    return pl.pallas_call(
        _combine_kernel,
        grid=(nt,),
        in_specs=[
            pl.BlockSpec((1, SUBLANES, 2 * TABLE_LANES), lambda i: (i, 0, 0), memory_space=pltpu.SMEM),
            pl.BlockSpec((1, SUBLANES, 2 * TABLE_LANES), lambda i: (jnp.minimum(i + 1, nt - 1), 0, 0),
                         memory_space=pltpu.SMEM),
            pl.BlockSpec((SUBLANES, ts), lambda i: (0, i)),
            pl.BlockSpec((SUBLANES, ts), lambda i: (0, i)),
            pl.BlockSpec((ts, D_MODEL), lambda i: (i, 0)),
            pl.BlockSpec((1, D_MODEL), lambda i: (0, 0)),
            pl.BlockSpec(memory_space=pl.ANY),
        ],
        out_specs=pl.BlockSpec((ts, D_MODEL), lambda i: (i, 0)),
        out_shape=jax.ShapeDtypeStruct((t, D_MODEL), f32),
        scratch_shapes=[pltpu.VMEM((2, lrows, PACKED), jnp.uint32), pltpu.SemaphoreType.DMA((2,))],
        compiler_params=pltpu.CompilerParams(
            dimension_semantics=("arbitrary",), vmem_limit_bytes=VMEM_LIMIT),
        name="combine",
    )(tab, tab, lpos, wts, h, g_final, ys)


def _prep_router(w_group, b_group, w_expert, b_expert):
    d = w_group.shape[0]
    w = jnp.zeros((d, LANES), f32)
    w = w.at[:, :N_GROUPS].set(w_group).at[:, SUBLANES:ROUTER_ROWS].set(w_expert)
    b = jnp.full((1, LANES), NEG_BIG, f32)
    b = b.at[0, :N_GROUPS].set(b_group).at[0, SUBLANES:ROUTER_ROWS].set(b_expert)
    return w.astype(bf16), b


def kernel(x, g_mix, w_in, w_dw, b_dw, ln_conv_g, ln_conv_b, sinks, w_conv_out, w_attn_out, w_out, g_ffn,
           w_group, b_group, w_expert, b_expert, w_gate, w_up, w_down, g_final):
    batch, seq, d = x.shape
    assert d == D_MODEL and seq % MIXER_TILE == 0 and seq % TOKEN_TILE == 0 and g_mix.shape[0] == 1
    t = batch * seq
    x2 = x.reshape(t, d)

    act, q, kv, sgates = _inproj(x2, g_mix[0][None, :], w_in[0], w_dw[0, :, 0, :], b_dw[0][None, :],
                                 ln_conv_g[0][None, :], ln_conv_b[0][None, :], seq)

    w_r, b_r = _prep_router(w_group[0], b_group[0], w_expert[0], b_expert[0])
    h, xn2, route, wts = _mixer(
        x2, act, q, kv, sgates, sinks[0], w_conv_out[0].astype(bf16), w_attn_out[0].astype(bf16),
        w_out[0].astype(bf16), g_ffn[0][None, :], w_r, b_r, batch, seq)

    lpos, tab, seg, meta = _route(route)
    nb = _num_blocks(t)
    xs = _scatter(tab, seg, lpos, xn2, nb)
    ys = _experts(meta[0, :nb], meta[1, :1], xs, w_gate[0], w_up[0], w_down[0], nb)
    out = _combine(tab, lpos, wts, h, g_final[None, :], ys)
    return out.reshape(batch, seq, d)
```

```python
import functools

import numpy as np
import jax
import jax.numpy as jnp
from jax import lax
from jax.experimental import pallas as pl
from jax.experimental.pallas import tpu as pltpu

D_MODEL = 1024
CONV_CH = 512
CONV_WIDTH = 31
N_HEADS = 8
N_KV_HEADS = 2
HEAD_DIM = 64
ATTN_BLOCK = 128
N_GROUPS = 4
EXPERTS_PER_GROUP = 8
N_EXPERTS = N_GROUPS * EXPERTS_PER_GROUP
D_FF_EXPERT = 512
NORM_EPS = 1e-6

Q_DIM = N_HEADS * HEAD_DIM
KV_DIM = N_KV_HEADS * HEAD_DIM

LANES = 128
SUBLANES = 8
CONV_HALO = 32
CONV_ROWS = 32
PROJ_PIECE = 256
ROUTER_ROWS = SUBLANES + N_EXPERTS
NEG_BIG = -1e30

TOKEN_TILE = 512
MIXER_TILE = 1024
ROUTE_TILE = 512
RUN_ALIGN = SUBLANES
EXPERT_BLOCK = 1024
VMEM_LIMIT = 58 * 1024 * 1024

f32 = jnp.float32
bf16 = jnp.bfloat16


def _rms(x, g):
    ms = jnp.mean(x * x, axis=-1, keepdims=True)
    return x * lax.rsqrt(ms + NORM_EPS) * g


def _sigmoid(x):
    return 1.0 / (1.0 + jnp.exp(-x))


def _exact_zero(x):
    bits = pltpu.bitcast(x, jnp.uint32)
    sixteen = jnp.uint32(16)
    return pltpu.bitcast(lax.shift_right_logical(lax.shift_right_logical(bits, sixteen), sixteen), f32)


PACKED = D_MODEL // 2
_HIGH_HALF = 0xFFFF0000


def _pack_rows(x, already_bf16=False):
    def bits(v):
        return pltpu.bitcast(v if already_bf16 else v.astype(bf16).astype(f32), jnp.uint32)
    low = lax.shift_right_logical(bits(x[:, :PACKED]), jnp.uint32(16))
    return low | (bits(x[:, PACKED:]) & jnp.uint32(_HIGH_HALF))


def _unpack_rows(p):
    low = pltpu.bitcast(lax.shift_left(p, jnp.uint32(16)), f32).astype(bf16)
    high = pltpu.bitcast(p & jnp.uint32(_HIGH_HALF), f32).astype(bf16)
    return low, high


def _load_w_in(w_hbm, stage, w_ref, sem):
    copy = pltpu.make_async_copy(w_hbm, stage, sem)
    copy.start()
    copy.wait()

    def put(dst, value):
        w_ref[:, dst:dst + value.shape[1]] = value.astype(bf16)

    for b in range(CONV_CH // LANES):
        put(2 * b * LANES, stage[:, b * LANES:(b + 1) * LANES])
        put((2 * b + 1) * LANES, stage[:, CONV_CH + b * LANES:CONV_CH + (b + 1) * LANES])
    src = dst = 2 * CONV_CH
    for lo in range(0, Q_DIM, LANES):
        put(dst + lo, stage[:, src + lo:src + lo + LANES] * (HEAD_DIM ** -0.5))
    src, dst = src + Q_DIM, dst + Q_DIM
    for _ in range(2):
        for h in range(N_KV_HEADS):
            head = stage[:, src + h * HEAD_DIM:src + (h + 1) * HEAD_DIM]
            put(dst + 2 * h * HEAD_DIM, jnp.concatenate([head, head], axis=1))
        src, dst = src + KV_DIM, dst + 2 * KV_DIM
    for lo in range(0, 2 * D_MODEL, LANES):
        put(dst + lo, stage[:, src + lo:src + lo + LANES])


def _inproj_kernel(tiles_per_seq, x_ref, g_ref, w_hbm, wdw_ref, bdw_ref, lng_ref, lnb_ref,
                   act_ref, q_ref, kv_ref, sgate_ref, xn_s, vbuf, w_stage, w_ref, w_sem):
    tm = x_ref.shape[0]

    @pl.when(pl.program_id(0) == 0)
    def _():
        _load_w_in(w_hbm, w_stage, w_ref, w_sem)

    xn_s[...] = _rms(x_ref[...], g_ref[...]).astype(bf16)

    def proj(lo, hi):
        return jnp.dot(xn_s[...], w_ref[:, lo:hi], preferred_element_type=f32)

    n_cb = CONV_CH // LANES
    first = lax.rem(pl.program_id(0), tiles_per_seq) == 0

    @pl.when(first)
    def _():
        vbuf[:, 0:CONV_HALO, :] = jnp.zeros((n_cb, CONV_HALO, LANES), f32)

    @pl.when(jnp.logical_not(first))
    def _():
        vbuf[:, 0:CONV_HALO, :] = vbuf[:, tm:tm + CONV_HALO, :]

    for b in range(n_cb):
        u = proj(2 * b * LANES, 2 * (b + 1) * LANES)
        vbuf[b, CONV_HALO:, :] = u[:, :LANES] * _sigmoid(u[:, LANES:])

    pieces, col = [], 2 * CONV_CH
    for out_ref, fn in ((q_ref, lambda z: z), (kv_ref, lambda z: z), (sgate_ref, _sigmoid)):
        pieces += [(out_ref, col, lo, fn) for lo in range(0, out_ref.shape[1], PROJ_PIECE)]
        col += out_ref.shape[1]

    rows = CONV_ROWS
    n_steps = tm // rows
    lag = n_steps - len(pieces)
    assert lag >= 0
    tap0 = CONV_HALO - (CONV_WIDTH - 1)

    def tie_rows(x):
        return jnp.concatenate([_exact_zero(x[0:SUBLANES, 0:LANES])] * (rows // SUBLANES), axis=0)

    tie = None
    for c in range(n_steps):
        if c >= lag:
            out_ref, col, lo, fn = pieces[c - lag]
            z = proj(col + lo, col + lo + PROJ_PIECE)
            out_ref[:, lo:lo + PROJ_PIECE] = fn(z).astype(bf16)
            tie = tie_rows(z) if tie is None else tie + tie_rows(z)
        accs = []
        for b in range(n_cb):
            acc = jnp.broadcast_to(bdw_ref[:, b * LANES:(b + 1) * LANES], (rows, LANES))
            if tie is not None:
                acc = acc + tie
            for j in range(CONV_WIDTH):
                r0 = c * rows + tap0 + j
                acc = acc + wdw_ref[j:j + 1, b * LANES:(b + 1) * LANES] * vbuf[b, r0:r0 + rows, :]
            accs.append(acc)
            tie = tie_rows(sum(acc[k:k + SUBLANES] for k in range(0, rows, SUBLANES)))
        mu = sum(jnp.sum(a, axis=-1, keepdims=True) for a in accs) * (1.0 / CONV_CH)
        ds = [a - mu for a in accs]
        var = sum(jnp.sum(d * d, axis=-1, keepdims=True) for d in ds) * (1.0 / CONV_CH)
        inv = lax.rsqrt(var + NORM_EPS)
        for b in range(n_cb):
            y = ds[b] * inv * lng_ref[:, b * LANES:(b + 1) * LANES] + lnb_ref[:, b * LANES:(b + 1) * LANES]
            act_ref[c * rows:(c + 1) * rows, b * LANES:(b + 1) * LANES] = (y * _sigmoid(y)).astype(bf16)


def _inproj(x2, g_mix, w_in, w_dw, b_dw, ln_g, ln_b, seq):
    t = x2.shape[0]
    tm = TOKEN_TILE
    widths = (CONV_CH, Q_DIM, 4 * KV_DIM, 2 * D_MODEL)

    def full(a):
        return pl.BlockSpec(a.shape, lambda i: (0,) * a.ndim)

    return pl.pallas_call(
        functools.partial(_inproj_kernel, seq // tm),
        grid=(t // tm,),
        in_specs=[pl.BlockSpec((tm, D_MODEL), lambda i: (i, 0)),
                  full(g_mix), pl.BlockSpec(memory_space=pl.ANY), full(w_dw), full(b_dw), full(ln_g), full(ln_b)],
        out_specs=[pl.BlockSpec((tm, w), lambda i: (i, 0)) for w in widths],
        out_shape=[jax.ShapeDtypeStruct((t, w), bf16) for w in widths],
        scratch_shapes=[pltpu.VMEM((tm, D_MODEL), bf16),
                        pltpu.VMEM((CONV_CH // LANES, tm + CONV_HALO, LANES), f32),
                        pltpu.VMEM(w_in.shape, f32),
                        pltpu.VMEM((D_MODEL, sum(widths) + CONV_CH), bf16),
                        pltpu.SemaphoreType.DMA],
        compiler_params=pltpu.CompilerParams(
            dimension_semantics=("arbitrary",), vmem_limit_bytes=VMEM_LIMIT),
        name="inproj",
    )(x2, g_mix, w_in, w_dw, b_dw, ln_g, ln_b)


def _mixer_kernel(x_ref, act_ref, q_ref, kv_ref, kvp_ref, sgate_ref, sink_ref, bias_ref, wco_ref, wao_ref,
                  wo_ref, gffn_ref, wr_ref, br_ref,
                  h_ref, xn2_ref, route_ref, wts_ref,
                  kvall, attn, s_scr, m_scr):
    ts = x_ref.shape[0]
    first = pl.program_id(1) == 0

    kvall[0:ATTN_BLOCK, :] = jnp.where(first, jnp.zeros_like(kvp_ref[...]), kvp_ref[...])
    kvall[ATTN_BLOCK:, :] = kv_ref[...]
    nkeys = 2 * ATTN_BLOCK
    left_kv = lax.broadcasted_iota(jnp.int32, (nkeys, LANES), 1) < HEAD_DIM
    left_q = lax.broadcasted_iota(jnp.int32, (ATTN_BLOCK, LANES), 1) < HEAD_DIM
    key_lane = lax.broadcasted_iota(jnp.int32, (ATTN_BLOCK, 2 * nkeys), 1)
    prev_keys = (key_lane % nkeys) < ATTN_BLOCK
    no_prev = jnp.where(prev_keys, jnp.where(first, -jnp.inf, 0.0), 0.0)

    def block_diag(x):
        z = jnp.zeros_like(x)
        return jnp.concatenate([jnp.where(left_kv, x, z), jnp.where(left_kv, z, x)], axis=0)

    n_pairs = N_HEADS // 2
    for j in range(ts // ATTN_BLOCK):
        r0 = j * ATTN_BLOCK
        for kvh in range(N_KV_HEADS):
            kbd = block_diag(kvall[r0:r0 + nkeys, kvh * LANES:(kvh + 1) * LANES])
            for ii in range(2):
                i = 2 * kvh + ii
                qb = q_ref[r0:r0 + ATTN_BLOCK, i * LANES:(i + 1) * LANES]
                s = lax.dot_general(qb, kbd, (((1,), (1,)), ((), ())), preferred_element_type=f32)
                s = s + bias_ref[i]
                if j == 0:
                    s = s + no_prev
                s_scr[j * n_pairs + i] = s
                for half in range(2):
                    m = jnp.max(s[:, half * nkeys:(half + 1) * nkeys], axis=-1, keepdims=True)
                    m_scr[2 * (j * n_pairs + i) + half] = jnp.maximum(m, sink_ref[2 * i + half])

    for j in range(ts // ATTN_BLOCK):
        r0 = j * ATTN_BLOCK
        for kvh in range(N_KV_HEADS):
            vbd = block_diag(kvall[r0:r0 + nkeys, (N_KV_HEADS + kvh) * LANES:(N_KV_HEADS + kvh + 1) * LANES])
            for ii in range(2):
                i = 2 * kvh + ii
                s = s_scr[j * n_pairs + i]
                ps, ls = [], []
                for half in range(2):
                    m = m_scr[2 * (j * n_pairs + i) + half]
                    p = jnp.exp(s[:, half * nkeys:(half + 1) * nkeys] - m)
                    ps.append(p)
                    ls.append(jnp.sum(p, axis=-1, keepdims=True) + jnp.exp(sink_ref[2 * i + half] - m))
                p = jnp.concatenate(ps, axis=1).astype(bf16)
                o = jnp.dot(p, vbd, preferred_element_type=f32)
                o = o * jnp.where(left_q, 1.0 / ls[0], 1.0 / ls[1])
                attn[r0:r0 + ATTN_BLOCK, i * LANES:(i + 1) * LANES] = o.astype(bf16)

    conv_o = jnp.dot(act_ref[...], wco_ref[...], preferred_element_type=f32)
    attn_o = jnp.dot(attn[...], wao_ref[...], preferred_element_type=f32)
    merged = (sgate_ref[:, :D_MODEL].astype(f32) * conv_o
              + sgate_ref[:, D_MODEL:].astype(f32) * attn_o).astype(bf16)
    h = x_ref[...] + jnp.dot(merged, wo_ref[...], preferred_element_type=f32)
    h_ref[...] = h

    xn2 = _rms(h, gffn_ref[...]).astype(bf16)
    xn2_ref[...] = xn2
    logits = jnp.dot(xn2, wr_ref[...], preferred_element_type=f32) + br_ref[...]
    lt = logits.T
    sub = lax.broadcasted_iota(jnp.int32, (SUBLANES, ts), 0)

    gl = lt[0:SUBLANES]
    gmax = jnp.max(gl, axis=0, keepdims=True)
    gsel = jnp.min(jnp.where(gl == gmax, sub, SUBLANES), axis=0, keepdims=True)
    p_group = 1.0 / jnp.sum(jnp.exp(gl - gmax), axis=0, keepdims=True)

    e_in = lt[SUBLANES:2 * SUBLANES]
    for g in range(1, N_GROUPS):
        e_in = jnp.where(gsel == g, lt[(g + 1) * SUBLANES:(g + 2) * SUBLANES], e_in)
    m1 = jnp.max(e_in, axis=0, keepdims=True)
    i1 = jnp.min(jnp.where(e_in == m1, sub, SUBLANES), axis=0, keepdims=True)
    rest = jnp.where(sub == i1, -jnp.inf, e_in)
    m2 = jnp.max(rest, axis=0, keepdims=True)
    i2 = jnp.min(jnp.where(rest == m2, sub, SUBLANES), axis=0, keepdims=True)
    t2 = jnp.exp(m2 - m1)
    w1 = p_group / (1.0 + t2)
    w2 = p_group * t2 / (1.0 + t2)
    base = gsel * EXPERTS_PER_GROUP
    route_ref[...] = jnp.where(sub == 0, base + i1, jnp.where(sub == 1, base + i2, 0))
    wts_ref[...] = jnp.where(sub == 0, w1, jnp.where(sub == 1, w2, 0.0))


def _attn_bias():
    qi = np.arange(ATTN_BLOCK)[:, None]
    kj = np.arange(2 * ATTN_BLOCK)[None, :]
    rel = (ATTN_BLOCK + qi - kj).astype(np.float32)
    ok = (rel >= 0) & (rel < ATTN_BLOCK)
    slopes = np.array([2.0 ** (-8.0 * (h + 1) / N_HEADS) for h in range(N_HEADS)], np.float32)
    per_head = [np.where(ok, -(slopes[h] * rel), -np.inf).astype(np.float32) for h in range(N_HEADS)]
    return np.stack([np.concatenate([per_head[2 * i], per_head[2 * i + 1]], axis=1)
                     for i in range(N_HEADS // 2)])


def _mixer(x2, act, q, kv, sgates, sinks, wco, wao, wo, g_ffn, w_r, b_r, batch, seq):
    t = x2.shape[0]
    ts = MIXER_TILE
    ns = seq // ts
    bias = jnp.asarray(_attn_bias())

    def row(b, s):
        return b * ns + s

    def full(a):
        return pl.BlockSpec(a.shape, lambda b, s: (0,) * a.ndim, pipeline_mode=pl.Buffered(1))

    in_specs = [
        pl.BlockSpec((ts, D_MODEL), lambda b, s: (row(b, s), 0)),
        pl.BlockSpec((ts, CONV_CH), lambda b, s: (row(b, s), 0)),
        pl.BlockSpec((ts, Q_DIM), lambda b, s: (row(b, s), 0)),
        pl.BlockSpec((ts, 4 * KV_DIM), lambda b, s: (row(b, s), 0)),
        pl.BlockSpec((ATTN_BLOCK, 4 * KV_DIM),
                     lambda b, s: (jnp.maximum(row(b, s) * (ts // ATTN_BLOCK) - 1, 0), 0)),
        pl.BlockSpec((ts, 2 * D_MODEL), lambda b, s: (row(b, s), 0)),
        pl.BlockSpec(memory_space=pltpu.SMEM),
        full(bias), full(wco), full(wao), full(wo), full(g_ffn), full(w_r), full(b_r),
    ]
    out_specs = [
        pl.BlockSpec((ts, D_MODEL), lambda b, s: (row(b, s), 0)),
        pl.BlockSpec((ts, D_MODEL), lambda b, s: (row(b, s), 0)),
        pl.BlockSpec((SUBLANES, ts), lambda b, s: (0, row(b, s))),
        pl.BlockSpec((SUBLANES, ts), lambda b, s: (0, row(b, s))),
    ]
    out_shape = [
        jax.ShapeDtypeStruct((t, D_MODEL), f32),
        jax.ShapeDtypeStruct((t, D_MODEL), bf16),
        jax.ShapeDtypeStruct((SUBLANES, t), jnp.int32),
        jax.ShapeDtypeStruct((SUBLANES, t), f32),
    ]
    return pl.pallas_call(
        _mixer_kernel,
        grid=(batch, ns),
        in_specs=in_specs,
        out_specs=out_specs,
        out_shape=out_shape,
        scratch_shapes=[
            pltpu.VMEM((ts + ATTN_BLOCK, 4 * KV_DIM), bf16),
            pltpu.VMEM((ts, Q_DIM), bf16),
            pltpu.VMEM((ts // ATTN_BLOCK * (N_HEADS // 2), ATTN_BLOCK, 4 * ATTN_BLOCK), f32),
            pltpu.VMEM((ts // ATTN_BLOCK * N_HEADS, ATTN_BLOCK, 1), f32),
        ],
        compiler_params=pltpu.CompilerParams(
            dimension_semantics=("arbitrary", "arbitrary"), vmem_limit_bytes=VMEM_LIMIT),
        name="mixer",
    )(x2, act, q, kv, kv, sgates, sinks, bias, wco, wao, wo, g_ffn, w_r, b_r)


def _local_rows(ts):
    return -(-(2 * ts + (RUN_ALIGN - 1) * N_EXPERTS) // LANES) * LANES


PAD_COARSE = 16
assert (EXPERT_BLOCK // RUN_ALIGN) % PAD_COARSE == 0
TABLE_LANES = 2 * LANES
SEG_NACT_LANE = 3 * N_EXPERTS
SEG_SPARE_LANE = 3 * N_EXPERTS + 1


def _num_blocks(t):
    run_rows = 2 * t + (RUN_ALIGN - 1) * N_EXPERTS * (t // ROUTE_TILE)
    return -(-(run_rows + N_EXPERTS * (EXPERT_BLOCK - RUN_ALIGN)) // EXPERT_BLOCK)


def _route_kernel(route_ref, lpos_ref, tab_ref, seg_ref, meta_ref):
    t = route_ref.shape[1]
    tr = ROUTE_TILE
    nbp = meta_ref.shape[1]
    chunks_per_block = EXPERT_BLOCK // RUN_ALIGN
    eiota = lax.broadcasted_iota(jnp.int32, (N_EXPERTS, tr), 0)
    before = (lax.broadcasted_iota(jnp.int32, (tr, tr), 0)
              < lax.broadcasted_iota(jnp.int32, (tr, tr), 1)).astype(bf16)
    lower = (lax.broadcasted_iota(jnp.int32, (N_EXPERTS, N_EXPERTS), 1)
             < lax.broadcasted_iota(jnp.int32, (N_EXPERTS, N_EXPERTS), 0)).astype(bf16)
    sub = lax.broadcasted_iota(jnp.int32, (N_EXPERTS, LANES), 0)
    lane = lax.broadcasted_iota(jnp.int32, (N_EXPERTS, LANES), 1)

    def to_lanes(col, offset):
        return jnp.sum(jnp.where(sub + offset == lane, col, 0.0), axis=0, keepdims=True)

    def expert_prefix(col):
        b = jnp.broadcast_to(col, (N_EXPERTS, LANES))
        hi = jnp.floor(b * (1.0 / 16.0))
        lo = b - 16.0 * hi
        return (16.0 * jnp.dot(lower, hi.astype(bf16), preferred_element_type=f32)
                + jnp.dot(lower, lo.astype(bf16), preferred_element_type=f32))[:, 0:1]

    lpos_ref[...] = jnp.zeros(lpos_ref.shape, jnp.int32)
    chunk_id = lax.broadcasted_iota(jnp.int32, (N_EXPERTS, TABLE_LANES), 1).astype(f32)
    chunk_expert = lax.broadcasted_iota(jnp.int32, (N_EXPERTS, TABLE_LANES), 0).astype(f32)

    def step(i, seen_chunks):
        off = pl.multiple_of(i * tr, tr)
        m1 = eiota == route_ref[0:1, pl.ds(off, tr)]
        m2 = eiota == route_ref[1:2, pl.ds(off, tr)]
        onehot = jnp.where(m1 | m2, 1.0, 0.0)
        within = jnp.dot(onehot.astype(bf16), before, preferred_element_type=f32)
        run_chunks = jnp.floor((jnp.sum(onehot, axis=1, keepdims=True) + (RUN_ALIGN - 1)) * (1.0 / RUN_ALIGN))
        run_start = expert_prefix(run_chunks)
        pos = within + RUN_ALIGN * run_start
        lpos_ref[0:1, pl.ds(off, tr)] = jnp.sum(jnp.where(m1, pos, 0.0), axis=0, keepdims=True).astype(jnp.int32)
        lpos_ref[1:2, pl.ds(off, tr)] = jnp.sum(jnp.where(m2, pos, 0.0), axis=0, keepdims=True).astype(jnp.int32)
        owner = (run_start <= chunk_id) & (chunk_id < run_start + run_chunks)
        rel = jnp.sum(jnp.where(owner, seen_chunks + chunk_id - run_start, 0.0), axis=0, keepdims=True)
        eid = jnp.sum(jnp.where(owner, chunk_expert, 0.0), axis=0, keepdims=True)
        n_used = jnp.sum(run_chunks, axis=0, keepdims=True)
        unused = chunk_id[0:1] >= n_used
        rel = jnp.where(unused, chunk_id[0:1] - n_used, rel)
        eid = jnp.where(unused, (SEG_SPARE_LANE + lax.rem(i, 2)).astype(f32), eid)
        row = jnp.concatenate([rel, eid], axis=1)
        tab_ref[i] = jnp.broadcast_to(row, (SUBLANES, 2 * TABLE_LANES)).astype(jnp.int32)
        return seen_chunks + run_chunks

    used_chunks = lax.fori_loop(0, t // tr, step, jnp.zeros((N_EXPERTS, 1), f32))

    nblk = jnp.floor((used_chunks + (chunks_per_block - 1)) * (1.0 / chunks_per_block))
    first_blk = expert_prefix(nblk)
    nact = jnp.sum(nblk, axis=0, keepdims=True)
    spare_chunk = _num_blocks(t) * chunks_per_block
    seg_row = (to_lanes(first_blk * chunks_per_block, 0) + to_lanes(used_chunks, N_EXPERTS)
               + to_lanes(nblk * chunks_per_block, 2 * N_EXPERTS)
               + jnp.where(lane[0:1] == SEG_NACT_LANE, nact, 0.0)
               + jnp.where(lane[0:1] == SEG_SPARE_LANE, float(spare_chunk), 0.0)
               + jnp.where(lane[0:1] == SEG_SPARE_LANE + 1, float(spare_chunk + _local_rows(tr) // RUN_ALIGN), 0.0))
    seg_ref[...] = jnp.broadcast_to(seg_row, (SUBLANES, LANES)).astype(jnp.int32)

    tab = tab_ref[...].astype(f32)
    rel, eid = tab[:, :, :TABLE_LANES], tab[:, :, TABLE_LANES:]
    start = jnp.zeros_like(rel)
    for seg_lane in list(range(N_EXPERTS)) + [SEG_SPARE_LANE, SEG_SPARE_LANE + 1]:
        start = start + jnp.where(eid == seg_lane, seg_row[:, seg_lane:seg_lane + 1], 0.0)
    write_row = (start + rel) * RUN_ALIGN
    read_row = jnp.where(eid >= SEG_SPARE_LANE, rel, start + rel) * RUN_ALIGN
    tab_ref[...] = jnp.concatenate([write_row, read_row], axis=2).astype(jnp.int32)

    blk = lax.broadcasted_iota(jnp.int32, (N_EXPERTS, nbp), 1).astype(f32)
    owner = (first_blk <= blk) & (blk < first_blk + nblk)
    expert_id = lax.broadcasted_iota(jnp.int32, (N_EXPERTS, nbp), 0).astype(f32)
    bexp = jnp.sum(jnp.where(owner, expert_id, 0.0), axis=0, keepdims=True)
    row8 = lax.broadcasted_iota(jnp.int32, (SUBLANES, nbp), 0)
    meta_ref[...] = jnp.where(row8 == 0, bexp, jnp.where(row8 == 1, nact, 0.0)).astype(jnp.int32)


def _route(route):
    t = route.shape[1]
    nt = t // ROUTE_TILE
    nbp = -(-_num_blocks(t) // LANES) * LANES
    return pl.pallas_call(
        _route_kernel,
        grid=(1,),
        in_specs=[pl.BlockSpec(route.shape, lambda i: (0, 0))],
        out_specs=[pl.BlockSpec(route.shape, lambda i: (0, 0)),
                   pl.BlockSpec((nt, SUBLANES, 2 * TABLE_LANES), lambda i: (0, 0, 0)),
                   pl.BlockSpec((SUBLANES, LANES), lambda i: (0, 0)),
                   pl.BlockSpec((SUBLANES, nbp), lambda i: (0, 0))],
        out_shape=[jax.ShapeDtypeStruct(route.shape, jnp.int32),
                   jax.ShapeDtypeStruct((nt, SUBLANES, 2 * TABLE_LANES), jnp.int32),
                   jax.ShapeDtypeStruct((SUBLANES, LANES), jnp.int32),
                   jax.ShapeDtypeStruct((SUBLANES, nbp), jnp.int32)],
        compiler_params=pltpu.CompilerParams(
            dimension_semantics=("arbitrary",), vmem_limit_bytes=VMEM_LIMIT),
        name="route",
    )(route)


def _chunk_row(tab_ref, k, reading):
    return pl.multiple_of(tab_ref[0, 0, (TABLE_LANES if reading else 0) + k], RUN_ALIGN)


def _scatter_kernel(tab_ref, seg_ref, lpos_ref, x_ref, xs_ref, xsl, zrows, sem, zsem):
    lrows, ts = xsl.shape[1], x_ref.shape[0]
    n_blocks = (xs_ref.shape[0] - 2 * lrows) // EXPERT_BLOCK
    i = pl.program_id(0)
    last = pl.num_programs(0) - 1
    slot = lax.rem(i, 2)

    def drain(s):
        pltpu.make_async_copy(xsl.at[s], xs_ref.at[pl.ds(0, lrows)], sem.at[s]).wait()

    @pl.when(i >= 2)
    def _():
        drain(slot)

    j = lax.broadcasted_iota(jnp.int32, (lrows, ts), 0)
    perm = jnp.where((j == lpos_ref[0:1, :]) | (j == lpos_ref[1:2, :]), 1.0, 0.0).astype(bf16)
    xsl[slot] = _pack_rows(jnp.dot(perm, x_ref[...], preferred_element_type=f32), already_bf16=True)

    for k in range(lrows // RUN_ALIGN):
        pltpu.make_async_copy(xsl.at[slot, pl.ds(k * RUN_ALIGN, RUN_ALIGN)],
                              xs_ref.at[pl.ds(_chunk_row(tab_ref, k, False), RUN_ALIGN)], sem.at[slot]).start()

    def zcopy(row, n):
        return pltpu.make_async_copy(zrows.at[pl.ds(0, n)], xs_ref.at[pl.ds(row, n)], zsem)

    def for_each_unused_block(fn):
        def body(b, carry):
            fn(pl.multiple_of(b * EXPERT_BLOCK, EXPERT_BLOCK), EXPERT_BLOCK)
            return carry

        lax.fori_loop(seg_ref[0, SEG_NACT_LANE], n_blocks, body, 0)

    def for_each_spare_piece(fn):
        for off in range(0, 2 * lrows, EXPERT_BLOCK):
            fn(n_blocks * EXPERT_BLOCK + off, min(EXPERT_BLOCK, 2 * lrows - off))

    def for_each_pad_chunk(fn):
        def seg(e, carry):
            g0 = seg_ref[0, e]
            used = seg_ref[0, N_EXPERTS + e]
            total = seg_ref[0, 2 * N_EXPERTS + e]

            def chunk(c, carry2):
                fn(pl.multiple_of((g0 + c) * RUN_ALIGN, RUN_ALIGN), RUN_ALIGN)
                return carry2

            def coarse(c, carry2):
                fn(pl.multiple_of((g0 + c * PAD_COARSE) * RUN_ALIGN, PAD_COARSE * RUN_ALIGN),
                   PAD_COARSE * RUN_ALIGN)
                return carry2

            aligned = lax.div(used + (PAD_COARSE - 1), PAD_COARSE)
            lax.fori_loop(used, aligned * PAD_COARSE, chunk, 0)
            lax.fori_loop(aligned, lax.div(total, PAD_COARSE), coarse, 0)
            return carry

        lax.fori_loop(0, N_EXPERTS, seg, 0)

    @pl.when(i == 0)
    def _():
        zrows[...] = jnp.zeros(zrows.shape, zrows.dtype)
        for_each_unused_block(lambda row, n: zcopy(row, n).start())
        for_each_pad_chunk(lambda row, n: zcopy(row, n).start())

    @pl.when(i == last)
    def _():
        drain(slot)

        @pl.when(i >= 1)
        def _():
            drain(1 - slot)

        for_each_unused_block(lambda row, n: zcopy(row, n).wait())
        for_each_pad_chunk(lambda row, n: zcopy(row, n).wait())
        for_each_spare_piece(lambda row, n: zcopy(row, n).start())
        for_each_spare_piece(lambda row, n: zcopy(row, n).wait())


def _scatter(tab, seg, lpos, xn2, n_blocks):
    t = xn2.shape[0]
    ts = ROUTE_TILE
    lrows = _local_rows(ts)
    return pl.pallas_call(
        _scatter_kernel,
        grid=(t // ts,),
        in_specs=[
            pl.BlockSpec((1, SUBLANES, 2 * TABLE_LANES), lambda i: (i, 0, 0), memory_space=pltpu.SMEM),
            pl.BlockSpec((SUBLANES, LANES), lambda i: (0, 0), memory_space=pltpu.SMEM),
            pl.BlockSpec((SUBLANES, ts), lambda i: (0, i)),
            pl.BlockSpec((ts, D_MODEL), lambda i: (i, 0)),
        ],
        out_specs=pl.BlockSpec(memory_space=pl.ANY),
        out_shape=jax.ShapeDtypeStruct((n_blocks * EXPERT_BLOCK + 2 * lrows, PACKED), jnp.uint32),
        scratch_shapes=[pltpu.VMEM((2, lrows, PACKED), jnp.uint32), pltpu.VMEM((EXPERT_BLOCK, PACKED), jnp.uint32),
                        pltpu.SemaphoreType.DMA((2,)), pltpu.SemaphoreType.DMA],
        compiler_params=pltpu.CompilerParams(
            dimension_semantics=("arbitrary",), vmem_limit_bytes=VMEM_LIMIT, has_side_effects=True),
        name="scatter",
    )(tab, seg, lpos, xn2)


def _expert_kernel(bexp_ref, nact_ref, xs_ref, wg_ref, wu_ref, wd_ref, ys_ref, wg_b, wu_b, wd_b):
    b = pl.program_id(0)
    active = b < nact_ref[0]
    new_expert = jnp.logical_or(b == 0, bexp_ref[b] != bexp_ref[jnp.maximum(b - 1, 0)])

    @pl.when(jnp.logical_not(active))
    def _():
        ys_ref[...] = jnp.zeros(ys_ref.shape, ys_ref.dtype)

    @pl.when(jnp.logical_and(active, new_expert))
    def _():
        wg_b[...] = wg_ref[0].astype(bf16)
        wu_b[...] = wu_ref[0].astype(bf16)
        wd_b[...] = wd_ref[0].astype(bf16)

    @pl.when(active)
    def _():
        x_lo, x_hi = _unpack_rows(xs_ref[...])

        def up(w_ref):
            return (jnp.dot(x_lo, w_ref[:PACKED, :], preferred_element_type=f32)
                    + jnp.dot(x_hi, w_ref[PACKED:, :], preferred_element_type=f32))

        g = up(wg_b)
        hmid = (g * _sigmoid(g) * up(wu_b)).astype(bf16)
        ys_ref[...] = _pack_rows(jnp.dot(hmid, wd_b[...], preferred_element_type=f32))


def _experts(bexp, nact, xs, wg, wu, wd, nb):
    n_rows = nb * EXPERT_BLOCK

    def blk(b, nact_ref):
        return jnp.minimum(b, nact_ref[0] - 1)

    grid_spec = pltpu.PrefetchScalarGridSpec(
        num_scalar_prefetch=2,
        grid=(nb,),
        in_specs=[
            pl.BlockSpec((EXPERT_BLOCK, PACKED), lambda b, e, n: (blk(b, n), 0)),
            pl.BlockSpec((1, D_MODEL, D_FF_EXPERT), lambda b, e, n: (e[blk(b, n)], 0, 0)),
            pl.BlockSpec((1, D_MODEL, D_FF_EXPERT), lambda b, e, n: (e[blk(b, n)], 0, 0)),
            pl.BlockSpec((1, D_FF_EXPERT, D_MODEL), lambda b, e, n: (e[blk(b, n)], 0, 0)),
        ],
        out_specs=pl.BlockSpec((EXPERT_BLOCK, PACKED), lambda b, e, n: (b, 0)),
        scratch_shapes=[pltpu.VMEM((D_MODEL, D_FF_EXPERT), bf16), pltpu.VMEM((D_MODEL, D_FF_EXPERT), bf16),
                        pltpu.VMEM((D_FF_EXPERT, D_MODEL), bf16)],
    )
    return pl.pallas_call(
        _expert_kernel,
        grid_spec=grid_spec,
        out_shape=jax.ShapeDtypeStruct((n_rows, PACKED), jnp.uint32),
        compiler_params=pltpu.CompilerParams(
            dimension_semantics=("arbitrary",), vmem_limit_bytes=VMEM_LIMIT),
        name="experts",
    )(bexp, nact, xs, wg, wu, wd)


def _combine_kernel(tab_ref, tab_next_ref, lpos_ref, wts_ref, h_ref, gf_ref, ys_ref, out_ref, ybuf, sem):
    ts, lrows = h_ref.shape[0], ybuf.shape[1]
    i = pl.program_id(0)
    slot = lax.rem(i, 2)

    def fetch(t_ref, s):
        for k in range(lrows // RUN_ALIGN):
            pltpu.make_async_copy(ys_ref.at[pl.ds(_chunk_row(t_ref, k, True), RUN_ALIGN)],
                                  ybuf.at[s, pl.ds(k * RUN_ALIGN, RUN_ALIGN)], sem.at[s]).start()

    @pl.when(i == 0)
    def _():
        fetch(tab_ref, 0)

    @pl.when(i + 1 < pl.num_programs(0))
    def _():
        fetch(tab_next_ref, 1 - slot)

    pltpu.make_async_copy(ys_ref.at[pl.ds(0, lrows)], ybuf.at[slot], sem.at[slot]).wait()

    info = jnp.concatenate([lpos_ref[...].astype(f32), wts_ref[...],
                            jnp.zeros((LANES - 2 * SUBLANES, ts), f32)], axis=0).T
    jl = lax.broadcasted_iota(jnp.int16, (ts, lrows), 1)
    zero = jnp.zeros((), bf16)
    mix = (jnp.where(jl == info[:, 0:1].astype(jnp.int16), info[:, SUBLANES:SUBLANES + 1].astype(bf16), zero)
           + jnp.where(jl == info[:, 1:2].astype(jnp.int16), info[:, SUBLANES + 1:SUBLANES + 2].astype(bf16), zero))
    moe = jnp.concatenate([jnp.dot(mix, y, preferred_element_type=f32) for y in _unpack_rows(ybuf[slot])],
                          axis=1)
    out_ref[...] = _rms(h_ref[...] + moe, gf_ref[...])


def _combine(tab, lpos, wts, h, g_final, ys):
    t = h.shape[0]
    ts = ROUTE_TILE
    nt = t // ts
    lrows = _local_rows(ts)
    assert ys.shape[0] >= lrows
    return pl.pallas_call(
        _combine_kernel,
        grid=(nt,),
        in_specs=[
            pl.BlockSpec((1, SUBLANES, 2 * TABLE_LANES), lambda i: (i, 0, 0), memory_space=pltpu.SMEM),
            pl.BlockSpec((1, SUBLANES, 2 * TABLE_LANES), lambda i: (jnp.minimum(i + 1, nt - 1), 0, 0),
                         memory_space=pltpu.SMEM),
            pl.BlockSpec((SUBLANES, ts), lambda i: (0, i)),
            pl.BlockSpec((SUBLANES, ts), lambda i: (0, i)),
            pl.BlockSpec((ts, D_MODEL), lambda i: (i, 0)),
            pl.BlockSpec((1, D_MODEL), lambda i: (0, 0)),
            pl.BlockSpec(memory_space=pl.ANY),
        ],
        out_specs=pl.BlockSpec((ts, D_MODEL), lambda i: (i, 0)),
        out_shape=jax.ShapeDtypeStruct((t, D_MODEL), f32),
        scratch_shapes=[pltpu.VMEM((2, lrows, PACKED), jnp.uint32), pltpu.SemaphoreType.DMA((2,))],
        compiler_params=pltpu.CompilerParams(
            dimension_semantics=("arbitrary",), vmem_limit_bytes=VMEM_LIMIT),
        name="combine",
    )(tab, tab, lpos, wts, h, g_final, ys)


def _prep_router(w_group, b_group, w_expert, b_expert):
    d = w_group.shape[0]
    w = jnp.zeros((d, LANES), f32)
    w = w.at[:, :N_GROUPS].set(w_group).at[:, SUBLANES:ROUTER_ROWS].set(w_expert)
    b = jnp.full((1, LANES), NEG_BIG, f32)
    b = b.at[0, :N_GROUPS].set(b_group).at[0, SUBLANES:ROUTER_ROWS].set(b_expert)
    return w.astype(bf16), b


def kernel(x, g_mix, w_in, w_dw, b_dw, ln_conv_g, ln_conv_b, sinks, w_conv_out, w_attn_out, w_out, g_ffn,
           w_group, b_group, w_expert, b_expert, w_gate, w_up, w_down, g_final):
    batch, seq, d = x.shape
    assert d == D_MODEL and seq % MIXER_TILE == 0 and seq % TOKEN_TILE == 0 and g_mix.shape[0] == 1
    t = batch * seq
    x2 = x.reshape(t, d)

    act, q, kv, sgates = _inproj(x2, g_mix[0][None, :], w_in[0], w_dw[0, :, 0, :], b_dw[0][None, :],
                                 ln_conv_g[0][None, :], ln_conv_b[0][None, :], seq)

    w_r, b_r = _prep_router(w_group[0], b_group[0], w_expert[0], b_expert[0])
    h, xn2, route, wts = _mixer(
        x2, act, q, kv, sgates, sinks[0], w_conv_out[0].astype(bf16), w_attn_out[0].astype(bf16),
        w_out[0].astype(bf16), g_ffn[0][None, :], w_r, b_r, batch, seq)

    lpos, tab, seg, meta = _route(route)
    nb = _num_blocks(t)
    xs = _scatter(tab, seg, lpos, xn2, nb)
    ys = _experts(meta[0, :nb], meta[1, :1], xs, w_gate[0], w_up[0], w_down[0], nb)
    out = _combine(tab, lpos, wts, h, g_final[None, :], ys)
    return out.reshape(batch, seq, d)
```

```python
import functools

import numpy as np
import jax
import jax.numpy as jnp
from jax import lax
from jax.experimental import pallas as pl
from jax.experimental.pallas import tpu as pltpu

D_MODEL = 1024
CONV_CH = 512
CONV_WIDTH = 31
N_HEADS = 8
N_KV_HEADS = 2
HEAD_DIM = 64
ATTN_BLOCK = 128
N_GROUPS = 4
EXPERTS_PER_GROUP = 8
N_EXPERTS = N_GROUPS * EXPERTS_PER_GROUP
D_FF_EXPERT = 512
NORM_EPS = 1e-6

Q_DIM = N_HEADS * HEAD_DIM
KV_DIM = N_KV_HEADS * HEAD_DIM

LANES = 128
SUBLANES = 8
CONV_HALO = 32
CONV_ROWS = 32
PROJ_PIECE = 256
ROUTER_ROWS = SUBLANES + N_EXPERTS
NEG_BIG = -1e30

TOKEN_TILE = 512
MIXER_TILE = 1024
ROUTE_TILE = 512
RUN_ALIGN = SUBLANES
EXPERT_BLOCK = 1024
VMEM_LIMIT = 58 * 1024 * 1024

f32 = jnp.float32
bf16 = jnp.bfloat16


def _rms(x, g):
    ms = jnp.mean(x * x, axis=-1, keepdims=True)
    return x * lax.rsqrt(ms + NORM_EPS) * g


def _sigmoid(x):
    return 1.0 / (1.0 + jnp.exp(-x))


def _exact_zero(x):
    bits = pltpu.bitcast(x, jnp.uint32)
    sixteen = jnp.uint32(16)
    return pltpu.bitcast(lax.shift_right_logical(lax.shift_right_logical(bits, sixteen), sixteen), f32)


PACKED = D_MODEL // 2
_HIGH_HALF = 0xFFFF0000


def _pack_rows(x, already_bf16=False):
    def bits(v):
        return pltpu.bitcast(v if already_bf16 else v.astype(bf16).astype(f32), jnp.uint32)
    low = lax.shift_right_logical(bits(x[:, :PACKED]), jnp.uint32(16))
    return low | (bits(x[:, PACKED:]) & jnp.uint32(_HIGH_HALF))


def _unpack_rows(p):
    low = pltpu.bitcast(lax.shift_left(p, jnp.uint32(16)), f32).astype(bf16)
    high = pltpu.bitcast(p & jnp.uint32(_HIGH_HALF), f32).astype(bf16)
    return low, high


def _load_w_in(w_hbm, stage, w_ref, sem):
    copy = pltpu.make_async_copy(w_hbm, stage, sem)
    copy.start()
    copy.wait()

    def put(dst, value):
        w_ref[:, dst:dst + value.shape[1]] = value.astype(bf16)

    for b in range(CONV_CH // LANES):
        put(2 * b * LANES, stage[:, b * LANES:(b + 1) * LANES])
        put((2 * b + 1) * LANES, stage[:, CONV_CH + b * LANES:CONV_CH + (b + 1) * LANES])
    src = dst = 2 * CONV_CH
    for lo in range(0, Q_DIM, LANES):
        put(dst + lo, stage[:, src + lo:src + lo + LANES] * (HEAD_DIM ** -0.5))
    src, dst = src + Q_DIM, dst + Q_DIM
    for _ in range(2):
        for h in range(N_KV_HEADS):
            head = stage[:, src + h * HEAD_DIM:src + (h + 1) * HEAD_DIM]
            put(dst + 2 * h * HEAD_DIM, jnp.concatenate([head, head], axis=1))
        src, dst = src + KV_DIM, dst + 2 * KV_DIM
    for lo in range(0, 2 * D_MODEL, LANES):
        put(dst + lo, stage[:, src + lo:src + lo + LANES])


def _inproj_kernel(tiles_per_seq, x_ref, g_ref, w_hbm, wdw_ref, bdw_ref, lng_ref, lnb_ref,
                   act_ref, q_ref, kv_ref, sgate_ref, xn_s, vbuf, w_stage, w_ref, w_sem):
    tm = x_ref.shape[0]

    @pl.when(pl.program_id(0) == 0)
    def _():
        _load_w_in(w_hbm, w_stage, w_ref, w_sem)

    xn_s[...] = _rms(x_ref[...], g_ref[...]).astype(bf16)

    def proj(lo, hi):
        return jnp.dot(xn_s[...], w_ref[:, lo:hi], preferred_element_type=f32)

    n_cb = CONV_CH // LANES
    first = lax.rem(pl.program_id(0), tiles_per_seq) == 0

    @pl.when(first)
    def _():
        vbuf[:, 0:CONV_HALO, :] = jnp.zeros((n_cb, CONV_HALO, LANES), f32)

    @pl.when(jnp.logical_not(first))
    def _():
        vbuf[:, 0:CONV_HALO, :] = vbuf[:, tm:tm + CONV_HALO, :]

    for b in range(n_cb):
        u = proj(2 * b * LANES, 2 * (b + 1) * LANES)
        vbuf[b, CONV_HALO:, :] = u[:, :LANES] * _sigmoid(u[:, LANES:])

    pieces, col = [], 2 * CONV_CH
    for out_ref, fn in ((q_ref, lambda z: z), (kv_ref, lambda z: z), (sgate_ref, _sigmoid)):
        pieces += [(out_ref, col, lo, fn) for lo in range(0, out_ref.shape[1], PROJ_PIECE)]
        col += out_ref.shape[1]

    rows = CONV_ROWS
    n_steps = tm // rows
    lag = n_steps - len(pieces)
    assert lag >= 0
    tap0 = CONV_HALO - (CONV_WIDTH - 1)

    def tie_rows(x):
        return jnp.concatenate([_exact_zero(x[0:SUBLANES, 0:LANES])] * (rows // SUBLANES), axis=0)

    tie = None
    for c in range(n_steps):
        if c >= lag:
            out_ref, col, lo, fn = pieces[c - lag]
            z = proj(col + lo, col + lo + PROJ_PIECE)
            out_ref[:, lo:lo + PROJ_PIECE] = fn(z).astype(bf16)
            tie = tie_rows(z) if tie is None else tie + tie_rows(z)
        accs = []
        for b in range(n_cb):
            acc = jnp.broadcast_to(bdw_ref[:, b * LANES:(b + 1) * LANES], (rows, LANES))
            if tie is not None:
                acc = acc + tie
            for j in range(CONV_WIDTH):
                r0 = c * rows + tap0 + j
                acc = acc + wdw_ref[j:j + 1, b * LANES:(b + 1) * LANES] * vbuf[b, r0:r0 + rows, :]
            accs.append(acc)
            tie = tie_rows(sum(acc[k:k + SUBLANES] for k in range(0, rows, SUBLANES)))
        mu = sum(jnp.sum(a, axis=-1, keepdims=True) for a in accs) * (1.0 / CONV_CH)
        ds = [a - mu for a in accs]
        var = sum(jnp.sum(d * d, axis=-1, keepdims=True) for d in ds) * (1.0 / CONV_CH)
        inv = lax.rsqrt(var + NORM_EPS)
        for b in range(n_cb):
            y = ds[b] * inv * lng_ref[:, b * LANES:(b + 1) * LANES] + lnb_ref[:, b * LANES:(b + 1) * LANES]
            act_ref[c * rows:(c + 1) * rows, b * LANES:(b + 1) * LANES] = (y * _sigmoid(y)).astype(bf16)


def _inproj(x2, g_mix, w_in, w_dw, b_dw, ln_g, ln_b, seq):
    t = x2.shape[0]
    tm = TOKEN_TILE
    widths = (CONV_CH, Q_DIM, 4 * KV_DIM, 2 * D_MODEL)

    def full(a):
        return pl.BlockSpec(a.shape, lambda i: (0,) * a.ndim)

    return pl.pallas_call(
        functools.partial(_inproj_kernel, seq // tm),
        grid=(t // tm,),
        in_specs=[pl.BlockSpec((tm, D_MODEL), lambda i: (i, 0)),
                  full(g_mix), pl.BlockSpec(memory_space=pl.ANY), full(w_dw), full(b_dw), full(ln_g), full(ln_b)],
        out_specs=[pl.BlockSpec((tm, w), lambda i: (i, 0)) for w in widths],
        out_shape=[jax.ShapeDtypeStruct((t, w), bf16) for w in widths],
        scratch_shapes=[pltpu.VMEM((tm, D_MODEL), bf16),
                        pltpu.VMEM((CONV_CH // LANES, tm + CONV_HALO, LANES), f32),
                        pltpu.VMEM(w_in.shape, f32),
                        pltpu.VMEM((D_MODEL, sum(widths) + CONV_CH), bf16),
                        pltpu.SemaphoreType.DMA],
        compiler_params=pltpu.CompilerParams(
            dimension_semantics=("arbitrary",), vmem_limit_bytes=VMEM_LIMIT),
        name="inproj",
    )(x2, g_mix, w_in, w_dw, b_dw, ln_g, ln_b)


def _mixer_kernel(x_ref, act_ref, q_ref, kv_ref, kvp_ref, sgate_ref, sink_ref, bias_ref, wco_ref, wao_ref,
                  wo_ref, gffn_ref, wr_ref, br_ref,
                  h_ref, xn2_ref, route_ref, wts_ref,
                  kvall, attn, s_scr, m_scr):
    ts = x_ref.shape[0]
    first = pl.program_id(1) == 0

    kvall[0:ATTN_BLOCK, :] = jnp.where(first, jnp.zeros_like(kvp_ref[...]), kvp_ref[...])
    kvall[ATTN_BLOCK:, :] = kv_ref[...]
    nkeys = 2 * ATTN_BLOCK
    left_kv = lax.broadcasted_iota(jnp.int32, (nkeys, LANES), 1) < HEAD_DIM
    left_q = lax.broadcasted_iota(jnp.int32, (ATTN_BLOCK, LANES), 1) < HEAD_DIM
    key_lane = lax.broadcasted_iota(jnp.int32, (ATTN_BLOCK, 2 * nkeys), 1)
    prev_keys = (key_lane % nkeys) < ATTN_BLOCK
    no_prev = jnp.where(prev_keys, jnp.where(first, -jnp.inf, 0.0), 0.0)

    def block_diag(x):
        z = jnp.zeros_like(x)
        return jnp.concatenate([jnp.where(left_kv, x, z), jnp.where(left_kv, z, x)], axis=0)

    n_pairs = N_HEADS // 2
    for j in range(ts // ATTN_BLOCK):
        r0 = j * ATTN_BLOCK
        for kvh in range(N_KV_HEADS):
            kbd = block_diag(kvall[r0:r0 + nkeys, kvh * LANES:(kvh + 1) * LANES])
            for ii in range(2):
                i = 2 * kvh + ii
                qb = q_ref[r0:r0 + ATTN_BLOCK, i * LANES:(i + 1) * LANES]
                s = lax.dot_general(qb, kbd, (((1,), (1,)), ((), ())), preferred_element_type=f32)
                s = s + bias_ref[i]
                if j == 0:
                    s = s + no_prev
                s_scr[j * n_pairs + i] = s
                for half in range(2):
                    m = jnp.max(s[:, half * nkeys:(half + 1) * nkeys], axis=-1, keepdims=True)
                    m_scr[2 * (j * n_pairs + i) + half] = jnp.maximum(m, sink_ref[2 * i + half])

    for j in range(ts // ATTN_BLOCK):
        r0 = j * ATTN_BLOCK
        for kvh in range(N_KV_HEADS):
            vbd = block_diag(kvall[r0:r0 + nkeys, (N_KV_HEADS + kvh) * LANES:(N_KV_HEADS + kvh + 1) * LANES])
            for ii in range(2):
                i = 2 * kvh + ii
                s = s_scr[j * n_pairs + i]
                ps, ls = [], []
                for half in range(2):
                    m = m_scr[2 * (j * n_pairs + i) + half]
                    p = jnp.exp(s[:, half * nkeys:(half + 1) * nkeys] - m)
                    ps.append(p)
                    ls.append(jnp.sum(p, axis=-1, keepdims=True) + jnp.exp(sink_ref[2 * i + half] - m))
                p = jnp.concatenate(ps, axis=1).astype(bf16)
                o = jnp.dot(p, vbd, preferred_element_type=f32)
                o = o * jnp.where(left_q, 1.0 / ls[0], 1.0 / ls[1])
                attn[r0:r0 + ATTN_BLOCK, i * LANES:(i + 1) * LANES] = o.astype(bf16)

    conv_o = jnp.dot(act_ref[...], wco_ref[...], preferred_element_type=f32)
    attn_o = jnp.dot(attn[...], wao_ref[...], preferred_element_type=f32)
    merged = (sgate_ref[:, :D_MODEL].astype(f32) * conv_o
              + sgate_ref[:, D_MODEL:].astype(f32) * attn_o).astype(bf16)
    h = x_ref[...] + jnp.dot(merged, wo_ref[...], preferred_element_type=f32)
    h_ref[...] = h

    xn2 = _rms(h, gffn_ref[...]).astype(bf16)
    xn2_ref[...] = xn2
    logits = jnp.dot(xn2, wr_ref[...], preferred_element_type=f32) + br_ref[...]
    lt = logits.T
    sub = lax.broadcasted_iota(jnp.int32, (SUBLANES, ts), 0)

    gl = lt[0:SUBLANES]
    gmax = jnp.max(gl, axis=0, keepdims=True)
    gsel = jnp.min(jnp.where(gl == gmax, sub, SUBLANES), axis=0, keepdims=True)
    p_group = 1.0 / jnp.sum(jnp.exp(gl - gmax), axis=0, keepdims=True)

    e_in = lt[SUBLANES:2 * SUBLANES]
    for g in range(1, N_GROUPS):
        e_in = jnp.where(gsel == g, lt[(g + 1) * SUBLANES:(g + 2) * SUBLANES], e_in)
    m1 = jnp.max(e_in, axis=0, keepdims=True)
    i1 = jnp.min(jnp.where(e_in == m1, sub, SUBLANES), axis=0, keepdims=True)
    rest = jnp.where(sub == i1, -jnp.inf, e_in)
    m2 = jnp.max(rest, axis=0, keepdims=True)
    i2 = jnp.min(jnp.where(rest == m2, sub, SUBLANES), axis=0, keepdims=True)
    t2 = jnp.exp(m2 - m1)
    w1 = p_group / (1.0 + t2)
    w2 = p_group * t2 / (1.0 + t2)
    base = gsel * EXPERTS_PER_GROUP
    route_ref[...] = jnp.where(sub == 0, base + i1, jnp.where(sub == 1, base + i2, 0))
    wts_ref[...] = jnp.where(sub == 0, w1, jnp.where(sub == 1, w2, 0.0))


def _attn_bias():
    qi = np.arange(ATTN_BLOCK)[:, None]
    kj = np.arange(2 * ATTN_BLOCK)[None, :]
    rel = (ATTN_BLOCK + qi - kj).astype(np.float32)
    ok = (rel >= 0) & (rel < ATTN_BLOCK)
    slopes = np.array([2.0 ** (-8.0 * (h + 1) / N_HEADS) for h in range(N_HEADS)], np.float32)
    per_head = [np.where(ok, -(slopes[h] * rel), -np.inf).astype(np.float32) for h in range(N_HEADS)]
    return np.stack([np.concatenate([per_head[2 * i], per_head[2 * i + 1]], axis=1)
                     for i in range(N_HEADS // 2)])


def _mixer(x2, act, q, kv, sgates, sinks, wco, wao, wo, g_ffn, w_r, b_r, batch, seq):
    t = x2.shape[0]
    ts = MIXER_TILE
    ns = seq // ts
    bias = jnp.asarray(_attn_bias())

    def row(b, s):
        return b * ns + s

    def full(a):
        return pl.BlockSpec(a.shape, lambda b, s: (0,) * a.ndim, pipeline_mode=pl.Buffered(1))

    in_specs = [
        pl.BlockSpec((ts, D_MODEL), lambda b, s: (row(b, s), 0)),
        pl.BlockSpec((ts, CONV_CH), lambda b, s: (row(b, s), 0)),
        pl.BlockSpec((ts, Q_DIM), lambda b, s: (row(b, s), 0)),
        pl.BlockSpec((ts, 4 * KV_DIM), lambda b, s: (row(b, s), 0)),
        pl.BlockSpec((ATTN_BLOCK, 4 * KV_DIM),
                     lambda b, s: (jnp.maximum(row(b, s) * (ts // ATTN_BLOCK) - 1, 0), 0)),
        pl.BlockSpec((ts, 2 * D_MODEL), lambda b, s: (row(b, s), 0)),
        pl.BlockSpec(memory_space=pltpu.SMEM),
        full(bias), full(wco), full(wao), full(wo), full(g_ffn), full(w_r), full(b_r),
    ]
    out_specs = [
        pl.BlockSpec((ts, D_MODEL), lambda b, s: (row(b, s), 0)),
        pl.BlockSpec((ts, D_MODEL), lambda b, s: (row(b, s), 0)),
        pl.BlockSpec((SUBLANES, ts), lambda b, s: (0, row(b, s))),
        pl.BlockSpec((SUBLANES, ts), lambda b, s: (0, row(b, s))),
    ]
    out_shape = [
        jax.ShapeDtypeStruct((t, D_MODEL), f32),
        jax.ShapeDtypeStruct((t, D_MODEL), bf16),
        jax.ShapeDtypeStruct((SUBLANES, t), jnp.int32),
        jax.ShapeDtypeStruct((SUBLANES, t), f32),
    ]
    return pl.pallas_call(
        _mixer_kernel,
        grid=(batch, ns),
        in_specs=in_specs,
        out_specs=out_specs,
        out_shape=out_shape,
        scratch_shapes=[
            pltpu.VMEM((ts + ATTN_BLOCK, 4 * KV_DIM), bf16),
            pltpu.VMEM((ts, Q_DIM), bf16),
            pltpu.VMEM((ts // ATTN_BLOCK * (N_HEADS // 2), ATTN_BLOCK, 4 * ATTN_BLOCK), f32),
            pltpu.VMEM((ts // ATTN_BLOCK * N_HEADS, ATTN_BLOCK, 1), f32),
        ],
        compiler_params=pltpu.CompilerParams(
            dimension_semantics=("arbitrary", "arbitrary"), vmem_limit_bytes=VMEM_LIMIT),
        name="mixer",
    )(x2, act, q, kv, kv, sgates, sinks, bias, wco, wao, wo, g_ffn, w_r, b_r)


def _local_rows(ts):
    return -(-(2 * ts + (RUN_ALIGN - 1) * N_EXPERTS) // LANES) * LANES


PAD_COARSE = 16
assert (EXPERT_BLOCK // RUN_ALIGN) % PAD_COARSE == 0
TABLE_LANES = 2 * LANES
SEG_NACT_LANE = 3 * N_EXPERTS
SEG_SPARE_LANE = 3 * N_EXPERTS + 1


def _num_blocks(t):
    run_rows = 2 * t + (RUN_ALIGN - 1) * N_EXPERTS * (t // ROUTE_TILE)
    return -(-(run_rows + N_EXPERTS * (EXPERT_BLOCK - RUN_ALIGN)) // EXPERT_BLOCK)


def _route_kernel(route_ref, lpos_ref, tab_ref, seg_ref, meta_ref):
    t = route_ref.shape[1]
    tr = ROUTE_TILE
    nbp = meta_ref.shape[1]
    chunks_per_block = EXPERT_BLOCK // RUN_ALIGN
    eiota = lax.broadcasted_iota(jnp.int32, (N_EXPERTS, tr), 0)
    before = (lax.broadcasted_iota(jnp.int32, (tr, tr), 0)
              < lax.broadcasted_iota(jnp.int32, (tr, tr), 1)).astype(bf16)
    lower = (lax.broadcasted_iota(jnp.int32, (N_EXPERTS, N_EXPERTS), 1)
             < lax.broadcasted_iota(jnp.int32, (N_EXPERTS, N_EXPERTS), 0)).astype(bf16)
    sub = lax.broadcasted_iota(jnp.int32, (N_EXPERTS, LANES), 0)
    lane = lax.broadcasted_iota(jnp.int32, (N_EXPERTS, LANES), 1)

    def to_lanes(col, offset):
        return jnp.sum(jnp.where(sub + offset == lane, col, 0.0), axis=0, keepdims=True)

    def expert_prefix(col):
        b = jnp.broadcast_to(col, (N_EXPERTS, LANES))
        hi = jnp.floor(b * (1.0 / 16.0))
        lo = b - 16.0 * hi
        return (16.0 * jnp.dot(lower, hi.astype(bf16), preferred_element_type=f32)
                + jnp.dot(lower, lo.astype(bf16), preferred_element_type=f32))[:, 0:1]

    lpos_ref[...] = jnp.zeros(lpos_ref.shape, jnp.int32)
    chunk_id = lax.broadcasted_iota(jnp.int32, (N_EXPERTS, TABLE_LANES), 1).astype(f32)
    chunk_expert = lax.broadcasted_iota(jnp.int32, (N_EXPERTS, TABLE_LANES), 0).astype(f32)

    def step(i, seen_chunks):
        off = pl.multiple_of(i * tr, tr)
        m1 = eiota == route_ref[0:1, pl.ds(off, tr)]
        m2 = eiota == route_ref[1:2, pl.ds(off, tr)]
        onehot = jnp.where(m1 | m2, 1.0, 0.0)
        within = jnp.dot(onehot.astype(bf16), before, preferred_element_type=f32)
        run_chunks = jnp.floor((jnp.sum(onehot, axis=1, keepdims=True) + (RUN_ALIGN - 1)) * (1.0 / RUN_ALIGN))
        run_start = expert_prefix(run_chunks)
        pos = within + RUN_ALIGN * run_start
        lpos_ref[0:1, pl.ds(off, tr)] = jnp.sum(jnp.where(m1, pos, 0.0), axis=0, keepdims=True).astype(jnp.int32)
        lpos_ref[1:2, pl.ds(off, tr)] = jnp.sum(jnp.where(m2, pos, 0.0), axis=0, keepdims=True).astype(jnp.int32)
        owner = (run_start <= chunk_id) & (chunk_id < run_start + run_chunks)
        rel = jnp.sum(jnp.where(owner, seen_chunks + chunk_id - run_start, 0.0), axis=0, keepdims=True)
        eid = jnp.sum(jnp.where(owner, chunk_expert, 0.0), axis=0, keepdims=True)
        n_used = jnp.sum(run_chunks, axis=0, keepdims=True)
        unused = chunk_id[0:1] >= n_used
        rel = jnp.where(unused, chunk_id[0:1] - n_used, rel)
        eid = jnp.where(unused, (SEG_SPARE_LANE + lax.rem(i, 2)).astype(f32), eid)
        row = jnp.concatenate([rel, eid], axis=1)
        tab_ref[i] = jnp.broadcast_to(row, (SUBLANES, 2 * TABLE_LANES)).astype(jnp.int32)
        return seen_chunks + run_chunks

    used_chunks = lax.fori_loop(0, t // tr, step, jnp.zeros((N_EXPERTS, 1), f32))

    nblk = jnp.floor((used_chunks + (chunks_per_block - 1)) * (1.0 / chunks_per_block))
    first_blk = expert_prefix(nblk)
    nact = jnp.sum(nblk, axis=0, keepdims=True)
    spare_chunk = _num_blocks(t) * chunks_per_block
    seg_row = (to_lanes(first_blk * chunks_per_block, 0) + to_lanes(used_chunks, N_EXPERTS)
               + to_lanes(nblk * chunks_per_block, 2 * N_EXPERTS)
               + jnp.where(lane[0:1] == SEG_NACT_LANE, nact, 0.0)
               + jnp.where(lane[0:1] == SEG_SPARE_LANE, float(spare_chunk), 0.0)
               + jnp.where(lane[0:1] == SEG_SPARE_LANE + 1, float(spare_chunk + _local_rows(tr) // RUN_ALIGN), 0.0))
    seg_ref[...] = jnp.broadcast_to(seg_row, (SUBLANES, LANES)).astype(jnp.int32)

    tab = tab_ref[...].astype(f32)
    rel, eid = tab[:, :, :TABLE_LANES], tab[:, :, TABLE_LANES:]
    start = jnp.zeros_like(rel)
    for seg_lane in list(range(N_EXPERTS)) + [SEG_SPARE_LANE, SEG_SPARE_LANE + 1]:
        start = start + jnp.where(eid == seg_lane, seg_row[:, seg_lane:seg_lane + 1], 0.0)
    write_row = (start + rel) * RUN_ALIGN
    read_row = jnp.where(eid >= SEG_SPARE_LANE, rel, start + rel) * RUN_ALIGN
    tab_ref[...] = jnp.concatenate([write_row, read_row], axis=2).astype(jnp.int32)

    blk = lax.broadcasted_iota(jnp.int32, (N_EXPERTS, nbp), 1).astype(f32)
    owner = (first_blk <= blk) & (blk < first_blk + nblk)
    expert_id = lax.broadcasted_iota(jnp.int32, (N_EXPERTS, nbp), 0).astype(f32)
    bexp = jnp.sum(jnp.where(owner, expert_id, 0.0), axis=0, keepdims=True)
    row8 = lax.broadcasted_iota(jnp.int32, (SUBLANES, nbp), 0)
    meta_ref[...] = jnp.where(row8 == 0, bexp, jnp.where(row8 == 1, nact, 0.0)).astype(jnp.int32)


def _route(route):
    t = route.shape[1]
    nt = t // ROUTE_TILE
    nbp = -(-_num_blocks(t) // LANES) * LANES
    return pl.pallas_call(
        _route_kernel,
        grid=(1,),
        in_specs=[pl.BlockSpec(route.shape, lambda i: (0, 0))],
        out_specs=[pl.BlockSpec(route.shape, lambda i: (0, 0)),
                   pl.BlockSpec((nt, SUBLANES, 2 * TABLE_LANES), lambda i: (0, 0, 0)),
                   pl.BlockSpec((SUBLANES, LANES), lambda i: (0, 0)),
                   pl.BlockSpec((SUBLANES, nbp), lambda i: (0, 0))],
        out_shape=[jax.ShapeDtypeStruct(route.shape, jnp.int32),
                   jax.ShapeDtypeStruct((nt, SUBLANES, 2 * TABLE_LANES), jnp.int32),
                   jax.ShapeDtypeStruct((SUBLANES, LANES), jnp.int32),
                   jax.ShapeDtypeStruct((SUBLANES, nbp), jnp.int32)],
        compiler_params=pltpu.CompilerParams(
            dimension_semantics=("arbitrary",), vmem_limit_bytes=VMEM_LIMIT),
        name="route",
    )(route)


def _chunk_row(tab_ref, k, reading):
    return pl.multiple_of(tab_ref[0, 0, (TABLE_LANES if reading else 0) + k], RUN_ALIGN)


def _scatter_kernel(tab_ref, seg_ref, lpos_ref, x_ref, xs_ref, xsl, zrows, sem, zsem):
    lrows, ts = xsl.shape[1], x_ref.shape[0]
    n_blocks = (xs_ref.shape[0] - 2 * lrows) // EXPERT_BLOCK
    i = pl.program_id(0)
    last = pl.num_programs(0) - 1
    slot = lax.rem(i, 2)

    def drain(s):
        pltpu.make_async_copy(xsl.at[s], xs_ref.at[pl.ds(0, lrows)], sem.at[s]).wait()

    @pl.when(i >= 2)
    def _():
        drain(slot)

    j = lax.broadcasted_iota(jnp.int32, (lrows, ts), 0)
    perm = jnp.where((j == lpos_ref[0:1, :]) | (j == lpos_ref[1:2, :]), 1.0, 0.0).astype(bf16)
    xsl[slot] = _pack_rows(jnp.dot(perm, x_ref[...], preferred_element_type=f32), already_bf16=True)

    for k in range(lrows // RUN_ALIGN):
        pltpu.make_async_copy(xsl.at[slot, pl.ds(k * RUN_ALIGN, RUN_ALIGN)],
                              xs_ref.at[pl.ds(_chunk_row(tab_ref, k, False), RUN_ALIGN)], sem.at[slot]).start()

    @pl.when(i == last)
    def _():
        drain(slot)

        @pl.when(i >= 1)
        def _():
            drain(1 - slot)

        zrows[...] = jnp.zeros(zrows.shape, zrows.dtype)

        def zcopy(row, n):
            return pltpu.make_async_copy(zrows.at[pl.ds(0, n)], xs_ref.at[pl.ds(row, n)], zsem)

        def for_each_unused_block(fn):
            def body(b, carry):
                fn(pl.multiple_of(b * EXPERT_BLOCK, EXPERT_BLOCK), EXPERT_BLOCK)
                return carry

            lax.fori_loop(seg_ref[0, SEG_NACT_LANE], n_blocks, body, 0)
            for off in range(0, 2 * lrows, EXPERT_BLOCK):
                fn(n_blocks * EXPERT_BLOCK + off, min(EXPERT_BLOCK, 2 * lrows - off))

        for_each_unused_block(lambda row, n: zcopy(row, n).start())
        for_each_unused_block(lambda row, n: zcopy(row, n).wait())

        def for_each_pad_chunk(fn):
            def seg(e, carry):
                g0 = seg_ref[0, e]
                used = seg_ref[0, N_EXPERTS + e]
                total = seg_ref[0, 2 * N_EXPERTS + e]

                def chunk(c, carry2):
                    fn(pl.multiple_of((g0 + c) * RUN_ALIGN, RUN_ALIGN), RUN_ALIGN)
                    return carry2

                def coarse(c, carry2):
                    fn(pl.multiple_of((g0 + c * PAD_COARSE) * RUN_ALIGN, PAD_COARSE * RUN_ALIGN),
                       PAD_COARSE * RUN_ALIGN)
                    return carry2

                aligned = lax.div(used + (PAD_COARSE - 1), PAD_COARSE)
                lax.fori_loop(used, aligned * PAD_COARSE, chunk, 0)
                lax.fori_loop(aligned, lax.div(total, PAD_COARSE), coarse, 0)
                return carry

            lax.fori_loop(0, N_EXPERTS, seg, 0)

        for_each_pad_chunk(lambda row, n: zcopy(row, n).start())
        for_each_pad_chunk(lambda row, n: zcopy(row, n).wait())


def _scatter(tab, seg, lpos, xn2, n_blocks):
    t = xn2.shape[0]
    ts = ROUTE_TILE
    lrows = _local_rows(ts)
    return pl.pallas_call(
        _scatter_kernel,
        grid=(t // ts,),
        in_specs=[
            pl.BlockSpec((1, SUBLANES, 2 * TABLE_LANES), lambda i: (i, 0, 0), memory_space=pltpu.SMEM),
            pl.BlockSpec((SUBLANES, LANES), lambda i: (0, 0), memory_space=pltpu.SMEM),
            pl.BlockSpec((SUBLANES, ts), lambda i: (0, i)),
            pl.BlockSpec((ts, D_MODEL), lambda i: (i, 0)),
        ],
        out_specs=pl.BlockSpec(memory_space=pl.ANY),
        out_shape=jax.ShapeDtypeStruct((n_blocks * EXPERT_BLOCK + 2 * lrows, PACKED), jnp.uint32),
        scratch_shapes=[pltpu.VMEM((2, lrows, PACKED), jnp.uint32), pltpu.VMEM((EXPERT_BLOCK, PACKED), jnp.uint32),
                        pltpu.SemaphoreType.DMA((2,)), pltpu.SemaphoreType.DMA],
        compiler_params=pltpu.CompilerParams(
            dimension_semantics=("arbitrary",), vmem_limit_bytes=VMEM_LIMIT, has_side_effects=True),
        name="scatter",
    )(tab, seg, lpos, xn2)


def _expert_kernel(bexp_ref, nact_ref, xs_ref, wg_ref, wu_ref, wd_ref, ys_ref, wg_b, wu_b, wd_b):
    b = pl.program_id(0)
    active = b < nact_ref[0]
    new_expert = jnp.logical_or(b == 0, bexp_ref[b] != bexp_ref[jnp.maximum(b - 1, 0)])

    @pl.when(jnp.logical_not(active))
    def _():
        ys_ref[...] = jnp.zeros(ys_ref.shape, ys_ref.dtype)

    def swiglu(wg, wu, wd):
        x = jnp.concatenate(_unpack_rows(xs_ref[...]), axis=1)
        g = jnp.dot(x, wg, preferred_element_type=f32)
        hmid = (g * _sigmoid(g) * jnp.dot(x, wu, preferred_element_type=f32)).astype(bf16)
        ys_ref[...] = _pack_rows(jnp.dot(hmid, wd, preferred_element_type=f32))

    @pl.when(jnp.logical_and(active, new_expert))
    def _():
        wg, wu, wd = (r[0].astype(bf16) for r in (wg_ref, wu_ref, wd_ref))
        wg_b[...] = wg
        wu_b[...] = wu
        wd_b[...] = wd
        swiglu(wg, wu, wd)

    @pl.when(jnp.logical_and(active, jnp.logical_not(new_expert)))
    def _():
        swiglu(wg_b[...], wu_b[...], wd_b[...])


def _experts(bexp, nact, xs, wg, wu, wd, nb):
    n_rows = nb * EXPERT_BLOCK

    def blk(b, nact_ref):
        return jnp.minimum(b, nact_ref[0] - 1)

    grid_spec = pltpu.PrefetchScalarGridSpec(
        num_scalar_prefetch=2,
        grid=(nb,),
        in_specs=[
            pl.BlockSpec((EXPERT_BLOCK, PACKED), lambda b, e, n: (blk(b, n), 0)),
            pl.BlockSpec((1, D_MODEL, D_FF_EXPERT), lambda b, e, n: (e[blk(b, n)], 0, 0)),
            pl.BlockSpec((1, D_MODEL, D_FF_EXPERT), lambda b, e, n: (e[blk(b, n)], 0, 0)),
            pl.BlockSpec((1, D_FF_EXPERT, D_MODEL), lambda b, e, n: (e[blk(b, n)], 0, 0)),
        ],
        out_specs=pl.BlockSpec((EXPERT_BLOCK, PACKED), lambda b, e, n: (b, 0)),
        scratch_shapes=[pltpu.VMEM((D_MODEL, D_FF_EXPERT), bf16), pltpu.VMEM((D_MODEL, D_FF_EXPERT), bf16),
                        pltpu.VMEM((D_FF_EXPERT, D_MODEL), bf16)],
    )
    return pl.pallas_call(
        _expert_kernel,
        grid_spec=grid_spec,
        out_shape=jax.ShapeDtypeStruct((n_rows, PACKED), jnp.uint32),
        compiler_params=pltpu.CompilerParams(
            dimension_semantics=("arbitrary",), vmem_limit_bytes=VMEM_LIMIT),
        name="experts",
    )(bexp, nact, xs, wg, wu, wd)


def _combine_kernel(tab_ref, tab_next_ref, lpos_ref, wts_ref, h_ref, gf_ref, ys_ref, out_ref, ybuf, sem):
    ts, lrows = h_ref.shape[0], ybuf.shape[1]
    i = pl.program_id(0)
    slot = lax.rem(i, 2)

    def fetch(t_ref, s):
        for k in range(lrows // RUN_ALIGN):
            pltpu.make_async_copy(ys_ref.at[pl.ds(_chunk_row(t_ref, k, True), RUN_ALIGN)],
                                  ybuf.at[s, pl.ds(k * RUN_ALIGN, RUN_ALIGN)], sem.at[s]).start()

    @pl.when(i == 0)
    def _():
        fetch(tab_ref, 0)

    @pl.when(i + 1 < pl.num_programs(0))
    def _():
        fetch(tab_next_ref, 1 - slot)

    pltpu.make_async_copy(ys_ref.at[pl.ds(0, lrows)], ybuf.at[slot], sem.at[slot]).wait()

    info = jnp.concatenate([lpos_ref[...].astype(f32), wts_ref[...],
                            jnp.zeros((LANES - 2 * SUBLANES, ts), f32)], axis=0).T
    jl = lax.broadcasted_iota(jnp.int16, (ts, lrows), 1)
    zero = jnp.zeros((), bf16)
    mix = (jnp.where(jl == info[:, 0:1].astype(jnp.int16), info[:, SUBLANES:SUBLANES + 1].astype(bf16), zero)
           + jnp.where(jl == info[:, 1:2].astype(jnp.int16), info[:, SUBLANES + 1:SUBLANES + 2].astype(bf16), zero))
    moe = jnp.concatenate([jnp.dot(mix, y, preferred_element_type=f32) for y in _unpack_rows(ybuf[slot])],
                          axis=1)
    out_ref[...] = _rms(h_ref[...] + moe, gf_ref[...])


def _combine(tab, lpos, wts, h, g_final, ys):
    t = h.shape[0]
    ts = ROUTE_TILE
    nt = t // ts
    lrows = _local_rows(ts)
    assert ys.shape[0] >= lrows
    return pl.pallas_call(
        _combine_kernel,
        grid=(nt,),
        in_specs=[
            pl.BlockSpec((1, SUBLANES, 2 * TABLE_LANES), lambda i: (i, 0, 0), memory_space=pltpu.SMEM),
            pl.BlockSpec((1, SUBLANES, 2 * TABLE_LANES), lambda i: (jnp.minimum(i + 1, nt - 1), 0, 0),
                         memory_space=pltpu.SMEM),
            pl.BlockSpec((SUBLANES, ts), lambda i: (0, i)),
            pl.BlockSpec((SUBLANES, ts), lambda i: (0, i)),
            pl.BlockSpec((ts, D_MODEL), lambda i: (i, 0)),
            pl.BlockSpec((1, D_MODEL), lambda i: (0, 0)),
            pl.BlockSpec(memory_space=pl.ANY),
        ],
        out_specs=pl.BlockSpec((ts, D_MODEL), lambda i: (i, 0)),
        out_shape=jax.ShapeDtypeStruct((t, D_MODEL), f32),
        scratch_shapes=[pltpu.VMEM((2, lrows, PACKED), jnp.uint32), pltpu.SemaphoreType.DMA((2,))],
        compiler_params=pltpu.CompilerParams(
            dimension_semantics=("arbitrary",), vmem_limit_bytes=VMEM_LIMIT),
        name="combine",
    )(tab, tab, lpos, wts, h, g_final, ys)


def _prep_router(w_group, b_group, w_expert, b_expert):
    d = w_group.shape[0]
    w = jnp.zeros((d, LANES), f32)
    w = w.at[:, :N_GROUPS].set(w_group).at[:, SUBLANES:ROUTER_ROWS].set(w_expert)
    b = jnp.full((1, LANES), NEG_BIG, f32)
    b = b.at[0, :N_GROUPS].set(b_group).at[0, SUBLANES:ROUTER_ROWS].set(b_expert)
    return w.astype(bf16), b


def kernel(x, g_mix, w_in, w_dw, b_dw, ln_conv_g, ln_conv_b, sinks, w_conv_out, w_attn_out, w_out, g_ffn,
           w_group, b_group, w_expert, b_expert, w_gate, w_up, w_down, g_final):
    batch, seq, d = x.shape
    assert d == D_MODEL and seq % MIXER_TILE == 0 and seq % TOKEN_TILE == 0 and g_mix.shape[0] == 1
    t = batch * seq
    x2 = x.reshape(t, d)

    act, q, kv, sgates = _inproj(x2, g_mix[0][None, :], w_in[0], w_dw[0, :, 0, :], b_dw[0][None, :],
                                 ln_conv_g[0][None, :], ln_conv_b[0][None, :], seq)

    w_r, b_r = _prep_router(w_group[0], b_group[0], w_expert[0], b_expert[0])
    h, xn2, route, wts = _mixer(
        x2, act, q, kv, sgates, sinks[0], w_conv_out[0].astype(bf16), w_attn_out[0].astype(bf16),
        w_out[0].astype(bf16), g_ffn[0][None, :], w_r, b_r, batch, seq)

    lpos, tab, seg, meta = _route(route)
    nb = _num_blocks(t)
    xs = _scatter(tab, seg, lpos, xn2, nb)
    ys = _experts(meta[0, :nb], meta[1, :1], xs, w_gate[0], w_up[0], w_down[0], nb)
    out = _combine(tab, lpos, wts, h, g_final[None, :], ys)
    return out.reshape(batch, seq, d)
```

```python
import functools

import numpy as np
import jax
import jax.numpy as jnp
from jax import lax
from jax.experimental import pallas as pl
from jax.experimental.pallas import tpu as pltpu

D_MODEL = 1024
CONV_CH = 512
CONV_WIDTH = 31
N_HEADS = 8
N_KV_HEADS = 2
HEAD_DIM = 64
ATTN_BLOCK = 128
N_GROUPS = 4
EXPERTS_PER_GROUP = 8
N_EXPERTS = N_GROUPS * EXPERTS_PER_GROUP
D_FF_EXPERT = 512
NORM_EPS = 1e-6

Q_DIM = N_HEADS * HEAD_DIM
KV_DIM = N_KV_HEADS * HEAD_DIM

LANES = 128
SUBLANES = 8
CONV_HALO = 32
CONV_ROWS = 32
PROJ_PIECE = 256
ROUTER_ROWS = SUBLANES + N_EXPERTS
NEG_BIG = -1e30

TOKEN_TILE = 512
MIXER_TILE = 1024
ROUTE_TILE = 512
RUN_ALIGN = SUBLANES
EXPERT_BLOCK = 1024
VMEM_LIMIT = 58 * 1024 * 1024

f32 = jnp.float32
bf16 = jnp.bfloat16


def _rms(x, g):
    ms = jnp.mean(x * x, axis=-1, keepdims=True)
    return x * lax.rsqrt(ms + NORM_EPS) * g


def _sigmoid(x):
    return 1.0 / (1.0 + jnp.exp(-x))


def _exact_zero(x):
    bits = pltpu.bitcast(x, jnp.uint32)
    sixteen = jnp.uint32(16)
    return pltpu.bitcast(lax.shift_right_logical(lax.shift_right_logical(bits, sixteen), sixteen), f32)


PACKED = D_MODEL // 2
_HIGH_HALF = 0xFFFF0000


def _pack_rows(x, already_bf16=False):
    def bits(v):
        return pltpu.bitcast(v if already_bf16 else v.astype(bf16).astype(f32), jnp.uint32)
    low = lax.shift_right_logical(bits(x[:, :PACKED]), jnp.uint32(16))
    return low | (bits(x[:, PACKED:]) & jnp.uint32(_HIGH_HALF))


def _unpack_rows(p):
    low = pltpu.bitcast(lax.shift_left(p, jnp.uint32(16)), f32).astype(bf16)
    high = pltpu.bitcast(p & jnp.uint32(_HIGH_HALF), f32).astype(bf16)
    return low, high


def _load_w_in(w_hbm, stage, w_ref, sem):
    copy = pltpu.make_async_copy(w_hbm, stage, sem)
    copy.start()
    copy.wait()

    def put(dst, value):
        w_ref[:, dst:dst + value.shape[1]] = value.astype(bf16)

    for b in range(CONV_CH // LANES):
        put(2 * b * LANES, stage[:, b * LANES:(b + 1) * LANES])
        put((2 * b + 1) * LANES, stage[:, CONV_CH + b * LANES:CONV_CH + (b + 1) * LANES])
    src = dst = 2 * CONV_CH
    for lo in range(0, Q_DIM, LANES):
        put(dst + lo, stage[:, src + lo:src + lo + LANES] * (HEAD_DIM ** -0.5))
    src, dst = src + Q_DIM, dst + Q_DIM
    for _ in range(2):
        for h in range(N_KV_HEADS):
            head = stage[:, src + h * HEAD_DIM:src + (h + 1) * HEAD_DIM]
            put(dst + 2 * h * HEAD_DIM, jnp.concatenate([head, head], axis=1))
        src, dst = src + KV_DIM, dst + 2 * KV_DIM
    for lo in range(0, 2 * D_MODEL, LANES):
        put(dst + lo, stage[:, src + lo:src + lo + LANES])


def _inproj_kernel(tiles_per_seq, x_ref, g_ref, w_hbm, wdw_ref, bdw_ref, lng_ref, lnb_ref,
                   act_ref, q_ref, kv_ref, sgate_ref, xn_s, vbuf, w_stage, w_ref, w_sem):
    tm = x_ref.shape[0]

    @pl.when(pl.program_id(0) == 0)
    def _():
        _load_w_in(w_hbm, w_stage, w_ref, w_sem)

    xn_s[...] = _rms(x_ref[...], g_ref[...]).astype(bf16)

    def proj(lo, hi):
        return jnp.dot(xn_s[...], w_ref[:, lo:hi], preferred_element_type=f32)

    n_cb = CONV_CH // LANES
    first = lax.rem(pl.program_id(0), tiles_per_seq) == 0

    @pl.when(first)
    def _():
        vbuf[:, 0:CONV_HALO, :] = jnp.zeros((n_cb, CONV_HALO, LANES), f32)

    @pl.when(jnp.logical_not(first))
    def _():
        vbuf[:, 0:CONV_HALO, :] = vbuf[:, tm:tm + CONV_HALO, :]

    for b in range(n_cb):
        u = proj(2 * b * LANES, 2 * (b + 1) * LANES)
        vbuf[b, CONV_HALO:, :] = u[:, :LANES] * _sigmoid(u[:, LANES:])

    pieces, col = [], 2 * CONV_CH
    for out_ref, fn in ((q_ref, lambda z: z), (kv_ref, lambda z: z), (sgate_ref, _sigmoid)):
        pieces += [(out_ref, col, lo, fn) for lo in range(0, out_ref.shape[1], PROJ_PIECE)]
        col += out_ref.shape[1]

    rows = CONV_ROWS
    n_steps = tm // rows
    lag = n_steps - len(pieces)
    assert lag >= 0
    tap0 = CONV_HALO - (CONV_WIDTH - 1)

    def tie_rows(x):
        return jnp.concatenate([_exact_zero(x[0:SUBLANES, 0:LANES])] * (rows // SUBLANES), axis=0)

    tie = None
    for c in range(n_steps):
        if c >= lag:
            out_ref, col, lo, fn = pieces[c - lag]
            z = proj(col + lo, col + lo + PROJ_PIECE)
            out_ref[:, lo:lo + PROJ_PIECE] = fn(z).astype(bf16)
            tie = tie_rows(z) if tie is None else tie + tie_rows(z)
        accs = []
        for b in range(n_cb):
            acc = jnp.broadcast_to(bdw_ref[:, b * LANES:(b + 1) * LANES], (rows, LANES))
            if tie is not None:
                acc = acc + tie
            for j in range(CONV_WIDTH):
                r0 = c * rows + tap0 + j
                acc = acc + wdw_ref[j:j + 1, b * LANES:(b + 1) * LANES] * vbuf[b, r0:r0 + rows, :]
            accs.append(acc)
            tie = tie_rows(sum(acc[k:k + SUBLANES] for k in range(0, rows, SUBLANES)))
        mu = sum(jnp.sum(a, axis=-1, keepdims=True) for a in accs) * (1.0 / CONV_CH)
        ds = [a - mu for a in accs]
        var = sum(jnp.sum(d * d, axis=-1, keepdims=True) for d in ds) * (1.0 / CONV_CH)
        inv = lax.rsqrt(var + NORM_EPS)
        for b in range(n_cb):
            y = ds[b] * inv * lng_ref[:, b * LANES:(b + 1) * LANES] + lnb_ref[:, b * LANES:(b + 1) * LANES]
            act_ref[c * rows:(c + 1) * rows, b * LANES:(b + 1) * LANES] = (y * _sigmoid(y)).astype(bf16)


def _inproj(x2, g_mix, w_in, w_dw, b_dw, ln_g, ln_b, seq):
    t = x2.shape[0]
    tm = TOKEN_TILE
    widths = (CONV_CH, Q_DIM, 4 * KV_DIM, 2 * D_MODEL)

    def full(a):
        return pl.BlockSpec(a.shape, lambda i: (0,) * a.ndim)

    return pl.pallas_call(
        functools.partial(_inproj_kernel, seq // tm),
        grid=(t // tm,),
        in_specs=[pl.BlockSpec((tm, D_MODEL), lambda i: (i, 0)),
                  full(g_mix), pl.BlockSpec(memory_space=pl.ANY), full(w_dw), full(b_dw), full(ln_g), full(ln_b)],
        out_specs=[pl.BlockSpec((tm, w), lambda i: (i, 0)) for w in widths],
        out_shape=[jax.ShapeDtypeStruct((t, w), bf16) for w in widths],
        scratch_shapes=[pltpu.VMEM((tm, D_MODEL), bf16),
                        pltpu.VMEM((CONV_CH // LANES, tm + CONV_HALO, LANES), f32),
                        pltpu.VMEM(w_in.shape, f32),
                        pltpu.VMEM((D_MODEL, sum(widths) + CONV_CH), bf16),
                        pltpu.SemaphoreType.DMA],
        compiler_params=pltpu.CompilerParams(
            dimension_semantics=("arbitrary",), vmem_limit_bytes=VMEM_LIMIT),
        name="inproj",
    )(x2, g_mix, w_in, w_dw, b_dw, ln_g, ln_b)


def _mixer_kernel(x_ref, act_ref, q_ref, kv_ref, kvp_ref, sgate_ref, sink_ref, bias_ref, wco_ref, wao_ref,
                  wo_ref, gffn_ref, wr_ref, br_ref,
                  h_ref, xn2_ref, route_ref, wts_ref,
                  kvall, attn, s_scr, m_scr):
    ts = x_ref.shape[0]
    first = pl.program_id(1) == 0

    kvall[0:ATTN_BLOCK, :] = jnp.where(first, jnp.zeros_like(kvp_ref[...]), kvp_ref[...])
    kvall[ATTN_BLOCK:, :] = kv_ref[...]
    nkeys = 2 * ATTN_BLOCK
    left_kv = lax.broadcasted_iota(jnp.int32, (nkeys, LANES), 1) < HEAD_DIM
    left_q = lax.broadcasted_iota(jnp.int32, (ATTN_BLOCK, LANES), 1) < HEAD_DIM
    key_lane = lax.broadcasted_iota(jnp.int32, (ATTN_BLOCK, 2 * nkeys), 1)
    prev_keys = (key_lane % nkeys) < ATTN_BLOCK
    no_prev = jnp.where(prev_keys, jnp.where(first, -jnp.inf, 0.0), 0.0)

    def block_diag(x):
        z = jnp.zeros_like(x)
        return jnp.concatenate([jnp.where(left_kv, x, z), jnp.where(left_kv, z, x)], axis=0)

    n_pairs = N_HEADS // 2
    for j in range(ts // ATTN_BLOCK):
        r0 = j * ATTN_BLOCK
        for kvh in range(N_KV_HEADS):
            kbd = block_diag(kvall[r0:r0 + nkeys, kvh * LANES:(kvh + 1) * LANES])
            for ii in range(2):
                i = 2 * kvh + ii
                qb = q_ref[r0:r0 + ATTN_BLOCK, i * LANES:(i + 1) * LANES]
                s = lax.dot_general(qb, kbd, (((1,), (1,)), ((), ())), preferred_element_type=f32)
                s = s + bias_ref[i]
                if j == 0:
                    s = s + no_prev
                s_scr[j * n_pairs + i] = s
                for half in range(2):
                    m = jnp.max(s[:, half * nkeys:(half + 1) * nkeys], axis=-1, keepdims=True)
                    m_scr[2 * (j * n_pairs + i) + half] = jnp.maximum(m, sink_ref[2 * i + half])

    for j in range(ts // ATTN_BLOCK):
        r0 = j * ATTN_BLOCK
        for kvh in range(N_KV_HEADS):
            vbd = block_diag(kvall[r0:r0 + nkeys, (N_KV_HEADS + kvh) * LANES:(N_KV_HEADS + kvh + 1) * LANES])
            for ii in range(2):
                i = 2 * kvh + ii
                s = s_scr[j * n_pairs + i]
                ps, ls = [], []
                for half in range(2):
                    m = m_scr[2 * (j * n_pairs + i) + half]
                    p = jnp.exp(s[:, half * nkeys:(half + 1) * nkeys] - m)
                    ps.append(p)
                    ls.append(jnp.sum(p, axis=-1, keepdims=True) + jnp.exp(sink_ref[2 * i + half] - m))
                p = jnp.concatenate(ps, axis=1).astype(bf16)
                o = jnp.dot(p, vbd, preferred_element_type=f32)
                o = o * jnp.where(left_q, 1.0 / ls[0], 1.0 / ls[1])
                attn[r0:r0 + ATTN_BLOCK, i * LANES:(i + 1) * LANES] = o.astype(bf16)

    conv_o = jnp.dot(act_ref[...], wco_ref[...], preferred_element_type=f32)
    attn_o = jnp.dot(attn[...], wao_ref[...], preferred_element_type=f32)
    merged = (sgate_ref[:, :D_MODEL].astype(f32) * conv_o
              + sgate_ref[:, D_MODEL:].astype(f32) * attn_o).astype(bf16)
    h = x_ref[...] + jnp.dot(merged, wo_ref[...], preferred_element_type=f32)
    h_ref[...] = h

    xn2 = _rms(h, gffn_ref[...]).astype(bf16)
    xn2_ref[...] = xn2
    logits = jnp.dot(xn2, wr_ref[...], preferred_element_type=f32) + br_ref[...]
    lt = logits.T
    sub = lax.broadcasted_iota(jnp.int32, (SUBLANES, ts), 0)

    gl = lt[0:SUBLANES]
    gmax = jnp.max(gl, axis=0, keepdims=True)
    gsel = jnp.min(jnp.where(gl == gmax, sub, SUBLANES), axis=0, keepdims=True)
    p_group = 1.0 / jnp.sum(jnp.exp(gl - gmax), axis=0, keepdims=True)

    e_in = lt[SUBLANES:2 * SUBLANES]
    for g in range(1, N_GROUPS):
        e_in = jnp.where(gsel == g, lt[(g + 1) * SUBLANES:(g + 2) * SUBLANES], e_in)
    m1 = jnp.max(e_in, axis=0, keepdims=True)
    i1 = jnp.min(jnp.where(e_in == m1, sub, SUBLANES), axis=0, keepdims=True)
    rest = jnp.where(sub == i1, -jnp.inf, e_in)
    m2 = jnp.max(rest, axis=0, keepdims=True)
    i2 = jnp.min(jnp.where(rest == m2, sub, SUBLANES), axis=0, keepdims=True)
    t2 = jnp.exp(m2 - m1)
    w1 = p_group / (1.0 + t2)
    w2 = p_group * t2 / (1.0 + t2)
    base = gsel * EXPERTS_PER_GROUP
    route_ref[...] = jnp.where(sub == 0, base + i1, jnp.where(sub == 1, base + i2, 0))
    wts_ref[...] = jnp.where(sub == 0, w1, jnp.where(sub == 1, w2, 0.0))


def _attn_bias():
    qi = np.arange(ATTN_BLOCK)[:, None]
    kj = np.arange(2 * ATTN_BLOCK)[None, :]
    rel = (ATTN_BLOCK + qi - kj).astype(np.float32)
    ok = (rel >= 0) & (rel < ATTN_BLOCK)
    slopes = np.array([2.0 ** (-8.0 * (h + 1) / N_HEADS) for h in range(N_HEADS)], np.float32)
    per_head = [np.where(ok, -(slopes[h] * rel), -np.inf).astype(np.float32) for h in range(N_HEADS)]
    return np.stack([np.concatenate([per_head[2 * i], per_head[2 * i + 1]], axis=1)
                     for i in range(N_HEADS // 2)])


def _mixer(x2, act, q, kv, sgates, sinks, wco, wao, wo, g_ffn, w_r, b_r, batch, seq):
    t = x2.shape[0]
    ts = MIXER_TILE
    ns = seq // ts
    bias = jnp.asarray(_attn_bias())

    def row(b, s):
        return b * ns + s

    def full(a):
        return pl.BlockSpec(a.shape, lambda b, s: (0,) * a.ndim, pipeline_mode=pl.Buffered(1))

    in_specs = [
        pl.BlockSpec((ts, D_MODEL), lambda b, s: (row(b, s), 0)),
        pl.BlockSpec((ts, CONV_CH), lambda b, s: (row(b, s), 0)),
        pl.BlockSpec((ts, Q_DIM), lambda b, s: (row(b, s), 0)),
        pl.BlockSpec((ts, 4 * KV_DIM), lambda b, s: (row(b, s), 0)),
        pl.BlockSpec((ATTN_BLOCK, 4 * KV_DIM),
                     lambda b, s: (jnp.maximum(row(b, s) * (ts // ATTN_BLOCK) - 1, 0), 0)),
        pl.BlockSpec((ts, 2 * D_MODEL), lambda b, s: (row(b, s), 0)),
        pl.BlockSpec(memory_space=pltpu.SMEM),
        full(bias), full(wco), full(wao), full(wo), full(g_ffn), full(w_r), full(b_r),
    ]
    out_specs = [
        pl.BlockSpec((ts, D_MODEL), lambda b, s: (row(b, s), 0)),
        pl.BlockSpec((ts, D_MODEL), lambda b, s: (row(b, s), 0)),
        pl.BlockSpec((SUBLANES, ts), lambda b, s: (0, row(b, s))),
        pl.BlockSpec((SUBLANES, ts), lambda b, s: (0, row(b, s))),
    ]
    out_shape = [
        jax.ShapeDtypeStruct((t, D_MODEL), f32),
        jax.ShapeDtypeStruct((t, D_MODEL), bf16),
        jax.ShapeDtypeStruct((SUBLANES, t), jnp.int32),
        jax.ShapeDtypeStruct((SUBLANES, t), f32),
    ]
    return pl.pallas_call(
        _mixer_kernel,
        grid=(batch, ns),
        in_specs=in_specs,
        out_specs=out_specs,
        out_shape=out_shape,
        scratch_shapes=[
            pltpu.VMEM((ts + ATTN_BLOCK, 4 * KV_DIM), bf16),
            pltpu.VMEM((ts, Q_DIM), bf16),
            pltpu.VMEM((ts // ATTN_BLOCK * (N_HEADS // 2), ATTN_BLOCK, 4 * ATTN_BLOCK), f32),
            pltpu.VMEM((ts // ATTN_BLOCK * N_HEADS, ATTN_BLOCK, 1), f32),
        ],
        compiler_params=pltpu.CompilerParams(
            dimension_semantics=("arbitrary", "arbitrary"), vmem_limit_bytes=VMEM_LIMIT),
        name="mixer",
    )(x2, act, q, kv, kv, sgates, sinks, bias, wco, wao, wo, g_ffn, w_r, b_r)


def _local_rows(ts):
    return -(-(2 * ts + (RUN_ALIGN - 1) * N_EXPERTS) // LANES) * LANES


PAD_COARSE = 16
assert (EXPERT_BLOCK // RUN_ALIGN) % PAD_COARSE == 0
TABLE_LANES = 2 * LANES
SEG_NACT_LANE = 3 * N_EXPERTS
SEG_SPARE_LANE = 3 * N_EXPERTS + 1
TILES_PER_STEP = 2
N_SPARE = 2 * TILES_PER_STEP


def _num_blocks(t):
    run_rows = 2 * t + (RUN_ALIGN - 1) * N_EXPERTS * (t // ROUTE_TILE)
    return -(-(run_rows + N_EXPERTS * (EXPERT_BLOCK - RUN_ALIGN)) // EXPERT_BLOCK)


def _route_kernel(route_ref, lpos_ref, tab_ref, seg_ref, meta_ref):
    t = route_ref.shape[1]
    tr = ROUTE_TILE
    nbp = meta_ref.shape[1]
    chunks_per_block = EXPERT_BLOCK // RUN_ALIGN
    eiota = lax.broadcasted_iota(jnp.int32, (N_EXPERTS, tr), 0)
    before = (lax.broadcasted_iota(jnp.int32, (tr, tr), 0)
              < lax.broadcasted_iota(jnp.int32, (tr, tr), 1)).astype(bf16)
    lower = (lax.broadcasted_iota(jnp.int32, (N_EXPERTS, N_EXPERTS), 1)
             < lax.broadcasted_iota(jnp.int32, (N_EXPERTS, N_EXPERTS), 0)).astype(bf16)
    sub = lax.broadcasted_iota(jnp.int32, (N_EXPERTS, LANES), 0)
    lane = lax.broadcasted_iota(jnp.int32, (N_EXPERTS, LANES), 1)

    def to_lanes(col, offset):
        return jnp.sum(jnp.where(sub + offset == lane, col, 0.0), axis=0, keepdims=True)

    def expert_prefix(col):
        b = jnp.broadcast_to(col, (N_EXPERTS, LANES))
        hi = jnp.floor(b * (1.0 / 16.0))
        lo = b - 16.0 * hi
        return (16.0 * jnp.dot(lower, hi.astype(bf16), preferred_element_type=f32)
                + jnp.dot(lower, lo.astype(bf16), preferred_element_type=f32))[:, 0:1]

    lpos_ref[...] = jnp.zeros(lpos_ref.shape, jnp.int32)
    chunk_id = lax.broadcasted_iota(jnp.int32, (N_EXPERTS, TABLE_LANES), 1).astype(f32)
    chunk_expert = lax.broadcasted_iota(jnp.int32, (N_EXPERTS, TABLE_LANES), 0).astype(f32)

    def step(i, seen_chunks):
        off = pl.multiple_of(i * tr, tr)
        m1 = eiota == route_ref[0:1, pl.ds(off, tr)]
        m2 = eiota == route_ref[1:2, pl.ds(off, tr)]
        onehot = jnp.where(m1 | m2, 1.0, 0.0)
        within = jnp.dot(onehot.astype(bf16), before, preferred_element_type=f32)
        run_chunks = jnp.floor((jnp.sum(onehot, axis=1, keepdims=True) + (RUN_ALIGN - 1)) * (1.0 / RUN_ALIGN))
        run_start = expert_prefix(run_chunks)
        pos = within + RUN_ALIGN * run_start
        lpos_ref[0:1, pl.ds(off, tr)] = jnp.sum(jnp.where(m1, pos, 0.0), axis=0, keepdims=True).astype(jnp.int32)
        lpos_ref[1:2, pl.ds(off, tr)] = jnp.sum(jnp.where(m2, pos, 0.0), axis=0, keepdims=True).astype(jnp.int32)
        owner = (run_start <= chunk_id) & (chunk_id < run_start + run_chunks)
        rel = jnp.sum(jnp.where(owner, seen_chunks + chunk_id - run_start, 0.0), axis=0, keepdims=True)
        eid = jnp.sum(jnp.where(owner, chunk_expert, 0.0), axis=0, keepdims=True)
        n_used = jnp.sum(run_chunks, axis=0, keepdims=True)
        unused = chunk_id[0:1] >= n_used
        rel = jnp.where(unused, chunk_id[0:1] - n_used, rel)
        eid = jnp.where(unused, (SEG_SPARE_LANE + lax.rem(i, N_SPARE)).astype(f32), eid)
        row = jnp.concatenate([rel, eid], axis=1)
        tab_ref[i] = jnp.broadcast_to(row, (SUBLANES, 2 * TABLE_LANES)).astype(jnp.int32)
        return seen_chunks + run_chunks

    used_chunks = lax.fori_loop(0, t // tr, step, jnp.zeros((N_EXPERTS, 1), f32))

    nblk = jnp.floor((used_chunks + (chunks_per_block - 1)) * (1.0 / chunks_per_block))
    first_blk = expert_prefix(nblk)
    nact = jnp.sum(nblk, axis=0, keepdims=True)
    spare_chunk = _num_blocks(t) * chunks_per_block
    seg_row = (to_lanes(first_blk * chunks_per_block, 0) + to_lanes(used_chunks, N_EXPERTS)
               + to_lanes(nblk * chunks_per_block, 2 * N_EXPERTS)
               + jnp.where(lane[0:1] == SEG_NACT_LANE, nact, 0.0)
               + sum(jnp.where(lane[0:1] == SEG_SPARE_LANE + a,
                               float(spare_chunk + a * (_local_rows(tr) // RUN_ALIGN)), 0.0)
                     for a in range(N_SPARE)))
    seg_ref[...] = jnp.broadcast_to(seg_row, (SUBLANES, LANES)).astype(jnp.int32)

    tab = tab_ref[...].astype(f32)
    rel, eid = tab[:, :, :TABLE_LANES], tab[:, :, TABLE_LANES:]
    start = jnp.zeros_like(rel)
    for seg_lane in list(range(N_EXPERTS)) + [SEG_SPARE_LANE + a for a in range(N_SPARE)]:
        start = start + jnp.where(eid == seg_lane, seg_row[:, seg_lane:seg_lane + 1], 0.0)
    write_row = (start + rel) * RUN_ALIGN
    read_row = jnp.where(eid >= SEG_SPARE_LANE, rel, start + rel) * RUN_ALIGN
    tab_ref[...] = jnp.concatenate([write_row, read_row], axis=2).astype(jnp.int32)

    blk = lax.broadcasted_iota(jnp.int32, (N_EXPERTS, nbp), 1).astype(f32)
    owner = (first_blk <= blk) & (blk < first_blk + nblk)
    expert_id = lax.broadcasted_iota(jnp.int32, (N_EXPERTS, nbp), 0).astype(f32)
    bexp = jnp.sum(jnp.where(owner, expert_id, 0.0), axis=0, keepdims=True)
    row8 = lax.broadcasted_iota(jnp.int32, (SUBLANES, nbp), 0)
    meta_ref[...] = jnp.where(row8 == 0, bexp, jnp.where(row8 == 1, nact, 0.0)).astype(jnp.int32)


def _route(route):
    t = route.shape[1]
    nt = t // ROUTE_TILE
    nbp = -(-_num_blocks(t) // LANES) * LANES
    return pl.pallas_call(
        _route_kernel,
        grid=(1,),
        in_specs=[pl.BlockSpec(route.shape, lambda i: (0, 0))],
        out_specs=[pl.BlockSpec(route.shape, lambda i: (0, 0)),
                   pl.BlockSpec((nt, SUBLANES, 2 * TABLE_LANES), lambda i: (0, 0, 0)),
                   pl.BlockSpec((SUBLANES, LANES), lambda i: (0, 0)),
                   pl.BlockSpec((SUBLANES, nbp), lambda i: (0, 0))],
        out_shape=[jax.ShapeDtypeStruct(route.shape, jnp.int32),
                   jax.ShapeDtypeStruct((nt, SUBLANES, 2 * TABLE_LANES), jnp.int32),
                   jax.ShapeDtypeStruct((SUBLANES, LANES), jnp.int32),
                   jax.ShapeDtypeStruct((SUBLANES, nbp), jnp.int32)],
        compiler_params=pltpu.CompilerParams(
            dimension_semantics=("arbitrary",), vmem_limit_bytes=VMEM_LIMIT),
        name="route",
    )(route)


def _chunk_row(tab_ref, tile, k, reading):
    return pl.multiple_of(tab_ref[tile, 0, (TABLE_LANES if reading else 0) + k], RUN_ALIGN)


def _scatter_kernel(tab_ref, seg_ref, lpos_ref, x_ref, xs_ref, xsl, zrows, sem, zsem):
    tps, lrows = xsl.shape[1], xsl.shape[2]
    ts = x_ref.shape[0] // tps
    n_blocks = (xs_ref.shape[0] - N_SPARE * lrows) // EXPERT_BLOCK
    i = pl.program_id(0)
    last = pl.num_programs(0) - 1
    slot = lax.rem(i, 2)

    def drain(s):
        for tile in range(tps):
            pltpu.make_async_copy(xsl.at[s, tile], xs_ref.at[pl.ds(0, lrows)], sem.at[s]).wait()

    @pl.when(i >= 2)
    def _():
        drain(slot)

    j = lax.broadcasted_iota(jnp.int32, (lrows, ts), 0)
    for tile in range(tps):
        cols = slice(tile * ts, (tile + 1) * ts)
        perm = jnp.where((j == lpos_ref[0:1, cols]) | (j == lpos_ref[1:2, cols]), 1.0, 0.0).astype(bf16)
        xsl[slot, tile] = _pack_rows(jnp.dot(perm, x_ref[cols, :], preferred_element_type=f32), already_bf16=True)

    for tile in range(tps):
        for k in range(lrows // RUN_ALIGN):
            pltpu.make_async_copy(xsl.at[slot, tile, pl.ds(k * RUN_ALIGN, RUN_ALIGN)],
                                  xs_ref.at[pl.ds(_chunk_row(tab_ref, tile, k, False), RUN_ALIGN)],
                                  sem.at[slot]).start()

    @pl.when(i == last)
    def _():
        drain(slot)

        @pl.when(i >= 1)
        def _():
            drain(1 - slot)

        zrows[...] = jnp.zeros(zrows.shape, zrows.dtype)

        def zcopy(row, n):
            return pltpu.make_async_copy(zrows.at[pl.ds(0, n)], xs_ref.at[pl.ds(row, n)], zsem)

        def for_each_unused_block(fn):
            def body(b, carry):
                fn(pl.multiple_of(b * EXPERT_BLOCK, EXPERT_BLOCK), EXPERT_BLOCK)
                return carry

            lax.fori_loop(seg_ref[0, SEG_NACT_LANE], n_blocks, body, 0)
            for off in range(0, N_SPARE * lrows, EXPERT_BLOCK):
                fn(n_blocks * EXPERT_BLOCK + off, min(EXPERT_BLOCK, N_SPARE * lrows - off))

        for_each_unused_block(lambda row, n: zcopy(row, n).start())
        for_each_unused_block(lambda row, n: zcopy(row, n).wait())

        def for_each_pad_chunk(fn):
            def seg(e, carry):
                g0 = seg_ref[0, e]
                used = seg_ref[0, N_EXPERTS + e]
                total = seg_ref[0, 2 * N_EXPERTS + e]

                def chunk(c, carry2):
                    fn(pl.multiple_of((g0 + c) * RUN_ALIGN, RUN_ALIGN), RUN_ALIGN)
                    return carry2

                def coarse(c, carry2):
                    fn(pl.multiple_of((g0 + c * PAD_COARSE) * RUN_ALIGN, PAD_COARSE * RUN_ALIGN),
                       PAD_COARSE * RUN_ALIGN)
                    return carry2

                aligned = lax.div(used + (PAD_COARSE - 1), PAD_COARSE)
                lax.fori_loop(used, aligned * PAD_COARSE, chunk, 0)
                lax.fori_loop(aligned, lax.div(total, PAD_COARSE), coarse, 0)
                return carry

            lax.fori_loop(0, N_EXPERTS, seg, 0)

        for_each_pad_chunk(lambda row, n: zcopy(row, n).start())
        for_each_pad_chunk(lambda row, n: zcopy(row, n).wait())


def _scatter(tab, seg, lpos, xn2, n_blocks):
    t = xn2.shape[0]
    tps = TILES_PER_STEP
    ts = ROUTE_TILE * tps
    lrows = _local_rows(ROUTE_TILE)
    return pl.pallas_call(
        _scatter_kernel,
        grid=(t // ts,),
        in_specs=[
            pl.BlockSpec((tps, SUBLANES, 2 * TABLE_LANES), lambda i: (i, 0, 0), memory_space=pltpu.SMEM),
            pl.BlockSpec((SUBLANES, LANES), lambda i: (0, 0), memory_space=pltpu.SMEM),
            pl.BlockSpec((SUBLANES, ts), lambda i: (0, i)),
            pl.BlockSpec((ts, D_MODEL), lambda i: (i, 0)),
        ],
        out_specs=pl.BlockSpec(memory_space=pl.ANY),
        out_shape=jax.ShapeDtypeStruct((n_blocks * EXPERT_BLOCK + N_SPARE * lrows, PACKED), jnp.uint32),
        scratch_shapes=[pltpu.VMEM((2, tps, lrows, PACKED), jnp.uint32),
                        pltpu.VMEM((EXPERT_BLOCK, PACKED), jnp.uint32),
                        pltpu.SemaphoreType.DMA((2,)), pltpu.SemaphoreType.DMA],
        compiler_params=pltpu.CompilerParams(
            dimension_semantics=("arbitrary",), vmem_limit_bytes=VMEM_LIMIT, has_side_effects=True),
        name="scatter",
    )(tab, seg, lpos, xn2)


def _expert_kernel(bexp_ref, nact_ref, xs_ref, wg_ref, wu_ref, wd_ref, ys_ref, wg_b, wu_b, wd_b):
    b = pl.program_id(0)
    active = b < nact_ref[0]
    new_expert = jnp.logical_or(b == 0, bexp_ref[b] != bexp_ref[jnp.maximum(b - 1, 0)])

    @pl.when(jnp.logical_not(active))
    def _():
        ys_ref[...] = jnp.zeros(ys_ref.shape, ys_ref.dtype)

    def swiglu(wg, wu, wd):
        x = jnp.concatenate(_unpack_rows(xs_ref[...]), axis=1)
        g = jnp.dot(x, wg, preferred_element_type=f32)
        hmid = (g * _sigmoid(g) * jnp.dot(x, wu, preferred_element_type=f32)).astype(bf16)
        ys_ref[...] = _pack_rows(jnp.dot(hmid, wd, preferred_element_type=f32))

    @pl.when(jnp.logical_and(active, new_expert))
    def _():
        wg, wu, wd = (r[0].astype(bf16) for r in (wg_ref, wu_ref, wd_ref))
        wg_b[...] = wg
        wu_b[...] = wu
        wd_b[...] = wd
        swiglu(wg, wu, wd)

    @pl.when(jnp.logical_and(active, jnp.logical_not(new_expert)))
    def _():
        swiglu(wg_b[...], wu_b[...], wd_b[...])


def _experts(bexp, nact, xs, wg, wu, wd, nb):
    n_rows = nb * EXPERT_BLOCK

    def blk(b, nact_ref):
        return jnp.minimum(b, nact_ref[0] - 1)

    grid_spec = pltpu.PrefetchScalarGridSpec(
        num_scalar_prefetch=2,
        grid=(nb,),
        in_specs=[
            pl.BlockSpec((EXPERT_BLOCK, PACKED), lambda b, e, n: (blk(b, n), 0)),
            pl.BlockSpec((1, D_MODEL, D_FF_EXPERT), lambda b, e, n: (e[blk(b, n)], 0, 0)),
            pl.BlockSpec((1, D_MODEL, D_FF_EXPERT), lambda b, e, n: (e[blk(b, n)], 0, 0)),
            pl.BlockSpec((1, D_FF_EXPERT, D_MODEL), lambda b, e, n: (e[blk(b, n)], 0, 0)),
        ],
        out_specs=pl.BlockSpec((EXPERT_BLOCK, PACKED), lambda b, e, n: (b, 0)),
        scratch_shapes=[pltpu.VMEM((D_MODEL, D_FF_EXPERT), bf16), pltpu.VMEM((D_MODEL, D_FF_EXPERT), bf16),
                        pltpu.VMEM((D_FF_EXPERT, D_MODEL), bf16)],
    )
    return pl.pallas_call(
        _expert_kernel,
        grid_spec=grid_spec,
        out_shape=jax.ShapeDtypeStruct((n_rows, PACKED), jnp.uint32),
        compiler_params=pltpu.CompilerParams(
            dimension_semantics=("arbitrary",), vmem_limit_bytes=VMEM_LIMIT),
        name="experts",
    )(bexp, nact, xs, wg, wu, wd)


def _combine_kernel(tab_ref, tab_next_ref, lpos_ref, wts_ref, h_ref, gf_ref, ys_ref, out_ref, ybuf, sem):
    tps, lrows = ybuf.shape[1], ybuf.shape[2]
    ts = h_ref.shape[0] // tps
    i = pl.program_id(0)
    slot = lax.rem(i, 2)

    def fetch(t_ref, s):
        for tile in range(tps):
            for k in range(lrows // RUN_ALIGN):
                pltpu.make_async_copy(ys_ref.at[pl.ds(_chunk_row(t_ref, tile, k, True), RUN_ALIGN)],
                                      ybuf.at[s, tile, pl.ds(k * RUN_ALIGN, RUN_ALIGN)], sem.at[s]).start()

    @pl.when(i == 0)
    def _():
        fetch(tab_ref, 0)

    @pl.when(i + 1 < pl.num_programs(0))
    def _():
        fetch(tab_next_ref, 1 - slot)

    for tile in range(tps):
        pltpu.make_async_copy(ys_ref.at[pl.ds(0, lrows)], ybuf.at[slot, tile], sem.at[slot]).wait()

    jl = lax.broadcasted_iota(jnp.int16, (ts, lrows), 1)
    zero = jnp.zeros((), bf16)
    for tile in range(tps):
        cols = slice(tile * ts, (tile + 1) * ts)
        info = jnp.concatenate([lpos_ref[:, cols].astype(f32), wts_ref[:, cols],
                                jnp.zeros((LANES - 2 * SUBLANES, ts), f32)], axis=0).T
        mix = (jnp.where(jl == info[:, 0:1].astype(jnp.int16), info[:, SUBLANES:SUBLANES + 1].astype(bf16), zero)
               + jnp.where(jl == info[:, 1:2].astype(jnp.int16), info[:, SUBLANES + 1:SUBLANES + 2].astype(bf16),
                           zero))
        moe = jnp.concatenate([jnp.dot(mix, y, preferred_element_type=f32)
                               for y in _unpack_rows(ybuf[slot, tile])], axis=1)
        out_ref[cols, :] = _rms(h_ref[cols, :] + moe, gf_ref[...])


def _combine(tab, lpos, wts, h, g_final, ys):
    t = h.shape[0]
    tps = TILES_PER_STEP
    ts = ROUTE_TILE * tps
    nt = t // ts
    lrows = _local_rows(ROUTE_TILE)
    assert ys.shape[0] >= lrows
    return pl.pallas_call(
        _combine_kernel,
        grid=(nt,),
        in_specs=[
            pl.BlockSpec((tps, SUBLANES, 2 * TABLE_LANES), lambda i: (i, 0, 0), memory_space=pltpu.SMEM),
            pl.BlockSpec((tps, SUBLANES, 2 * TABLE_LANES), lambda i: (jnp.minimum(i + 1, nt - 1), 0, 0),
                         memory_space=pltpu.SMEM),
            pl.BlockSpec((SUBLANES, ts), lambda i: (0, i)),
            pl.BlockSpec((SUBLANES, ts), lambda i: (0, i)),
            pl.BlockSpec((ts, D_MODEL), lambda i: (i, 0)),
            pl.BlockSpec((1, D_MODEL), lambda i: (0, 0)),
            pl.BlockSpec(memory_space=pl.ANY),
        ],
        out_specs=pl.BlockSpec((ts, D_MODEL), lambda i: (i, 0)),
        out_shape=jax.ShapeDtypeStruct((t, D_MODEL), f32),
        scratch_shapes=[pltpu.VMEM((2, tps, lrows, PACKED), jnp.uint32), pltpu.SemaphoreType.DMA((2,))],
        compiler_params=pltpu.CompilerParams(
            dimension_semantics=("arbitrary",), vmem_limit_bytes=VMEM_LIMIT),
        name="combine",
    )(tab, tab, lpos, wts, h, g_final, ys)


def _prep_router(w_group, b_group, w_expert, b_expert):
    d = w_group.shape[0]
    w = jnp.zeros((d, LANES), f32)
    w = w.at[:, :N_GROUPS].set(w_group).at[:, SUBLANES:ROUTER_ROWS].set(w_expert)
    b = jnp.full((1, LANES), NEG_BIG, f32)
    b = b.at[0, :N_GROUPS].set(b_group).at[0, SUBLANES:ROUTER_ROWS].set(b_expert)
    return w.astype(bf16), b


def kernel(x, g_mix, w_in, w_dw, b_dw, ln_conv_g, ln_conv_b, sinks, w_conv_out, w_attn_out, w_out, g_ffn,
           w_group, b_group, w_expert, b_expert, w_gate, w_up, w_down, g_final):
    batch, seq, d = x.shape
    assert d == D_MODEL and seq % MIXER_TILE == 0 and seq % TOKEN_TILE == 0 and g_mix.shape[0] == 1
    assert (batch * seq) % (ROUTE_TILE * TILES_PER_STEP) == 0
    t = batch * seq
    x2 = x.reshape(t, d)

    act, q, kv, sgates = _inproj(x2, g_mix[0][None, :], w_in[0], w_dw[0, :, 0, :], b_dw[0][None, :],
                                 ln_conv_g[0][None, :], ln_conv_b[0][None, :], seq)

    w_r, b_r = _prep_router(w_group[0], b_group[0], w_expert[0], b_expert[0])
    h, xn2, route, wts = _mixer(
        x2, act, q, kv, sgates, sinks[0], w_conv_out[0].astype(bf16), w_attn_out[0].astype(bf16),
        w_out[0].astype(bf16), g_ffn[0][None, :], w_r, b_r, batch, seq)

    lpos, tab, seg, meta = _route(route)
    nb = _num_blocks(t)
    xs = _scatter(tab, seg, lpos, xn2, nb)
    ys = _experts(meta[0, :nb], meta[1, :1], xs, w_gate[0], w_up[0], w_down[0], nb)
    out = _combine(tab, lpos, wts, h, g_final[None, :], ys)
    return out.reshape(batch, seq, d)
```

```python
import functools

import numpy as np
import jax
import jax.numpy as jnp
from jax import lax
from jax.experimental import pallas as pl
from jax.experimental.pallas import tpu as pltpu

D_MODEL = 1024
CONV_CH = 512
CONV_WIDTH = 31
N_HEADS = 8
N_KV_HEADS = 2
HEAD_DIM = 64
ATTN_BLOCK = 128
N_GROUPS = 4
EXPERTS_PER_GROUP = 8
N_EXPERTS = N_GROUPS * EXPERTS_PER_GROUP
D_FF_EXPERT = 512
NORM_EPS = 1e-6

Q_DIM = N_HEADS * HEAD_DIM
KV_DIM = N_KV_HEADS * HEAD_DIM

LANES = 128
SUBLANES = 8
CONV_HALO = 32
CONV_ROWS = 32
PROJ_PIECE = 256
ROUTER_ROWS = SUBLANES + N_EXPERTS
NEG_BIG = -1e30

TOKEN_TILE = 512
MIXER_TILE = 1024
ROUTE_TILE = 512
RUN_ALIGN = SUBLANES
EXPERT_BLOCK = 1024
VMEM_LIMIT = 58 * 1024 * 1024

f32 = jnp.float32
bf16 = jnp.bfloat16


def _rms(x, g):
    ms = jnp.mean(x * x, axis=-1, keepdims=True)
    return x * lax.rsqrt(ms + NORM_EPS) * g


def _sigmoid(x):
    return 1.0 / (1.0 + jnp.exp(-x))


def _exact_zero(x):
    bits = pltpu.bitcast(x, jnp.uint32)
    sixteen = jnp.uint32(16)
    return pltpu.bitcast(lax.shift_right_logical(lax.shift_right_logical(bits, sixteen), sixteen), f32)


PACKED = D_MODEL // 2
_HIGH_HALF = 0xFFFF0000


def _pack_rows(x, already_bf16=False):
    def bits(v):
        return pltpu.bitcast(v if already_bf16 else v.astype(bf16).astype(f32), jnp.uint32)
    low = lax.shift_right_logical(bits(x[:, :PACKED]), jnp.uint32(16))
    return low | (bits(x[:, PACKED:]) & jnp.uint32(_HIGH_HALF))


def _unpack_rows(p):
    low = pltpu.bitcast(lax.shift_left(p, jnp.uint32(16)), f32).astype(bf16)
    high = pltpu.bitcast(p & jnp.uint32(_HIGH_HALF), f32).astype(bf16)
    return low, high


def _load_w_in(w_hbm, stage, w_ref, sem):
    copy = pltpu.make_async_copy(w_hbm, stage, sem)
    copy.start()
    copy.wait()

    def put(dst, value):
        w_ref[:, dst:dst + value.shape[1]] = value.astype(bf16)

    for b in range(CONV_CH // LANES):
        put(2 * b * LANES, stage[:, b * LANES:(b + 1) * LANES])
        put((2 * b + 1) * LANES, stage[:, CONV_CH + b * LANES:CONV_CH + (b + 1) * LANES])
    src = dst = 2 * CONV_CH
    for lo in range(0, Q_DIM, LANES):
        put(dst + lo, stage[:, src + lo:src + lo + LANES] * (HEAD_DIM ** -0.5))
    src, dst = src + Q_DIM, dst + Q_DIM
    for _ in range(2):
        for h in range(N_KV_HEADS):
            head = stage[:, src + h * HEAD_DIM:src + (h + 1) * HEAD_DIM]
            put(dst + 2 * h * HEAD_DIM, jnp.concatenate([head, head], axis=1))
        src, dst = src + KV_DIM, dst + 2 * KV_DIM
    for lo in range(0, 2 * D_MODEL, LANES):
        put(dst + lo, stage[:, src + lo:src + lo + LANES])


def _inproj_kernel(tiles_per_seq, x_ref, g_ref, w_hbm, wdw_ref, bdw_ref, lng_ref, lnb_ref,
                   act_ref, q_ref, kv_ref, sgate_ref, xn_s, vbuf, w_stage, w_ref, w_sem):
    tm = x_ref.shape[0]

    @pl.when(pl.program_id(0) == 0)
    def _():
        _load_w_in(w_hbm, w_stage, w_ref, w_sem)

    xn_s[...] = _rms(x_ref[...], g_ref[...]).astype(bf16)

    def proj(lo, hi):
        return jnp.dot(xn_s[...], w_ref[:, lo:hi], preferred_element_type=f32)

    n_cb = CONV_CH // LANES
    first = lax.rem(pl.program_id(0), tiles_per_seq) == 0

    @pl.when(first)
    def _():
        vbuf[:, 0:CONV_HALO, :] = jnp.zeros((n_cb, CONV_HALO, LANES), f32)

    @pl.when(jnp.logical_not(first))
    def _():
        vbuf[:, 0:CONV_HALO, :] = vbuf[:, tm:tm + CONV_HALO, :]

    for b in range(n_cb):
        u = proj(2 * b * LANES, 2 * (b + 1) * LANES)
        vbuf[b, CONV_HALO:, :] = u[:, :LANES] * _sigmoid(u[:, LANES:])

    pieces, col = [], 2 * CONV_CH
    for out_ref, fn in ((q_ref, lambda z: z), (kv_ref, lambda z: z), (sgate_ref, _sigmoid)):
        pieces += [(out_ref, col, lo, fn) for lo in range(0, out_ref.shape[1], PROJ_PIECE)]
        col += out_ref.shape[1]

    rows = CONV_ROWS
    n_steps = tm // rows
    lag = n_steps - len(pieces)
    assert lag >= 0
    tap0 = CONV_HALO - (CONV_WIDTH - 1)

    def tie_rows(x):
        return jnp.concatenate([_exact_zero(x[0:SUBLANES, 0:LANES])] * (rows // SUBLANES), axis=0)

    tie = None
    for c in range(n_steps):
        if c >= lag:
            out_ref, col, lo, fn = pieces[c - lag]
            z = proj(col + lo, col + lo + PROJ_PIECE)
            out_ref[:, lo:lo + PROJ_PIECE] = fn(z).astype(bf16)
            tie = tie_rows(z) if tie is None else tie + tie_rows(z)
        accs = []
        for b in range(n_cb):
            acc = jnp.broadcast_to(bdw_ref[:, b * LANES:(b + 1) * LANES], (rows, LANES))
            if tie is not None:
                acc = acc + tie
            for j in range(CONV_WIDTH):
                r0 = c * rows + tap0 + j
                acc = acc + wdw_ref[j:j + 1, b * LANES:(b + 1) * LANES] * vbuf[b, r0:r0 + rows, :]
            accs.append(acc)
            tie = tie_rows(sum(acc[k:k + SUBLANES] for k in range(0, rows, SUBLANES)))
        mu = sum(jnp.sum(a, axis=-1, keepdims=True) for a in accs) * (1.0 / CONV_CH)
        ds = [a - mu for a in accs]
        var = sum(jnp.sum(d * d, axis=-1, keepdims=True) for d in ds) * (1.0 / CONV_CH)
        inv = lax.rsqrt(var + NORM_EPS)
        for b in range(n_cb):
            y = ds[b] * inv * lng_ref[:, b * LANES:(b + 1) * LANES] + lnb_ref[:, b * LANES:(b + 1) * LANES]
            act_ref[c * rows:(c + 1) * rows, b * LANES:(b + 1) * LANES] = (y * _sigmoid(y)).astype(bf16)


def _inproj(x2, g_mix, w_in, w_dw, b_dw, ln_g, ln_b, seq):
    t = x2.shape[0]
    tm = TOKEN_TILE
    widths = (CONV_CH, Q_DIM, 4 * KV_DIM, 2 * D_MODEL)

    def full(a):
        return pl.BlockSpec(a.shape, lambda i: (0,) * a.ndim)

    return pl.pallas_call(
        functools.partial(_inproj_kernel, seq // tm),
        grid=(t // tm,),
        in_specs=[pl.BlockSpec((tm, D_MODEL), lambda i: (i, 0)),
                  full(g_mix), pl.BlockSpec(memory_space=pl.ANY), full(w_dw), full(b_dw), full(ln_g), full(ln_b)],
        out_specs=[pl.BlockSpec((tm, w), lambda i: (i, 0)) for w in widths],
        out_shape=[jax.ShapeDtypeStruct((t, w), bf16) for w in widths],
        scratch_shapes=[pltpu.VMEM((tm, D_MODEL), bf16),
                        pltpu.VMEM((CONV_CH // LANES, tm + CONV_HALO, LANES), f32),
                        pltpu.VMEM(w_in.shape, f32),
                        pltpu.VMEM((D_MODEL, sum(widths) + CONV_CH), bf16),
                        pltpu.SemaphoreType.DMA],
        compiler_params=pltpu.CompilerParams(
            dimension_semantics=("arbitrary",), vmem_limit_bytes=VMEM_LIMIT),
        name="inproj",
    )(x2, g_mix, w_in, w_dw, b_dw, ln_g, ln_b)


def _mixer_kernel(x_ref, act_ref, q_ref, kv_ref, kvp_ref, sgate_ref, sink_ref, bias_ref, wco_ref, wao_ref,
                  wo_ref, gffn_ref, wr_ref, br_ref,
                  h_ref, xn2_ref, route_ref, wts_ref,
                  kvall, attn, s_scr, m_scr):
    ts = x_ref.shape[0]
    first = pl.program_id(1) == 0

    kvall[0:ATTN_BLOCK, :] = jnp.where(first, jnp.zeros_like(kvp_ref[...]), kvp_ref[...])
    kvall[ATTN_BLOCK:, :] = kv_ref[...]
    nkeys = 2 * ATTN_BLOCK
    left_kv = lax.broadcasted_iota(jnp.int32, (nkeys, LANES), 1) < HEAD_DIM
    left_q = lax.broadcasted_iota(jnp.int32, (ATTN_BLOCK, LANES), 1) < HEAD_DIM
    key_lane = lax.broadcasted_iota(jnp.int32, (ATTN_BLOCK, 2 * nkeys), 1)
    prev_keys = (key_lane % nkeys) < ATTN_BLOCK
    no_prev = jnp.where(prev_keys, jnp.where(first, -jnp.inf, 0.0), 0.0)

    def block_diag(x):
        z = jnp.zeros_like(x)
        return jnp.concatenate([jnp.where(left_kv, x, z), jnp.where(left_kv, z, x)], axis=0)

    n_pairs = N_HEADS // 2
    for j in range(ts // ATTN_BLOCK):
        r0 = j * ATTN_BLOCK
        for kvh in range(N_KV_HEADS):
            kbd = block_diag(kvall[r0:r0 + nkeys, kvh * LANES:(kvh + 1) * LANES])
            for ii in range(2):
                i = 2 * kvh + ii
                qb = q_ref[r0:r0 + ATTN_BLOCK, i * LANES:(i + 1) * LANES]
                s = lax.dot_general(qb, kbd, (((1,), (1,)), ((), ())), preferred_element_type=f32)
                s = s + bias_ref[i]
                if j == 0:
                    s = s + no_prev
                s_scr[j * n_pairs + i] = s
                for half in range(2):
                    m = jnp.max(s[:, half * nkeys:(half + 1) * nkeys], axis=-1, keepdims=True)
                    m_scr[2 * (j * n_pairs + i) + half] = jnp.maximum(m, sink_ref[2 * i + half])

    for j in range(ts // ATTN_BLOCK):
        r0 = j * ATTN_BLOCK
        for kvh in range(N_KV_HEADS):
            vbd = block_diag(kvall[r0:r0 + nkeys, (N_KV_HEADS + kvh) * LANES:(N_KV_HEADS + kvh + 1) * LANES])
            for ii in range(2):
                i = 2 * kvh + ii
                s = s_scr[j * n_pairs + i]
                ps, ls = [], []
                for half in range(2):
                    m = m_scr[2 * (j * n_pairs + i) + half]
                    p = jnp.exp(s[:, half * nkeys:(half + 1) * nkeys] - m)
                    ps.append(p)
                    ls.append(jnp.sum(p, axis=-1, keepdims=True) + jnp.exp(sink_ref[2 * i + half] - m))
                p = jnp.concatenate(ps, axis=1).astype(bf16)
                o = jnp.dot(p, vbd, preferred_element_type=f32)
                o = o * jnp.where(left_q, 1.0 / ls[0], 1.0 / ls[1])
                attn[r0:r0 + ATTN_BLOCK, i * LANES:(i + 1) * LANES] = o.astype(bf16)

    conv_o = jnp.dot(act_ref[...], wco_ref[...], preferred_element_type=f32)
    attn_o = jnp.dot(attn[...], wao_ref[...], preferred_element_type=f32)
    merged = (sgate_ref[:, :D_MODEL].astype(f32) * conv_o
              + sgate_ref[:, D_MODEL:].astype(f32) * attn_o).astype(bf16)
    h = x_ref[...] + jnp.dot(merged, wo_ref[...], preferred_element_type=f32)
    h_ref[...] = h

    xn2 = _rms(h, gffn_ref[...]).astype(bf16)
    xn2_ref[...] = xn2
    logits = jnp.dot(xn2, wr_ref[...], preferred_element_type=f32) + br_ref[...]
    lt = logits.T
    sub = lax.broadcasted_iota(jnp.int32, (SUBLANES, ts), 0)

    gl = lt[0:SUBLANES]
    gmax = jnp.max(gl, axis=0, keepdims=True)
    gsel = jnp.min(jnp.where(gl == gmax, sub, SUBLANES), axis=0, keepdims=True)
    p_group = 1.0 / jnp.sum(jnp.exp(gl - gmax), axis=0, keepdims=True)

    e_in = lt[SUBLANES:2 * SUBLANES]
    for g in range(1, N_GROUPS):
        e_in = jnp.where(gsel == g, lt[(g + 1) * SUBLANES:(g + 2) * SUBLANES], e_in)
    m1 = jnp.max(e_in, axis=0, keepdims=True)
    i1 = jnp.min(jnp.where(e_in == m1, sub, SUBLANES), axis=0, keepdims=True)
    rest = jnp.where(sub == i1, -jnp.inf, e_in)
    m2 = jnp.max(rest, axis=0, keepdims=True)
    i2 = jnp.min(jnp.where(rest == m2, sub, SUBLANES), axis=0, keepdims=True)
    t2 = jnp.exp(m2 - m1)
    w1 = p_group / (1.0 + t2)
    w2 = p_group * t2 / (1.0 + t2)
    base = gsel * EXPERTS_PER_GROUP
    route_ref[...] = jnp.where(sub == 0, base + i1, jnp.where(sub == 1, base + i2, 0))
    wts_ref[...] = jnp.where(sub == 0, w1, jnp.where(sub == 1, w2, 0.0))


def _attn_bias():
    qi = np.arange(ATTN_BLOCK)[:, None]
    kj = np.arange(2 * ATTN_BLOCK)[None, :]
    rel = (ATTN_BLOCK + qi - kj).astype(np.float32)
    ok = (rel >= 0) & (rel < ATTN_BLOCK)
    slopes = np.array([2.0 ** (-8.0 * (h + 1) / N_HEADS) for h in range(N_HEADS)], np.float32)
    per_head = [np.where(ok, -(slopes[h] * rel), -np.inf).astype(np.float32) for h in range(N_HEADS)]
    return np.stack([np.concatenate([per_head[2 * i], per_head[2 * i + 1]], axis=1)
                     for i in range(N_HEADS // 2)])


def _mixer(x2, act, q, kv, sgates, sinks, wco, wao, wo, g_ffn, w_r, b_r, batch, seq):
    t = x2.shape[0]
    ts = MIXER_TILE
    ns = seq // ts
    bias = jnp.asarray(_attn_bias())

    def row(b, s):
        return b * ns + s

    def full(a):
        return pl.BlockSpec(a.shape, lambda b, s: (0,) * a.ndim, pipeline_mode=pl.Buffered(1))

    in_specs = [
        pl.BlockSpec((ts, D_MODEL), lambda b, s: (row(b, s), 0)),
        pl.BlockSpec((ts, CONV_CH), lambda b, s: (row(b, s), 0)),
        pl.BlockSpec((ts, Q_DIM), lambda b, s: (row(b, s), 0)),
        pl.BlockSpec((ts, 4 * KV_DIM), lambda b, s: (row(b, s), 0)),
        pl.BlockSpec((ATTN_BLOCK, 4 * KV_DIM),
                     lambda b, s: (jnp.maximum(row(b, s) * (ts // ATTN_BLOCK) - 1, 0), 0)),
        pl.BlockSpec((ts, 2 * D_MODEL), lambda b, s: (row(b, s), 0)),
        pl.BlockSpec(memory_space=pltpu.SMEM),
        full(bias), full(wco), full(wao), full(wo), full(g_ffn), full(w_r), full(b_r),
    ]
    out_specs = [
        pl.BlockSpec((ts, D_MODEL), lambda b, s: (row(b, s), 0)),
        pl.BlockSpec((ts, D_MODEL), lambda b, s: (row(b, s), 0)),
        pl.BlockSpec((SUBLANES, ts), lambda b, s: (0, row(b, s))),
        pl.BlockSpec((SUBLANES, ts), lambda b, s: (0, row(b, s))),
    ]
    out_shape = [
        jax.ShapeDtypeStruct((t, D_MODEL), f32),
        jax.ShapeDtypeStruct((t, D_MODEL), bf16),
        jax.ShapeDtypeStruct((SUBLANES, t), jnp.int32),
        jax.ShapeDtypeStruct((SUBLANES, t), f32),
    ]
    return pl.pallas_call(
        _mixer_kernel,
        grid=(batch, ns),
        in_specs=in_specs,
        out_specs=out_specs,
        out_shape=out_shape,
        scratch_shapes=[
            pltpu.VMEM((ts + ATTN_BLOCK, 4 * KV_DIM), bf16),
            pltpu.VMEM((ts, Q_DIM), bf16),
            pltpu.VMEM((ts // ATTN_BLOCK * (N_HEADS // 2), ATTN_BLOCK, 4 * ATTN_BLOCK), f32),
            pltpu.VMEM((ts // ATTN_BLOCK * N_HEADS, ATTN_BLOCK, 1), f32),
        ],
        compiler_params=pltpu.CompilerParams(
            dimension_semantics=("arbitrary", "arbitrary"), vmem_limit_bytes=VMEM_LIMIT),
        name="mixer",
    )(x2, act, q, kv, kv, sgates, sinks, bias, wco, wao, wo, g_ffn, w_r, b_r)


def _local_rows(ts):
    return -(-(2 * ts + (RUN_ALIGN - 1) * N_EXPERTS) // LANES) * LANES


PAD_COARSE = 16
assert (EXPERT_BLOCK // RUN_ALIGN) % PAD_COARSE == 0
TABLE_LANES = 2 * LANES
SEG_NACT_LANE = 3 * N_EXPERTS
SEG_SPARE_LANE = 3 * N_EXPERTS + 1
TILES_PER_STEP = 2
N_SPARE = 2 * TILES_PER_STEP


def _num_blocks(t):
    run_rows = 2 * t + (RUN_ALIGN - 1) * N_EXPERTS * (t // ROUTE_TILE)
    return -(-(run_rows + N_EXPERTS * (EXPERT_BLOCK - RUN_ALIGN)) // EXPERT_BLOCK)


def _route_kernel(route_ref, lpos_ref, tab_ref, seg_ref, meta_ref):
    t = route_ref.shape[1]
    tr = ROUTE_TILE
    nbp = meta_ref.shape[1]
    chunks_per_block = EXPERT_BLOCK // RUN_ALIGN
    eiota = lax.broadcasted_iota(jnp.int32, (N_EXPERTS, tr), 0)
    before = (lax.broadcasted_iota(jnp.int32, (tr, tr), 0)
              < lax.broadcasted_iota(jnp.int32, (tr, tr), 1)).astype(bf16)
    lower = (lax.broadcasted_iota(jnp.int32, (N_EXPERTS, N_EXPERTS), 1)
             < lax.broadcasted_iota(jnp.int32, (N_EXPERTS, N_EXPERTS), 0)).astype(bf16)
    sub = lax.broadcasted_iota(jnp.int32, (N_EXPERTS, LANES), 0)
    lane = lax.broadcasted_iota(jnp.int32, (N_EXPERTS, LANES), 1)

    def to_lanes(col, offset):
        return jnp.sum(jnp.where(sub + offset == lane, col, 0.0), axis=0, keepdims=True)

    def expert_prefix(col):
        b = jnp.broadcast_to(col, (N_EXPERTS, LANES))
        hi = jnp.floor(b * (1.0 / 16.0))
        lo = b - 16.0 * hi
        return (16.0 * jnp.dot(lower, hi.astype(bf16), preferred_element_type=f32)
                + jnp.dot(lower, lo.astype(bf16), preferred_element_type=f32))[:, 0:1]

    lpos_ref[...] = jnp.zeros(lpos_ref.shape, jnp.int32)
    chunk_id = lax.broadcasted_iota(jnp.int32, (N_EXPERTS, TABLE_LANES), 1).astype(f32)
    chunk_expert = lax.broadcasted_iota(jnp.int32, (N_EXPERTS, TABLE_LANES), 0).astype(f32)

    def step(i, seen_chunks):
        off = pl.multiple_of(i * tr, tr)
        m1 = eiota == route_ref[0:1, pl.ds(off, tr)]
        m2 = eiota == route_ref[1:2, pl.ds(off, tr)]
        onehot = jnp.where(m1 | m2, 1.0, 0.0)
        within = jnp.dot(onehot.astype(bf16), before, preferred_element_type=f32)
        run_chunks = jnp.floor((jnp.sum(onehot, axis=1, keepdims=True) + (RUN_ALIGN - 1)) * (1.0 / RUN_ALIGN))
        run_start = expert_prefix(run_chunks)
        pos = within + RUN_ALIGN * run_start
        lpos_ref[0:1, pl.ds(off, tr)] = jnp.sum(jnp.where(m1, pos, 0.0), axis=0, keepdims=True).astype(jnp.int32)
        lpos_ref[1:2, pl.ds(off, tr)] = jnp.sum(jnp.where(m2, pos, 0.0), axis=0, keepdims=True).astype(jnp.int32)
        owner = (run_start <= chunk_id) & (chunk_id < run_start + run_chunks)
        rel = jnp.sum(jnp.where(owner, seen_chunks + chunk_id - run_start, 0.0), axis=0, keepdims=True)
        eid = jnp.sum(jnp.where(owner, chunk_expert, 0.0), axis=0, keepdims=True)
        n_used = jnp.sum(run_chunks, axis=0, keepdims=True)
        unused = chunk_id[0:1] >= n_used
        rel = jnp.where(unused, chunk_id[0:1] - n_used, rel)
        eid = jnp.where(unused, (SEG_SPARE_LANE + lax.rem(i, N_SPARE)).astype(f32), eid)
        row = jnp.concatenate([rel, eid], axis=1)
        tab_ref[i] = jnp.broadcast_to(row, (SUBLANES, 2 * TABLE_LANES)).astype(jnp.int32)
        return seen_chunks + run_chunks

    used_chunks = lax.fori_loop(0, t // tr, step, jnp.zeros((N_EXPERTS, 1), f32), unroll=8)

    nblk = jnp.floor((used_chunks + (chunks_per_block - 1)) * (1.0 / chunks_per_block))
    first_blk = expert_prefix(nblk)
    nact = jnp.sum(nblk, axis=0, keepdims=True)
    spare_chunk = _num_blocks(t) * chunks_per_block
    seg_row = (to_lanes(first_blk * chunks_per_block, 0) + to_lanes(used_chunks, N_EXPERTS)
               + to_lanes(nblk * chunks_per_block, 2 * N_EXPERTS)
               + jnp.where(lane[0:1] == SEG_NACT_LANE, nact, 0.0)
               + sum(jnp.where(lane[0:1] == SEG_SPARE_LANE + a,
                               float(spare_chunk + a * (_local_rows(tr) // RUN_ALIGN)), 0.0)
                     for a in range(N_SPARE)))
    seg_ref[...] = jnp.broadcast_to(seg_row, (SUBLANES, LANES)).astype(jnp.int32)

    tab = tab_ref[...].astype(f32)
    rel, eid = tab[:, :, :TABLE_LANES], tab[:, :, TABLE_LANES:]
    start = jnp.zeros_like(rel)
    for seg_lane in list(range(N_EXPERTS)) + [SEG_SPARE_LANE + a for a in range(N_SPARE)]:
        start = start + jnp.where(eid == seg_lane, seg_row[:, seg_lane:seg_lane + 1], 0.0)
    write_row = (start + rel) * RUN_ALIGN
    read_row = jnp.where(eid >= SEG_SPARE_LANE, rel, start + rel) * RUN_ALIGN
    tab_ref[...] = jnp.concatenate([write_row, read_row], axis=2).astype(jnp.int32)

    blk = lax.broadcasted_iota(jnp.int32, (N_EXPERTS, nbp), 1).astype(f32)
    owner = (first_blk <= blk) & (blk < first_blk + nblk)
    expert_id = lax.broadcasted_iota(jnp.int32, (N_EXPERTS, nbp), 0).astype(f32)
    bexp = jnp.sum(jnp.where(owner, expert_id, 0.0), axis=0, keepdims=True)
    row8 = lax.broadcasted_iota(jnp.int32, (SUBLANES, nbp), 0)
    meta_ref[...] = jnp.where(row8 == 0, bexp, jnp.where(row8 == 1, nact, 0.0)).astype(jnp.int32)


def _route(route):
    t = route.shape[1]
    nt = t // ROUTE_TILE
    nbp = -(-_num_blocks(t) // LANES) * LANES
    return pl.pallas_call(
        _route_kernel,
        grid=(1,),
        in_specs=[pl.BlockSpec(route.shape, lambda i: (0, 0))],
        out_specs=[pl.BlockSpec(route.shape, lambda i: (0, 0)),
                   pl.BlockSpec((nt, SUBLANES, 2 * TABLE_LANES), lambda i: (0, 0, 0)),
                   pl.BlockSpec((SUBLANES, LANES), lambda i: (0, 0)),
                   pl.BlockSpec((SUBLANES, nbp), lambda i: (0, 0))],
        out_shape=[jax.ShapeDtypeStruct(route.shape, jnp.int32),
                   jax.ShapeDtypeStruct((nt, SUBLANES, 2 * TABLE_LANES), jnp.int32),
                   jax.ShapeDtypeStruct((SUBLANES, LANES), jnp.int32),
                   jax.ShapeDtypeStruct((SUBLANES, nbp), jnp.int32)],
        compiler_params=pltpu.CompilerParams(
            dimension_semantics=("arbitrary",), vmem_limit_bytes=VMEM_LIMIT),
        name="route",
    )(route)


def _chunk_row(tab_ref, tile, k, reading):
    return pl.multiple_of(tab_ref[tile, 0, (TABLE_LANES if reading else 0) + k], RUN_ALIGN)


def _scatter_kernel(tab_ref, seg_ref, lpos_ref, x_ref, xs_ref, xsl, zrows, sem, zsem):
    tps, lrows = xsl.shape[1], xsl.shape[2]
    ts = x_ref.shape[0] // tps
    n_blocks = (xs_ref.shape[0] - N_SPARE * lrows) // EXPERT_BLOCK
    i = pl.program_id(0)
    last = pl.num_programs(0) - 1
    slot = lax.rem(i, 2)

    def drain(s):
        for tile in range(tps):
            pltpu.make_async_copy(xsl.at[s, tile], xs_ref.at[pl.ds(0, lrows)], sem.at[s]).wait()

    @pl.when(i >= 2)
    def _():
        drain(slot)

    j = lax.broadcasted_iota(jnp.int32, (lrows, ts), 0)
    for tile in range(tps):
        cols = slice(tile * ts, (tile + 1) * ts)
        perm = jnp.where((j == lpos_ref[0:1, cols]) | (j == lpos_ref[1:2, cols]), 1.0, 0.0).astype(bf16)
        xsl[slot, tile] = _pack_rows(jnp.dot(perm, x_ref[cols, :], preferred_element_type=f32), already_bf16=True)

    for tile in range(tps):
        for k in range(lrows // RUN_ALIGN):
            pltpu.make_async_copy(xsl.at[slot, tile, pl.ds(k * RUN_ALIGN, RUN_ALIGN)],
                                  xs_ref.at[pl.ds(_chunk_row(tab_ref, tile, k, False), RUN_ALIGN)],
                                  sem.at[slot]).start()

    @pl.when(i == last)
    def _():
        drain(slot)

        @pl.when(i >= 1)
        def _():
            drain(1 - slot)

        zrows[...] = jnp.zeros(zrows.shape, zrows.dtype)

        def zcopy(row, n):
            return pltpu.make_async_copy(zrows.at[pl.ds(0, n)], xs_ref.at[pl.ds(row, n)], zsem)

        def for_each_unused_block(fn):
            def body(b, carry):
                fn(pl.multiple_of(b * EXPERT_BLOCK, EXPERT_BLOCK), EXPERT_BLOCK)
                return carry

            lax.fori_loop(seg_ref[0, SEG_NACT_LANE], n_blocks, body, 0)
            for off in range(0, N_SPARE * lrows, EXPERT_BLOCK):
                fn(n_blocks * EXPERT_BLOCK + off, min(EXPERT_BLOCK, N_SPARE * lrows - off))

        for_each_unused_block(lambda row, n: zcopy(row, n).start())
        for_each_unused_block(lambda row, n: zcopy(row, n).wait())

        def for_each_pad_chunk(fn):
            def seg(e, carry):
                g0 = seg_ref[0, e]
                used = seg_ref[0, N_EXPERTS + e]
                total = seg_ref[0, 2 * N_EXPERTS + e]

                def chunk(c, carry2):
                    fn(pl.multiple_of((g0 + c) * RUN_ALIGN, RUN_ALIGN), RUN_ALIGN)
                    return carry2

                def coarse(c, carry2):
                    fn(pl.multiple_of((g0 + c * PAD_COARSE) * RUN_ALIGN, PAD_COARSE * RUN_ALIGN),
                       PAD_COARSE * RUN_ALIGN)
                    return carry2

                aligned = lax.div(used + (PAD_COARSE - 1), PAD_COARSE)
                lax.fori_loop(used, aligned * PAD_COARSE, chunk, 0)
                lax.fori_loop(aligned, lax.div(total, PAD_COARSE), coarse, 0)
                return carry

            lax.fori_loop(0, N_EXPERTS, seg, 0)

        for_each_pad_chunk(lambda row, n: zcopy(row, n).start())
        for_each_pad_chunk(lambda row, n: zcopy(row, n).wait())


def _scatter(tab, seg, lpos, xn2, n_blocks):
    t = xn2.shape[0]
    tps = TILES_PER_STEP
    ts = ROUTE_TILE * tps
    lrows = _local_rows(ROUTE_TILE)
    return pl.pallas_call(
        _scatter_kernel,
        grid=(t // ts,),
        in_specs=[
            pl.BlockSpec((tps, SUBLANES, 2 * TABLE_LANES), lambda i: (i, 0, 0), memory_space=pltpu.SMEM),
            pl.BlockSpec((SUBLANES, LANES), lambda i: (0, 0), memory_space=pltpu.SMEM),
            pl.BlockSpec((SUBLANES, ts), lambda i: (0, i)),
            pl.BlockSpec((ts, D_MODEL), lambda i: (i, 0)),
        ],
        out_specs=pl.BlockSpec(memory_space=pl.ANY),
        out_shape=jax.ShapeDtypeStruct((n_blocks * EXPERT_BLOCK + N_SPARE * lrows, PACKED), jnp.uint32),
        scratch_shapes=[pltpu.VMEM((2, tps, lrows, PACKED), jnp.uint32),
                        pltpu.VMEM((EXPERT_BLOCK, PACKED), jnp.uint32),
                        pltpu.SemaphoreType.DMA((2,)), pltpu.SemaphoreType.DMA],
        compiler_params=pltpu.CompilerParams(
            dimension_semantics=("arbitrary",), vmem_limit_bytes=VMEM_LIMIT, has_side_effects=True),
        name="scatter",
    )(tab, seg, lpos, xn2)


def _expert_kernel(bexp_ref, nact_ref, xs_ref, wg_ref, wu_ref, wd_ref, ys_ref, wg_b, wu_b, wd_b):
    b = pl.program_id(0)
    active = b < nact_ref[0]
    new_expert = jnp.logical_or(b == 0, bexp_ref[b] != bexp_ref[jnp.maximum(b - 1, 0)])

    @pl.when(jnp.logical_not(active))
    def _():
        ys_ref[...] = jnp.zeros(ys_ref.shape, ys_ref.dtype)

    def swiglu(wg, wu, wd):
        x = jnp.concatenate(_unpack_rows(xs_ref[...]), axis=1)
        g = jnp.dot(x, wg, preferred_element_type=f32)
        hmid = (g * _sigmoid(g) * jnp.dot(x, wu, preferred_element_type=f32)).astype(bf16)
        ys_ref[...] = _pack_rows(jnp.dot(hmid, wd, preferred_element_type=f32))

    @pl.when(jnp.logical_and(active, new_expert))
    def _():
        wg, wu, wd = (r[0].astype(bf16) for r in (wg_ref, wu_ref, wd_ref))
        wg_b[...] = wg
        wu_b[...] = wu
        wd_b[...] = wd
        swiglu(wg, wu, wd)

    @pl.when(jnp.logical_and(active, jnp.logical_not(new_expert)))
    def _():
        swiglu(wg_b[...], wu_b[...], wd_b[...])


def _experts(bexp, nact, xs, wg, wu, wd, nb):
    n_rows = nb * EXPERT_BLOCK

    def blk(b, nact_ref):
        return jnp.minimum(b, nact_ref[0] - 1)

    grid_spec = pltpu.PrefetchScalarGridSpec(
        num_scalar_prefetch=2,
        grid=(nb,),
        in_specs=[
            pl.BlockSpec((EXPERT_BLOCK, PACKED), lambda b, e, n: (blk(b, n), 0)),
            pl.BlockSpec((1, D_MODEL, D_FF_EXPERT), lambda b, e, n: (e[blk(b, n)], 0, 0)),
            pl.BlockSpec((1, D_MODEL, D_FF_EXPERT), lambda b, e, n: (e[blk(b, n)], 0, 0)),
            pl.BlockSpec((1, D_FF_EXPERT, D_MODEL), lambda b, e, n: (e[blk(b, n)], 0, 0)),
        ],
        out_specs=pl.BlockSpec((EXPERT_BLOCK, PACKED), lambda b, e, n: (b, 0)),
        scratch_shapes=[pltpu.VMEM((D_MODEL, D_FF_EXPERT), bf16), pltpu.VMEM((D_MODEL, D_FF_EXPERT), bf16),
                        pltpu.VMEM((D_FF_EXPERT, D_MODEL), bf16)],
    )
    return pl.pallas_call(
        _expert_kernel,
        grid_spec=grid_spec,
        out_shape=jax.ShapeDtypeStruct((n_rows, PACKED), jnp.uint32),
        compiler_params=pltpu.CompilerParams(
            dimension_semantics=("arbitrary",), vmem_limit_bytes=VMEM_LIMIT),
        name="experts",
    )(bexp, nact, xs, wg, wu, wd)


def _combine_kernel(tab_ref, tab_next_ref, lpos_ref, wts_ref, h_ref, gf_ref, ys_ref, out_ref, ybuf, sem):
    tps, lrows = ybuf.shape[1], ybuf.shape[2]
    ts = h_ref.shape[0] // tps
    i = pl.program_id(0)
    slot = lax.rem(i, 2)

    def fetch(t_ref, s):
        for tile in range(tps):
            for k in range(lrows // RUN_ALIGN):
                pltpu.make_async_copy(ys_ref.at[pl.ds(_chunk_row(t_ref, tile, k, True), RUN_ALIGN)],
                                      ybuf.at[s, tile, pl.ds(k * RUN_ALIGN, RUN_ALIGN)], sem.at[s]).start()

    @pl.when(i == 0)
    def _():
        fetch(tab_ref, 0)

    @pl.when(i + 1 < pl.num_programs(0))
    def _():
        fetch(tab_next_ref, 1 - slot)

    for tile in range(tps):
        pltpu.make_async_copy(ys_ref.at[pl.ds(0, lrows)], ybuf.at[slot, tile], sem.at[slot]).wait()

    jl = lax.broadcasted_iota(jnp.int16, (ts, lrows), 1)
    zero = jnp.zeros((), bf16)
    for tile in range(tps):
        cols = slice(tile * ts, (tile + 1) * ts)
        info = jnp.concatenate([lpos_ref[:, cols].astype(f32), wts_ref[:, cols],
                                jnp.zeros((LANES - 2 * SUBLANES, ts), f32)], axis=0).T
        mix = (jnp.where(jl == info[:, 0:1].astype(jnp.int16), info[:, SUBLANES:SUBLANES + 1].astype(bf16), zero)
               + jnp.where(jl == info[:, 1:2].astype(jnp.int16), info[:, SUBLANES + 1:SUBLANES + 2].astype(bf16),
                           zero))
        moe = jnp.concatenate([jnp.dot(mix, y, preferred_element_type=f32)
                               for y in _unpack_rows(ybuf[slot, tile])], axis=1)
        out_ref[cols, :] = _rms(h_ref[cols, :] + moe, gf_ref[...])


def _combine(tab, lpos, wts, h, g_final, ys):
    t = h.shape[0]
    tps = TILES_PER_STEP
    ts = ROUTE_TILE * tps
    nt = t // ts
    lrows = _local_rows(ROUTE_TILE)
    assert ys.shape[0] >= lrows
    return pl.pallas_call(
        _combine_kernel,
        grid=(nt,),
        in_specs=[
            pl.BlockSpec((tps, SUBLANES, 2 * TABLE_LANES), lambda i: (i, 0, 0), memory_space=pltpu.SMEM),
            pl.BlockSpec((tps, SUBLANES, 2 * TABLE_LANES), lambda i: (jnp.minimum(i + 1, nt - 1), 0, 0),
                         memory_space=pltpu.SMEM),
            pl.BlockSpec((SUBLANES, ts), lambda i: (0, i)),
            pl.BlockSpec((SUBLANES, ts), lambda i: (0, i)),
            pl.BlockSpec((ts, D_MODEL), lambda i: (i, 0)),
            pl.BlockSpec((1, D_MODEL), lambda i: (0, 0)),
            pl.BlockSpec(memory_space=pl.ANY),
        ],
        out_specs=pl.BlockSpec((ts, D_MODEL), lambda i: (i, 0)),
        out_shape=jax.ShapeDtypeStruct((t, D_MODEL), f32),
        scratch_shapes=[pltpu.VMEM((2, tps, lrows, PACKED), jnp.uint32), pltpu.SemaphoreType.DMA((2,))],
        compiler_params=pltpu.CompilerParams(
            dimension_semantics=("arbitrary",), vmem_limit_bytes=VMEM_LIMIT),
        name="combine",
    )(tab, tab, lpos, wts, h, g_final, ys)


def _prep_router(w_group, b_group, w_expert, b_expert):
    d = w_group.shape[0]
    w = jnp.zeros((d, LANES), f32)
    w = w.at[:, :N_GROUPS].set(w_group).at[:, SUBLANES:ROUTER_ROWS].set(w_expert)
    b = jnp.full((1, LANES), NEG_BIG, f32)
    b = b.at[0, :N_GROUPS].set(b_group).at[0, SUBLANES:ROUTER_ROWS].set(b_expert)
    return w.astype(bf16), b


def kernel(x, g_mix, w_in, w_dw, b_dw, ln_conv_g, ln_conv_b, sinks, w_conv_out, w_attn_out, w_out, g_ffn,
           w_group, b_group, w_expert, b_expert, w_gate, w_up, w_down, g_final):
    batch, seq, d = x.shape
    assert d == D_MODEL and seq % MIXER_TILE == 0 and seq % TOKEN_TILE == 0 and g_mix.shape[0] == 1
    assert (batch * seq) % (ROUTE_TILE * TILES_PER_STEP) == 0
    t = batch * seq
    x2 = x.reshape(t, d)

    act, q, kv, sgates = _inproj(x2, g_mix[0][None, :], w_in[0], w_dw[0, :, 0, :], b_dw[0][None, :],
                                 ln_conv_g[0][None, :], ln_conv_b[0][None, :], seq)

    w_r, b_r = _prep_router(w_group[0], b_group[0], w_expert[0], b_expert[0])
    h, xn2, route, wts = _mixer(
        x2, act, q, kv, sgates, sinks[0], w_conv_out[0].astype(bf16), w_attn_out[0].astype(bf16),
        w_out[0].astype(bf16), g_ffn[0][None, :], w_r, b_r, batch, seq)

    lpos, tab, seg, meta = _route(route)
    nb = _num_blocks(t)
    xs = _scatter(tab, seg, lpos, xn2, nb)
    ys = _experts(meta[0, :nb], meta[1, :1], xs, w_gate[0], w_up[0], w_down[0], nb)
    out = _combine(tab, lpos, wts, h, g_final[None, :], ys)
    return out.reshape(batch, seq, d)
```

```python
import functools

import numpy as np
import jax
import jax.numpy as jnp
from jax import lax
from jax.experimental import pallas as pl
from jax.experimental.pallas import tpu as pltpu

D_MODEL = 1024
CONV_CH = 512
CONV_WIDTH = 31
N_HEADS = 8
N_KV_HEADS = 2
HEAD_DIM = 64
ATTN_BLOCK = 128
N_GROUPS = 4
EXPERTS_PER_GROUP = 8
N_EXPERTS = N_GROUPS * EXPERTS_PER_GROUP
D_FF_EXPERT = 512
NORM_EPS = 1e-6

Q_DIM = N_HEADS * HEAD_DIM
KV_DIM = N_KV_HEADS * HEAD_DIM

LANES = 128
SUBLANES = 8
CONV_HALO = 32
CONV_ROWS = 32
PROJ_PIECE = 256
ROUTER_ROWS = SUBLANES + N_EXPERTS
NEG_BIG = -1e30

TOKEN_TILE = 512
MIXER_TILE = 1024
ROUTE_TILE = 512
RUN_ALIGN = SUBLANES
EXPERT_BLOCK = 1024
VMEM_LIMIT = 58 * 1024 * 1024

f32 = jnp.float32
bf16 = jnp.bfloat16


def _rms(x, g):
    ms = jnp.mean(x * x, axis=-1, keepdims=True)
    return x * lax.rsqrt(ms + NORM_EPS) * g


def _sigmoid(x):
    return 1.0 / (1.0 + jnp.exp(-x))


def _exact_zero(x):
    bits = pltpu.bitcast(x, jnp.uint32)
    sixteen = jnp.uint32(16)
    return pltpu.bitcast(lax.shift_right_logical(lax.shift_right_logical(bits, sixteen), sixteen), f32)


PACKED = D_MODEL // 2
_HIGH_HALF = 0xFFFF0000


def _pack_rows(x, already_bf16=False):
    def bits(v):
        return pltpu.bitcast(v if already_bf16 else v.astype(bf16).astype(f32), jnp.uint32)
    low = lax.shift_right_logical(bits(x[:, :PACKED]), jnp.uint32(16))
    return low | (bits(x[:, PACKED:]) & jnp.uint32(_HIGH_HALF))


def _unpack_rows(p):
    low = pltpu.bitcast(lax.shift_left(p, jnp.uint32(16)), f32).astype(bf16)
    high = pltpu.bitcast(p & jnp.uint32(_HIGH_HALF), f32).astype(bf16)
    return low, high


def _load_w_in(w_hbm, stage, w_ref, sem):
    copy = pltpu.make_async_copy(w_hbm, stage, sem)
    copy.start()
    copy.wait()

    def put(dst, value):
        w_ref[:, dst:dst + value.shape[1]] = value.astype(bf16)

    for b in range(CONV_CH // LANES):
        put(2 * b * LANES, stage[:, b * LANES:(b + 1) * LANES])
        put((2 * b + 1) * LANES, stage[:, CONV_CH + b * LANES:CONV_CH + (b + 1) * LANES])
    src = dst = 2 * CONV_CH
    for lo in range(0, Q_DIM, LANES):
        put(dst + lo, stage[:, src + lo:src + lo + LANES] * (HEAD_DIM ** -0.5))
    src, dst = src + Q_DIM, dst + Q_DIM
    for _ in range(2):
        for h in range(N_KV_HEADS):
            head = stage[:, src + h * HEAD_DIM:src + (h + 1) * HEAD_DIM]
            put(dst + 2 * h * HEAD_DIM, jnp.concatenate([head, head], axis=1))
        src, dst = src + KV_DIM, dst + 2 * KV_DIM
    for lo in range(0, 2 * D_MODEL, LANES):
        put(dst + lo, stage[:, src + lo:src + lo + LANES])


def _inproj_kernel(tiles_per_seq, x_ref, g_ref, w_hbm, wdw_ref, bdw_ref, lng_ref, lnb_ref,
                   act_ref, q_ref, kv_ref, sgate_ref, xn_s, vbuf, w_stage, w_ref, w_sem):
    tm = x_ref.shape[0]

    @pl.when(pl.program_id(0) == 0)
    def _():
        _load_w_in(w_hbm, w_stage, w_ref, w_sem)

    xn_s[...] = _rms(x_ref[...], g_ref[...]).astype(bf16)

    def proj(lo, hi):
        return jnp.dot(xn_s[...], w_ref[:, lo:hi], preferred_element_type=f32)

    n_cb = CONV_CH // LANES
    first = lax.rem(pl.program_id(0), tiles_per_seq) == 0

    @pl.when(first)
    def _():
        vbuf[:, 0:CONV_HALO, :] = jnp.zeros((n_cb, CONV_HALO, LANES), f32)

    @pl.when(jnp.logical_not(first))
    def _():
        vbuf[:, 0:CONV_HALO, :] = vbuf[:, tm:tm + CONV_HALO, :]

    for b in range(n_cb):
        u = proj(2 * b * LANES, 2 * (b + 1) * LANES)
        vbuf[b, CONV_HALO:, :] = u[:, :LANES] * _sigmoid(u[:, LANES:])

    pieces, col = [], 2 * CONV_CH
    for out_ref, fn in ((q_ref, lambda z: z), (kv_ref, lambda z: z), (sgate_ref, _sigmoid)):
        pieces += [(out_ref, col, lo, fn) for lo in range(0, out_ref.shape[1], PROJ_PIECE)]
        col += out_ref.shape[1]

    rows = CONV_ROWS
    n_steps = tm // rows
    lag = n_steps - len(pieces)
    assert lag >= 0
    tap0 = CONV_HALO - (CONV_WIDTH - 1)

    def tie_rows(x):
        return jnp.concatenate([_exact_zero(x[0:SUBLANES, 0:LANES])] * (rows // SUBLANES), axis=0)

    tie = None
    for c in range(n_steps):
        if c >= lag:
            out_ref, col, lo, fn = pieces[c - lag]
            z = proj(col + lo, col + lo + PROJ_PIECE)
            out_ref[:, lo:lo + PROJ_PIECE] = fn(z).astype(bf16)
            tie = tie_rows(z) if tie is None else tie + tie_rows(z)
        accs = []
        for b in range(n_cb):
            acc = jnp.broadcast_to(bdw_ref[:, b * LANES:(b + 1) * LANES], (rows, LANES))
            if tie is not None:
                acc = acc + tie
            for j in range(CONV_WIDTH):
                r0 = c * rows + tap0 + j
                acc = acc + wdw_ref[j:j + 1, b * LANES:(b + 1) * LANES] * vbuf[b, r0:r0 + rows, :]
            accs.append(acc)
            tie = tie_rows(sum(acc[k:k + SUBLANES] for k in range(0, rows, SUBLANES)))
        mu = sum(jnp.sum(a, axis=-1, keepdims=True) for a in accs) * (1.0 / CONV_CH)
        ds = [a - mu for a in accs]
        var = sum(jnp.sum(d * d, axis=-1, keepdims=True) for d in ds) * (1.0 / CONV_CH)
        inv = lax.rsqrt(var + NORM_EPS)
        for b in range(n_cb):
            y = ds[b] * inv * lng_ref[:, b * LANES:(b + 1) * LANES] + lnb_ref[:, b * LANES:(b + 1) * LANES]
            act_ref[c * rows:(c + 1) * rows, b * LANES:(b + 1) * LANES] = (y * _sigmoid(y)).astype(bf16)


def _inproj(x2, g_mix, w_in, w_dw, b_dw, ln_g, ln_b, seq):
    t = x2.shape[0]
    tm = TOKEN_TILE
    widths = (CONV_CH, Q_DIM, 4 * KV_DIM, 2 * D_MODEL)

    def full(a):
        return pl.BlockSpec(a.shape, lambda i: (0,) * a.ndim)

    return pl.pallas_call(
        functools.partial(_inproj_kernel, seq // tm),
        grid=(t // tm,),
        in_specs=[pl.BlockSpec((tm, D_MODEL), lambda i: (i, 0)),
                  full(g_mix), pl.BlockSpec(memory_space=pl.ANY), full(w_dw), full(b_dw), full(ln_g), full(ln_b)],
        out_specs=[pl.BlockSpec((tm, w), lambda i: (i, 0)) for w in widths],
        out_shape=[jax.ShapeDtypeStruct((t, w), bf16) for w in widths],
        scratch_shapes=[pltpu.VMEM((tm, D_MODEL), bf16),
                        pltpu.VMEM((CONV_CH // LANES, tm + CONV_HALO, LANES), f32),
                        pltpu.VMEM(w_in.shape, f32),
                        pltpu.VMEM((D_MODEL, sum(widths) + CONV_CH), bf16),
                        pltpu.SemaphoreType.DMA],
        compiler_params=pltpu.CompilerParams(
            dimension_semantics=("arbitrary",), vmem_limit_bytes=VMEM_LIMIT),
        name="inproj",
    )(x2, g_mix, w_in, w_dw, b_dw, ln_g, ln_b)


def _mixer_kernel(x_ref, act_ref, q_ref, kv_ref, kvp_ref, sgate_ref, sink_ref, bias_ref, wco_ref, wao_ref,
                  wo_ref, gffn_ref, wr_ref, br_ref,
                  h_ref, xn2_ref, route_ref, wts_ref,
                  kvall, attn, s_scr, m_scr):
    ts = x_ref.shape[0]
    first = pl.program_id(1) == 0

    kvall[0:ATTN_BLOCK, :] = jnp.where(first, jnp.zeros_like(kvp_ref[...]), kvp_ref[...])
    kvall[ATTN_BLOCK:, :] = kv_ref[...]
    nkeys = 2 * ATTN_BLOCK
    left_kv = lax.broadcasted_iota(jnp.int32, (nkeys, LANES), 1) < HEAD_DIM
    left_q = lax.broadcasted_iota(jnp.int32, (ATTN_BLOCK, LANES), 1) < HEAD_DIM
    key_lane = lax.broadcasted_iota(jnp.int32, (ATTN_BLOCK, 2 * nkeys), 1)
    prev_keys = (key_lane % nkeys) < ATTN_BLOCK
    no_prev = jnp.where(prev_keys, jnp.where(first, -jnp.inf, 0.0), 0.0)

    def block_diag(x):
        z = jnp.zeros_like(x)
        return jnp.concatenate([jnp.where(left_kv, x, z), jnp.where(left_kv, z, x)], axis=0)

    n_pairs = N_HEADS // 2
    for j in range(ts // ATTN_BLOCK):
        r0 = j * ATTN_BLOCK
        for kvh in range(N_KV_HEADS):
            kbd = block_diag(kvall[r0:r0 + nkeys, kvh * LANES:(kvh + 1) * LANES])
            for ii in range(2):
                i = 2 * kvh + ii
                qb = q_ref[r0:r0 + ATTN_BLOCK, i * LANES:(i + 1) * LANES]
                s = lax.dot_general(qb, kbd, (((1,), (1,)), ((), ())), preferred_element_type=f32)
                s = s + bias_ref[i]
                if j == 0:
                    s = s + no_prev
                s_scr[j * n_pairs + i] = s
                for half in range(2):
                    m = jnp.max(s[:, half * nkeys:(half + 1) * nkeys], axis=-1, keepdims=True)
                    m_scr[2 * (j * n_pairs + i) + half] = jnp.maximum(m, sink_ref[2 * i + half])

    for j in range(ts // ATTN_BLOCK):
        r0 = j * ATTN_BLOCK
        for kvh in range(N_KV_HEADS):
            vbd = block_diag(kvall[r0:r0 + nkeys, (N_KV_HEADS + kvh) * LANES:(N_KV_HEADS + kvh + 1) * LANES])
            for ii in range(2):
                i = 2 * kvh + ii
                s = s_scr[j * n_pairs + i]
                ps, ls = [], []
                for half in range(2):
                    m = m_scr[2 * (j * n_pairs + i) + half]
                    p = jnp.exp(s[:, half * nkeys:(half + 1) * nkeys] - m)
                    ps.append(p)
                    ls.append(jnp.sum(p, axis=-1, keepdims=True) + jnp.exp(sink_ref[2 * i + half] - m))
                p = jnp.concatenate(ps, axis=1).astype(bf16)
                o = jnp.dot(p, vbd, preferred_element_type=f32)
                o = o * jnp.where(left_q, 1.0 / ls[0], 1.0 / ls[1])
                attn[r0:r0 + ATTN_BLOCK, i * LANES:(i + 1) * LANES] = o.astype(bf16)

    conv_o = jnp.dot(act_ref[...], wco_ref[...], preferred_element_type=f32)
    attn_o = jnp.dot(attn[...], wao_ref[...], preferred_element_type=f32)
    merged = (sgate_ref[:, :D_MODEL].astype(f32) * conv_o
              + sgate_ref[:, D_MODEL:].astype(f32) * attn_o).astype(bf16)
    h = x_ref[...] + jnp.dot(merged, wo_ref[...], preferred_element_type=f32)
    h_ref[...] = h

    xn2 = _rms(h, gffn_ref[...]).astype(bf16)
    xn2_ref[...] = xn2
    logits = jnp.dot(xn2, wr_ref[...], preferred_element_type=f32) + br_ref[...]
    lt = logits.T
    sub = lax.broadcasted_iota(jnp.int32, (SUBLANES, ts), 0)

    gl = lt[0:SUBLANES]
    gmax = jnp.max(gl, axis=0, keepdims=True)
    gsel = jnp.min(jnp.where(gl == gmax, sub, SUBLANES), axis=0, keepdims=True)
    p_group = 1.0 / jnp.sum(jnp.exp(gl - gmax), axis=0, keepdims=True)

    e_in = lt[SUBLANES:2 * SUBLANES]
    for g in range(1, N_GROUPS):
        e_in = jnp.where(gsel == g, lt[(g + 1) * SUBLANES:(g + 2) * SUBLANES], e_in)
    m1 = jnp.max(e_in, axis=0, keepdims=True)
    i1 = jnp.min(jnp.where(e_in == m1, sub, SUBLANES), axis=0, keepdims=True)
    rest = jnp.where(sub == i1, -jnp.inf, e_in)
    m2 = jnp.max(rest, axis=0, keepdims=True)
    i2 = jnp.min(jnp.where(rest == m2, sub, SUBLANES), axis=0, keepdims=True)
    t2 = jnp.exp(m2 - m1)
    w1 = p_group / (1.0 + t2)
    w2 = p_group * t2 / (1.0 + t2)
    base = gsel * EXPERTS_PER_GROUP
    route_ref[...] = jnp.where(sub == 0, base + i1, jnp.where(sub == 1, base + i2, 0))
    wts_ref[...] = jnp.where(sub == 0, w1, jnp.where(sub == 1, w2, 0.0))


def _attn_bias():
    qi = np.arange(ATTN_BLOCK)[:, None]
    kj = np.arange(2 * ATTN_BLOCK)[None, :]
    rel = (ATTN_BLOCK + qi - kj).astype(np.float32)
    ok = (rel >= 0) & (rel < ATTN_BLOCK)
    slopes = np.array([2.0 ** (-8.0 * (h + 1) / N_HEADS) for h in range(N_HEADS)], np.float32)
    per_head = [np.where(ok, -(slopes[h] * rel), -np.inf).astype(np.float32) for h in range(N_HEADS)]
    return np.stack([np.concatenate([per_head[2 * i], per_head[2 * i + 1]], axis=1)
                     for i in range(N_HEADS // 2)])


def _mixer(x2, act, q, kv, sgates, sinks, wco, wao, wo, g_ffn, w_r, b_r, batch, seq):
    t = x2.shape[0]
    ts = MIXER_TILE
    ns = seq // ts
    bias = jnp.asarray(_attn_bias())

    def row(b, s):
        return b * ns + s

    def full(a):
        return pl.BlockSpec(a.shape, lambda b, s: (0,) * a.ndim, pipeline_mode=pl.Buffered(1))

    in_specs = [
        pl.BlockSpec((ts, D_MODEL), lambda b, s: (row(b, s), 0)),
        pl.BlockSpec((ts, CONV_CH), lambda b, s: (row(b, s), 0)),
        pl.BlockSpec((ts, Q_DIM), lambda b, s: (row(b, s), 0)),
        pl.BlockSpec((ts, 4 * KV_DIM), lambda b, s: (row(b, s), 0)),
        pl.BlockSpec((ATTN_BLOCK, 4 * KV_DIM),
                     lambda b, s: (jnp.maximum(row(b, s) * (ts // ATTN_BLOCK) - 1, 0), 0)),
        pl.BlockSpec((ts, 2 * D_MODEL), lambda b, s: (row(b, s), 0)),
        pl.BlockSpec(memory_space=pltpu.SMEM),
        full(bias), full(wco), full(wao), full(wo), full(g_ffn), full(w_r), full(b_r),
    ]
    out_specs = [
        pl.BlockSpec((ts, D_MODEL), lambda b, s: (row(b, s), 0)),
        pl.BlockSpec((ts, D_MODEL), lambda b, s: (row(b, s), 0)),
        pl.BlockSpec((SUBLANES, ts), lambda b, s: (0, row(b, s))),
        pl.BlockSpec((SUBLANES, ts), lambda b, s: (0, row(b, s))),
    ]
    out_shape = [
        jax.ShapeDtypeStruct((t, D_MODEL), f32),
        jax.ShapeDtypeStruct((t, D_MODEL), bf16),
        jax.ShapeDtypeStruct((SUBLANES, t), jnp.int32),
        jax.ShapeDtypeStruct((SUBLANES, t), f32),
    ]
    return pl.pallas_call(
        _mixer_kernel,
        grid=(batch, ns),
        in_specs=in_specs,
        out_specs=out_specs,
        out_shape=out_shape,
        scratch_shapes=[
            pltpu.VMEM((ts + ATTN_BLOCK, 4 * KV_DIM), bf16),
            pltpu.VMEM((ts, Q_DIM), bf16),
            pltpu.VMEM((ts // ATTN_BLOCK * (N_HEADS // 2), ATTN_BLOCK, 4 * ATTN_BLOCK), f32),
            pltpu.VMEM((ts // ATTN_BLOCK * N_HEADS, ATTN_BLOCK, 1), f32),
        ],
        compiler_params=pltpu.CompilerParams(
            dimension_semantics=("arbitrary", "arbitrary"), vmem_limit_bytes=VMEM_LIMIT),
        name="mixer",
    )(x2, act, q, kv, kv, sgates, sinks, bias, wco, wao, wo, g_ffn, w_r, b_r)


def _local_rows(ts):
    return -(-(2 * ts + (RUN_ALIGN - 1) * N_EXPERTS) // LANES) * LANES


PAD_COARSE = 16
assert (EXPERT_BLOCK // RUN_ALIGN) % PAD_COARSE == 0
TABLE_LANES = 2 * LANES
SEG_NACT_LANE = 3 * N_EXPERTS
SEG_SPARE_LANE = 3 * N_EXPERTS + 1
TILES_PER_STEP = 2
N_SPARE = 2 * TILES_PER_STEP


def _num_blocks(t):
    run_rows = 2 * t + (RUN_ALIGN - 1) * N_EXPERTS * (t // ROUTE_TILE)
    return -(-(run_rows + N_EXPERTS * (EXPERT_BLOCK - RUN_ALIGN)) // EXPERT_BLOCK)


def _route_kernel(route_ref, lpos_ref, tab_ref, seg_ref, meta_ref):
    t = route_ref.shape[1]
    tr = ROUTE_TILE
    nbp = meta_ref.shape[1]
    chunks_per_block = EXPERT_BLOCK // RUN_ALIGN
    eiota = lax.broadcasted_iota(jnp.int32, (N_EXPERTS, tr), 0)
    before = (lax.broadcasted_iota(jnp.int32, (tr, tr), 0)
              < lax.broadcasted_iota(jnp.int32, (tr, tr), 1)).astype(bf16)
    lower = (lax.broadcasted_iota(jnp.int32, (N_EXPERTS, N_EXPERTS), 1)
             < lax.broadcasted_iota(jnp.int32, (N_EXPERTS, N_EXPERTS), 0)).astype(bf16)
    sub = lax.broadcasted_iota(jnp.int32, (N_EXPERTS, LANES), 0)
    lane = lax.broadcasted_iota(jnp.int32, (N_EXPERTS, LANES), 1)

    def to_lanes(col, offset):
        return jnp.sum(jnp.where(sub + offset == lane, col, 0.0), axis=0, keepdims=True)

    def expert_prefix(col):
        b = jnp.broadcast_to(col, (N_EXPERTS, LANES))
        hi = jnp.floor(b * (1.0 / 16.0))
        lo = b - 16.0 * hi
        return (16.0 * jnp.dot(lower, hi.astype(bf16), preferred_element_type=f32)
                + jnp.dot(lower, lo.astype(bf16), preferred_element_type=f32))[:, 0:1]

    lpos_ref[...] = jnp.zeros(lpos_ref.shape, jnp.int32)
    chunk_id = lax.broadcasted_iota(jnp.int32, (N_EXPERTS, TABLE_LANES), 1).astype(f32)
    chunk_expert = lax.broadcasted_iota(jnp.int32, (N_EXPERTS, TABLE_LANES), 0).astype(f32)

    def step(i, seen_chunks):
        off = pl.multiple_of(i * tr, tr)
        m1 = eiota == route_ref[0:1, pl.ds(off, tr)]
        m2 = eiota == route_ref[1:2, pl.ds(off, tr)]
        onehot = jnp.where(m1 | m2, 1.0, 0.0)
        within = jnp.dot(onehot.astype(bf16), before, preferred_element_type=f32)
        run_chunks = jnp.floor((jnp.sum(onehot, axis=1, keepdims=True) + (RUN_ALIGN - 1)) * (1.0 / RUN_ALIGN))
        run_start = expert_prefix(run_chunks)
        pos = within + RUN_ALIGN * run_start
        lpos_ref[0:1, pl.ds(off, tr)] = jnp.sum(jnp.where(m1, pos, 0.0), axis=0, keepdims=True).astype(jnp.int32)
        lpos_ref[1:2, pl.ds(off, tr)] = jnp.sum(jnp.where(m2, pos, 0.0), axis=0, keepdims=True).astype(jnp.int32)
        owner = (run_start <= chunk_id) & (chunk_id < run_start + run_chunks)
        rel = jnp.sum(jnp.where(owner, seen_chunks + chunk_id - run_start, 0.0), axis=0, keepdims=True)
        eid = jnp.sum(jnp.where(owner, chunk_expert, 0.0), axis=0, keepdims=True)
        n_used = jnp.sum(run_chunks, axis=0, keepdims=True)
        unused = chunk_id[0:1] >= n_used
        rel = jnp.where(unused, chunk_id[0:1] - n_used, rel)
        eid = jnp.where(unused, (SEG_SPARE_LANE + lax.rem(i, N_SPARE)).astype(f32), eid)
        row = jnp.concatenate([rel, eid], axis=1)
        tab_ref[i] = jnp.broadcast_to(row, (SUBLANES, 2 * TABLE_LANES)).astype(jnp.int32)
        return seen_chunks + run_chunks

    used_chunks = lax.fori_loop(0, t // tr, step, jnp.zeros((N_EXPERTS, 1), f32), unroll=8)

    nblk = jnp.floor((used_chunks + (chunks_per_block - 1)) * (1.0 / chunks_per_block))
    first_blk = expert_prefix(nblk)
    nact = jnp.sum(nblk, axis=0, keepdims=True)
    spare_chunk = _num_blocks(t) * chunks_per_block
    seg_row = (to_lanes(first_blk * chunks_per_block, 0) + to_lanes(used_chunks, N_EXPERTS)
               + to_lanes(nblk * chunks_per_block, 2 * N_EXPERTS)
               + jnp.where(lane[0:1] == SEG_NACT_LANE, nact, 0.0)
               + sum(jnp.where(lane[0:1] == SEG_SPARE_LANE + a,
                               float(spare_chunk + a * (_local_rows(tr) // RUN_ALIGN)), 0.0)
                     for a in range(N_SPARE)))
    seg_ref[...] = jnp.broadcast_to(seg_row, (SUBLANES, LANES)).astype(jnp.int32)

    tab = tab_ref[...].astype(f32)
    rel, eid = tab[:, :, :TABLE_LANES], tab[:, :, TABLE_LANES:]
    start = jnp.zeros_like(rel)
    for seg_lane in list(range(N_EXPERTS)) + [SEG_SPARE_LANE + a for a in range(N_SPARE)]:
        start = start + jnp.where(eid == seg_lane, seg_row[:, seg_lane:seg_lane + 1], 0.0)
    write_row = (start + rel) * RUN_ALIGN
    read_row = jnp.where(eid >= SEG_SPARE_LANE, rel, start + rel) * RUN_ALIGN
    tab_ref[...] = jnp.concatenate([write_row, read_row], axis=2).astype(jnp.int32)

    blk = lax.broadcasted_iota(jnp.int32, (N_EXPERTS, nbp), 1).astype(f32)
    owner = (first_blk <= blk) & (blk < first_blk + nblk)
    expert_id = lax.broadcasted_iota(jnp.int32, (N_EXPERTS, nbp), 0).astype(f32)
    bexp = jnp.sum(jnp.where(owner, expert_id, 0.0), axis=0, keepdims=True)
    row8 = lax.broadcasted_iota(jnp.int32, (SUBLANES, nbp), 0)
    meta_ref[...] = jnp.where(row8 == 0, bexp, jnp.where(row8 == 1, nact, 0.0)).astype(jnp.int32)


def _route(route):
    t = route.shape[1]
    nt = t // ROUTE_TILE
    nbp = -(-_num_blocks(t) // LANES) * LANES
    return pl.pallas_call(
        _route_kernel,
        grid=(1,),
        in_specs=[pl.BlockSpec(route.shape, lambda i: (0, 0))],
        out_specs=[pl.BlockSpec(route.shape, lambda i: (0, 0)),
                   pl.BlockSpec((nt, SUBLANES, 2 * TABLE_LANES), lambda i: (0, 0, 0)),
                   pl.BlockSpec((SUBLANES, LANES), lambda i: (0, 0)),
                   pl.BlockSpec((SUBLANES, nbp), lambda i: (0, 0))],
        out_shape=[jax.ShapeDtypeStruct(route.shape, jnp.int32),
                   jax.ShapeDtypeStruct((nt, SUBLANES, 2 * TABLE_LANES), jnp.int32),
                   jax.ShapeDtypeStruct((SUBLANES, LANES), jnp.int32),
                   jax.ShapeDtypeStruct((SUBLANES, nbp), jnp.int32)],
        compiler_params=pltpu.CompilerParams(
            dimension_semantics=("arbitrary",), vmem_limit_bytes=VMEM_LIMIT),
        name="route",
    )(route)


def _chunk_row(tab_ref, tile, k, reading):
    return pl.multiple_of(tab_ref[tile, 0, (TABLE_LANES if reading else 0) + k], RUN_ALIGN)


def _scatter_kernel(tab_ref, seg_ref, lpos_ref, x_ref, xs_ref, xsl, zrows, sem, zsem):
    tps, lrows = xsl.shape[1], xsl.shape[2]
    ts = x_ref.shape[0] // tps
    n_blocks = (xs_ref.shape[0] - N_SPARE * lrows) // EXPERT_BLOCK
    i = pl.program_id(0)
    last = pl.num_programs(0) - 1
    slot = lax.rem(i, 2)

    def drain(s):
        for tile in range(tps):
            pltpu.make_async_copy(xsl.at[s, tile], xs_ref.at[pl.ds(0, lrows)], sem.at[s]).wait()

    @pl.when(i >= 2)
    def _():
        drain(slot)

    j = lax.broadcasted_iota(jnp.int32, (lrows, ts), 0)
    for tile in range(tps):
        cols = slice(tile * ts, (tile + 1) * ts)
        perm = jnp.where((j == lpos_ref[0:1, cols]) | (j == lpos_ref[1:2, cols]), 1.0, 0.0).astype(bf16)
        xsl[slot, tile] = _pack_rows(jnp.dot(perm, x_ref[cols, :], preferred_element_type=f32), already_bf16=True)

    for tile in range(tps):
        for k in range(lrows // RUN_ALIGN):
            pltpu.make_async_copy(xsl.at[slot, tile, pl.ds(k * RUN_ALIGN, RUN_ALIGN)],
                                  xs_ref.at[pl.ds(_chunk_row(tab_ref, tile, k, False), RUN_ALIGN)],
                                  sem.at[slot]).start()

    @pl.when(i == last)
    def _():
        drain(slot)

        @pl.when(i >= 1)
        def _():
            drain(1 - slot)

        zrows[...] = jnp.zeros(zrows.shape, zrows.dtype)

        def zcopy(row, n):
            return pltpu.make_async_copy(zrows.at[pl.ds(0, n)], xs_ref.at[pl.ds(row, n)], zsem)

        def for_each_unused_block(fn):
            def body(b, carry):
                fn(pl.multiple_of(b * EXPERT_BLOCK, EXPERT_BLOCK), EXPERT_BLOCK)
                return carry

            lax.fori_loop(seg_ref[0, SEG_NACT_LANE], n_blocks, body, 0)
            for off in range(0, N_SPARE * lrows, EXPERT_BLOCK):
                fn(n_blocks * EXPERT_BLOCK + off, min(EXPERT_BLOCK, N_SPARE * lrows - off))

        for_each_unused_block(lambda row, n: zcopy(row, n).start())
        for_each_unused_block(lambda row, n: zcopy(row, n).wait())

        def for_each_pad_chunk(fn):
            def seg(e, carry):
                g0 = seg_ref[0, e]
                used = seg_ref[0, N_EXPERTS + e]
                total = seg_ref[0, 2 * N_EXPERTS + e]

                def chunk(c, carry2):
                    fn(pl.multiple_of((g0 + c) * RUN_ALIGN, RUN_ALIGN), RUN_ALIGN)
                    return carry2

                def coarse(c, carry2):
                    fn(pl.multiple_of((g0 + c * PAD_COARSE) * RUN_ALIGN, PAD_COARSE * RUN_ALIGN),
                       PAD_COARSE * RUN_ALIGN)
                    return carry2

                aligned = lax.div(used + (PAD_COARSE - 1), PAD_COARSE)
                lax.fori_loop(used, aligned * PAD_COARSE, chunk, 0)
                lax.fori_loop(aligned, lax.div(total, PAD_COARSE), coarse, 0)
                return carry

            lax.fori_loop(0, N_EXPERTS, seg, 0)

        for_each_pad_chunk(lambda row, n: zcopy(row, n).start())
        for_each_pad_chunk(lambda row, n: zcopy(row, n).wait())


def _scatter(tab, seg, lpos, xn2, n_blocks):
    t = xn2.shape[0]
    tps = TILES_PER_STEP
    ts = ROUTE_TILE * tps
    lrows = _local_rows(ROUTE_TILE)
    return pl.pallas_call(
        _scatter_kernel,
        grid=(t // ts,),
        in_specs=[
            pl.BlockSpec((tps, SUBLANES, 2 * TABLE_LANES), lambda i: (i, 0, 0), memory_space=pltpu.SMEM),
            pl.BlockSpec((SUBLANES, LANES), lambda i: (0, 0), memory_space=pltpu.SMEM),
            pl.BlockSpec((SUBLANES, ts), lambda i: (0, i)),
            pl.BlockSpec((ts, D_MODEL), lambda i: (i, 0)),
        ],
        out_specs=pl.BlockSpec(memory_space=pl.ANY),
        out_shape=jax.ShapeDtypeStruct((n_blocks * EXPERT_BLOCK + N_SPARE * lrows, PACKED), jnp.uint32),
        scratch_shapes=[pltpu.VMEM((2, tps, lrows, PACKED), jnp.uint32),
                        pltpu.VMEM((EXPERT_BLOCK, PACKED), jnp.uint32),
                        pltpu.SemaphoreType.DMA((2,)), pltpu.SemaphoreType.DMA],
        compiler_params=pltpu.CompilerParams(
            dimension_semantics=("arbitrary",), vmem_limit_bytes=VMEM_LIMIT, has_side_effects=True),
        name="scatter",
    )(tab, seg, lpos, xn2)


def _expert_kernel(bexp_ref, nact_ref, xs_ref, wg_ref, wu_ref, wd_ref, ys_ref, wg_b, wu_b, wd_b):
    b = pl.program_id(0)
    active = b < nact_ref[0]
    new_expert = jnp.logical_or(b == 0, bexp_ref[b] != bexp_ref[jnp.maximum(b - 1, 0)])

    @pl.when(jnp.logical_not(active))
    def _():
        ys_ref[...] = jnp.zeros(ys_ref.shape, ys_ref.dtype)

    def swiglu(wg, wu, wd):
        x = jnp.concatenate(_unpack_rows(xs_ref[...]), axis=1)
        g = jnp.dot(x, wg, preferred_element_type=f32)
        hmid = (g * _sigmoid(g) * jnp.dot(x, wu, preferred_element_type=f32)).astype(bf16)
        ys_ref[...] = _pack_rows(jnp.dot(hmid, wd, preferred_element_type=f32))

    @pl.when(jnp.logical_and(active, new_expert))
    def _():
        wg, wu, wd = (r[0].astype(bf16) for r in (wg_ref, wu_ref, wd_ref))
        wg_b[...] = wg
        wu_b[...] = wu
        wd_b[...] = wd
        swiglu(wg, wu, wd)

    @pl.when(jnp.logical_and(active, jnp.logical_not(new_expert)))
    def _():
        swiglu(wg_b[...], wu_b[...], wd_b[...])


def _experts(bexp, nact, xs, wg, wu, wd, nb):
    n_rows = nb * EXPERT_BLOCK

    def blk(b, nact_ref):
        return jnp.minimum(b, nact_ref[0] - 1)

    grid_spec = pltpu.PrefetchScalarGridSpec(
        num_scalar_prefetch=2,
        grid=(nb,),
        in_specs=[
            pl.BlockSpec((EXPERT_BLOCK, PACKED), lambda b, e, n: (blk(b, n), 0)),
            pl.BlockSpec((1, D_MODEL, D_FF_EXPERT), lambda b, e, n: (e[blk(b, n)], 0, 0)),
            pl.BlockSpec((1, D_MODEL, D_FF_EXPERT), lambda b, e, n: (e[blk(b, n)], 0, 0)),
            pl.BlockSpec((1, D_FF_EXPERT, D_MODEL), lambda b, e, n: (e[blk(b, n)], 0, 0)),
        ],
        out_specs=pl.BlockSpec((EXPERT_BLOCK, PACKED), lambda b, e, n: (b, 0)),
        scratch_shapes=[pltpu.VMEM((D_MODEL, D_FF_EXPERT), bf16), pltpu.VMEM((D_MODEL, D_FF_EXPERT), bf16),
                        pltpu.VMEM((D_FF_EXPERT, D_MODEL), bf16)],
    )
    return pl.pallas_call(
        _expert_kernel,
        grid_spec=grid_spec,
        out_shape=jax.ShapeDtypeStruct((n_rows, PACKED), jnp.uint32),
        compiler_params=pltpu.CompilerParams(
            dimension_semantics=("arbitrary",), vmem_limit_bytes=VMEM_LIMIT),
        name="experts",
    )(bexp, nact, xs, wg, wu, wd)


def _combine_kernel(tab_ref, tab_next_ref, lpos_ref, wts_ref, h_ref, gf_ref, ys_ref, out_ref, ybuf, sem):
    tps, lrows = ybuf.shape[1], ybuf.shape[2]
    ts = h_ref.shape[0] // tps
    i = pl.program_id(0)
    slot = lax.rem(i, 2)

    def fetch(t_ref, s):
        for tile in range(tps):
            for k in range(lrows // RUN_ALIGN):
                pltpu.make_async_copy(ys_ref.at[pl.ds(_chunk_row(t_ref, tile, k, True), RUN_ALIGN)],
                                      ybuf.at[s, tile, pl.ds(k * RUN_ALIGN, RUN_ALIGN)], sem.at[s]).start()

    @pl.when(i == 0)
    def _():
        fetch(tab_ref, 0)

    @pl.when(i + 1 < pl.num_programs(0))
    def _():
        fetch(tab_next_ref, 1 - slot)

    for tile in range(tps):
        pltpu.make_async_copy(ys_ref.at[pl.ds(0, lrows)], ybuf.at[slot, tile], sem.at[slot]).wait()

    jl = lax.broadcasted_iota(jnp.int16, (ts, lrows), 1)
    zero = jnp.zeros((), bf16)
    for tile in range(tps):
        cols = slice(tile * ts, (tile + 1) * ts)
        info = jnp.concatenate([lpos_ref[:, cols].astype(f32), wts_ref[:, cols],
                                jnp.zeros((LANES - 2 * SUBLANES, ts), f32)], axis=0).T
        mix = (jnp.where(jl == info[:, 0:1].astype(jnp.int16), info[:, SUBLANES:SUBLANES + 1].astype(bf16), zero)
               + jnp.where(jl == info[:, 1:2].astype(jnp.int16), info[:, SUBLANES + 1:SUBLANES + 2].astype(bf16),
                           zero))
        moe = jnp.concatenate([jnp.dot(mix, y, preferred_element_type=f32)
                               for y in _unpack_rows(ybuf[slot, tile])], axis=1)
        out_ref[cols, :] = _rms(h_ref[cols, :] + moe, gf_ref[...])


def _combine(tab, lpos, wts, h, g_final, ys):
    t = h.shape[0]
    tps = TILES_PER_STEP
    ts = ROUTE_TILE * tps
    nt = t // ts
    lrows = _local_rows(ROUTE_TILE)
    assert ys.shape[0] >= lrows
    return pl.pallas_call(
        _combine_kernel,
        grid=(nt,),
        in_specs=[
            pl.BlockSpec((tps, SUBLANES, 2 * TABLE_LANES), lambda i: (i, 0, 0), memory_space=pltpu.SMEM),
            pl.BlockSpec((tps, SUBLANES, 2 * TABLE_LANES), lambda i: (jnp.minimum(i + 1, nt - 1), 0, 0),
                         memory_space=pltpu.SMEM),
            pl.BlockSpec((SUBLANES, ts), lambda i: (0, i)),
            pl.BlockSpec((SUBLANES, ts), lambda i: (0, i)),
            pl.BlockSpec((ts, D_MODEL), lambda i: (i, 0)),
            pl.BlockSpec((1, D_MODEL), lambda i: (0, 0)),
            pl.BlockSpec(memory_space=pl.ANY),
        ],
        out_specs=pl.BlockSpec((ts, D_MODEL), lambda i: (i, 0)),
        out_shape=jax.ShapeDtypeStruct((t, D_MODEL), f32),
        scratch_shapes=[pltpu.VMEM((2, tps, lrows, PACKED), jnp.uint32), pltpu.SemaphoreType.DMA((2,))],
        compiler_params=pltpu.CompilerParams(
            dimension_semantics=("arbitrary",), vmem_limit_bytes=VMEM_LIMIT),
        name="combine",
    )(tab, tab, lpos, wts, h, g_final, ys)


def _prep_router(w_group, b_group, w_expert, b_expert):
    d = w_group.shape[0]
    gaps = (SUBLANES - N_GROUPS, LANES - ROUTER_ROWS)
    w = jnp.concatenate([w_group, jnp.zeros((d, gaps[0]), f32), w_expert, jnp.zeros((d, gaps[1]), f32)], axis=1)
    b = jnp.concatenate([b_group, jnp.full((gaps[0],), NEG_BIG, f32), b_expert, jnp.full((gaps[1],), NEG_BIG, f32)])
    return w.astype(bf16), b[None, :]


def kernel(x, g_mix, w_in, w_dw, b_dw, ln_conv_g, ln_conv_b, sinks, w_conv_out, w_attn_out, w_out, g_ffn,
           w_group, b_group, w_expert, b_expert, w_gate, w_up, w_down, g_final):
    batch, seq, d = x.shape
    assert d == D_MODEL and seq % MIXER_TILE == 0 and seq % TOKEN_TILE == 0 and g_mix.shape[0] == 1
    assert (batch * seq) % (ROUTE_TILE * TILES_PER_STEP) == 0
    t = batch * seq
    x2 = x.reshape(t, d)

    act, q, kv, sgates = _inproj(x2, g_mix[0][None, :], w_in[0], w_dw[0, :, 0, :], b_dw[0][None, :],
                                 ln_conv_g[0][None, :], ln_conv_b[0][None, :], seq)

    w_r, b_r = _prep_router(w_group[0], b_group[0], w_expert[0], b_expert[0])
    h, xn2, route, wts = _mixer(
        x2, act, q, kv, sgates, sinks[0], w_conv_out[0].astype(bf16), w_attn_out[0].astype(bf16),
        w_out[0].astype(bf16), g_ffn[0][None, :], w_r, b_r, batch, seq)

    lpos, tab, seg, meta = _route(route)
    nb = _num_blocks(t)
    xs = _scatter(tab, seg, lpos, xn2, nb)
    ys = _experts(meta[0, :nb], meta[1, :1], xs, w_gate[0], w_up[0], w_down[0], nb)
    out = _combine(tab, lpos, wts, h, g_final[None, :], ys)
    return out.reshape(batch, seq, d)
```

```python
import functools

import numpy as np
import jax
import jax.numpy as jnp
from jax import lax
from jax.experimental import pallas as pl
from jax.experimental.pallas import tpu as pltpu

D_MODEL = 1024
CONV_CH = 512
CONV_WIDTH = 31
N_HEADS = 8
N_KV_HEADS = 2
HEAD_DIM = 64
ATTN_BLOCK = 128
N_GROUPS = 4
EXPERTS_PER_GROUP = 8
N_EXPERTS = N_GROUPS * EXPERTS_PER_GROUP
D_FF_EXPERT = 512
NORM_EPS = 1e-6

Q_DIM = N_HEADS * HEAD_DIM
KV_DIM = N_KV_HEADS * HEAD_DIM

LANES = 128
SUBLANES = 8
CONV_HALO = 32
CONV_ROWS = 32
PROJ_PIECE = 256
ROUTER_ROWS = SUBLANES + N_EXPERTS
NEG_BIG = -1e30

TOKEN_TILE = 512
MIXER_TILE = 1024
ROUTE_TILE = 512
RUN_ALIGN = SUBLANES
EXPERT_BLOCK = 1024
VMEM_LIMIT = 58 * 1024 * 1024

f32 = jnp.float32
bf16 = jnp.bfloat16


def _rms(x, g):
    ms = jnp.mean(x * x, axis=-1, keepdims=True)
    return x * lax.rsqrt(ms + NORM_EPS) * g


def _sigmoid(x):
    return 1.0 / (1.0 + jnp.exp(-x))


def _exact_zero(x):
    bits = pltpu.bitcast(x, jnp.uint32)
    sixteen = jnp.uint32(16)
    return pltpu.bitcast(lax.shift_right_logical(lax.shift_right_logical(bits, sixteen), sixteen), f32)


PACKED = D_MODEL // 2
_HIGH_HALF = 0xFFFF0000


def _pack_rows(x, already_bf16=False):
    def bits(v):
        return pltpu.bitcast(v if already_bf16 else v.astype(bf16).astype(f32), jnp.uint32)
    low = lax.shift_right_logical(bits(x[:, :PACKED]), jnp.uint32(16))
    return low | (bits(x[:, PACKED:]) & jnp.uint32(_HIGH_HALF))


def _unpack_rows(p):
    low = pltpu.bitcast(lax.shift_left(p, jnp.uint32(16)), f32).astype(bf16)
    high = pltpu.bitcast(p & jnp.uint32(_HIGH_HALF), f32).astype(bf16)
    return low, high


def _load_w_in(w_hbm, stage, w_ref, sem):
    copy = pltpu.make_async_copy(w_hbm, stage, sem)
    copy.start()
    copy.wait()

    def put(dst, value):
        w_ref[:, dst:dst + value.shape[1]] = value.astype(bf16)

    for b in range(CONV_CH // LANES):
        put(2 * b * LANES, stage[:, b * LANES:(b + 1) * LANES])
        put((2 * b + 1) * LANES, stage[:, CONV_CH + b * LANES:CONV_CH + (b + 1) * LANES])
    src = dst = 2 * CONV_CH
    for lo in range(0, Q_DIM, LANES):
        put(dst + lo, stage[:, src + lo:src + lo + LANES] * (HEAD_DIM ** -0.5))
    src, dst = src + Q_DIM, dst + Q_DIM
    for _ in range(2):
        for h in range(N_KV_HEADS):
            head = stage[:, src + h * HEAD_DIM:src + (h + 1) * HEAD_DIM]
            put(dst + 2 * h * HEAD_DIM, jnp.concatenate([head, head], axis=1))
        src, dst = src + KV_DIM, dst + 2 * KV_DIM
    for lo in range(0, 2 * D_MODEL, LANES):
        put(dst + lo, stage[:, src + lo:src + lo + LANES])


def _inproj_kernel(tiles_per_seq, x_ref, g_ref, w_hbm, wdw_ref, bdw_ref, lng_ref, lnb_ref,
                   act_ref, q_ref, kv_ref, sgate_ref, xn_s, vbuf, w_stage, w_ref, w_sem):
    tm = x_ref.shape[0]

    @pl.when(pl.program_id(0) == 0)
    def _():
        _load_w_in(w_hbm, w_stage, w_ref, w_sem)

    xn_s[...] = _rms(x_ref[...], g_ref[...]).astype(bf16)

    def proj(lo, hi):
        return jnp.dot(xn_s[...], w_ref[:, lo:hi], preferred_element_type=f32)

    n_cb = CONV_CH // LANES
    first = lax.rem(pl.program_id(0), tiles_per_seq) == 0

    @pl.when(first)
    def _():
        vbuf[:, 0:CONV_HALO, :] = jnp.zeros((n_cb, CONV_HALO, LANES), f32)

    @pl.when(jnp.logical_not(first))
    def _():
        vbuf[:, 0:CONV_HALO, :] = vbuf[:, tm:tm + CONV_HALO, :]

    for b in range(n_cb):
        u = proj(2 * b * LANES, 2 * (b + 1) * LANES)
        vbuf[b, CONV_HALO:, :] = u[:, :LANES] * _sigmoid(u[:, LANES:])

    pieces, col = [], 2 * CONV_CH
    for out_ref, fn in ((q_ref, lambda z: z), (kv_ref, lambda z: z), (sgate_ref, _sigmoid)):
        pieces += [(out_ref, col, lo, fn) for lo in range(0, out_ref.shape[1], PROJ_PIECE)]
        col += out_ref.shape[1]

    rows = CONV_ROWS
    n_steps = tm // rows
    lag = n_steps - len(pieces)
    assert lag >= 0
    tap0 = CONV_HALO - (CONV_WIDTH - 1)

    def tie_rows(x):
        return jnp.concatenate([_exact_zero(x[0:SUBLANES, 0:LANES])] * (rows // SUBLANES), axis=0)

    tie = None
    for c in range(n_steps):
        if c >= lag:
            out_ref, col, lo, fn = pieces[c - lag]
            z = proj(col + lo, col + lo + PROJ_PIECE)
            out_ref[:, lo:lo + PROJ_PIECE] = fn(z).astype(bf16)
            tie = tie_rows(z) if tie is None else tie + tie_rows(z)
        accs = []
        for b in range(n_cb):
            acc = jnp.broadcast_to(bdw_ref[:, b * LANES:(b + 1) * LANES], (rows, LANES))
            if tie is not None:
                acc = acc + tie
            for j in range(CONV_WIDTH):
                r0 = c * rows + tap0 + j
                acc = acc + wdw_ref[j:j + 1, b * LANES:(b + 1) * LANES] * vbuf[b, r0:r0 + rows, :]
            accs.append(acc)
            tie = tie_rows(sum(acc[k:k + SUBLANES] for k in range(0, rows, SUBLANES)))
        mu = sum(jnp.sum(a, axis=-1, keepdims=True) for a in accs) * (1.0 / CONV_CH)
        ds = [a - mu for a in accs]
        var = sum(jnp.sum(d * d, axis=-1, keepdims=True) for d in ds) * (1.0 / CONV_CH)
        inv = lax.rsqrt(var + NORM_EPS)
        for b in range(n_cb):
            y = ds[b] * inv * lng_ref[:, b * LANES:(b + 1) * LANES] + lnb_ref[:, b * LANES:(b + 1) * LANES]
            act_ref[c * rows:(c + 1) * rows, b * LANES:(b + 1) * LANES] = (y * _sigmoid(y)).astype(bf16)


def _inproj(x2, g_mix, w_in, w_dw, b_dw, ln_g, ln_b, seq):
    t = x2.shape[0]
    tm = TOKEN_TILE
    widths = (CONV_CH, Q_DIM, 4 * KV_DIM, 2 * D_MODEL)

    def full(a):
        return pl.BlockSpec(a.shape, lambda i: (0,) * a.ndim)

    return pl.pallas_call(
        functools.partial(_inproj_kernel, seq // tm),
        grid=(t // tm,),
        in_specs=[pl.BlockSpec((tm, D_MODEL), lambda i: (i, 0)),
                  full(g_mix), pl.BlockSpec(memory_space=pl.ANY), full(w_dw), full(b_dw), full(ln_g), full(ln_b)],
        out_specs=[pl.BlockSpec((tm, w), lambda i: (i, 0)) for w in widths],
        out_shape=[jax.ShapeDtypeStruct((t, w), bf16) for w in widths],
        scratch_shapes=[pltpu.VMEM((tm, D_MODEL), bf16),
                        pltpu.VMEM((CONV_CH // LANES, tm + CONV_HALO, LANES), f32),
                        pltpu.VMEM(w_in.shape, f32),
                        pltpu.VMEM((D_MODEL, sum(widths) + CONV_CH), bf16),
                        pltpu.SemaphoreType.DMA],
        compiler_params=pltpu.CompilerParams(
            dimension_semantics=("arbitrary",), vmem_limit_bytes=VMEM_LIMIT),
        name="inproj",
    )(x2, g_mix, w_in, w_dw, b_dw, ln_g, ln_b)


def _mixer_kernel(x_ref, act_ref, q_ref, kv_ref, kvp_ref, sgate_ref, sink_ref, bias_ref, wco_ref, wao_ref,
                  wo_ref, gffn_ref, wr_ref, br_ref,
                  h_ref, xn2_ref, route_ref, wts_ref,
                  kvall, attn, s_scr, m_scr):
    ts = x_ref.shape[0]
    first = pl.program_id(1) == 0

    kvall[0:ATTN_BLOCK, :] = jnp.where(first, jnp.zeros_like(kvp_ref[...]), kvp_ref[...])
    kvall[ATTN_BLOCK:, :] = kv_ref[...]
    nkeys = 2 * ATTN_BLOCK
    left_kv = lax.broadcasted_iota(jnp.int32, (nkeys, LANES), 1) < HEAD_DIM
    left_q = lax.broadcasted_iota(jnp.int32, (ATTN_BLOCK, LANES), 1) < HEAD_DIM
    key_lane = lax.broadcasted_iota(jnp.int32, (ATTN_BLOCK, 2 * nkeys), 1)
    prev_keys = (key_lane % nkeys) < ATTN_BLOCK
    no_prev = jnp.where(prev_keys, jnp.where(first, -jnp.inf, 0.0), 0.0)

    def block_diag(x):
        z = jnp.zeros_like(x)
        return jnp.concatenate([jnp.where(left_kv, x, z), jnp.where(left_kv, z, x)], axis=0)

    n_pairs = N_HEADS // 2
    for j in range(ts // ATTN_BLOCK):
        r0 = j * ATTN_BLOCK
        for kvh in range(N_KV_HEADS):
            kbd = block_diag(kvall[r0:r0 + nkeys, kvh * LANES:(kvh + 1) * LANES])
            for ii in range(2):
                i = 2 * kvh + ii
                qb = q_ref[r0:r0 + ATTN_BLOCK, i * LANES:(i + 1) * LANES]
                s = lax.dot_general(qb, kbd, (((1,), (1,)), ((), ())), preferred_element_type=f32)
                s = s + bias_ref[i]
                if j == 0:
                    s = s + no_prev
                s_scr[j * n_pairs + i] = s
                for half in range(2):
                    m = jnp.max(s[:, half * nkeys:(half + 1) * nkeys], axis=-1, keepdims=True)
                    m_scr[2 * (j * n_pairs + i) + half] = jnp.maximum(m, sink_ref[2 * i + half])

    for j in range(ts // ATTN_BLOCK):
        r0 = j * ATTN_BLOCK
        for kvh in range(N_KV_HEADS):
            vbd = block_diag(kvall[r0:r0 + nkeys, (N_KV_HEADS + kvh) * LANES:(N_KV_HEADS + kvh + 1) * LANES])
            for ii in range(2):
                i = 2 * kvh + ii
                s = s_scr[j * n_pairs + i]
                ps, ls = [], []
                for half in range(2):
                    m = m_scr[2 * (j * n_pairs + i) + half]
                    p = jnp.exp(s[:, half * nkeys:(half + 1) * nkeys] - m)
                    ps.append(p)
                    ls.append(jnp.sum(p, axis=-1, keepdims=True) + jnp.exp(sink_ref[2 * i + half] - m))
                p = jnp.concatenate(ps, axis=1).astype(bf16)
                o = jnp.dot(p, vbd, preferred_element_type=f32)
                o = o * jnp.where(left_q, 1.0 / ls[0], 1.0 / ls[1])
                attn[r0:r0 + ATTN_BLOCK, i * LANES:(i + 1) * LANES] = o.astype(bf16)

    conv_o = jnp.dot(act_ref[...], wco_ref[...], preferred_element_type=f32)
    attn_o = jnp.dot(attn[...], wao_ref[...], preferred_element_type=f32)
    merged = (sgate_ref[:, :D_MODEL].astype(f32) * conv_o
              + sgate_ref[:, D_MODEL:].astype(f32) * attn_o).astype(bf16)
    h = x_ref[...] + jnp.dot(merged, wo_ref[...], preferred_element_type=f32)
    h_ref[...] = h

    xn2 = _rms(h, gffn_ref[...]).astype(bf16)
    xn2_ref[...] = xn2
    logits = jnp.dot(xn2, wr_ref[...], preferred_element_type=f32) + br_ref[...]
    lt = logits.T
    sub = lax.broadcasted_iota(jnp.int32, (SUBLANES, ts), 0)

    gl = lt[0:SUBLANES]
    gmax = jnp.max(gl, axis=0, keepdims=True)
    gsel = jnp.min(jnp.where(gl == gmax, sub, SUBLANES), axis=0, keepdims=True)
    p_group = 1.0 / jnp.sum(jnp.exp(gl - gmax), axis=0, keepdims=True)

    e_in = lt[SUBLANES:2 * SUBLANES]
    for g in range(1, N_GROUPS):
        e_in = jnp.where(gsel == g, lt[(g + 1) * SUBLANES:(g + 2) * SUBLANES], e_in)
    m1 = jnp.max(e_in, axis=0, keepdims=True)
    i1 = jnp.min(jnp.where(e_in == m1, sub, SUBLANES), axis=0, keepdims=True)
    rest = jnp.where(sub == i1, -jnp.inf, e_in)
    m2 = jnp.max(rest, axis=0, keepdims=True)
    i2 = jnp.min(jnp.where(rest == m2, sub, SUBLANES), axis=0, keepdims=True)
    t2 = jnp.exp(m2 - m1)
    w1 = p_group / (1.0 + t2)
    w2 = p_group * t2 / (1.0 + t2)
    base = gsel * EXPERTS_PER_GROUP
    route_ref[...] = jnp.where(sub == 0, base + i1, jnp.where(sub == 1, base + i2, 0))
    wts_ref[...] = jnp.where(sub == 0, w1, jnp.where(sub == 1, w2, 0.0))


def _attn_bias():
    qi = np.arange(ATTN_BLOCK)[:, None]
    kj = np.arange(2 * ATTN_BLOCK)[None, :]
    rel = (ATTN_BLOCK + qi - kj).astype(np.float32)
    ok = (rel >= 0) & (rel < ATTN_BLOCK)
    slopes = np.array([2.0 ** (-8.0 * (h + 1) / N_HEADS) for h in range(N_HEADS)], np.float32)
    per_head = [np.where(ok, -(slopes[h] * rel), -np.inf).astype(np.float32) for h in range(N_HEADS)]
    return np.stack([np.concatenate([per_head[2 * i], per_head[2 * i + 1]], axis=1)
                     for i in range(N_HEADS // 2)])


def _mixer(x2, act, q, kv, sgates, sinks, wco, wao, wo, g_ffn, w_r, b_r, batch, seq):
    t = x2.shape[0]
    ts = MIXER_TILE
    ns = seq // ts
    bias = jnp.asarray(_attn_bias())

    def row(b, s):
        return b * ns + s

    def full(a):
        return pl.BlockSpec(a.shape, lambda b, s: (0,) * a.ndim, pipeline_mode=pl.Buffered(1))

    in_specs = [
        pl.BlockSpec((ts, D_MODEL), lambda b, s: (row(b, s), 0)),
        pl.BlockSpec((ts, CONV_CH), lambda b, s: (row(b, s), 0)),
        pl.BlockSpec((ts, Q_DIM), lambda b, s: (row(b, s), 0)),
        pl.BlockSpec((ts, 4 * KV_DIM), lambda b, s: (row(b, s), 0)),
        pl.BlockSpec((ATTN_BLOCK, 4 * KV_DIM),
                     lambda b, s: (jnp.maximum(row(b, s) * (ts // ATTN_BLOCK) - 1, 0), 0)),
        pl.BlockSpec((ts, 2 * D_MODEL), lambda b, s: (row(b, s), 0)),
        pl.BlockSpec(memory_space=pltpu.SMEM),
        full(bias), full(wco), full(wao), full(wo), full(g_ffn), full(w_r), full(b_r),
    ]
    out_specs = [
        pl.BlockSpec((ts, D_MODEL), lambda b, s: (row(b, s), 0)),
        pl.BlockSpec((ts, D_MODEL), lambda b, s: (row(b, s), 0)),
        pl.BlockSpec((SUBLANES, ts), lambda b, s: (0, row(b, s))),
        pl.BlockSpec((SUBLANES, ts), lambda b, s: (0, row(b, s))),
    ]
    out_shape = [
        jax.ShapeDtypeStruct((t, D_MODEL), f32),
        jax.ShapeDtypeStruct((t, D_MODEL), bf16),
        jax.ShapeDtypeStruct((SUBLANES, t), jnp.int32),
        jax.ShapeDtypeStruct((SUBLANES, t), f32),
    ]
    return pl.pallas_call(
        _mixer_kernel,
        grid=(batch, ns),
        in_specs=in_specs,
        out_specs=out_specs,
        out_shape=out_shape,
        scratch_shapes=[
            pltpu.VMEM((ts + ATTN_BLOCK, 4 * KV_DIM), bf16),
            pltpu.VMEM((ts, Q_DIM), bf16),
            pltpu.VMEM((ts // ATTN_BLOCK * (N_HEADS // 2), ATTN_BLOCK, 4 * ATTN_BLOCK), f32),
            pltpu.VMEM((ts // ATTN_BLOCK * N_HEADS, ATTN_BLOCK, 1), f32),
        ],
        compiler_params=pltpu.CompilerParams(
            dimension_semantics=("arbitrary", "arbitrary"), vmem_limit_bytes=VMEM_LIMIT),
        name="mixer",
    )(x2, act, q, kv, kv, sgates, sinks, bias, wco, wao, wo, g_ffn, w_r, b_r)


def _local_rows(ts):
    return -(-(2 * ts + (RUN_ALIGN - 1) * N_EXPERTS) // LANES) * LANES


PAD_COARSE = 16
assert (EXPERT_BLOCK // RUN_ALIGN) % PAD_COARSE == 0
TABLE_LANES = 2 * LANES
SEG_NACT_LANE = 3 * N_EXPERTS
SEG_SPARE_LANE = 3 * N_EXPERTS + 1
TILES_PER_STEP = 2
N_SPARE = 2 * TILES_PER_STEP


def _num_blocks(t):
    run_rows = 2 * t + (RUN_ALIGN - 1) * N_EXPERTS * (t // ROUTE_TILE)
    return -(-(run_rows + N_EXPERTS * (EXPERT_BLOCK - RUN_ALIGN)) // EXPERT_BLOCK)


def _route_kernel(route_ref, lpos_ref, tab_ref, seg_ref, meta_ref):
    t = route_ref.shape[1]
    tr = ROUTE_TILE
    nbp = meta_ref.shape[1]
    chunks_per_block = EXPERT_BLOCK // RUN_ALIGN
    eiota = lax.broadcasted_iota(jnp.int32, (N_EXPERTS, tr), 0)
    before = (lax.broadcasted_iota(jnp.int32, (tr, tr), 0)
              < lax.broadcasted_iota(jnp.int32, (tr, tr), 1)).astype(bf16)
    lower = (lax.broadcasted_iota(jnp.int32, (N_EXPERTS, N_EXPERTS), 1)
             < lax.broadcasted_iota(jnp.int32, (N_EXPERTS, N_EXPERTS), 0)).astype(bf16)
    sub = lax.broadcasted_iota(jnp.int32, (N_EXPERTS, LANES), 0)
    lane = lax.broadcasted_iota(jnp.int32, (N_EXPERTS, LANES), 1)

    def to_lanes(col, offset):
        return jnp.sum(jnp.where(sub + offset == lane, col, 0.0), axis=0, keepdims=True)

    def expert_prefix(col):
        b = jnp.broadcast_to(col, (N_EXPERTS, LANES))
        hi = jnp.floor(b * (1.0 / 16.0))
        lo = b - 16.0 * hi
        return (16.0 * jnp.dot(lower, hi.astype(bf16), preferred_element_type=f32)
                + jnp.dot(lower, lo.astype(bf16), preferred_element_type=f32))[:, 0:1]

    lpos_ref[...] = jnp.zeros(lpos_ref.shape, jnp.int32)
    chunk_id = lax.broadcasted_iota(jnp.int32, (N_EXPERTS, TABLE_LANES), 1).astype(f32)
    chunk_expert = lax.broadcasted_iota(jnp.int32, (N_EXPERTS, TABLE_LANES), 0).astype(f32)

    def step(i, seen_chunks):
        off = pl.multiple_of(i * tr, tr)
        m1 = eiota == route_ref[0:1, pl.ds(off, tr)]
        m2 = eiota == route_ref[1:2, pl.ds(off, tr)]
        onehot = jnp.where(m1 | m2, 1.0, 0.0)
        within = jnp.dot(onehot.astype(bf16), before, preferred_element_type=f32)
        run_chunks = jnp.floor((jnp.sum(onehot, axis=1, keepdims=True) + (RUN_ALIGN - 1)) * (1.0 / RUN_ALIGN))
        run_start = expert_prefix(run_chunks)
        pos = within + RUN_ALIGN * run_start
        lpos_ref[0:1, pl.ds(off, tr)] = jnp.sum(jnp.where(m1, pos, 0.0), axis=0, keepdims=True).astype(jnp.int32)
        lpos_ref[1:2, pl.ds(off, tr)] = jnp.sum(jnp.where(m2, pos, 0.0), axis=0, keepdims=True).astype(jnp.int32)
        owner = (run_start <= chunk_id) & (chunk_id < run_start + run_chunks)
        rel = jnp.sum(jnp.where(owner, seen_chunks + chunk_id - run_start, 0.0), axis=0, keepdims=True)
        eid = jnp.sum(jnp.where(owner, chunk_expert, 0.0), axis=0, keepdims=True)
        n_used = jnp.sum(run_chunks, axis=0, keepdims=True)
        unused = chunk_id[0:1] >= n_used
        rel = jnp.where(unused, chunk_id[0:1] - n_used, rel)
        eid = jnp.where(unused, (SEG_SPARE_LANE + lax.rem(i, N_SPARE)).astype(f32), eid)
        row = jnp.concatenate([rel, eid], axis=1)
        tab_ref[i] = jnp.broadcast_to(row, (SUBLANES, 2 * TABLE_LANES)).astype(jnp.int32)
        return seen_chunks + run_chunks

    used_chunks = lax.fori_loop(0, t // tr, step, jnp.zeros((N_EXPERTS, 1), f32), unroll=8)

    nblk = jnp.floor((used_chunks + (chunks_per_block - 1)) * (1.0 / chunks_per_block))
    first_blk = expert_prefix(nblk)
    nact = jnp.sum(nblk, axis=0, keepdims=True)
    spare_chunk = _num_blocks(t) * chunks_per_block
    seg_row = (to_lanes(first_blk * chunks_per_block, 0) + to_lanes(used_chunks, N_EXPERTS)
               + to_lanes(nblk * chunks_per_block, 2 * N_EXPERTS)
               + jnp.where(lane[0:1] == SEG_NACT_LANE, nact, 0.0)
               + sum(jnp.where(lane[0:1] == SEG_SPARE_LANE + a,
                               float(spare_chunk + a * (_local_rows(tr) // RUN_ALIGN)), 0.0)
                     for a in range(N_SPARE)))
    seg_ref[...] = jnp.broadcast_to(seg_row, (SUBLANES, LANES)).astype(jnp.int32)

    tab = tab_ref[...].astype(f32)
    rel, eid = tab[:, :, :TABLE_LANES], tab[:, :, TABLE_LANES:]
    start = jnp.zeros_like(rel)
    for seg_lane in list(range(N_EXPERTS)) + [SEG_SPARE_LANE + a for a in range(N_SPARE)]:
        start = start + jnp.where(eid == seg_lane, seg_row[:, seg_lane:seg_lane + 1], 0.0)
    write_row = (start + rel) * RUN_ALIGN
    read_row = jnp.where(eid >= SEG_SPARE_LANE, rel, start + rel) * RUN_ALIGN
    tab_ref[...] = jnp.concatenate([write_row, read_row], axis=2).astype(jnp.int32)

    blk = lax.broadcasted_iota(jnp.int32, (N_EXPERTS, nbp), 1).astype(f32)
    owner = (first_blk <= blk) & (blk < first_blk + nblk)
    expert_id = lax.broadcasted_iota(jnp.int32, (N_EXPERTS, nbp), 0).astype(f32)
    bexp = jnp.sum(jnp.where(owner, expert_id, 0.0), axis=0, keepdims=True)
    row8 = lax.broadcasted_iota(jnp.int32, (SUBLANES, nbp), 0)
    meta_ref[...] = jnp.where(row8 == 0, bexp, jnp.where(row8 == 1, nact, 0.0)).astype(jnp.int32)


def _route(route):
    t = route.shape[1]
    nt = t // ROUTE_TILE
    nbp = -(-_num_blocks(t) // LANES) * LANES
    return pl.pallas_call(
        _route_kernel,
        grid=(1,),
        in_specs=[pl.BlockSpec(route.shape, lambda i: (0, 0))],
        out_specs=[pl.BlockSpec(route.shape, lambda i: (0, 0)),
                   pl.BlockSpec((nt, SUBLANES, 2 * TABLE_LANES), lambda i: (0, 0, 0)),
                   pl.BlockSpec((SUBLANES, LANES), lambda i: (0, 0)),
                   pl.BlockSpec((SUBLANES, nbp), lambda i: (0, 0))],
        out_shape=[jax.ShapeDtypeStruct(route.shape, jnp.int32),
                   jax.ShapeDtypeStruct((nt, SUBLANES, 2 * TABLE_LANES), jnp.int32),
                   jax.ShapeDtypeStruct((SUBLANES, LANES), jnp.int32),
                   jax.ShapeDtypeStruct((SUBLANES, nbp), jnp.int32)],
        compiler_params=pltpu.CompilerParams(
            dimension_semantics=("arbitrary",), vmem_limit_bytes=VMEM_LIMIT),
        name="route",
    )(route)


def _chunk_row(tab_ref, tile, k, reading):
    return pl.multiple_of(tab_ref[tile, 0, (TABLE_LANES if reading else 0) + k], RUN_ALIGN)


def _scatter_kernel(tab_ref, seg_ref, lpos_ref, x_ref, xs_ref, xsl, zrows, sem, zsem):
    tps, lrows = xsl.shape[1], xsl.shape[2]
    ts = x_ref.shape[0] // tps
    n_blocks = (xs_ref.shape[0] - N_SPARE * lrows) // EXPERT_BLOCK
    i = pl.program_id(0)
    last = pl.num_programs(0) - 1
    slot = lax.rem(i, 2)

    def drain(s):
        for tile in range(tps):
            pltpu.make_async_copy(xsl.at[s, tile], xs_ref.at[pl.ds(0, lrows)], sem.at[s]).wait()

    @pl.when(i >= 2)
    def _():
        drain(slot)

    j = lax.broadcasted_iota(jnp.int32, (lrows, ts), 0)
    for tile in range(tps):
        cols = slice(tile * ts, (tile + 1) * ts)
        perm = jnp.where((j == lpos_ref[0:1, cols]) | (j == lpos_ref[1:2, cols]), 1.0, 0.0).astype(bf16)
        xsl[slot, tile] = _pack_rows(jnp.dot(perm, x_ref[cols, :], preferred_element_type=f32), already_bf16=True)

    for tile in range(tps):
        for k in range(lrows // RUN_ALIGN):
            pltpu.make_async_copy(xsl.at[slot, tile, pl.ds(k * RUN_ALIGN, RUN_ALIGN)],
                                  xs_ref.at[pl.ds(_chunk_row(tab_ref, tile, k, False), RUN_ALIGN)],
                                  sem.at[slot]).start()

    @pl.when(i == last)
    def _():
        drain(slot)

        @pl.when(i >= 1)
        def _():
            drain(1 - slot)

        zrows[...] = jnp.zeros(zrows.shape, zrows.dtype)

        def zcopy(row, n):
            return pltpu.make_async_copy(zrows.at[pl.ds(0, n)], xs_ref.at[pl.ds(row, n)], zsem)

        def for_each_unused_block(fn):
            def body(b, carry):
                fn(pl.multiple_of(b * EXPERT_BLOCK, EXPERT_BLOCK), EXPERT_BLOCK)
                return carry

            lax.fori_loop(seg_ref[0, SEG_NACT_LANE], n_blocks, body, 0)
            for off in range(0, N_SPARE * lrows, EXPERT_BLOCK):
                fn(n_blocks * EXPERT_BLOCK + off, min(EXPERT_BLOCK, N_SPARE * lrows - off))

        for_each_unused_block(lambda row, n: zcopy(row, n).start())
        for_each_unused_block(lambda row, n: zcopy(row, n).wait())

        def for_each_pad_chunk(fn):
            def seg(e, carry):
                g0 = seg_ref[0, e]
                used = seg_ref[0, N_EXPERTS + e]
                total = seg_ref[0, 2 * N_EXPERTS + e]

                def chunk(c, carry2):
                    fn(pl.multiple_of((g0 + c) * RUN_ALIGN, RUN_ALIGN), RUN_ALIGN)
                    return carry2

                def coarse(c, carry2):
                    fn(pl.multiple_of((g0 + c * PAD_COARSE) * RUN_ALIGN, PAD_COARSE * RUN_ALIGN),
                       PAD_COARSE * RUN_ALIGN)
                    return carry2

                aligned = lax.div(used + (PAD_COARSE - 1), PAD_COARSE)
                lax.fori_loop(used, aligned * PAD_COARSE, chunk, 0)
                lax.fori_loop(aligned, lax.div(total, PAD_COARSE), coarse, 0)
                return carry

            lax.fori_loop(0, N_EXPERTS, seg, 0)

        for_each_pad_chunk(lambda row, n: zcopy(row, n).start())
        for_each_pad_chunk(lambda row, n: zcopy(row, n).wait())


def _scatter(tab, seg, lpos, xn2, n_blocks):
    t = xn2.shape[0]
    tps = TILES_PER_STEP
    ts = ROUTE_TILE * tps
    lrows = _local_rows(ROUTE_TILE)
    return pl.pallas_call(
        _scatter_kernel,
        grid=(t // ts,),
        in_specs=[
            pl.BlockSpec((tps, SUBLANES, 2 * TABLE_LANES), lambda i: (i, 0, 0), memory_space=pltpu.SMEM),
            pl.BlockSpec((SUBLANES, LANES), lambda i: (0, 0), memory_space=pltpu.SMEM),
            pl.BlockSpec((SUBLANES, ts), lambda i: (0, i)),
            pl.BlockSpec((ts, D_MODEL), lambda i: (i, 0)),
        ],
        out_specs=pl.BlockSpec(memory_space=pl.ANY),
        out_shape=jax.ShapeDtypeStruct((n_blocks * EXPERT_BLOCK + N_SPARE * lrows, PACKED), jnp.uint32),
        scratch_shapes=[pltpu.VMEM((2, tps, lrows, PACKED), jnp.uint32),
                        pltpu.VMEM((EXPERT_BLOCK, PACKED), jnp.uint32),
                        pltpu.SemaphoreType.DMA((2,)), pltpu.SemaphoreType.DMA],
        compiler_params=pltpu.CompilerParams(
            dimension_semantics=("arbitrary",), vmem_limit_bytes=VMEM_LIMIT, has_side_effects=True),
        name="scatter",
    )(tab, seg, lpos, xn2)


def _expert_kernel(meta_ref, xs_ref, wg_ref, wu_ref, wd_ref, ys_ref, wg_b, wu_b, wd_b, obuf, zbuf, osem, zsem):
    b = pl.program_id(0)
    nb = pl.num_programs(0)
    nact = meta_ref[1, 0]
    active = b < nact
    slot = lax.rem(b, 2)
    new_expert = jnp.logical_or(b == 0, meta_ref[0, b] != meta_ref[0, jnp.maximum(b - 1, 0)])

    def block_rows(block):
        return ys_ref.at[pl.ds(pl.multiple_of(block * EXPERT_BLOCK, EXPERT_BLOCK), EXPERT_BLOCK)]

    def out_copy(block, s):
        return pltpu.make_async_copy(obuf.at[s], block_rows(block), osem.at[s])

    def zero_copy(block):
        return pltpu.make_async_copy(zbuf, block_rows(block), zsem)

    @pl.when(b == 0)
    def _():
        zbuf[...] = jnp.zeros(zbuf.shape, zbuf.dtype)

    @pl.when(jnp.where(active, nact + b < nb, b >= 2 * nact))
    def _():
        zero_copy(jnp.where(active, nact + b, b)).start()

    @pl.when(jnp.logical_and(active, b >= 2))
    def _():
        out_copy(b - 2, slot).wait()

    def swiglu(wg, wu, wd):
        x = jnp.concatenate(_unpack_rows(xs_ref[...]), axis=1)
        g = jnp.dot(x, wg, preferred_element_type=f32)
        hmid = (g * _sigmoid(g) * jnp.dot(x, wu, preferred_element_type=f32)).astype(bf16)
        obuf[slot] = _pack_rows(jnp.dot(hmid, wd, preferred_element_type=f32))
        out_copy(b, slot).start()

    @pl.when(jnp.logical_and(active, new_expert))
    def _():
        wg, wu, wd = (r[0].astype(bf16) for r in (wg_ref, wu_ref, wd_ref))
        wg_b[...] = wg
        wu_b[...] = wu
        wd_b[...] = wd
        swiglu(wg, wu, wd)

    @pl.when(jnp.logical_and(active, jnp.logical_not(new_expert)))
    def _():
        swiglu(wg_b[...], wu_b[...], wd_b[...])

    @pl.when(b == nb - 1)
    def _():
        @pl.when(nact >= 1)
        def _():
            out_copy(0, lax.rem(nact - 1, 2)).wait()

        @pl.when(nact >= 2)
        def _():
            out_copy(0, lax.rem(nact, 2)).wait()

        def drain(_, carry):
            zero_copy(0).wait()
            return carry

        lax.fori_loop(0, nb - nact, drain, 0)


def _experts(meta, xs, wg, wu, wd, nb):
    n_rows = nb * EXPERT_BLOCK
    assert meta.shape[1] >= nb

    def blk(b, m):
        return jnp.minimum(b, m[1, 0] - 1)

    def expert(b, m):
        return (m[0, blk(b, m)], 0, 0)

    grid_spec = pltpu.PrefetchScalarGridSpec(
        num_scalar_prefetch=1,
        grid=(nb,),
        in_specs=[
            pl.BlockSpec((EXPERT_BLOCK, PACKED), lambda b, m: (blk(b, m), 0)),
            pl.BlockSpec((1, D_MODEL, D_FF_EXPERT), expert),
            pl.BlockSpec((1, D_MODEL, D_FF_EXPERT), expert),
            pl.BlockSpec((1, D_FF_EXPERT, D_MODEL), expert),
        ],
        out_specs=pl.BlockSpec(memory_space=pl.ANY),
        scratch_shapes=[pltpu.VMEM((D_MODEL, D_FF_EXPERT), bf16), pltpu.VMEM((D_MODEL, D_FF_EXPERT), bf16),
                        pltpu.VMEM((D_FF_EXPERT, D_MODEL), bf16),
                        pltpu.VMEM((2, EXPERT_BLOCK, PACKED), jnp.uint32), pltpu.VMEM((EXPERT_BLOCK, PACKED), jnp.uint32),
                        pltpu.SemaphoreType.DMA((2,)), pltpu.SemaphoreType.DMA],
    )
    return pl.pallas_call(
        _expert_kernel,
        grid_spec=grid_spec,
        out_shape=jax.ShapeDtypeStruct((n_rows, PACKED), jnp.uint32),
        compiler_params=pltpu.CompilerParams(
            dimension_semantics=("arbitrary",), vmem_limit_bytes=VMEM_LIMIT),
        name="experts",
    )(meta, xs, wg, wu, wd)


def _combine_kernel(tab_ref, tab_next_ref, lpos_ref, wts_ref, h_ref, gf_ref, ys_ref, out_ref, ybuf, sem):
    tps, lrows = ybuf.shape[1], ybuf.shape[2]
    ts = h_ref.shape[0] // tps
    i = pl.program_id(0)
    slot = lax.rem(i, 2)

    def fetch(t_ref, s):
        for tile in range(tps):
            for k in range(lrows // RUN_ALIGN):
                pltpu.make_async_copy(ys_ref.at[pl.ds(_chunk_row(t_ref, tile, k, True), RUN_ALIGN)],
                                      ybuf.at[s, tile, pl.ds(k * RUN_ALIGN, RUN_ALIGN)], sem.at[s]).start()

    @pl.when(i == 0)
    def _():
        fetch(tab_ref, 0)

    @pl.when(i + 1 < pl.num_programs(0))
    def _():
        fetch(tab_next_ref, 1 - slot)

    for tile in range(tps):
        pltpu.make_async_copy(ys_ref.at[pl.ds(0, lrows)], ybuf.at[slot, tile], sem.at[slot]).wait()

    jl = lax.broadcasted_iota(jnp.int16, (ts, lrows), 1)
    zero = jnp.zeros((), bf16)
    for tile in range(tps):
        cols = slice(tile * ts, (tile + 1) * ts)
        info = jnp.concatenate([lpos_ref[:, cols].astype(f32), wts_ref[:, cols],
                                jnp.zeros((LANES - 2 * SUBLANES, ts), f32)], axis=0).T
        mix = (jnp.where(jl == info[:, 0:1].astype(jnp.int16), info[:, SUBLANES:SUBLANES + 1].astype(bf16), zero)
               + jnp.where(jl == info[:, 1:2].astype(jnp.int16), info[:, SUBLANES + 1:SUBLANES + 2].astype(bf16),
                           zero))
        moe = jnp.concatenate([jnp.dot(mix, y, preferred_element_type=f32)
                               for y in _unpack_rows(ybuf[slot, tile])], axis=1)
        out_ref[cols, :] = _rms(h_ref[cols, :] + moe, gf_ref[...])


def _combine(tab, lpos, wts, h, g_final, ys):
    t = h.shape[0]
    tps = TILES_PER_STEP
    ts = ROUTE_TILE * tps
    nt = t // ts
    lrows = _local_rows(ROUTE_TILE)
    assert ys.shape[0] >= lrows
    return pl.pallas_call(
        _combine_kernel,
        grid=(nt,),
        in_specs=[
            pl.BlockSpec((tps, SUBLANES, 2 * TABLE_LANES), lambda i: (i, 0, 0), memory_space=pltpu.SMEM),
            pl.BlockSpec((tps, SUBLANES, 2 * TABLE_LANES), lambda i: (jnp.minimum(i + 1, nt - 1), 0, 0),
                         memory_space=pltpu.SMEM),
            pl.BlockSpec((SUBLANES, ts), lambda i: (0, i)),
            pl.BlockSpec((SUBLANES, ts), lambda i: (0, i)),
            pl.BlockSpec((ts, D_MODEL), lambda i: (i, 0)),
            pl.BlockSpec((1, D_MODEL), lambda i: (0, 0)),
            pl.BlockSpec(memory_space=pl.ANY),
        ],
        out_specs=pl.BlockSpec((ts, D_MODEL), lambda i: (i, 0)),
        out_shape=jax.ShapeDtypeStruct((t, D_MODEL), f32),
        scratch_shapes=[pltpu.VMEM((2, tps, lrows, PACKED), jnp.uint32), pltpu.SemaphoreType.DMA((2,))],
        compiler_params=pltpu.CompilerParams(
            dimension_semantics=("arbitrary",), vmem_limit_bytes=VMEM_LIMIT),
        name="combine",
    )(tab, tab, lpos, wts, h, g_final, ys)


def _prep_router(w_group, b_group, w_expert, b_expert):
    d = w_group.shape[0]
    gaps = (SUBLANES - N_GROUPS, LANES - ROUTER_ROWS)
    w = jnp.concatenate([w_group, jnp.zeros((d, gaps[0]), f32), w_expert, jnp.zeros((d, gaps[1]), f32)], axis=1)
    b = jnp.concatenate([b_group, jnp.full((gaps[0],), NEG_BIG, f32), b_expert, jnp.full((gaps[1],), NEG_BIG, f32)])
    return w.astype(bf16), b[None, :]


def kernel(x, g_mix, w_in, w_dw, b_dw, ln_conv_g, ln_conv_b, sinks, w_conv_out, w_attn_out, w_out, g_ffn,
           w_group, b_group, w_expert, b_expert, w_gate, w_up, w_down, g_final):
    batch, seq, d = x.shape
    assert d == D_MODEL and seq % MIXER_TILE == 0 and seq % TOKEN_TILE == 0 and g_mix.shape[0] == 1
    assert (batch * seq) % (ROUTE_TILE * TILES_PER_STEP) == 0
    t = batch * seq
    x2 = x.reshape(t, d)

    act, q, kv, sgates = _inproj(x2, g_mix[0][None, :], w_in[0], w_dw[0, :, 0, :], b_dw[0][None, :],
                                 ln_conv_g[0][None, :], ln_conv_b[0][None, :], seq)

    w_r, b_r = _prep_router(w_group[0], b_group[0], w_expert[0], b_expert[0])
    h, xn2, route, wts = _mixer(
        x2, act, q, kv, sgates, sinks[0], w_conv_out[0].astype(bf16), w_attn_out[0].astype(bf16),
        w_out[0].astype(bf16), g_ffn[0][None, :], w_r, b_r, batch, seq)

    lpos, tab, seg, meta = _route(route)
    nb = _num_blocks(t)
    xs = _scatter(tab, seg, lpos, xn2, nb)
    ys = _experts(meta, xs, w_gate[0], w_up[0], w_down[0], nb)
    out = _combine(tab, lpos, wts, h, g_final[None, :], ys)
    return out.reshape(batch, seq, d)
```
